```python
import math
import jax, jax.numpy as jnp
from jax import lax
import numpy as np

D_MODEL = 1024
BATCH = 8
SEQ = 2048
DEPTH = 2

HEAD_DIM = 64
BLOCK = 128
WINDOW = 128
GRID_W = 64
A_HEADS = 6
A_KV_HEADS = 2
B_HEADS = 6
B_KV_HEADS = 2
C_HEADS = 6
C_V_DIM = 2 * HEAD_DIM
X_HEADS = 4
MEM_LEN = 256
REL_HEADS = 6
REL_BUCKETS = 32
REL_MAX_DIST = 128
ROPE_THETA = 10000.0
EPS = 1e-6
NEG_INF = -1e30
N_EVEN = (DEPTH + 1) // 2
N_ODD = DEPTH // 2

EVEN_MIX = (A_HEADS + B_HEADS + X_HEADS) * HEAD_DIM
ODD_MIX = C_HEADS * C_V_DIM + X_HEADS * HEAD_DIM
EVEN_SPLITS = (A_HEADS * HEAD_DIM, A_KV_HEADS * HEAD_DIM, A_KV_HEADS * HEAD_DIM,
               B_HEADS * HEAD_DIM, B_KV_HEADS * HEAD_DIM, B_KV_HEADS * HEAD_DIM,
               X_HEADS * HEAD_DIM, EVEN_MIX)
ODD_SPLITS = (C_HEADS * HEAD_DIM, C_HEADS * HEAD_DIM, C_HEADS * HEAD_DIM, C_HEADS * HEAD_DIM,
              C_HEADS * C_V_DIM, X_HEADS * HEAD_DIM, ODD_MIX)
EVEN_IN = sum(EVEN_SPLITS)
ODD_IN = sum(ODD_SPLITS)

kernel_name = "hybrid_window_axial_diff_encoder"


def _split(t, sizes):
    idx = [int(s) for s in np.cumsum(sizes)[:-1]]
    return jnp.split(t, idx, axis=-1)


def rmsnorm(x, g):
    xf = x.astype(jnp.float32)
    y = xf * lax.rsqrt(jnp.mean(xf * xf, axis=-1, keepdims=True) + EPS)
    return (y * g.astype(jnp.float32)).astype(x.dtype)


def t5_bucket(rel):
    nb = REL_BUCKETS // 2
    max_exact = nb // 2
    ret = jnp.where(rel > 0, nb, 0)
    n = jnp.abs(rel)
    nf = jnp.maximum(n, 1).astype(jnp.float32)
    large = max_exact + (jnp.log(nf / max_exact) / math.log(REL_MAX_DIST / max_exact)
                         * (nb - max_exact)).astype(jnp.int32)
    large = jnp.minimum(large, nb - 1)
    return ret + jnp.where(n < max_exact, n, large)


def rel_bias_lookup(table, rel):
    return jnp.moveaxis(table[t5_bucket(rel)], -1, 0).astype(jnp.float32)


def windowed_gqa_sink(q, k, v, sink, table):
    Bn, S, H, D = q.shape
    KVH = k.shape[2]
    G = H // KVH
    nb = S // BLOCK
    pad = ((0, 0), (WINDOW, WINDOW), (0, 0), (0, 0))
    kp = jnp.pad(k, pad).reshape(Bn, nb + 2, BLOCK, KVH, D)
    vp = jnp.pad(v, pad).reshape(Bn, nb + 2, BLOCK, KVH, D)
    kw = jnp.concatenate([kp[:, :-2], kp[:, 1:-1], kp[:, 2:]], axis=2)
    vw = jnp.concatenate([vp[:, :-2], vp[:, 1:-1], vp[:, 2:]], axis=2)
    qb = q.reshape(Bn, nb, BLOCK, KVH, G, D)
    logits = jnp.einsum("bnqkgd,bnskd->bnkgqs", qb, kw).astype(jnp.float32) * (D ** -0.5)
    a = jnp.arange(BLOCK)[:, None]
    c = jnp.arange(3 * BLOCK)[None, :]
    rel = c - BLOCK - a
    bias = rel_bias_lookup(table, rel).reshape(KVH, G, BLOCK, 3 * BLOCK)
    kpos = jnp.arange(nb)[:, None] * BLOCK - BLOCK + c
    valid = ((jnp.abs(rel) <= WINDOW)[None]
             & (kpos >= 0)[:, None, :] & (kpos < S)[:, None, :])
    logits = jnp.where(valid[None, :, None, None], logits + bias, NEG_INF)
    sink_col = jnp.broadcast_to(sink.astype(jnp.float32).reshape(1, 1, KVH, G, 1, 1),
                                logits.shape[:-1] + (1,))
    p = jax.nn.softmax(jnp.concatenate([logits, sink_col], axis=-1), axis=-1)[..., :-1]
    out = jnp.einsum("bnkgqs,bnskd->bnqkgd", p.astype(v.dtype), vw)
    return out.reshape(Bn, S, H, D)


def axial_rope_tables(S):
    rows = S // GRID_W
    row = jnp.broadcast_to(jnp.arange(rows)[:, None], (rows, GRID_W)).reshape(-1)
    col = jnp.broadcast_to(jnp.arange(GRID_W)[None, :], (rows, GRID_W)).reshape(-1)
    half = HEAD_DIM // 2
    inv = 1.0 / (ROPE_THETA ** (jnp.arange(0, half, 2, dtype=jnp.float32) / half))
    ang = jnp.concatenate([row.astype(jnp.float32)[:, None] * inv,
                           col.astype(jnp.float32)[:, None] * inv], axis=-1)
    return jnp.cos(ang), jnp.sin(ang)


def apply_axial_rope(x, cos, sin):
    Bn, S, H, D = x.shape
    quarter = D // 4
    xr = x.astype(jnp.float32).reshape(Bn, S, H, 2, 2, quarter)
    x1, x2 = xr[..., 0, :], xr[..., 1, :]
    c = cos.reshape(S, 2, quarter)[None, :, None]
    s = sin.reshape(S, 2, quarter)[None, :, None]
    out = jnp.stack([x1 * c - x2 * s, x1 * s + x2 * c], axis=-2)
    return out.reshape(Bn, S, H, D).astype(x.dtype)


def dense_gqa_blocked(q, k, v):
    Bn, S, H, D = q.shape
    KVH = k.shape[2]
    G = H // KVH
    nb = S // BLOCK
    qb = q.reshape(Bn, nb, BLOCK, KVH, G, D).transpose(1, 0, 2, 3, 4, 5)

    def blk(qi):
        s = jnp.einsum("bqkgd,bskd->bkgqs", qi, k).astype(jnp.float32) * (D ** -0.5)
        p = jax.nn.softmax(s, axis=-1)
        return jnp.einsum("bkgqs,bskd->bqkgd", p.astype(v.dtype), v)

    out = lax.map(blk, qb)
    return out.transpose(1, 0, 2, 3, 4, 5).reshape(Bn, S, H, D)


def diff_attention_blocked(q1, q2, k1, k2, v, lam, table):
    Bn, S, H, D = q1.shape
    nb = S // BLOCK
    qb = jnp.stack([q1, q2], 0).reshape(2, Bn, nb, BLOCK, H, D).transpose(2, 0, 1, 3, 4, 5)
    kk = jnp.stack([k1, k2], 0)
    kpos = jnp.arange(S)

    def blk(args):
        n, qi = args
        s = jnp.einsum("ibqhd,ibshd->ibhqs", qi, kk).astype(jnp.float32) * (D ** -0.5)
        rel = kpos[None, :] - (n * BLOCK + jnp.arange(BLOCK))[:, None]
        p = jax.nn.softmax(s + rel_bias_lookup(table, rel)[None, None], axis=-1)
        w = p[0] - lam * p[1]
        return jnp.einsum("bhqs,bshe->bqhe", w.astype(v.dtype), v)

    out = lax.map(blk, (jnp.arange(nb), qb))
    return out.transpose(1, 0, 2, 3, 4).reshape(Bn, S, H, v.shape[-1])


def memory_cross_attention(q, mem_n, w_mem_kv):
    Bn, M, _ = mem_n.shape
    mk, mv = jnp.split(mem_n @ w_mem_kv, 2, axis=-1)
    mk = mk.reshape(Bn, M, X_HEADS, HEAD_DIM)
    mv = mv.reshape(Bn, M, X_HEADS, HEAD_DIM)
    s = jnp.einsum("bshd,bmhd->bhsm", q, mk).astype(jnp.float32) * (HEAD_DIM ** -0.5)
    p = jax.nn.softmax(s, axis=-1)
    return jnp.einsum("bhsm,bmhd->bshd", p.astype(mv.dtype), mv)


def even_layer(x, mem_n, table, norm_g, w_in, sink, q_norm, k_norm, w_mem_kv, w_out):
    Bn, S, _ = x.shape
    h = rmsnorm(x, norm_g)
    aq, ak, av, bq, bk, bv, xq, gate = _split(h @ w_in, EVEN_SPLITS)
    heads = lambda t, n: t.reshape(Bn, S, n, HEAD_DIM)
    y_a = windowed_gqa_sink(heads(aq, A_HEADS), heads(ak, A_KV_HEADS), heads(av, A_KV_HEADS),
                            sink, table)
    cos, sin = axial_rope_tables(S)
    qb = apply_axial_rope(rmsnorm(heads(bq, B_HEADS), q_norm), cos, sin)
    kb = apply_axial_rope(rmsnorm(heads(bk, B_KV_HEADS), k_norm), cos, sin)
    y_b = dense_gqa_blocked(qb, kb, heads(bv, B_KV_HEADS))
    y_x = memory_cross_attention(heads(xq, X_HEADS), mem_n, w_mem_kv)
    y = jnp.concatenate([y_a.reshape(Bn, S, -1), y_b.reshape(Bn, S, -1),
                         y_x.reshape(Bn, S, -1)], axis=-1) * jax.nn.silu(gate)
    return x + y @ w_out


def odd_layer(x, mem_n, table, norm_g, w_in, lq1, lk1, lq2, lk2, subln_g, w_mem_kv, w_out,
              lam_init):
    Bn, S, _ = x.shape
    h = rmsnorm(x, norm_g)
    q1, q2, k1, k2, v, xq, gate = _split(h @ w_in, ODD_SPLITS)
    heads = lambda t, n, d: t.reshape(Bn, S, n, d)
    lam = (jnp.exp(jnp.sum(lq1.astype(jnp.float32) * lk1.astype(jnp.float32)))
           - jnp.exp(jnp.sum(lq2.astype(jnp.float32) * lk2.astype(jnp.float32))) + lam_init)
    y_c = diff_attention_blocked(heads(q1, C_HEADS, HEAD_DIM), heads(q2, C_HEADS, HEAD_DIM),
                                 heads(k1, C_HEADS, HEAD_DIM), heads(k2, C_HEADS, HEAD_DIM),
                                 heads(v, C_HEADS, C_V_DIM), lam, table)
    y_c = rmsnorm(y_c, subln_g) * (1.0 - lam_init)
    y_x = memory_cross_attention(heads(xq, X_HEADS, HEAD_DIM), mem_n, w_mem_kv)
    y = jnp.concatenate([y_c.reshape(Bn, S, -1), y_x.reshape(Bn, S, -1)], axis=-1) \
        * jax.nn.silu(gate)
    return x + y @ w_out


def setup_inputs(seed: int = 0) -> dict:
    key = jax.random.key(seed)
    ks = jax.random.split(key, 21)
    f32 = jnp.float32
    nrm = lambda k, shape, s: jax.random.normal(k, shape, f32) * s
    gain = lambda k, shape: 1.0 + 0.02 * jax.random.normal(k, shape, f32)
    dinv = D_MODEL ** -0.5
    return {
        "x": nrm(ks[0], (BATCH, SEQ, D_MODEL), 1.0),
        "mem": nrm(ks[1], (BATCH, MEM_LEN, D_MODEL), 1.0),
        "rel_bias": nrm(ks[2], (REL_BUCKETS, REL_HEADS), 0.5),
        "mem_norm": gain(ks[3], (D_MODEL,)),
        "final_norm": gain(ks[4], (D_MODEL,)),
        "even_norm": gain(ks[5], (N_EVEN, D_MODEL)),
        "even_w_in": nrm(ks[6], (N_EVEN, D_MODEL, EVEN_IN), dinv),
        "even_sink": nrm(ks[7], (N_EVEN, A_HEADS), 1.0),
        "even_q_norm": gain(ks[8], (N_EVEN, HEAD_DIM)),
        "even_k_norm": gain(ks[9], (N_EVEN, HEAD_DIM)),
        "even_w_mem_kv": nrm(ks[10], (N_EVEN, D_MODEL, 2 * X_HEADS * HEAD_DIM), dinv),
        "even_w_out": nrm(ks[11], (N_EVEN, EVEN_MIX, D_MODEL), EVEN_MIX ** -0.5),
        "odd_norm": gain(ks[12], (N_ODD, D_MODEL)),
        "odd_w_in": nrm(ks[13], (N_ODD, D_MODEL, ODD_IN), dinv),
        "odd_lambda_q1": nrm(ks[14], (N_ODD, HEAD_DIM), 0.1),
        "odd_lambda_k1": nrm(ks[15], (N_ODD, HEAD_DIM), 0.1),
        "odd_lambda_q2": nrm(ks[16], (N_ODD, HEAD_DIM), 0.1),
        "odd_lambda_k2": nrm(ks[17], (N_ODD, HEAD_DIM), 0.1),
        "odd_subln": gain(ks[18], (N_ODD, C_V_DIM)),
        "odd_w_mem_kv": nrm(ks[19], (N_ODD, D_MODEL, 2 * X_HEADS * HEAD_DIM), dinv),
        "odd_w_out": nrm(ks[20], (N_ODD, ODD_MIX, D_MODEL), ODD_MIX ** -0.5),
    }


def reference(x, mem, rel_bias, mem_norm, final_norm, even_norm, even_w_in, even_sink,
              even_q_norm, even_k_norm, even_w_mem_kv, even_w_out, odd_norm, odd_w_in,
              odd_lambda_q1, odd_lambda_k1, odd_lambda_q2, odd_lambda_k2, odd_subln,
              odd_w_mem_kv, odd_w_out):
    mem_n = rmsnorm(mem, mem_norm)
    h = x
    for i in range(DEPTH):
        j = i // 2
        if i % 2 == 0:
            h = even_layer(h, mem_n, rel_bias, even_norm[j], even_w_in[j], even_sink[j],
                           even_q_norm[j], even_k_norm[j], even_w_mem_kv[j], even_w_out[j])
        else:
            lam_init = 0.8 - 0.6 * math.exp(-0.3 * i)
            h = odd_layer(h, mem_n, rel_bias, odd_norm[j], odd_w_in[j], odd_lambda_q1[j],
                          odd_lambda_k1[j], odd_lambda_q2[j], odd_lambda_k2[j], odd_subln[j],
                          odd_w_mem_kv[j], odd_w_out[j], lam_init)
    return rmsnorm(h, final_norm)
```

```python
import functools
import math

import numpy as np
import jax
import jax.numpy as jnp
from jax import lax
from jax.experimental import pallas as pl
from jax.experimental.pallas import tpu as pltpu

D_MODEL = 1024
HEAD_DIM = 64
BLOCK = 128
WINDOW = 128
GRID_W = 64
A_HEADS = 6
B_HEADS = 6
C_HEADS = 6
C_V_DIM = 128
X_HEADS = 4
REL_BUCKETS = 32
REL_MAX_DIST = 128
ROPE_THETA = 10000.0
EPS = 1e-6
NEG_INF = -1e30
LOG2E = math.log2(math.e)
LANES = 128
VMEM_LIMIT = 56 * 1024 * 1024

F32 = jnp.float32
BF16 = jnp.bfloat16

GQA_HEAD_ORDER = (0, 3, 1, 4, 2, 5)


def _cparams(n_axes):
    return pltpu.CompilerParams(dimension_semantics=("arbitrary",) * n_axes,
                                vmem_limit_bytes=VMEM_LIMIT)


def _lane_lo(shape):
    return lax.broadcasted_iota(jnp.int32, shape, len(shape) - 1) < HEAD_DIM


def _rmsnorm_rows(x, g):
    ms = jnp.mean(x * x, axis=-1, keepdims=True)
    return x * lax.rsqrt(ms + EPS) * g


def _nt_dot(a, b):
    return lax.dot_general(a, b, (((1,), (1,)), ((), ())), preferred_element_type=F32)


def _norm_proj_kernel(x_ref, g_ref, w_ref, *out_refs):
    xn = _rmsnorm_rows(x_ref[...], g_ref[...]).astype(BF16)
    c0 = 0
    for o_ref in out_refs:
        width = o_ref.shape[1]
        o_ref[...] = jnp.dot(xn, w_ref[:, c0:c0 + width],
                             preferred_element_type=F32).astype(o_ref.dtype)
        c0 += width


def _norm_proj(x, g, w_bf16, splits, tm):
    rows, d = x.shape
    n = w_bf16.shape[1]
    assert sum(splits) == n and rows % tm == 0
    return pl.pallas_call(
        _norm_proj_kernel,
        grid=(rows // tm,),
        in_specs=[pl.BlockSpec((tm, d), lambda i: (i, 0)),
                  pl.BlockSpec((1, d), lambda i: (0, 0)),
                  pl.BlockSpec((d, n), lambda i: (0, 0))],
        out_specs=[pl.BlockSpec((tm, s), lambda i: (i, 0)) for s in splits],
        out_shape=[jax.ShapeDtypeStruct((rows, s), BF16) for s in splits],
        compiler_params=_cparams(1),
        name="norm_proj",
    )(x, g.reshape(1, d), w_bf16)


def _gate_out_kernel(*refs, n_parts, final):
    y_refs = refs[:n_parts]
    gate_ref, w_ref, res_ref = refs[n_parts:n_parts + 3]
    o_ref = refs[-1]
    acc = res_ref[...]
    c0 = 0
    for y_ref in y_refs:
        width = y_ref.shape[1]
        g = gate_ref[:, c0:c0 + width].astype(F32)
        silu = g * (1.0 / (1.0 + jnp.exp(-g)))
        yy = (y_ref[...].astype(F32) * silu).astype(BF16)
        acc = acc + jnp.dot(yy, w_ref[c0:c0 + width, :], preferred_element_type=F32)
        c0 += width
    if final:
        acc = _rmsnorm_rows(acc, refs[n_parts + 3][...])
    o_ref[...] = acc


def _gate_out(y_parts, gate, w_bf16, resid, final_g, tm):
    rows, d = resid.shape
    mix = w_bf16.shape[0]
    final = final_g is not None
    in_specs = [pl.BlockSpec((tm, y.shape[1]), lambda i: (i, 0)) for y in y_parts]
    in_specs += [pl.BlockSpec((tm, mix), lambda i: (i, 0)),
                 pl.BlockSpec((mix, d), lambda i: (0, 0)),
                 pl.BlockSpec((tm, d), lambda i: (i, 0))]
    args = list(y_parts) + [gate, w_bf16, resid]
    if final:
        in_specs.append(pl.BlockSpec((1, d), lambda i: (0, 0)))
        args.append(final_g.reshape(1, d))
    return pl.pallas_call(
        functools.partial(_gate_out_kernel, n_parts=len(y_parts), final=final),
        grid=(rows // tm,),
        in_specs=in_specs,
        out_specs=pl.BlockSpec((tm, d), lambda i: (i, 0)),
        out_shape=jax.ShapeDtypeStruct((rows, d), F32),
        compiler_params=_cparams(1),
        name="gate_out",
    )(*args)


def _bucket_thresholds():
    nb = REL_BUCKETS // 2
    max_exact = nb // 2
    n = np.arange(0, 4 * REL_MAX_DIST)
    nf = np.maximum(n, 1).astype(np.float32)
    large = max_exact + (np.log(nf / np.float32(max_exact))
                         / np.float32(math.log(REL_MAX_DIST / max_exact))
                         * np.float32(nb - max_exact)).astype(np.int32)
    bucket = np.where(n < max_exact, n, np.minimum(large, nb - 1))
    assert np.all(np.diff(bucket) >= 0) and bucket[0] == 0 and bucket[-1] == nb - 1
    thr = [int(np.argmax(bucket >= k)) for k in range(1, nb)]
    assert thr[-1] < REL_MAX_DIST
    return thr


def _bias_tiles_kernel(tab_ref, full_ref, win_ref, *, n_tiles, n_far):
    h = pl.program_id(0)
    thr = _bucket_thresholds()
    nb = REL_BUCKETS // 2
    row = lax.broadcasted_iota(jnp.int32, (BLOCK, BLOCK), 0)
    col = lax.broadcasted_iota(jnp.int32, (BLOCK, BLOCK), 1)
    for k in range(n_tiles):
        rel = (k - n_far) * BLOCK + col - row
        n = jnp.abs(rel)
        neg = jnp.full((BLOCK, BLOCK), tab_ref[0, h], F32)
        pos = jnp.full((BLOCK, BLOCK), tab_ref[nb, h], F32)
        for b in range(1, nb):
            ge = n >= thr[b - 1]
            neg = jnp.where(ge, tab_ref[b, h], neg)
            pos = jnp.where(ge, tab_ref[nb + b, h], pos)
        tile = jnp.where(rel > 0, pos, neg) * LOG2E
        full_ref[0, k] = tile
        j = k - (n_far - 1)
        if 0 <= j < 3:
            win_ref[:, j * BLOCK:(j + 1) * BLOCK] = jnp.where(n <= WINDOW, tile, NEG_INF)


def _bias_tiles(rel_bias, n_blocks):
    n_far = n_blocks - 1
    n_tiles = 2 * n_blocks - 1
    n_heads = rel_bias.shape[1]
    return pl.pallas_call(
        functools.partial(_bias_tiles_kernel, n_tiles=n_tiles, n_far=n_far),
        grid=(n_heads,),
        in_specs=[pl.BlockSpec(memory_space=pltpu.SMEM)],
        out_specs=[pl.BlockSpec((1, n_tiles, BLOCK, BLOCK), lambda h: (h, 0, 0, 0)),
                   pl.BlockSpec((BLOCK, 3 * BLOCK), lambda h: (h, 0))],
        out_shape=[jax.ShapeDtypeStruct((n_heads, n_tiles, BLOCK, BLOCK), F32),
                   jax.ShapeDtypeStruct((n_heads * BLOCK, 3 * BLOCK), F32)],
        compiler_params=_cparams(1),
        name="bias_tiles",
    )(rel_bias)


def _stack_gqa_heads(q_tiles):
    lo = _lane_lo(q_tiles[0].shape)
    return jnp.concatenate([jnp.where(lo, q, 0.0).astype(BF16) for q in q_tiles]
                           + [jnp.where(lo, 0.0, q).astype(BF16) for q in q_tiles], axis=0)


def _stack_pair_heads(q):
    lo = _lane_lo(q.shape)
    return jnp.concatenate([jnp.where(lo, q, 0.0).astype(BF16),
                            jnp.where(lo, 0.0, q).astype(BF16)], axis=0)


def _v_with_ones(v_pair):
    lane = lax.broadcasted_iota(jnp.int32, v_pair.shape, 1)
    vf = v_pair.astype(F32)
    v_lo = jnp.where(lane < HEAD_DIM, vf, jnp.where(lane == HEAD_DIM, 1.0, 0.0))
    v_hi = jnp.where(lane >= HEAD_DIM, vf, jnp.where(lane == 0, 1.0, 0.0))
    return v_lo.astype(BF16), v_hi.astype(BF16)


def _normalise_pair(o_lo, o_hi, extra_lo=None, extra_hi=None):
    lane = lax.broadcasted_iota(jnp.int32, o_lo.shape, 1)
    l_lo = jnp.sum(jnp.where(lane == HEAD_DIM, o_lo, 0.0), axis=-1, keepdims=True)
    l_hi = jnp.sum(jnp.where(lane == 0, o_hi, 0.0), axis=-1, keepdims=True)
    if extra_lo is not None:
        l_lo = l_lo + extra_lo
        l_hi = l_hi + extra_hi
    return jnp.where(lane < HEAD_DIM, o_lo * (1.0 / l_lo), o_hi * (1.0 / l_hi))


def _window_attn_kernel(q_ref, kp_ref, kc_ref, kn_ref, vp_ref, vc_ref, vn_ref,
                        bias_ref, sink_ref, o_ref, *, n_blocks):
    n = pl.program_id(1)
    qf = q_ref[...].astype(F32) * (HEAD_DIM ** -0.5 * LOG2E)
    lhs = _stack_gqa_heads([qf[:, j * LANES:(j + 1) * LANES] for j in range(3)])
    s = jnp.concatenate([_nt_dot(lhs, kp_ref[...]), _nt_dot(lhs, kc_ref[...]),
                         _nt_dot(lhs, kn_ref[...])], axis=1)
    col = lax.broadcasted_iota(jnp.int32, s.shape, 1)
    valid = jnp.logical_and(jnp.logical_or(col >= BLOCK, n > 0),
                            jnp.logical_or(col < 2 * BLOCK, n < n_blocks - 1))
    s = jnp.where(valid, s + bias_ref[...], NEG_INF)
    sink = sink_ref[...]
    m = jnp.maximum(jnp.max(s, axis=-1, keepdims=True), sink)
    p = jnp.exp2(s - m).astype(BF16)
    p_sink = jnp.exp2(sink - m)
    v_lo, v_hi = _v_with_ones(jnp.concatenate([vp_ref[...], vc_ref[...], vn_ref[...]], axis=0))
    half = 3 * BLOCK
    o_lo = jnp.dot(p[:half], v_lo, preferred_element_type=F32)
    o_hi = jnp.dot(p[half:], v_hi, preferred_element_type=F32)
    for j in range(3):
        rows = slice(j * BLOCK, (j + 1) * BLOCK)
        o_ref[:, j * LANES:(j + 1) * LANES] = _normalise_pair(
            o_lo[rows], o_hi[rows], p_sink[:half][rows], p_sink[half:][rows]).astype(o_ref.dtype)


def _window_attn(q, k, v, bias_win, sink_rows, batch, seq):
    nb = seq // BLOCK
    kv_spec = lambda f: pl.BlockSpec((BLOCK, LANES), f)
    prev = lambda b, n: (b * nb + jnp.maximum(n - 1, 0), 0)
    cur = lambda b, n: (b * nb + n, 0)
    nxt = lambda b, n: (b * nb + jnp.minimum(n + 1, nb - 1), 0)
    return pl.pallas_call(
        functools.partial(_window_attn_kernel, n_blocks=nb),
        grid=(batch, nb),
        in_specs=[pl.BlockSpec((BLOCK, 3 * LANES), cur),
                  kv_spec(prev), kv_spec(cur), kv_spec(nxt),
                  kv_spec(prev), kv_spec(cur), kv_spec(nxt),
                  pl.BlockSpec(bias_win.shape, lambda b, n: (0, 0)),
                  pl.BlockSpec(sink_rows.shape, lambda b, n: (0, 0))],
        out_specs=pl.BlockSpec((BLOCK, 3 * LANES), cur),
        out_shape=jax.ShapeDtypeStruct(q.shape, BF16),
        compiler_params=_cparams(2),
        name="window_attn",
    )(q, k, k, k, v, v, v, bias_win, sink_rows)


def _norm_rope_pair(x, g, cos, sin_signed):
    lo = _lane_lo(x.shape)
    x2 = x * x
    ss_lo = jnp.sum(jnp.where(lo, x2, 0.0), axis=-1, keepdims=True)
    ss_hi = jnp.sum(jnp.where(lo, 0.0, x2), axis=-1, keepdims=True)
    ms = jnp.where(lo, ss_lo, ss_hi) * (1.0 / HEAD_DIM)
    y = x * lax.rsqrt(ms + EPS) * g
    lane = lax.broadcasted_iota(jnp.int32, x.shape, 1)
    quarter = HEAD_DIM // 4
    first = (lane & quarter) == 0
    partner = jnp.where(first, pltpu.roll(y, LANES - quarter, 1), pltpu.roll(y, quarter, 1))
    return y * cos + partner * sin_signed


def _dense_attn_kernel(q_ref, k_ref, v_ref, cos_ref, sin_ref, gq_ref, gk_ref, o_ref,
                       k_scr, vlo_scr, vhi_scr, *, tq):
    i = pl.program_id(1)

    @pl.when(i == 0)
    def _():
        k_scr[...] = _norm_rope_pair(k_ref[...].astype(F32), gk_ref[...],
                                     cos_ref[...], sin_ref[...]).astype(BF16)
        v_lo, v_hi = _v_with_ones(v_ref[...])
        vlo_scr[...] = v_lo
        vhi_scr[...] = v_hi

    rows = pl.ds(pl.multiple_of(i * tq, tq), tq)
    cos = cos_ref[rows, :]
    sin = sin_ref[rows, :]
    q_tiles = []
    for j in range(3):
        qj = _norm_rope_pair(q_ref[:, j * LANES:(j + 1) * LANES].astype(F32),
                             gq_ref[...], cos, sin)
        q_tiles.append(qj * (HEAD_DIM ** -0.5 * LOG2E))
    lhs = _stack_gqa_heads(q_tiles)
    s = _nt_dot(lhs, k_scr[...])
    m = jnp.max(s, axis=-1, keepdims=True)
    p = jnp.exp2(s - m).astype(BF16)
    half = 3 * tq
    o_lo = jnp.dot(p[:half], vlo_scr[...], preferred_element_type=F32)
    o_hi = jnp.dot(p[half:], vhi_scr[...], preferred_element_type=F32)
    for j in range(3):
        r = slice(j * tq, (j + 1) * tq)
        o_ref[:, j * LANES:(j + 1) * LANES] = _normalise_pair(o_lo[r], o_hi[r]).astype(o_ref.dtype)


def _dense_attn(q, k, v, cos, sin_signed, gq, gk, batch, seq, tq):
    nq = seq // tq
    return pl.pallas_call(
        functools.partial(_dense_attn_kernel, tq=tq),
        grid=(batch, nq),
        in_specs=[pl.BlockSpec((tq, 3 * LANES), lambda b, i: (b * nq + i, 0)),
                  pl.BlockSpec((seq, LANES), lambda b, i: (b, 0)),
                  pl.BlockSpec((seq, LANES), lambda b, i: (b, 0)),
                  pl.BlockSpec((seq, LANES), lambda b, i: (0, 0)),
                  pl.BlockSpec((seq, LANES), lambda b, i: (0, 0)),
                  pl.BlockSpec((1, LANES), lambda b, i: (0, 0)),
                  pl.BlockSpec((1, LANES), lambda b, i: (0, 0))],
        out_specs=pl.BlockSpec((tq, 3 * LANES), lambda b, i: (b * nq + i, 0)),
        out_shape=jax.ShapeDtypeStruct(q.shape, BF16),
        scratch_shapes=[pltpu.VMEM((seq, LANES), BF16)] * 3,
        compiler_params=_cparams(2),
        name="dense_attn",
    )(q, k, v, cos, sin_signed, gq, gk)


def _cross_attn_kernel(q_ref, mk_ref, mv_ref, o_ref, *, tq):
    n_pairs = q_ref.shape[1] // LANES
    for j in range(n_pairs):
        cols = slice(j * LANES, (j + 1) * LANES)
        lhs = _stack_pair_heads(q_ref[:, cols].astype(F32) * (HEAD_DIM ** -0.5 * LOG2E))
        s = _nt_dot(lhs, mk_ref[:, cols])
        m = jnp.max(s, axis=-1, keepdims=True)
        p = jnp.exp2(s - m).astype(BF16)
        v_lo, v_hi = _v_with_ones(mv_ref[:, cols])
        o_lo = jnp.dot(p[:tq], v_lo, preferred_element_type=F32)
        o_hi = jnp.dot(p[tq:], v_hi, preferred_element_type=F32)
        o_ref[:, cols] = _normalise_pair(o_lo, o_hi).astype(o_ref.dtype)


def _cross_attn(q, mk, mv, batch, seq, tq):
    nq = seq // tq
    mem_len = mk.shape[0] // batch
    width = q.shape[1]
    return pl.pallas_call(
        functools.partial(_cross_attn_kernel, tq=tq),
        grid=(batch, nq),
        in_specs=[pl.BlockSpec((tq, width), lambda b, i: (b * nq + i, 0)),
                  pl.BlockSpec((mem_len, width), lambda b, i: (b, 0)),
                  pl.BlockSpec((mem_len, width), lambda b, i: (b, 0))],
        out_specs=pl.BlockSpec((tq, width), lambda b, i: (b * nq + i, 0)),
        out_shape=jax.ShapeDtypeStruct(q.shape, BF16),
        compiler_params=_cparams(2),
        name="cross_attn",
    )(q, mk, mv)


def _diff_attn_kernel(q1_ref, q2_ref, k1_ref, k2_ref, v_ref, bias_ref, lam_ref, g_ref,
                      o_ref, *, tq, n_blocks, lam_init):
    i = pl.program_id(2)
    sub = tq // BLOCK
    lam_vec = lam_ref[...]
    lam = (jnp.exp(jnp.sum(lam_vec[0:1] * lam_vec[1:2], axis=-1, keepdims=True))
           - jnp.exp(jnp.sum(lam_vec[2:3] * lam_vec[3:4], axis=-1, keepdims=True)) + lam_init)

    def probs(q_ref, k_ref):
        lhs = _stack_pair_heads(q_ref[...].astype(F32) * (HEAD_DIM ** -0.5 * LOG2E))
        s = _nt_dot(lhs, k_ref[...])
        cols = []
        for mblk in range(n_blocks):
            tiles = []
            for head in range(2):
                for u in range(sub):
                    tiles.append(bias_ref[head, (n_blocks - 1) - (i * sub + u) + mblk])
            cols.append(s[:, mblk * BLOCK:(mblk + 1) * BLOCK] + jnp.concatenate(tiles, axis=0))
        s = jnp.concatenate(cols, axis=1)
        m = jnp.max(s, axis=-1, keepdims=True)
        p = jnp.exp2(s - m)
        return p, jnp.sum(p, axis=-1, keepdims=True)

    p1, l1 = probs(q1_ref, k1_ref)
    p2, l2 = probs(q2_ref, k2_ref)
    w = (p1 * (1.0 / l1) - p2 * (lam / l2)).astype(BF16)
    g = g_ref[...] * (1.0 - lam_init)
    for head in range(2):
        cols = slice(head * C_V_DIM, (head + 1) * C_V_DIM)
        out = jnp.dot(w[head * tq:(head + 1) * tq], v_ref[:, cols], preferred_element_type=F32)
        o_ref[:, cols] = _rmsnorm_rows(out, g).astype(o_ref.dtype)


def _diff_attn(q1, q2, k1, k2, v, bias_full, lam_vecs, subln_g, lam_init, batch, seq, tq):
    nq = seq // tq
    nb = seq // BLOCK
    n_pairs = C_HEADS // 2
    n_tiles = bias_full.shape[1]
    q_spec = pl.BlockSpec((tq, LANES), lambda b, p, i: (b * nq + i, p))
    k_spec = pl.BlockSpec((seq, LANES), lambda b, p, i: (b, p))
    return pl.pallas_call(
        functools.partial(_diff_attn_kernel, tq=tq, n_blocks=nb, lam_init=lam_init),
        grid=(batch, n_pairs, nq),
        in_specs=[q_spec, q_spec, k_spec, k_spec,
                  pl.BlockSpec((seq, 2 * C_V_DIM), lambda b, p, i: (b, p)),
                  pl.BlockSpec((2, n_tiles, BLOCK, BLOCK), lambda b, p, i: (p, 0, 0, 0)),
                  pl.BlockSpec(lam_vecs.shape, lambda b, p, i: (0, 0)),
                  pl.BlockSpec((1, C_V_DIM), lambda b, p, i: (0, 0))],
        out_specs=pl.BlockSpec((tq, 2 * C_V_DIM), lambda b, p, i: (b * nq + i, p)),
        out_shape=jax.ShapeDtypeStruct(v.shape, BF16),
        compiler_params=_cparams(3),
        name="diff_attn",
    )(q1, q2, k1, k2, v, bias_full, lam_vecs, subln_g.reshape(1, C_V_DIM))


def _rope_tables(seq):
    rows = seq // GRID_W
    row = jnp.broadcast_to(jnp.arange(rows)[:, None], (rows, GRID_W)).reshape(-1)
    col = jnp.broadcast_to(jnp.arange(GRID_W)[None, :], (rows, GRID_W)).reshape(-1)
    half = HEAD_DIM // 2
    inv = 1.0 / (ROPE_THETA ** (jnp.arange(0, half, 2, dtype=F32) / half))
    ang_row = row.astype(F32)[:, None] * inv
    ang_col = col.astype(F32)[:, None] * inv
    cos = jnp.concatenate([jnp.cos(ang_row)] * 2 + [jnp.cos(ang_col)] * 2, axis=-1)
    sin = jnp.concatenate([-jnp.sin(ang_row), jnp.sin(ang_row),
                           -jnp.sin(ang_col), jnp.sin(ang_col)], axis=-1)
    return jnp.tile(cos, (1, 2)), jnp.tile(sin, (1, 2))


def _gqa_cols():
    return np.concatenate([np.arange(h * HEAD_DIM, (h + 1) * HEAD_DIM) for h in GQA_HEAD_ORDER])


def kernel(x, mem, rel_bias, mem_norm, final_norm, even_norm, even_w_in, even_sink, even_q_norm, even_k_norm, even_w_mem_kv, even_w_out, odd_norm, odd_w_in, odd_lambda_q1, odd_lambda_k1, odd_lambda_q2, odd_lambda_k2, odd_subln, odd_w_mem_kv, odd_w_out):
    batch, seq, d = x.shape
    mem_len = mem.shape[1]
    tokens = batch * seq
    nb = seq // BLOCK
    xw = X_HEADS * HEAD_DIM
    gw = A_HEADS * HEAD_DIM

    gq = _gqa_cols()
    aq0, ak0, av0 = 0, gw, gw + 128
    bq0 = av0 + 128
    bk0, bv0 = bq0 + gw, bq0 + gw + 128
    xq0 = bv0 + 128
    gate0 = xq0 + xw
    even_cols = np.concatenate([aq0 + gq, np.arange(ak0, bq0), bq0 + gq, np.arange(bk0, gate0),
                                gate0 + gq, gate0 + gw + gq,
                                np.arange(gate0 + 2 * gw, gate0 + D_MODEL)])
    mix_rows = np.concatenate([gq, gw + gq, np.arange(2 * gw, D_MODEL)])
    w_in0 = even_w_in[0][:, even_cols].astype(BF16)
    w_out0 = even_w_out[0][mix_rows, :].astype(BF16)
    w_in1 = odd_w_in[0].astype(BF16)
    w_out1 = odd_w_out[0].astype(BF16)

    x2 = x.reshape(tokens, d)
    mem2 = mem.reshape(batch * mem_len, d)

    bias_full, bias_win = _bias_tiles(rel_bias, nb)
    cos, sin_signed = _rope_tables(seq)
    sink_rows = jnp.repeat(even_sink[0] * LOG2E, BLOCK).reshape(A_HEADS * BLOCK, 1)

    mk0, mv0 = _norm_proj(mem2, mem_norm, even_w_mem_kv[0].astype(BF16), (xw, xw), 512)
    aq, ak, av, bq, bk, bv, xq, gate = _norm_proj(
        x2, even_norm[0], w_in0, (gw, 128, 128, gw, 128, 128, xw, D_MODEL), 512)
    y_a = _window_attn(aq, ak, av, bias_win, sink_rows, batch, seq)
    y_b = _dense_attn(bq, bk, bv, cos, sin_signed,
                      jnp.tile(even_q_norm[0], 2).reshape(1, LANES),
                      jnp.tile(even_k_norm[0], 2).reshape(1, LANES), batch, seq, 128)
    y_x = _cross_attn(xq, mk0, mv0, batch, seq, 512)
    h1 = _gate_out([y_a, y_b, y_x], gate, w_out0, x2, None, 512)

    lam_init = 0.8 - 0.6 * math.exp(-0.3 * 1)
    mk1, mv1 = _norm_proj(mem2, mem_norm, odd_w_mem_kv[0].astype(BF16), (xw, xw), 512)
    cw = C_HEADS * HEAD_DIM
    q1, q2, k1, k2, v, xq1, gate1 = _norm_proj(
        h1, odd_norm[0], w_in1, (cw, cw, cw, cw, C_HEADS * C_V_DIM, xw, D_MODEL), 512)
    lam_vecs = jnp.stack([odd_lambda_q1[0], odd_lambda_k1[0], odd_lambda_q2[0], odd_lambda_k2[0]])
    y_c = _diff_attn(q1, q2, k1, k2, v, bias_full, lam_vecs, odd_subln[0], lam_init,
                     batch, seq, 128)
    y_x1 = _cross_attn(xq1, mk1, mv1, batch, seq, 512)
    out = _gate_out([y_c, y_x1], gate1, w_out1, h1, final_norm, 512)
    return out.reshape(batch, seq, d)
```

```python
import functools
import math

import numpy as np
import jax
import jax.numpy as jnp
from jax import lax
from jax.experimental import pallas as pl
from jax.experimental.pallas import tpu as pltpu

D_MODEL = 1024
HEAD_DIM = 64
BLOCK = 128
WINDOW = 128
GRID_W = 64
A_HEADS = 6
B_HEADS = 6
C_HEADS = 6
C_V_DIM = 128
X_HEADS = 4
REL_BUCKETS = 32
REL_MAX_DIST = 128
ROPE_THETA = 10000.0
EPS = 1e-6
NEG_INF = -1e30
LOG2E = math.log2(math.e)
LANES = 128
VMEM_LIMIT = 56 * 1024 * 1024

F32 = jnp.float32
BF16 = jnp.bfloat16

GQA_HEAD_ORDER = (0, 3, 1, 4, 2, 5)


def _cparams(n_axes):
    return pltpu.CompilerParams(dimension_semantics=("arbitrary",) * n_axes,
                                vmem_limit_bytes=VMEM_LIMIT)


def _lane_lo(shape):
    return lax.broadcasted_iota(jnp.int32, shape, len(shape) - 1) < HEAD_DIM


def _rmsnorm_rows(x, g):
    ms = jnp.mean(x * x, axis=-1, keepdims=True)
    return x * lax.rsqrt(ms + EPS) * g


def _nt_dot(a, b):
    return lax.dot_general(a, b, (((1,), (1,)), ((), ())), preferred_element_type=F32)


def _norm_proj_kernel(x_ref, g_ref, w_ref, *out_refs):
    xn = _rmsnorm_rows(x_ref[...], g_ref[...]).astype(BF16)
    c0 = 0
    for o_ref in out_refs:
        width = o_ref.shape[1]
        o_ref[...] = jnp.dot(xn, w_ref[:, c0:c0 + width],
                             preferred_element_type=F32).astype(o_ref.dtype)
        c0 += width


def _norm_proj(x, g, w_bf16, splits, tm):
    rows, d = x.shape
    n = w_bf16.shape[1]
    assert sum(splits) == n and rows % tm == 0
    return pl.pallas_call(
        _norm_proj_kernel,
        grid=(rows // tm,),
        in_specs=[pl.BlockSpec((tm, d), lambda i: (i, 0)),
                  pl.BlockSpec((1, d), lambda i: (0, 0)),
                  pl.BlockSpec((d, n), lambda i: (0, 0))],
        out_specs=[pl.BlockSpec((tm, s), lambda i: (i, 0)) for s in splits],
        out_shape=[jax.ShapeDtypeStruct((rows, s), BF16) for s in splits],
        compiler_params=_cparams(1),
        name="norm_proj",
    )(x, g.reshape(1, d), w_bf16)


def _gate_out_kernel(*refs, n_parts, final):
    y_refs = refs[:n_parts]
    gate_ref, w_ref, res_ref = refs[n_parts:n_parts + 3]
    o_ref = refs[-1]
    acc = res_ref[...]
    c0 = 0
    for y_ref in y_refs:
        width = y_ref.shape[1]
        g = gate_ref[:, c0:c0 + width].astype(F32)
        silu = g * (1.0 / (1.0 + jnp.exp(-g)))
        yy = (y_ref[...].astype(F32) * silu).astype(BF16)
        acc = acc + jnp.dot(yy, w_ref[c0:c0 + width, :], preferred_element_type=F32)
        c0 += width
    if final:
        acc = _rmsnorm_rows(acc, refs[n_parts + 3][...])
    o_ref[...] = acc


def _gate_out(y_parts, gate, w_bf16, resid, final_g, tm):
    rows, d = resid.shape
    mix = w_bf16.shape[0]
    final = final_g is not None
    in_specs = [pl.BlockSpec((tm, y.shape[1]), lambda i: (i, 0)) for y in y_parts]
    in_specs += [pl.BlockSpec((tm, mix), lambda i: (i, 0)),
                 pl.BlockSpec((mix, d), lambda i: (0, 0)),
                 pl.BlockSpec((tm, d), lambda i: (i, 0))]
    args = list(y_parts) + [gate, w_bf16, resid]
    if final:
        in_specs.append(pl.BlockSpec((1, d), lambda i: (0, 0)))
        args.append(final_g.reshape(1, d))
    return pl.pallas_call(
        functools.partial(_gate_out_kernel, n_parts=len(y_parts), final=final),
        grid=(rows // tm,),
        in_specs=in_specs,
        out_specs=pl.BlockSpec((tm, d), lambda i: (i, 0)),
        out_shape=jax.ShapeDtypeStruct((rows, d), F32),
        compiler_params=_cparams(1),
        name="gate_out",
    )(*args)


def _bucket_thresholds():
    nb = REL_BUCKETS // 2
    max_exact = nb // 2
    n = np.arange(0, 4 * REL_MAX_DIST)
    nf = np.maximum(n, 1).astype(np.float32)
    large = max_exact + (np.log(nf / np.float32(max_exact))
                         / np.float32(math.log(REL_MAX_DIST / max_exact))
                         * np.float32(nb - max_exact)).astype(np.int32)
    bucket = np.where(n < max_exact, n, np.minimum(large, nb - 1))
    assert np.all(np.diff(bucket) >= 0) and bucket[0] == 0 and bucket[-1] == nb - 1
    thr = [int(np.argmax(bucket >= k)) for k in range(1, nb)]
    assert thr[-1] < REL_MAX_DIST
    return thr


def _bias_tiles_kernel(tab_ref, full_ref, win_ref, *, n_tiles, n_far):
    h = pl.program_id(0)
    thr = _bucket_thresholds()
    nb = REL_BUCKETS // 2
    row = lax.broadcasted_iota(jnp.int32, (BLOCK, BLOCK), 0)
    col = lax.broadcasted_iota(jnp.int32, (BLOCK, BLOCK), 1)
    for k in range(n_tiles):
        rel = (k - n_far) * BLOCK + col - row
        n = jnp.abs(rel)
        neg = jnp.full((BLOCK, BLOCK), tab_ref[0, h], F32)
        pos = jnp.full((BLOCK, BLOCK), tab_ref[nb, h], F32)
        for b in range(1, nb):
            ge = n >= thr[b - 1]
            neg = jnp.where(ge, tab_ref[b, h], neg)
            pos = jnp.where(ge, tab_ref[nb + b, h], pos)
        tile = jnp.where(rel > 0, pos, neg) * LOG2E
        full_ref[0, k] = tile
        j = k - (n_far - 1)
        if 0 <= j < 3:
            win_ref[:, j * BLOCK:(j + 1) * BLOCK] = jnp.where(n <= WINDOW, tile, NEG_INF)


def _bias_tiles(rel_bias, n_blocks):
    n_far = n_blocks - 1
    n_tiles = 2 * n_blocks - 1
    n_heads = rel_bias.shape[1]
    return pl.pallas_call(
        functools.partial(_bias_tiles_kernel, n_tiles=n_tiles, n_far=n_far),
        grid=(n_heads,),
        in_specs=[pl.BlockSpec(memory_space=pltpu.SMEM)],
        out_specs=[pl.BlockSpec((1, n_tiles, BLOCK, BLOCK), lambda h: (h, 0, 0, 0)),
                   pl.BlockSpec((BLOCK, 3 * BLOCK), lambda h: (h, 0))],
        out_shape=[jax.ShapeDtypeStruct((n_heads, n_tiles, BLOCK, BLOCK), F32),
                   jax.ShapeDtypeStruct((n_heads * BLOCK, 3 * BLOCK), F32)],
        compiler_params=_cparams(1),
        name="bias_tiles",
    )(rel_bias)


def _stack_gqa_heads(q_tiles):
    lo = _lane_lo(q_tiles[0].shape)
    return jnp.concatenate([jnp.where(lo, q, 0.0).astype(BF16) for q in q_tiles]
                           + [jnp.where(lo, 0.0, q).astype(BF16) for q in q_tiles], axis=0)


def _stack_pair_heads(q):
    lo = _lane_lo(q.shape)
    return jnp.concatenate([jnp.where(lo, q, 0.0).astype(BF16),
                            jnp.where(lo, 0.0, q).astype(BF16)], axis=0)


def _v_with_ones(v_pair):
    lane = lax.broadcasted_iota(jnp.int32, v_pair.shape, 1)
    vf = v_pair.astype(F32)
    v_lo = jnp.where(lane < HEAD_DIM, vf, jnp.where(lane == HEAD_DIM, 1.0, 0.0))
    v_hi = jnp.where(lane >= HEAD_DIM, vf, jnp.where(lane == 0, 1.0, 0.0))
    return v_lo.astype(BF16), v_hi.astype(BF16)


def _normalise_pair(o_lo, o_hi, extra_lo=None, extra_hi=None):
    lane = lax.broadcasted_iota(jnp.int32, o_lo.shape, 1)
    l_lo = jnp.sum(jnp.where(lane == HEAD_DIM, o_lo, 0.0), axis=-1, keepdims=True)
    l_hi = jnp.sum(jnp.where(lane == 0, o_hi, 0.0), axis=-1, keepdims=True)
    if extra_lo is not None:
        l_lo = l_lo + extra_lo
        l_hi = l_hi + extra_hi
    return jnp.where(lane < HEAD_DIM, o_lo * (1.0 / l_lo), o_hi * (1.0 / l_hi))


def _window_attn_kernel(q_ref, kp_ref, kc_ref, kn_ref, vp_ref, vc_ref, vn_ref,
                        bias_ref, sink_ref, o_ref, *, n_blocks):
    n = pl.program_id(1)
    qf = q_ref[...].astype(F32) * (HEAD_DIM ** -0.5 * LOG2E)
    lhs = _stack_gqa_heads([qf[:, j * LANES:(j + 1) * LANES] for j in range(3)])
    s = jnp.concatenate([_nt_dot(lhs, kp_ref[...]), _nt_dot(lhs, kc_ref[...]),
                         _nt_dot(lhs, kn_ref[...])], axis=1)
    col = lax.broadcasted_iota(jnp.int32, s.shape, 1)
    valid = jnp.logical_and(jnp.logical_or(col >= BLOCK, n > 0),
                            jnp.logical_or(col < 2 * BLOCK, n < n_blocks - 1))
    s = jnp.where(valid, s + bias_ref[...], NEG_INF)
    sink = sink_ref[...]
    m = jnp.maximum(jnp.max(s, axis=-1, keepdims=True), sink)
    p = jnp.exp2(s - m).astype(BF16)
    p_sink = jnp.exp2(sink - m)
    v_lo, v_hi = _v_with_ones(jnp.concatenate([vp_ref[...], vc_ref[...], vn_ref[...]], axis=0))
    half = 3 * BLOCK
    o_lo = jnp.dot(p[:half], v_lo, preferred_element_type=F32)
    o_hi = jnp.dot(p[half:], v_hi, preferred_element_type=F32)
    for j in range(3):
        rows = slice(j * BLOCK, (j + 1) * BLOCK)
        o_ref[:, j * LANES:(j + 1) * LANES] = _normalise_pair(
            o_lo[rows], o_hi[rows], p_sink[:half][rows], p_sink[half:][rows]).astype(o_ref.dtype)


def _window_attn(q, k, v, bias_win, sink_rows, batch, seq):
    nb = seq // BLOCK
    kv_spec = lambda f: pl.BlockSpec((BLOCK, LANES), f)
    prev = lambda b, n: (b * nb + jnp.maximum(n - 1, 0), 0)
    cur = lambda b, n: (b * nb + n, 0)
    nxt = lambda b, n: (b * nb + jnp.minimum(n + 1, nb - 1), 0)
    return pl.pallas_call(
        functools.partial(_window_attn_kernel, n_blocks=nb),
        grid=(batch, nb),
        in_specs=[pl.BlockSpec((BLOCK, 3 * LANES), cur),
                  kv_spec(prev), kv_spec(cur), kv_spec(nxt),
                  kv_spec(prev), kv_spec(cur), kv_spec(nxt),
                  pl.BlockSpec(bias_win.shape, lambda b, n: (0, 0)),
                  pl.BlockSpec(sink_rows.shape, lambda b, n: (0, 0))],
        out_specs=pl.BlockSpec((BLOCK, 3 * LANES), cur),
        out_shape=jax.ShapeDtypeStruct(q.shape, BF16),
        compiler_params=_cparams(2),
        name="window_attn",
    )(q, k, k, k, v, v, v, bias_win, sink_rows)


def _norm_rope_pair(x, g, cos, sin_signed):
    lo = _lane_lo(x.shape)
    x2 = x * x
    ss_lo = jnp.sum(jnp.where(lo, x2, 0.0), axis=-1, keepdims=True)
    ss_hi = jnp.sum(jnp.where(lo, 0.0, x2), axis=-1, keepdims=True)
    ms = jnp.where(lo, ss_lo, ss_hi) * (1.0 / HEAD_DIM)
    y = x * lax.rsqrt(ms + EPS) * g
    lane = lax.broadcasted_iota(jnp.int32, x.shape, 1)
    quarter = HEAD_DIM // 4
    first = (lane & quarter) == 0
    partner = jnp.where(first, pltpu.roll(y, LANES - quarter, 1), pltpu.roll(y, quarter, 1))
    return y * cos + partner * sin_signed


def _dense_attn_kernel(q_ref, k_ref, v_ref, cos_ref, sin_ref, gq_ref, gk_ref, o_ref,
                       k_scr, vlo_scr, vhi_scr, *, tq):
    i = pl.program_id(1)

    @pl.when(i == 0)
    def _():
        k_scr[...] = _norm_rope_pair(k_ref[...].astype(F32), gk_ref[...],
                                     cos_ref[...], sin_ref[...]).astype(BF16)
        v_lo, v_hi = _v_with_ones(v_ref[...])
        vlo_scr[...] = v_lo
        vhi_scr[...] = v_hi

    rows = pl.ds(pl.multiple_of(i * tq, tq), tq)
    cos = cos_ref[rows, :]
    sin = sin_ref[rows, :]
    q_tiles = []
    for j in range(3):
        qj = _norm_rope_pair(q_ref[:, j * LANES:(j + 1) * LANES].astype(F32),
                             gq_ref[...], cos, sin)
        q_tiles.append(qj * (HEAD_DIM ** -0.5 * LOG2E))
    lhs = _stack_gqa_heads(q_tiles)
    s = _nt_dot(lhs, k_scr[...])
    m = jnp.max(s, axis=-1, keepdims=True)
    p = jnp.exp2(s - m).astype(BF16)
    half = 3 * tq
    o_lo = jnp.dot(p[:half], vlo_scr[...], preferred_element_type=F32)
    o_hi = jnp.dot(p[half:], vhi_scr[...], preferred_element_type=F32)
    for j in range(3):
        r = slice(j * tq, (j + 1) * tq)
        o_ref[:, j * LANES:(j + 1) * LANES] = _normalise_pair(o_lo[r], o_hi[r]).astype(o_ref.dtype)


def _dense_attn(q, k, v, cos, sin_signed, gq, gk, batch, seq, tq):
    nq = seq // tq
    return pl.pallas_call(
        functools.partial(_dense_attn_kernel, tq=tq),
        grid=(batch, nq),
        in_specs=[pl.BlockSpec((tq, 3 * LANES), lambda b, i: (b * nq + i, 0)),
                  pl.BlockSpec((seq, LANES), lambda b, i: (b, 0)),
                  pl.BlockSpec((seq, LANES), lambda b, i: (b, 0)),
                  pl.BlockSpec((seq, LANES), lambda b, i: (0, 0)),
                  pl.BlockSpec((seq, LANES), lambda b, i: (0, 0)),
                  pl.BlockSpec((1, LANES), lambda b, i: (0, 0)),
                  pl.BlockSpec((1, LANES), lambda b, i: (0, 0))],
        out_specs=pl.BlockSpec((tq, 3 * LANES), lambda b, i: (b * nq + i, 0)),
        out_shape=jax.ShapeDtypeStruct(q.shape, BF16),
        scratch_shapes=[pltpu.VMEM((seq, LANES), BF16)] * 3,
        compiler_params=_cparams(2),
        name="dense_attn",
    )(q, k, v, cos, sin_signed, gq, gk)


def _cross_attn_kernel(q_ref, mk_ref, mv_ref, o_ref, *, tq):
    n_pairs = q_ref.shape[1] // LANES
    for j in range(n_pairs):
        cols = slice(j * LANES, (j + 1) * LANES)
        lhs = _stack_pair_heads(q_ref[:, cols].astype(F32) * (HEAD_DIM ** -0.5 * LOG2E))
        s = _nt_dot(lhs, mk_ref[:, cols])
        m = jnp.max(s, axis=-1, keepdims=True)
        p = jnp.exp2(s - m).astype(BF16)
        v_lo, v_hi = _v_with_ones(mv_ref[:, cols])
        o_lo = jnp.dot(p[:tq], v_lo, preferred_element_type=F32)
        o_hi = jnp.dot(p[tq:], v_hi, preferred_element_type=F32)
        o_ref[:, cols] = _normalise_pair(o_lo, o_hi).astype(o_ref.dtype)


def _cross_attn(q, mk, mv, batch, seq, tq):
    nq = seq // tq
    mem_len = mk.shape[0] // batch
    width = q.shape[1]
    return pl.pallas_call(
        functools.partial(_cross_attn_kernel, tq=tq),
        grid=(batch, nq),
        in_specs=[pl.BlockSpec((tq, width), lambda b, i: (b * nq + i, 0)),
                  pl.BlockSpec((mem_len, width), lambda b, i: (b, 0)),
                  pl.BlockSpec((mem_len, width), lambda b, i: (b, 0))],
        out_specs=pl.BlockSpec((tq, width), lambda b, i: (b * nq + i, 0)),
        out_shape=jax.ShapeDtypeStruct(q.shape, BF16),
        compiler_params=_cparams(2),
        name="cross_attn",
    )(q, mk, mv)


def _diff_attn_kernel(q1_ref, q2_ref, k1_ref, k2_ref, v_ref, bias_ref, lam_ref, g_ref,
                      o_ref, vaug_scr, s_scr, *, tq, n_blocks, lam_init):
    i = pl.program_id(2)
    sub = tq // BLOCK
    seq = k1_ref.shape[0]
    kc = 2 * BLOCK

    @pl.when(i == 0)
    def _():
        for head in range(2):
            vaug_scr[head, :, :C_V_DIM] = v_ref[:, head * C_V_DIM:(head + 1) * C_V_DIM]
            vaug_scr[head, :, C_V_DIM:] = jnp.ones((seq, C_V_DIM), BF16)

    lam_vec = lam_ref[...]
    lam = (jnp.exp(jnp.sum(lam_vec[0:1] * lam_vec[1:2], axis=-1, keepdims=True))
           - jnp.exp(jnp.sum(lam_vec[2:3] * lam_vec[3:4], axis=-1, keepdims=True)) + lam_init)

    n_chunks = seq // kc
    k_refs = (k1_ref, k2_ref)
    lhs = [_stack_pair_heads(q_ref[...].astype(F32) * (HEAD_DIM ** -0.5 * LOG2E))
           for q_ref in (q1_ref, q2_ref)]

    def score_chunk(t, c):
        bias = jnp.concatenate(
            [jnp.concatenate([bias_ref[head, (n_blocks - 1) - (i * sub + u) + 2 * c + e]
                              for e in range(2)], axis=1)
             for head in range(2) for u in range(sub)], axis=0)
        s = _nt_dot(lhs[t], k_refs[t][c * kc:(c + 1) * kc, :]) + bias
        s_scr[t, :, c * kc:(c + 1) * kc] = s
        return s

    def pv_chunk(t, m, c, acc):
        p = jnp.exp2(s_scr[t, :, c * kc:(c + 1) * kc] - m).astype(BF16)
        return [a + jnp.dot(p[head * tq:(head + 1) * tq], vaug_scr[head, c * kc:(c + 1) * kc, :],
                            preferred_element_type=F32)
                for head, a in enumerate(acc)]

    def row_max(chunks):
        mx = chunks[0]
        for s in chunks[1:]:
            mx = jnp.maximum(mx, s)
        return jnp.max(mx, axis=-1, keepdims=True)

    def normalised(acc):
        return [a[:, :C_V_DIM] * (1.0 / a[:, C_V_DIM:]) for a in acc]

    zero_acc = [jnp.zeros((tq, 2 * C_V_DIM), F32)] * 2
    m1 = row_max([score_chunk(0, c) for c in range(n_chunks)])
    acc1, s2_chunks = zero_acc, []
    for c in range(n_chunks):
        acc1 = pv_chunk(0, m1, c, acc1)
        s2_chunks.append(score_chunk(1, c))
    o1 = normalised(acc1)
    m2 = row_max(s2_chunks)
    acc2 = zero_acc
    for c in range(n_chunks):
        acc2 = pv_chunk(1, m2, c, acc2)
    o2 = normalised(acc2)

    g = g_ref[...] * (1.0 - lam_init)
    for head in range(2):
        out = o1[head] - lam * o2[head]
        o_ref[:, head * C_V_DIM:(head + 1) * C_V_DIM] = _rmsnorm_rows(out, g).astype(o_ref.dtype)


def _diff_attn(q1, q2, k1, k2, v, bias_full, lam_vecs, subln_g, lam_init, batch, seq, tq):
    nq = seq // tq
    nb = seq // BLOCK
    n_pairs = C_HEADS // 2
    n_tiles = bias_full.shape[1]
    q_spec = pl.BlockSpec((tq, LANES), lambda b, p, i: (b * nq + i, p))
    k_spec = pl.BlockSpec((seq, LANES), lambda b, p, i: (b, p))
    return pl.pallas_call(
        functools.partial(_diff_attn_kernel, tq=tq, n_blocks=nb, lam_init=lam_init),
        grid=(batch, n_pairs, nq),
        in_specs=[q_spec, q_spec, k_spec, k_spec,
                  pl.BlockSpec((seq, 2 * C_V_DIM), lambda b, p, i: (b, p)),
                  pl.BlockSpec((2, n_tiles, BLOCK, BLOCK), lambda b, p, i: (p, 0, 0, 0)),
                  pl.BlockSpec(lam_vecs.shape, lambda b, p, i: (0, 0)),
                  pl.BlockSpec((1, C_V_DIM), lambda b, p, i: (0, 0))],
        out_specs=pl.BlockSpec((tq, 2 * C_V_DIM), lambda b, p, i: (b * nq + i, p)),
        out_shape=jax.ShapeDtypeStruct(v.shape, BF16),
        scratch_shapes=[pltpu.VMEM((2, seq, 2 * C_V_DIM), BF16),
                        pltpu.VMEM((2, 2 * tq, seq), F32)],
        compiler_params=_cparams(3),
        name="diff_attn",
    )(q1, q2, k1, k2, v, bias_full, lam_vecs, subln_g.reshape(1, C_V_DIM))


def _rope_tables(seq):
    rows = seq // GRID_W
    row = jnp.broadcast_to(jnp.arange(rows)[:, None], (rows, GRID_W)).reshape(-1)
    col = jnp.broadcast_to(jnp.arange(GRID_W)[None, :], (rows, GRID_W)).reshape(-1)
    half = HEAD_DIM // 2
    inv = 1.0 / (ROPE_THETA ** (jnp.arange(0, half, 2, dtype=F32) / half))
    ang_row = row.astype(F32)[:, None] * inv
    ang_col = col.astype(F32)[:, None] * inv
    cos = jnp.concatenate([jnp.cos(ang_row)] * 2 + [jnp.cos(ang_col)] * 2, axis=-1)
    sin = jnp.concatenate([-jnp.sin(ang_row), jnp.sin(ang_row),
                           -jnp.sin(ang_col), jnp.sin(ang_col)], axis=-1)
    return jnp.tile(cos, (1, 2)), jnp.tile(sin, (1, 2))


def _gqa_cols():
    return np.concatenate([np.arange(h * HEAD_DIM, (h + 1) * HEAD_DIM) for h in GQA_HEAD_ORDER])


def kernel(x, mem, rel_bias, mem_norm, final_norm, even_norm, even_w_in, even_sink, even_q_norm, even_k_norm, even_w_mem_kv, even_w_out, odd_norm, odd_w_in, odd_lambda_q1, odd_lambda_k1, odd_lambda_q2, odd_lambda_k2, odd_subln, odd_w_mem_kv, odd_w_out):
    batch, seq, d = x.shape
    mem_len = mem.shape[1]
    tokens = batch * seq
    nb = seq // BLOCK
    xw = X_HEADS * HEAD_DIM
    gw = A_HEADS * HEAD_DIM

    gq = _gqa_cols()
    aq0, ak0, av0 = 0, gw, gw + 128
    bq0 = av0 + 128
    bk0, bv0 = bq0 + gw, bq0 + gw + 128
    xq0 = bv0 + 128
    gate0 = xq0 + xw
    even_cols = np.concatenate([aq0 + gq, np.arange(ak0, bq0), bq0 + gq, np.arange(bk0, gate0),
                                gate0 + gq, gate0 + gw + gq,
                                np.arange(gate0 + 2 * gw, gate0 + D_MODEL)])
    mix_rows = np.concatenate([gq, gw + gq, np.arange(2 * gw, D_MODEL)])
    w_in0 = even_w_in[0][:, even_cols].astype(BF16)
    w_out0 = even_w_out[0][mix_rows, :].astype(BF16)
    w_in1 = odd_w_in[0].astype(BF16)
    w_out1 = odd_w_out[0].astype(BF16)

    x2 = x.reshape(tokens, d)
    mem2 = mem.reshape(batch * mem_len, d)

    bias_full, bias_win = _bias_tiles(rel_bias, nb)
    cos, sin_signed = _rope_tables(seq)
    sink_rows = jnp.repeat(even_sink[0] * LOG2E, BLOCK).reshape(A_HEADS * BLOCK, 1)

    mk0, mv0 = _norm_proj(mem2, mem_norm, even_w_mem_kv[0].astype(BF16), (xw, xw), 512)
    aq, ak, av, bq, bk, bv, xq, gate = _norm_proj(
        x2, even_norm[0], w_in0, (gw, 128, 128, gw, 128, 128, xw, D_MODEL), 512)
    y_a = _window_attn(aq, ak, av, bias_win, sink_rows, batch, seq)
    y_b = _dense_attn(bq, bk, bv, cos, sin_signed,
                      jnp.tile(even_q_norm[0], 2).reshape(1, LANES),
                      jnp.tile(even_k_norm[0], 2).reshape(1, LANES), batch, seq, 128)
    y_x = _cross_attn(xq, mk0, mv0, batch, seq, 512)
    h1 = _gate_out([y_a, y_b, y_x], gate, w_out0, x2, None, 512)

    lam_init = 0.8 - 0.6 * math.exp(-0.3 * 1)
    mk1, mv1 = _norm_proj(mem2, mem_norm, odd_w_mem_kv[0].astype(BF16), (xw, xw), 512)
    cw = C_HEADS * HEAD_DIM
    q1, q2, k1, k2, v, xq1, gate1 = _norm_proj(
        h1, odd_norm[0], w_in1, (cw, cw, cw, cw, C_HEADS * C_V_DIM, xw, D_MODEL), 512)
    lam_vecs = jnp.stack([odd_lambda_q1[0], odd_lambda_k1[0], odd_lambda_q2[0], odd_lambda_k2[0]])
    y_c = _diff_attn(q1, q2, k1, k2, v, bias_full, lam_vecs, odd_subln[0], lam_init,
                     batch, seq, 256)
    y_x1 = _cross_attn(xq1, mk1, mv1, batch, seq, 512)
    out = _gate_out([y_c, y_x1], gate1, w_out1, h1, final_norm, 512)
    return out.reshape(batch, seq, d)
```

```python
import functools
import math

import numpy as np
import jax
import jax.numpy as jnp
from jax import lax
from jax.experimental import pallas as pl
from jax.experimental.pallas import tpu as pltpu

D_MODEL = 1024
HEAD_DIM = 64
BLOCK = 128
WINDOW = 128
GRID_W = 64
A_HEADS = 6
B_HEADS = 6
C_HEADS = 6
C_V_DIM = 128
X_HEADS = 4
REL_BUCKETS = 32
REL_MAX_DIST = 128
ROPE_THETA = 10000.0
EPS = 1e-6
NEG_INF = -1e30
LOG2E = math.log2(math.e)
LANES = 128
ONES_ROWS = 16
SUM_FLOOR = 2.0 ** -64
VMEM_LIMIT = 56 * 1024 * 1024

F32 = jnp.float32
BF16 = jnp.bfloat16

GQA_HEAD_ORDER = (0, 3, 1, 4, 2, 5)


def _cparams(n_axes, flags=None):
    return pltpu.CompilerParams(dimension_semantics=("arbitrary",) * n_axes,
                                vmem_limit_bytes=VMEM_LIMIT, flags=flags)


def _lane_lo(shape):
    return lax.broadcasted_iota(jnp.int32, shape, len(shape) - 1) < HEAD_DIM


def _rmsnorm_rows(x, g):
    ms = jnp.mean(x * x, axis=-1, keepdims=True)
    return x * lax.rsqrt(ms + EPS) * g


def _nt_dot(a, b):
    return lax.dot_general(a, b, (((1,), (1,)), ((), ())), preferred_element_type=F32)


def _norm_proj_kernel(x_ref, g_ref, w_ref, *out_refs):
    xn = _rmsnorm_rows(x_ref[...], g_ref[...]).astype(BF16)
    c0 = 0
    for o_ref in out_refs:
        width = o_ref.shape[1]
        o_ref[...] = jnp.dot(xn, w_ref[:, c0:c0 + width],
                             preferred_element_type=F32).astype(o_ref.dtype)
        c0 += width


def _norm_proj(x, g, w_bf16, splits, tm):
    rows, d = x.shape
    n = w_bf16.shape[1]
    assert sum(splits) == n and rows % tm == 0
    return pl.pallas_call(
        _norm_proj_kernel,
        grid=(rows // tm,),
        in_specs=[pl.BlockSpec((tm, d), lambda i: (i, 0)),
                  pl.BlockSpec((1, d), lambda i: (0, 0)),
                  pl.BlockSpec((d, n), lambda i: (0, 0))],
        out_specs=[pl.BlockSpec((tm, s), lambda i: (i, 0)) for s in splits],
        out_shape=[jax.ShapeDtypeStruct((rows, s), BF16) for s in splits],
        compiler_params=_cparams(1),
        name="norm_proj",
    )(x, g.reshape(1, d), w_bf16)


def _gate_out_kernel(*refs, n_parts, final):
    y_refs = refs[:n_parts]
    gate_ref, w_ref, res_ref = refs[n_parts:n_parts + 3]
    o_ref = refs[-1]
    acc = res_ref[...]
    c0 = 0
    for y_ref in y_refs:
        width = y_ref.shape[1]
        g = gate_ref[:, c0:c0 + width].astype(F32)
        silu = g * (1.0 / (1.0 + jnp.exp(-g)))
        yy = (y_ref[...].astype(F32) * silu).astype(BF16)
        acc = acc + jnp.dot(yy, w_ref[c0:c0 + width, :], preferred_element_type=F32)
        c0 += width
    if final:
        acc = _rmsnorm_rows(acc, refs[n_parts + 3][...])
    o_ref[...] = acc


def _gate_out(y_parts, gate, w_bf16, resid, final_g, tm):
    rows, d = resid.shape
    mix = w_bf16.shape[0]
    final = final_g is not None
    in_specs = [pl.BlockSpec((tm, y.shape[1]), lambda i: (i, 0)) for y in y_parts]
    in_specs += [pl.BlockSpec((tm, mix), lambda i: (i, 0)),
                 pl.BlockSpec((mix, d), lambda i: (0, 0)),
                 pl.BlockSpec((tm, d), lambda i: (i, 0))]
    args = list(y_parts) + [gate, w_bf16, resid]
    if final:
        in_specs.append(pl.BlockSpec((1, d), lambda i: (0, 0)))
        args.append(final_g.reshape(1, d))
    return pl.pallas_call(
        functools.partial(_gate_out_kernel, n_parts=len(y_parts), final=final),
        grid=(rows // tm,),
        in_specs=in_specs,
        out_specs=pl.BlockSpec((tm, d), lambda i: (i, 0)),
        out_shape=jax.ShapeDtypeStruct((rows, d), F32),
        compiler_params=_cparams(1),
        name="gate_out",
    )(*args)


def _bucket_thresholds():
    nb = REL_BUCKETS // 2
    max_exact = nb // 2
    n = np.arange(0, 4 * REL_MAX_DIST)
    nf = np.maximum(n, 1).astype(np.float32)
    large = max_exact + (np.log(nf / np.float32(max_exact))
                         / np.float32(math.log(REL_MAX_DIST / max_exact))
                         * np.float32(nb - max_exact)).astype(np.int32)
    bucket = np.where(n < max_exact, n, np.minimum(large, nb - 1))
    assert np.all(np.diff(bucket) >= 0) and bucket[0] == 0 and bucket[-1] == nb - 1
    thr = [int(np.argmax(bucket >= k)) for k in range(1, nb)]
    assert thr[-1] < REL_MAX_DIST
    return thr


def _bias_tiles_kernel(tab_ref, full_ref, win_ref, *, n_tiles, n_far):
    h = pl.program_id(0)
    thr = _bucket_thresholds()
    nb = REL_BUCKETS // 2
    row = lax.broadcasted_iota(jnp.int32, (BLOCK, BLOCK), 0)
    col = lax.broadcasted_iota(jnp.int32, (BLOCK, BLOCK), 1)
    def tile_of(rel):
        n = jnp.abs(rel)
        neg = jnp.full((BLOCK, BLOCK), tab_ref[0, h], F32)
        pos = jnp.full((BLOCK, BLOCK), tab_ref[nb, h], F32)
        for b in range(1, nb):
            ge = n >= thr[b - 1]
            neg = jnp.where(ge, tab_ref[b, h], neg)
            pos = jnp.where(ge, tab_ref[nb + b, h], pos)
        return jnp.where(rel > 0, pos, neg) * LOG2E

    for k in range(n_tiles):
        full_ref[0, k] = tile_of((k - n_far) * BLOCK + row - col)
        j = k - (n_far - 1)
        if 0 <= j < 3:
            rel = (k - n_far) * BLOCK + col - row
            win_ref[:, j * BLOCK:(j + 1) * BLOCK] = jnp.where(jnp.abs(rel) <= WINDOW,
                                                              tile_of(rel), NEG_INF)


def _bias_tiles(rel_bias, n_blocks):
    n_far = n_blocks - 1
    n_tiles = 2 * n_blocks - 1
    n_heads = rel_bias.shape[1]
    return pl.pallas_call(
        functools.partial(_bias_tiles_kernel, n_tiles=n_tiles, n_far=n_far),
        grid=(n_heads,),
        in_specs=[pl.BlockSpec(memory_space=pltpu.SMEM)],
        out_specs=[pl.BlockSpec((1, n_tiles, BLOCK, BLOCK), lambda h: (h, 0, 0, 0)),
                   pl.BlockSpec((BLOCK, 3 * BLOCK), lambda h: (h, 0))],
        out_shape=[jax.ShapeDtypeStruct((n_heads, n_tiles, BLOCK, BLOCK), F32),
                   jax.ShapeDtypeStruct((n_heads * BLOCK, 3 * BLOCK), F32)],
        compiler_params=_cparams(1),
        name="bias_tiles",
    )(rel_bias)


def _stack_gqa_heads(q_tiles):
    lo = _lane_lo(q_tiles[0].shape)
    return jnp.concatenate([jnp.where(lo, q, 0.0).astype(BF16) for q in q_tiles]
                           + [jnp.where(lo, 0.0, q).astype(BF16) for q in q_tiles], axis=0)


def _stack_pair_heads(q):
    lo = _lane_lo(q.shape)
    return jnp.concatenate([jnp.where(lo, q, 0.0).astype(BF16),
                            jnp.where(lo, 0.0, q).astype(BF16)], axis=0)


def _v_with_ones(v_pair):
    lane = lax.broadcasted_iota(jnp.int32, v_pair.shape, 1)
    vf = v_pair.astype(F32)
    v_lo = jnp.where(lane < HEAD_DIM, vf, jnp.where(lane == HEAD_DIM, 1.0, 0.0))
    v_hi = jnp.where(lane >= HEAD_DIM, vf, jnp.where(lane == 0, 1.0, 0.0))
    return v_lo.astype(BF16), v_hi.astype(BF16)


def _normalise_pair(o_lo, o_hi, extra_lo=None, extra_hi=None):
    lane = lax.broadcasted_iota(jnp.int32, o_lo.shape, 1)
    l_lo = jnp.sum(jnp.where(lane == HEAD_DIM, o_lo, 0.0), axis=-1, keepdims=True)
    l_hi = jnp.sum(jnp.where(lane == 0, o_hi, 0.0), axis=-1, keepdims=True)
    if extra_lo is not None:
        l_lo = l_lo + extra_lo
        l_hi = l_hi + extra_hi
    return jnp.where(lane < HEAD_DIM, o_lo * (1.0 / l_lo), o_hi * (1.0 / l_hi))


def _window_attn_kernel(q_ref, kp_ref, kc_ref, kn_ref, vp_ref, vc_ref, vn_ref,
                        bias_ref, sink_ref, o_ref, *, n_blocks):
    n = pl.program_id(1)
    qf = q_ref[...].astype(F32) * (HEAD_DIM ** -0.5 * LOG2E)
    lhs = _stack_gqa_heads([qf[:, j * LANES:(j + 1) * LANES] for j in range(3)])
    s = jnp.concatenate([_nt_dot(lhs, kp_ref[...]), _nt_dot(lhs, kc_ref[...]),
                         _nt_dot(lhs, kn_ref[...])], axis=1)
    col = lax.broadcasted_iota(jnp.int32, s.shape, 1)
    valid = jnp.logical_and(jnp.logical_or(col >= BLOCK, n > 0),
                            jnp.logical_or(col < 2 * BLOCK, n < n_blocks - 1))
    s = jnp.where(valid, s + bias_ref[...], NEG_INF)
    sink = sink_ref[...]
    m = jnp.maximum(jnp.max(s, axis=-1, keepdims=True), sink)
    p = jnp.exp2(s - m).astype(BF16)
    p_sink = jnp.exp2(sink - m)
    v_lo, v_hi = _v_with_ones(jnp.concatenate([vp_ref[...], vc_ref[...], vn_ref[...]], axis=0))
    half = 3 * BLOCK
    o_lo = jnp.dot(p[:half], v_lo, preferred_element_type=F32)
    o_hi = jnp.dot(p[half:], v_hi, preferred_element_type=F32)
    for j in range(3):
        rows = slice(j * BLOCK, (j + 1) * BLOCK)
        o_ref[:, j * LANES:(j + 1) * LANES] = _normalise_pair(
            o_lo[rows], o_hi[rows], p_sink[:half][rows], p_sink[half:][rows]).astype(o_ref.dtype)


def _window_attn(q, k, v, bias_win, sink_rows, batch, seq):
    nb = seq // BLOCK
    kv_spec = lambda f: pl.BlockSpec((BLOCK, LANES), f)
    prev = lambda b, n: (b * nb + jnp.maximum(n - 1, 0), 0)
    cur = lambda b, n: (b * nb + n, 0)
    nxt = lambda b, n: (b * nb + jnp.minimum(n + 1, nb - 1), 0)
    return pl.pallas_call(
        functools.partial(_window_attn_kernel, n_blocks=nb),
        grid=(batch, nb),
        in_specs=[pl.BlockSpec((BLOCK, 3 * LANES), cur),
                  kv_spec(prev), kv_spec(cur), kv_spec(nxt),
                  kv_spec(prev), kv_spec(cur), kv_spec(nxt),
                  pl.BlockSpec(bias_win.shape, lambda b, n: (0, 0)),
                  pl.BlockSpec(sink_rows.shape, lambda b, n: (0, 0))],
        out_specs=pl.BlockSpec((BLOCK, 3 * LANES), cur),
        out_shape=jax.ShapeDtypeStruct(q.shape, BF16),
        compiler_params=_cparams(2),
        name="window_attn",
    )(q, k, k, k, v, v, v, bias_win, sink_rows)


def _norm_rope_pair(x, g, cos, sin_signed):
    lo = _lane_lo(x.shape)
    x2 = x * x
    ss_lo = jnp.sum(jnp.where(lo, x2, 0.0), axis=-1, keepdims=True)
    ss_hi = jnp.sum(jnp.where(lo, 0.0, x2), axis=-1, keepdims=True)
    ms = jnp.where(lo, ss_lo, ss_hi) * (1.0 / HEAD_DIM)
    y = x * lax.rsqrt(ms + EPS) * g
    lane = lax.broadcasted_iota(jnp.int32, x.shape, 1)
    quarter = HEAD_DIM // 4
    first = (lane & quarter) == 0
    partner = jnp.where(first, pltpu.roll(y, LANES - quarter, 1), pltpu.roll(y, quarter, 1))
    return y * cos + partner * sin_signed


def _dense_attn_kernel(q_ref, k_ref, v_ref, cos_ref, sin_ref, gq_ref, gk_ref, o_ref,
                       k_scr, vlo_scr, vhi_scr, *, tq):
    i = pl.program_id(1)

    @pl.when(i == 0)
    def _():
        k_scr[...] = _norm_rope_pair(k_ref[...].astype(F32), gk_ref[...],
                                     cos_ref[...], sin_ref[...]).astype(BF16)
        v_lo, v_hi = _v_with_ones(v_ref[...])
        vlo_scr[...] = v_lo
        vhi_scr[...] = v_hi

    rows = pl.ds(pl.multiple_of(i * tq, tq), tq)
    cos = cos_ref[rows, :]
    sin = sin_ref[rows, :]
    q_tiles = []
    for j in range(3):
        qj = _norm_rope_pair(q_ref[:, j * LANES:(j + 1) * LANES].astype(F32),
                             gq_ref[...], cos, sin)
        q_tiles.append(qj * (HEAD_DIM ** -0.5 * LOG2E))
    lhs = _stack_gqa_heads(q_tiles)
    s = _nt_dot(lhs, k_scr[...])
    m = jnp.max(s, axis=-1, keepdims=True)
    p = jnp.exp2(s - m).astype(BF16)
    half = 3 * tq
    o_lo = jnp.dot(p[:half], vlo_scr[...], preferred_element_type=F32)
    o_hi = jnp.dot(p[half:], vhi_scr[...], preferred_element_type=F32)
    for j in range(3):
        r = slice(j * tq, (j + 1) * tq)
        o_ref[:, j * LANES:(j + 1) * LANES] = _normalise_pair(o_lo[r], o_hi[r]).astype(o_ref.dtype)


def _dense_attn(q, k, v, cos, sin_signed, gq, gk, batch, seq, tq):
    nq = seq // tq
    return pl.pallas_call(
        functools.partial(_dense_attn_kernel, tq=tq),
        grid=(batch, nq),
        in_specs=[pl.BlockSpec((tq, 3 * LANES), lambda b, i: (b * nq + i, 0)),
                  pl.BlockSpec((seq, LANES), lambda b, i: (b, 0)),
                  pl.BlockSpec((seq, LANES), lambda b, i: (b, 0)),
                  pl.BlockSpec((seq, LANES), lambda b, i: (0, 0)),
                  pl.BlockSpec((seq, LANES), lambda b, i: (0, 0)),
                  pl.BlockSpec((1, LANES), lambda b, i: (0, 0)),
                  pl.BlockSpec((1, LANES), lambda b, i: (0, 0))],
        out_specs=pl.BlockSpec((tq, 3 * LANES), lambda b, i: (b * nq + i, 0)),
        out_shape=jax.ShapeDtypeStruct(q.shape, BF16),
        scratch_shapes=[pltpu.VMEM((seq, LANES), BF16)] * 3,
        compiler_params=_cparams(2),
        name="dense_attn",
    )(q, k, v, cos, sin_signed, gq, gk)


def _cross_attn_kernel(q_ref, mk_ref, mv_ref, o_ref, *, tq):
    n_pairs = q_ref.shape[1] // LANES
    for j in range(n_pairs):
        cols = slice(j * LANES, (j + 1) * LANES)
        lhs = _stack_pair_heads(q_ref[:, cols].astype(F32) * (HEAD_DIM ** -0.5 * LOG2E))
        s = _nt_dot(lhs, mk_ref[:, cols])
        m = jnp.max(s, axis=-1, keepdims=True)
        p = jnp.exp2(s - m).astype(BF16)
        v_lo, v_hi = _v_with_ones(mv_ref[:, cols])
        o_lo = jnp.dot(p[:tq], v_lo, preferred_element_type=F32)
        o_hi = jnp.dot(p[tq:], v_hi, preferred_element_type=F32)
        o_ref[:, cols] = _normalise_pair(o_lo, o_hi).astype(o_ref.dtype)


def _cross_attn(q, mk, mv, batch, seq, tq):
    nq = seq // tq
    mem_len = mk.shape[0] // batch
    width = q.shape[1]
    return pl.pallas_call(
        functools.partial(_cross_attn_kernel, tq=tq),
        grid=(batch, nq),
        in_specs=[pl.BlockSpec((tq, width), lambda b, i: (b * nq + i, 0)),
                  pl.BlockSpec((mem_len, width), lambda b, i: (b, 0)),
                  pl.BlockSpec((mem_len, width), lambda b, i: (b, 0))],
        out_specs=pl.BlockSpec((tq, width), lambda b, i: (b * nq + i, 0)),
        out_shape=jax.ShapeDtypeStruct(q.shape, BF16),
        compiler_params=_cparams(2),
        name="cross_attn",
    )(q, mk, mv)


def _shift_column(lhs, kmax_lo, kmax_hi, bias_max_lo, bias_max_hi):
    half = lhs.shape[0] // 2
    lf = lhs.astype(F32)
    q_norm = jnp.sqrt(jnp.sum(lf * lf, axis=-1, keepdims=True))
    row = lax.broadcasted_iota(jnp.int32, lhs.shape, 0)
    lane = lax.broadcasted_iota(jnp.int32, lhs.shape, 1)
    shift = (q_norm * jnp.where(row < half, kmax_lo, kmax_hi)
             + jnp.where(row < half, bias_max_lo, bias_max_hi))
    shift = shift + jnp.abs(shift) * 2.0 ** -7
    return jnp.where(lane == 0, -shift, 0.0).astype(BF16)


def _pair_key_norm_max(k_pair):
    kf = k_pair.astype(F32)
    k2 = kf * kf
    lo = _lane_lo(k2.shape)
    n_lo = jnp.max(jnp.sum(jnp.where(lo, k2, 0.0), axis=-1, keepdims=True), axis=0, keepdims=True)
    n_hi = jnp.max(jnp.sum(jnp.where(lo, 0.0, k2), axis=-1, keepdims=True), axis=0, keepdims=True)
    return (jnp.broadcast_to(jnp.sqrt(n_lo), (1, LANES)),
            jnp.broadcast_to(jnp.sqrt(n_hi), (1, LANES)))


def _diff_attn_kernel(q1_ref, q2_ref, k1_ref, k2_ref, v_ref, bias_ref, bmax_ref, lam_ref, g_ref,
                      o_ref, vt_scr, ka_scr, kmax_scr, *, tq, n_blocks, lam_init):
    i = pl.program_id(2)
    sub = tq // BLOCK
    seq = k1_ref.shape[0]
    kc = 2 * BLOCK
    n_chunks = seq // kc
    k_refs = (k1_ref, k2_ref)

    @pl.when(i == 0)
    def _():
        for head in range(2):
            v_t = v_ref[:, head * C_V_DIM:(head + 1) * C_V_DIM].astype(F32).T
            vt_scr[head, :C_V_DIM, :] = v_t.astype(BF16)
            vt_scr[head, C_V_DIM:, :] = jnp.ones((ONES_ROWS, seq), BF16)
        lane = lax.broadcasted_iota(jnp.int32, (seq, LANES), 1)
        ones_col = jnp.where(lane == 0, 1.0, 0.0).astype(BF16)
        for t in range(2):
            ka_scr[t, :, :LANES] = k_refs[t][...]
            ka_scr[t, :, LANES:] = ones_col
            kmax_lo, kmax_hi = _pair_key_norm_max(k_refs[t][...])
            kmax_scr[t, 0:1, :] = kmax_lo
            kmax_scr[t, 1:2, :] = kmax_hi

    lam_vec = lam_ref[...]
    lam = (jnp.exp(jnp.sum(lam_vec[0:1] * lam_vec[1:2], axis=-1, keepdims=True))
           - jnp.exp(jnp.sum(lam_vec[2:3] * lam_vec[3:4], axis=-1, keepdims=True)) + lam_init)
    lhs = [_stack_pair_heads(q_ref[...].astype(F32) * (HEAD_DIM ** -0.5 * LOG2E))
           for q_ref in (q1_ref, q2_ref)]

    def bias_chunk(c):
        return jnp.concatenate(
            [jnp.concatenate([bias_ref[head, (n_blocks - 1) - (i * sub + u) + 2 * c + e]
                              for head in range(2) for u in range(sub)], axis=1)
             for e in range(2)], axis=0)

    def shifted_scores(t):
        aug = _shift_column(lhs[t], kmax_scr[t, 0:1, :], kmax_scr[t, 1:2, :],
                            bmax_ref[0], bmax_ref[1])
        return _nt_dot(ka_scr[t], jnp.concatenate([lhs[t], aug], axis=1))

    def pv_of_shifted(s_all):
        p = jnp.concatenate([jnp.exp2(s_all[c * kc:(c + 1) * kc, :] + bias_chunk(c)).astype(BF16)
                             for c in range(n_chunks)], axis=0)
        return [jnp.dot(vt_scr[head], p[:, head * tq:(head + 1) * tq],
                        preferred_element_type=F32) for head in range(2)]

    def pv_running_max(s_all):
        m = None
        acc = [None, None]
        for c in range(n_chunks):
            s = s_all[c * kc:(c + 1) * kc, :] + bias_chunk(c)
            m_c = jnp.max(s, axis=0, keepdims=True)
            m_new = m_c if m is None else jnp.maximum(m, m_c)
            p = jnp.exp2(s - m_new).astype(BF16)
            if m is not None:
                alpha = jnp.exp2(m - m_new)
            for head in range(2):
                cols = slice(head * tq, (head + 1) * tq)
                pv = jnp.dot(vt_scr[head, :, c * kc:(c + 1) * kc], p[:, cols],
                             preferred_element_type=F32)
                acc[head] = pv if m is None else acc[head] * alpha[:, cols] + pv
            m = m_new
        return acc

    def write(acc1, acc2):
        g = g_ref[...] * (1.0 - lam_init)
        for head in range(2):
            o1 = acc1[head][:C_V_DIM] * (1.0 / acc1[head][C_V_DIM:C_V_DIM + 1])
            o2 = acc2[head][:C_V_DIM] * (1.0 / acc2[head][C_V_DIM:C_V_DIM + 1])
            out = o1 - lam * o2
            ms = jnp.mean(out * out, axis=0, keepdims=True)
            y = out * lax.rsqrt(ms + EPS) * g
            o_ref[:, head * C_V_DIM:(head + 1) * C_V_DIM] = y.T.astype(o_ref.dtype)

    s1 = shifted_scores(0)
    s2 = shifted_scores(1)
    acc1 = pv_of_shifted(s1)
    acc2 = pv_of_shifted(s2)
    write(acc1, acc2)
    sums = jnp.concatenate([a[C_V_DIM:C_V_DIM + 1] for a in acc1 + acc2], axis=0)

    @pl.when(jnp.logical_not(jnp.min(sums) >= SUM_FLOOR))
    def _():
        write(pv_running_max(_nt_dot(k1_ref[...], lhs[0])),
              pv_running_max(_nt_dot(k2_ref[...], lhs[1])))


def _diff_attn(q1, q2, k1, k2, v, bias_t, bias_max, lam_vecs, subln_g, lam_init, batch, seq, tq):
    nq = seq // tq
    nb = seq // BLOCK
    n_pairs = C_HEADS // 2
    n_tiles = bias_t.shape[1]
    q_spec = pl.BlockSpec((tq, LANES), lambda b, p, i: (b * nq + i, p))
    k_spec = pl.BlockSpec((seq, LANES), lambda b, p, i: (b, p))
    g_cols = jnp.broadcast_to(subln_g.reshape(C_V_DIM, 1), (C_V_DIM, tq))
    return pl.pallas_call(
        functools.partial(_diff_attn_kernel, tq=tq, n_blocks=nb, lam_init=lam_init),
        grid=(batch, n_pairs, nq),
        in_specs=[q_spec, q_spec, k_spec, k_spec,
                  pl.BlockSpec((seq, 2 * C_V_DIM), lambda b, p, i: (b, p)),
                  pl.BlockSpec((2, n_tiles, BLOCK, BLOCK), lambda b, p, i: (p, 0, 0, 0)),
                  pl.BlockSpec((2, 1, LANES), lambda b, p, i: (p, 0, 0)),
                  pl.BlockSpec(lam_vecs.shape, lambda b, p, i: (0, 0)),
                  pl.BlockSpec((C_V_DIM, tq), lambda b, p, i: (0, 0))],
        out_specs=pl.BlockSpec((tq, 2 * C_V_DIM), lambda b, p, i: (b * nq + i, p)),
        out_shape=jax.ShapeDtypeStruct(v.shape, BF16),
        scratch_shapes=[pltpu.VMEM((2, C_V_DIM + ONES_ROWS, seq), BF16),
                        pltpu.VMEM((2, seq, 2 * LANES), BF16),
                        pltpu.VMEM((2, 2, LANES), F32)],
        compiler_params=_cparams(3),
        name="diff_attn",
    )(q1, q2, k1, k2, v, bias_t, bias_max, lam_vecs, g_cols)


def _rope_tables(seq):
    rows = seq // GRID_W
    row = jnp.broadcast_to(jnp.arange(rows)[:, None], (rows, GRID_W)).reshape(-1)
    col = jnp.broadcast_to(jnp.arange(GRID_W)[None, :], (rows, GRID_W)).reshape(-1)
    half = HEAD_DIM // 2
    inv = 1.0 / (ROPE_THETA ** (jnp.arange(0, half, 2, dtype=F32) / half))
    ang_row = row.astype(F32)[:, None] * inv
    ang_col = col.astype(F32)[:, None] * inv
    cos = jnp.concatenate([jnp.cos(ang_row)] * 2 + [jnp.cos(ang_col)] * 2, axis=-1)
    sin = jnp.concatenate([-jnp.sin(ang_row), jnp.sin(ang_row),
                           -jnp.sin(ang_col), jnp.sin(ang_col)], axis=-1)
    return jnp.tile(cos, (1, 2)), jnp.tile(sin, (1, 2))


def _gqa_cols():
    return np.concatenate([np.arange(h * HEAD_DIM, (h + 1) * HEAD_DIM) for h in GQA_HEAD_ORDER])


def kernel(x, mem, rel_bias, mem_norm, final_norm, even_norm, even_w_in, even_sink, even_q_norm, even_k_norm, even_w_mem_kv, even_w_out, odd_norm, odd_w_in, odd_lambda_q1, odd_lambda_k1, odd_lambda_q2, odd_lambda_k2, odd_subln, odd_w_mem_kv, odd_w_out):
    batch, seq, d = x.shape
    mem_len = mem.shape[1]
    tokens = batch * seq
    nb = seq // BLOCK
    xw = X_HEADS * HEAD_DIM
    gw = A_HEADS * HEAD_DIM

    gq = _gqa_cols()
    aq0, ak0, av0 = 0, gw, gw + 128
    bq0 = av0 + 128
    bk0, bv0 = bq0 + gw, bq0 + gw + 128
    xq0 = bv0 + 128
    gate0 = xq0 + xw
    even_cols = np.concatenate([aq0 + gq, np.arange(ak0, bq0), bq0 + gq, np.arange(bk0, gate0),
                                gate0 + gq, gate0 + gw + gq,
                                np.arange(gate0 + 2 * gw, gate0 + D_MODEL)])
    mix_rows = np.concatenate([gq, gw + gq, np.arange(2 * gw, D_MODEL)])
    w_in0 = even_w_in[0][:, even_cols].astype(BF16)
    w_out0 = even_w_out[0][mix_rows, :].astype(BF16)
    w_in1 = odd_w_in[0].astype(BF16)
    w_out1 = odd_w_out[0].astype(BF16)

    x2 = x.reshape(tokens, d)
    mem2 = mem.reshape(batch * mem_len, d)

    bias_full, bias_win = _bias_tiles(rel_bias, nb)
    cos, sin_signed = _rope_tables(seq)
    sink_rows = jnp.repeat(even_sink[0] * LOG2E, BLOCK).reshape(A_HEADS * BLOCK, 1)

    mk0, mv0 = _norm_proj(mem2, mem_norm, even_w_mem_kv[0].astype(BF16), (xw, xw), 512)
    aq, ak, av, bq, bk, bv, xq, gate = _norm_proj(
        x2, even_norm[0], w_in0, (gw, 128, 128, gw, 128, 128, xw, D_MODEL), 512)
    y_a = _window_attn(aq, ak, av, bias_win, sink_rows, batch, seq)
    y_b = _dense_attn(bq, bk, bv, cos, sin_signed,
                      jnp.tile(even_q_norm[0], 2).reshape(1, LANES),
                      jnp.tile(even_k_norm[0], 2).reshape(1, LANES), batch, seq, 128)
    y_x = _cross_attn(xq, mk0, mv0, batch, seq, 512)
    h1 = _gate_out([y_a, y_b, y_x], gate, w_out0, x2, None, 512)

    lam_init = 0.8 - 0.6 * math.exp(-0.3 * 1)
    mk1, mv1 = _norm_proj(mem2, mem_norm, odd_w_mem_kv[0].astype(BF16), (xw, xw), 512)
    cw = C_HEADS * HEAD_DIM
    q1, q2, k1, k2, v, xq1, gate1 = _norm_proj(
        h1, odd_norm[0], w_in1, (cw, cw, cw, cw, C_HEADS * C_V_DIM, xw, D_MODEL), 512)
    lam_vecs = jnp.stack([odd_lambda_q1[0], odd_lambda_k1[0], odd_lambda_q2[0], odd_lambda_k2[0]])
    bias_max = jnp.broadcast_to((jnp.max(rel_bias, axis=0) * LOG2E).reshape(C_HEADS, 1, 1),
                                (C_HEADS, 1, LANES))
    y_c = _diff_attn(q1, q2, k1, k2, v, bias_full, bias_max, lam_vecs, odd_subln[0], lam_init,
                     batch, seq, 256)
    y_x1 = _cross_attn(xq1, mk1, mv1, batch, seq, 512)
    out = _gate_out([y_c, y_x1], gate1, w_out1, h1, final_norm, 512)
    return out.reshape(batch, seq, d)
```

```python
import functools
import math

import numpy as np
import jax
import jax.numpy as jnp
from jax import lax
from jax.experimental import pallas as pl
from jax.experimental.pallas import tpu as pltpu

D_MODEL = 1024
HEAD_DIM = 64
BLOCK = 128
WINDOW = 128
GRID_W = 64
A_HEADS = 6
B_HEADS = 6
C_HEADS = 6
C_V_DIM = 128
X_HEADS = 4
REL_BUCKETS = 32
REL_MAX_DIST = 128
ROPE_THETA = 10000.0
EPS = 1e-6
NEG_INF = -1e30
LOG2E = math.log2(math.e)
LANES = 128
ONES_ROWS = 16
SUM_FLOOR = 2.0 ** -64
VMEM_LIMIT = 56 * 1024 * 1024

F32 = jnp.float32
BF16 = jnp.bfloat16

GQA_HEAD_ORDER = (0, 3, 1, 4, 2, 5)


def _cparams(n_axes, flags=None):
    return pltpu.CompilerParams(dimension_semantics=("arbitrary",) * n_axes,
                                vmem_limit_bytes=VMEM_LIMIT, flags=flags)


def _lane_lo(shape):
    return lax.broadcasted_iota(jnp.int32, shape, len(shape) - 1) < HEAD_DIM


def _rmsnorm_rows(x, g):
    ms = jnp.mean(x * x, axis=-1, keepdims=True)
    return x * lax.rsqrt(ms + EPS) * g


def _nt_dot(a, b):
    return lax.dot_general(a, b, (((1,), (1,)), ((), ())), preferred_element_type=F32)


def _norm_proj_kernel(x_ref, g_ref, w_ref, *out_refs):
    xn = _rmsnorm_rows(x_ref[...], g_ref[...]).astype(BF16)
    c0 = 0
    for o_ref in out_refs:
        width = o_ref.shape[1]
        o_ref[...] = jnp.dot(xn, w_ref[:, c0:c0 + width],
                             preferred_element_type=F32).astype(o_ref.dtype)
        c0 += width


def _norm_proj(x, g, w_bf16, splits, tm):
    rows, d = x.shape
    n = w_bf16.shape[1]
    assert sum(splits) == n and rows % tm == 0
    return pl.pallas_call(
        _norm_proj_kernel,
        grid=(rows // tm,),
        in_specs=[pl.BlockSpec((tm, d), lambda i: (i, 0)),
                  pl.BlockSpec((1, d), lambda i: (0, 0)),
                  pl.BlockSpec((d, n), lambda i: (0, 0))],
        out_specs=[pl.BlockSpec((tm, s), lambda i: (i, 0)) for s in splits],
        out_shape=[jax.ShapeDtypeStruct((rows, s), BF16) for s in splits],
        compiler_params=_cparams(1),
        name="norm_proj",
    )(x, g.reshape(1, d), w_bf16)


def _gate_out_kernel(*refs, n_parts, final):
    y_refs = refs[:n_parts]
    gate_ref, w_ref, res_ref = refs[n_parts:n_parts + 3]
    o_ref = refs[-1]
    acc = res_ref[...]
    c0 = 0
    for y_ref in y_refs:
        width = y_ref.shape[1]
        g = gate_ref[:, c0:c0 + width].astype(F32)
        silu = g * (1.0 / (1.0 + jnp.exp(-g)))
        yy = (y_ref[...].astype(F32) * silu).astype(BF16)
        acc = acc + jnp.dot(yy, w_ref[c0:c0 + width, :], preferred_element_type=F32)
        c0 += width
    if final:
        acc = _rmsnorm_rows(acc, refs[n_parts + 3][...])
    o_ref[...] = acc


def _gate_out(y_parts, gate, w_bf16, resid, final_g, tm):
    rows, d = resid.shape
    mix = w_bf16.shape[0]
    final = final_g is not None
    in_specs = [pl.BlockSpec((tm, y.shape[1]), lambda i: (i, 0)) for y in y_parts]
    in_specs += [pl.BlockSpec((tm, mix), lambda i: (i, 0)),
                 pl.BlockSpec((mix, d), lambda i: (0, 0)),
                 pl.BlockSpec((tm, d), lambda i: (i, 0))]
    args = list(y_parts) + [gate, w_bf16, resid]
    if final:
        in_specs.append(pl.BlockSpec((1, d), lambda i: (0, 0)))
        args.append(final_g.reshape(1, d))
    return pl.pallas_call(
        functools.partial(_gate_out_kernel, n_parts=len(y_parts), final=final),
        grid=(rows // tm,),
        in_specs=in_specs,
        out_specs=pl.BlockSpec((tm, d), lambda i: (i, 0)),
        out_shape=jax.ShapeDtypeStruct((rows, d), F32),
        compiler_params=_cparams(1),
        name="gate_out",
    )(*args)


def _bucket_thresholds():
    nb = REL_BUCKETS // 2
    max_exact = nb // 2
    n = np.arange(0, 4 * REL_MAX_DIST)
    nf = np.maximum(n, 1).astype(np.float32)
    large = max_exact + (np.log(nf / np.float32(max_exact))
                         / np.float32(math.log(REL_MAX_DIST / max_exact))
                         * np.float32(nb - max_exact)).astype(np.int32)
    bucket = np.where(n < max_exact, n, np.minimum(large, nb - 1))
    assert np.all(np.diff(bucket) >= 0) and bucket[0] == 0 and bucket[-1] == nb - 1
    thr = [int(np.argmax(bucket >= k)) for k in range(1, nb)]
    assert thr[-1] < REL_MAX_DIST
    return thr


def _bias_tiles_kernel(tab_ref, full_ref, win_ref, *, n_tiles, n_far):
    h = pl.program_id(0)
    thr = _bucket_thresholds()
    nb = REL_BUCKETS // 2
    row = lax.broadcasted_iota(jnp.int32, (BLOCK, BLOCK), 0)
    col = lax.broadcasted_iota(jnp.int32, (BLOCK, BLOCK), 1)
    def tile_of(rel):
        n = jnp.abs(rel)
        neg = jnp.full((BLOCK, BLOCK), tab_ref[0, h], F32)
        pos = jnp.full((BLOCK, BLOCK), tab_ref[nb, h], F32)
        for b in range(1, nb):
            ge = n >= thr[b - 1]
            neg = jnp.where(ge, tab_ref[b, h], neg)
            pos = jnp.where(ge, tab_ref[nb + b, h], pos)
        return jnp.where(rel > 0, pos, neg) * LOG2E

    for k in range(n_tiles):
        full_ref[0, k] = tile_of((k - n_far) * BLOCK + row - col)
        j = k - (n_far - 1)
        if 0 <= j < 3:
            rel = (k - n_far) * BLOCK + col - row
            win_ref[:, j * BLOCK:(j + 1) * BLOCK] = jnp.where(jnp.abs(rel) <= WINDOW,
                                                              tile_of(rel), NEG_INF)


def _bias_tiles(rel_bias, n_blocks):
    n_far = n_blocks - 1
    n_tiles = 2 * n_blocks - 1
    n_heads = rel_bias.shape[1]
    return pl.pallas_call(
        functools.partial(_bias_tiles_kernel, n_tiles=n_tiles, n_far=n_far),
        grid=(n_heads,),
        in_specs=[pl.BlockSpec(memory_space=pltpu.SMEM)],
        out_specs=[pl.BlockSpec((1, n_tiles, BLOCK, BLOCK), lambda h: (h, 0, 0, 0)),
                   pl.BlockSpec((BLOCK, 3 * BLOCK), lambda h: (h, 0))],
        out_shape=[jax.ShapeDtypeStruct((n_heads, n_tiles, BLOCK, BLOCK), F32),
                   jax.ShapeDtypeStruct((n_heads * BLOCK, 3 * BLOCK), F32)],
        compiler_params=_cparams(1),
        name="bias_tiles",
    )(rel_bias)


def _stack_gqa_heads(q_tiles):
    lo = _lane_lo(q_tiles[0].shape)
    return jnp.concatenate([jnp.where(lo, q, 0.0).astype(BF16) for q in q_tiles]
                           + [jnp.where(lo, 0.0, q).astype(BF16) for q in q_tiles], axis=0)


def _stack_pair_heads(q):
    lo = _lane_lo(q.shape)
    return jnp.concatenate([jnp.where(lo, q, 0.0).astype(BF16),
                            jnp.where(lo, 0.0, q).astype(BF16)], axis=0)


def _v_with_ones(v_pair):
    lane = lax.broadcasted_iota(jnp.int32, v_pair.shape, 1)
    vf = v_pair.astype(F32)
    v_lo = jnp.where(lane < HEAD_DIM, vf, jnp.where(lane == HEAD_DIM, 1.0, 0.0))
    v_hi = jnp.where(lane >= HEAD_DIM, vf, jnp.where(lane == 0, 1.0, 0.0))
    return v_lo.astype(BF16), v_hi.astype(BF16)


def _normalise_pair(o_lo, o_hi, extra_lo=None, extra_hi=None):
    lane = lax.broadcasted_iota(jnp.int32, o_lo.shape, 1)
    l_lo = jnp.sum(jnp.where(lane == HEAD_DIM, o_lo, 0.0), axis=-1, keepdims=True)
    l_hi = jnp.sum(jnp.where(lane == 0, o_hi, 0.0), axis=-1, keepdims=True)
    if extra_lo is not None:
        l_lo = l_lo + extra_lo
        l_hi = l_hi + extra_hi
    return jnp.where(lane < HEAD_DIM, o_lo * (1.0 / l_lo), o_hi * (1.0 / l_hi))


def _window_attn_kernel(q_ref, kp_ref, kc_ref, kn_ref, vp_ref, vc_ref, vn_ref,
                        bias_ref, sink_ref, o_ref, *, n_blocks):
    n = pl.program_id(1)
    qf = q_ref[...].astype(F32) * (HEAD_DIM ** -0.5 * LOG2E)
    lhs = _stack_gqa_heads([qf[:, j * LANES:(j + 1) * LANES] for j in range(3)])
    s = jnp.concatenate([_nt_dot(lhs, kp_ref[...]), _nt_dot(lhs, kc_ref[...]),
                         _nt_dot(lhs, kn_ref[...])], axis=1)
    col = lax.broadcasted_iota(jnp.int32, s.shape, 1)
    valid = jnp.logical_and(jnp.logical_or(col >= BLOCK, n > 0),
                            jnp.logical_or(col < 2 * BLOCK, n < n_blocks - 1))
    s = jnp.where(valid, s + bias_ref[...], NEG_INF)
    sink = sink_ref[...]
    m = jnp.maximum(jnp.max(s, axis=-1, keepdims=True), sink)
    p = jnp.exp2(s - m).astype(BF16)
    p_sink = jnp.exp2(sink - m)
    v_lo, v_hi = _v_with_ones(jnp.concatenate([vp_ref[...], vc_ref[...], vn_ref[...]], axis=0))
    half = 3 * BLOCK
    o_lo = jnp.dot(p[:half], v_lo, preferred_element_type=F32)
    o_hi = jnp.dot(p[half:], v_hi, preferred_element_type=F32)
    for j in range(3):
        rows = slice(j * BLOCK, (j + 1) * BLOCK)
        o_ref[:, j * LANES:(j + 1) * LANES] = _normalise_pair(
            o_lo[rows], o_hi[rows], p_sink[:half][rows], p_sink[half:][rows]).astype(o_ref.dtype)


def _window_attn(q, k, v, bias_win, sink_rows, batch, seq):
    nb = seq // BLOCK
    kv_spec = lambda f: pl.BlockSpec((BLOCK, LANES), f)
    prev = lambda b, n: (b * nb + jnp.maximum(n - 1, 0), 0)
    cur = lambda b, n: (b * nb + n, 0)
    nxt = lambda b, n: (b * nb + jnp.minimum(n + 1, nb - 1), 0)
    return pl.pallas_call(
        functools.partial(_window_attn_kernel, n_blocks=nb),
        grid=(batch, nb),
        in_specs=[pl.BlockSpec((BLOCK, 3 * LANES), cur),
                  kv_spec(prev), kv_spec(cur), kv_spec(nxt),
                  kv_spec(prev), kv_spec(cur), kv_spec(nxt),
                  pl.BlockSpec(bias_win.shape, lambda b, n: (0, 0)),
                  pl.BlockSpec(sink_rows.shape, lambda b, n: (0, 0))],
        out_specs=pl.BlockSpec((BLOCK, 3 * LANES), cur),
        out_shape=jax.ShapeDtypeStruct(q.shape, BF16),
        compiler_params=_cparams(2),
        name="window_attn",
    )(q, k, k, k, v, v, v, bias_win, sink_rows)


def _norm_rope_pair(x, g, cos, sin_signed):
    lo = _lane_lo(x.shape)
    x2 = x * x
    ss_lo = jnp.sum(jnp.where(lo, x2, 0.0), axis=-1, keepdims=True)
    ss_hi = jnp.sum(jnp.where(lo, 0.0, x2), axis=-1, keepdims=True)
    ms = jnp.where(lo, ss_lo, ss_hi) * (1.0 / HEAD_DIM)
    y = x * lax.rsqrt(ms + EPS) * g
    lane = lax.broadcasted_iota(jnp.int32, x.shape, 1)
    quarter = HEAD_DIM // 4
    first = (lane & quarter) == 0
    partner = jnp.where(first, pltpu.roll(y, LANES - quarter, 1), pltpu.roll(y, quarter, 1))
    return y * cos + partner * sin_signed


def _dense_attn_kernel(q_ref, k_ref, v_ref, cos_ref, sin_ref, gq_ref, gk_ref, o_ref,
                       ka_scr, vt_scr, kmax_scr, *, tq):
    i = pl.program_id(1)
    seq = k_ref.shape[0]
    kc = 2 * BLOCK
    n_chunks = seq // kc
    grp = 3 * tq

    @pl.when(i == 0)
    def _():
        k = _norm_rope_pair(k_ref[...].astype(F32), gk_ref[...],
                            cos_ref[...], sin_ref[...]).astype(BF16)
        lane = lax.broadcasted_iota(jnp.int32, (seq, LANES), 1)
        ka_scr[:, :LANES] = k
        ka_scr[:, LANES:] = jnp.where(lane == 0, 1.0, 0.0).astype(BF16)
        kmax_lo, kmax_hi = _pair_key_norm_max(k)
        kmax_scr[0:1, :] = kmax_lo
        kmax_scr[1:2, :] = kmax_hi
        v_t = v_ref[...].astype(F32).T.astype(BF16)
        for g in range(2):
            vt_scr[g, :HEAD_DIM, :] = v_t[g * HEAD_DIM:(g + 1) * HEAD_DIM]
            vt_scr[g, HEAD_DIM:, :] = jnp.ones((ONES_ROWS, seq), BF16)

    rows = pl.ds(pl.multiple_of(i * tq, tq), tq)
    cos = cos_ref[rows, :]
    sin = sin_ref[rows, :]
    q_tiles = []
    for j in range(3):
        qj = _norm_rope_pair(q_ref[:, j * LANES:(j + 1) * LANES].astype(F32),
                             gq_ref[...], cos, sin)
        q_tiles.append(qj * (HEAD_DIM ** -0.5 * LOG2E))
    lhs = _stack_gqa_heads(q_tiles)

    def write(acc):
        outs = [a[:HEAD_DIM] * (1.0 / a[HEAD_DIM:HEAD_DIM + 1]) for a in acc]
        for j in range(3):
            cols = slice(j * tq, (j + 1) * tq)
            pair_t = jnp.concatenate([outs[0][:, cols], outs[1][:, cols]], axis=0)
            o_ref[:, j * LANES:(j + 1) * LANES] = pair_t.T.astype(o_ref.dtype)

    aug = _shift_column(lhs, kmax_scr[0:1, :], kmax_scr[1:2, :], 0.0, 0.0)
    s_all = _nt_dot(ka_scr[...], jnp.concatenate([lhs, aug], axis=1))
    p = jnp.exp2(s_all).astype(BF16)
    acc = [jnp.dot(vt_scr[g], p[:, g * grp:(g + 1) * grp], preferred_element_type=F32)
           for g in range(2)]
    write(acc)
    sums = jnp.concatenate([a[HEAD_DIM:HEAD_DIM + 1] for a in acc], axis=0)

    @pl.when(jnp.logical_not(jnp.min(sums) >= SUM_FLOOR))
    def _():
        s_exact = _nt_dot(ka_scr[:, :LANES], lhs)
        m = None
        acc2 = [None, None]
        for c in range(n_chunks):
            s = s_exact[c * kc:(c + 1) * kc, :]
            m_c = jnp.max(s, axis=0, keepdims=True)
            m_new = m_c if m is None else jnp.maximum(m, m_c)
            pc = jnp.exp2(s - m_new).astype(BF16)
            if m is not None:
                alpha = jnp.exp2(m - m_new)
            for g in range(2):
                cols = slice(g * grp, (g + 1) * grp)
                pv = jnp.dot(vt_scr[g, :, c * kc:(c + 1) * kc], pc[:, cols],
                             preferred_element_type=F32)
                acc2[g] = pv if m is None else acc2[g] * alpha[:, cols] + pv
            m = m_new
        write(acc2)


def _dense_attn(q, k, v, cos, sin_signed, gq, gk, batch, seq, tq):
    nq = seq // tq
    return pl.pallas_call(
        functools.partial(_dense_attn_kernel, tq=tq),
        grid=(batch, nq),
        in_specs=[pl.BlockSpec((tq, 3 * LANES), lambda b, i: (b * nq + i, 0)),
                  pl.BlockSpec((seq, LANES), lambda b, i: (b, 0)),
                  pl.BlockSpec((seq, LANES), lambda b, i: (b, 0)),
                  pl.BlockSpec((seq, LANES), lambda b, i: (0, 0)),
                  pl.BlockSpec((seq, LANES), lambda b, i: (0, 0)),
                  pl.BlockSpec((1, LANES), lambda b, i: (0, 0)),
                  pl.BlockSpec((1, LANES), lambda b, i: (0, 0))],
        out_specs=pl.BlockSpec((tq, 3 * LANES), lambda b, i: (b * nq + i, 0)),
        out_shape=jax.ShapeDtypeStruct(q.shape, BF16),
        scratch_shapes=[pltpu.VMEM((seq, 2 * LANES), BF16),
                        pltpu.VMEM((2, HEAD_DIM + ONES_ROWS, seq), BF16),
                        pltpu.VMEM((2, LANES), F32)],
        compiler_params=_cparams(2),
        name="dense_attn",
    )(q, k, v, cos, sin_signed, gq, gk)


def _cross_attn_kernel(q_ref, mk_ref, mv_ref, o_ref, *, tq):
    n_pairs = q_ref.shape[1] // LANES
    for j in range(n_pairs):
        cols = slice(j * LANES, (j + 1) * LANES)
        lhs = _stack_pair_heads(q_ref[:, cols].astype(F32) * (HEAD_DIM ** -0.5 * LOG2E))
        s = _nt_dot(lhs, mk_ref[:, cols])
        m = jnp.max(s, axis=-1, keepdims=True)
        p = jnp.exp2(s - m).astype(BF16)
        v_lo, v_hi = _v_with_ones(mv_ref[:, cols])
        o_lo = jnp.dot(p[:tq], v_lo, preferred_element_type=F32)
        o_hi = jnp.dot(p[tq:], v_hi, preferred_element_type=F32)
        o_ref[:, cols] = _normalise_pair(o_lo, o_hi).astype(o_ref.dtype)


def _cross_attn(q, mk, mv, batch, seq, tq):
    nq = seq // tq
    mem_len = mk.shape[0] // batch
    width = q.shape[1]
    return pl.pallas_call(
        functools.partial(_cross_attn_kernel, tq=tq),
        grid=(batch, nq),
        in_specs=[pl.BlockSpec((tq, width), lambda b, i: (b * nq + i, 0)),
                  pl.BlockSpec((mem_len, width), lambda b, i: (b, 0)),
                  pl.BlockSpec((mem_len, width), lambda b, i: (b, 0))],
        out_specs=pl.BlockSpec((tq, width), lambda b, i: (b * nq + i, 0)),
        out_shape=jax.ShapeDtypeStruct(q.shape, BF16),
        compiler_params=_cparams(2),
        name="cross_attn",
    )(q, mk, mv)


def _shift_column(lhs, kmax_lo, kmax_hi, bias_max_lo, bias_max_hi):
    half = lhs.shape[0] // 2
    lf = lhs.astype(F32)
    q_norm = jnp.sqrt(jnp.sum(lf * lf, axis=-1, keepdims=True))
    row = lax.broadcasted_iota(jnp.int32, lhs.shape, 0)
    lane = lax.broadcasted_iota(jnp.int32, lhs.shape, 1)
    shift = (q_norm * jnp.where(row < half, kmax_lo, kmax_hi)
             + jnp.where(row < half, bias_max_lo, bias_max_hi))
    shift = shift + jnp.abs(shift) * 2.0 ** -7
    return jnp.where(lane == 0, -shift, 0.0).astype(BF16)


def _pair_key_norm_max(k_pair):
    kf = k_pair.astype(F32)
    k2 = kf * kf
    lo = _lane_lo(k2.shape)
    n_lo = jnp.max(jnp.sum(jnp.where(lo, k2, 0.0), axis=-1, keepdims=True), axis=0, keepdims=True)
    n_hi = jnp.max(jnp.sum(jnp.where(lo, 0.0, k2), axis=-1, keepdims=True), axis=0, keepdims=True)
    return (jnp.broadcast_to(jnp.sqrt(n_lo), (1, LANES)),
            jnp.broadcast_to(jnp.sqrt(n_hi), (1, LANES)))


def _diff_attn_kernel(q1_ref, q2_ref, k1_ref, k2_ref, v_ref, bias_ref, bmax_ref, lam_ref, g_ref,
                      o_ref, vt_scr, ka_scr, kmax_scr, *, tq, n_blocks, lam_init):
    i = pl.program_id(2)
    sub = tq // BLOCK
    seq = k1_ref.shape[0]
    kc = 2 * BLOCK
    n_chunks = seq // kc
    k_refs = (k1_ref, k2_ref)

    @pl.when(i == 0)
    def _():
        for head in range(2):
            v_t = v_ref[:, head * C_V_DIM:(head + 1) * C_V_DIM].astype(F32).T
            vt_scr[head, :C_V_DIM, :] = v_t.astype(BF16)
            vt_scr[head, C_V_DIM:, :] = jnp.ones((ONES_ROWS, seq), BF16)
        lane = lax.broadcasted_iota(jnp.int32, (seq, LANES), 1)
        ones_col = jnp.where(lane == 0, 1.0, 0.0).astype(BF16)
        for t in range(2):
            ka_scr[t, :, :LANES] = k_refs[t][...]
            ka_scr[t, :, LANES:] = ones_col
            kmax_lo, kmax_hi = _pair_key_norm_max(k_refs[t][...])
            kmax_scr[t, 0:1, :] = kmax_lo
            kmax_scr[t, 1:2, :] = kmax_hi

    lam_vec = lam_ref[...]
    lam = (jnp.exp(jnp.sum(lam_vec[0:1] * lam_vec[1:2], axis=-1, keepdims=True))
           - jnp.exp(jnp.sum(lam_vec[2:3] * lam_vec[3:4], axis=-1, keepdims=True)) + lam_init)
    lhs = [_stack_pair_heads(q_ref[...].astype(F32) * (HEAD_DIM ** -0.5 * LOG2E))
           for q_ref in (q1_ref, q2_ref)]

    def bias_chunk(c):
        return jnp.concatenate(
            [jnp.concatenate([bias_ref[head, (n_blocks - 1) - (i * sub + u) + 2 * c + e]
                              for head in range(2) for u in range(sub)], axis=1)
             for e in range(2)], axis=0)

    def shifted_scores(t):
        aug = _shift_column(lhs[t], kmax_scr[t, 0:1, :], kmax_scr[t, 1:2, :],
                            bmax_ref[0], bmax_ref[1])
        return _nt_dot(ka_scr[t], jnp.concatenate([lhs[t], aug], axis=1))

    def pv_of_shifted(s_all):
        p = jnp.concatenate([jnp.exp2(s_all[c * kc:(c + 1) * kc, :] + bias_chunk(c)).astype(BF16)
                             for c in range(n_chunks)], axis=0)
        return [jnp.dot(vt_scr[head], p[:, head * tq:(head + 1) * tq],
                        preferred_element_type=F32) for head in range(2)]

    def pv_running_max(s_all):
        m = None
        acc = [None, None]
        for c in range(n_chunks):
            s = s_all[c * kc:(c + 1) * kc, :] + bias_chunk(c)
            m_c = jnp.max(s, axis=0, keepdims=True)
            m_new = m_c if m is None else jnp.maximum(m, m_c)
            p = jnp.exp2(s - m_new).astype(BF16)
            if m is not None:
                alpha = jnp.exp2(m - m_new)
            for head in range(2):
                cols = slice(head * tq, (head + 1) * tq)
                pv = jnp.dot(vt_scr[head, :, c * kc:(c + 1) * kc], p[:, cols],
                             preferred_element_type=F32)
                acc[head] = pv if m is None else acc[head] * alpha[:, cols] + pv
            m = m_new
        return acc

    def write(acc1, acc2):
        g = g_ref[...] * (1.0 - lam_init)
        for head in range(2):
            o1 = acc1[head][:C_V_DIM] * (1.0 / acc1[head][C_V_DIM:C_V_DIM + 1])
            o2 = acc2[head][:C_V_DIM] * (1.0 / acc2[head][C_V_DIM:C_V_DIM + 1])
            out = o1 - lam * o2
            ms = jnp.mean(out * out, axis=0, keepdims=True)
            y = out * lax.rsqrt(ms + EPS) * g
            o_ref[:, head * C_V_DIM:(head + 1) * C_V_DIM] = y.T.astype(o_ref.dtype)

    s1 = shifted_scores(0)
    s2 = shifted_scores(1)
    acc1 = pv_of_shifted(s1)
    acc2 = pv_of_shifted(s2)
    write(acc1, acc2)
    sums = jnp.concatenate([a[C_V_DIM:C_V_DIM + 1] for a in acc1 + acc2], axis=0)

    @pl.when(jnp.logical_not(jnp.min(sums) >= SUM_FLOOR))
    def _():
        write(pv_running_max(_nt_dot(k1_ref[...], lhs[0])),
              pv_running_max(_nt_dot(k2_ref[...], lhs[1])))


def _diff_attn(q1, q2, k1, k2, v, bias_t, bias_max, lam_vecs, subln_g, lam_init, batch, seq, tq):
    nq = seq // tq
    nb = seq // BLOCK
    n_pairs = C_HEADS // 2
    n_tiles = bias_t.shape[1]
    q_spec = pl.BlockSpec((tq, LANES), lambda b, p, i: (b * nq + i, p))
    k_spec = pl.BlockSpec((seq, LANES), lambda b, p, i: (b, p))
    g_cols = jnp.broadcast_to(subln_g.reshape(C_V_DIM, 1), (C_V_DIM, tq))
    return pl.pallas_call(
        functools.partial(_diff_attn_kernel, tq=tq, n_blocks=nb, lam_init=lam_init),
        grid=(batch, n_pairs, nq),
        in_specs=[q_spec, q_spec, k_spec, k_spec,
                  pl.BlockSpec((seq, 2 * C_V_DIM), lambda b, p, i: (b, p)),
                  pl.BlockSpec((2, n_tiles, BLOCK, BLOCK), lambda b, p, i: (p, 0, 0, 0)),
                  pl.BlockSpec((2, 1, LANES), lambda b, p, i: (p, 0, 0)),
                  pl.BlockSpec(lam_vecs.shape, lambda b, p, i: (0, 0)),
                  pl.BlockSpec((C_V_DIM, tq), lambda b, p, i: (0, 0))],
        out_specs=pl.BlockSpec((tq, 2 * C_V_DIM), lambda b, p, i: (b * nq + i, p)),
        out_shape=jax.ShapeDtypeStruct(v.shape, BF16),
        scratch_shapes=[pltpu.VMEM((2, C_V_DIM + ONES_ROWS, seq), BF16),
                        pltpu.VMEM((2, seq, 2 * LANES), BF16),
                        pltpu.VMEM((2, 2, LANES), F32)],
        compiler_params=_cparams(3),
        name="diff_attn",
    )(q1, q2, k1, k2, v, bias_t, bias_max, lam_vecs, g_cols)


def _rope_tables(seq):
    rows = seq // GRID_W
    row = jnp.broadcast_to(jnp.arange(rows)[:, None], (rows, GRID_W)).reshape(-1)
    col = jnp.broadcast_to(jnp.arange(GRID_W)[None, :], (rows, GRID_W)).reshape(-1)
    half = HEAD_DIM // 2
    inv = 1.0 / (ROPE_THETA ** (jnp.arange(0, half, 2, dtype=F32) / half))
    ang_row = row.astype(F32)[:, None] * inv
    ang_col = col.astype(F32)[:, None] * inv
    cos = jnp.concatenate([jnp.cos(ang_row)] * 2 + [jnp.cos(ang_col)] * 2, axis=-1)
    sin = jnp.concatenate([-jnp.sin(ang_row), jnp.sin(ang_row),
                           -jnp.sin(ang_col), jnp.sin(ang_col)], axis=-1)
    return jnp.tile(cos, (1, 2)), jnp.tile(sin, (1, 2))


def _gqa_cols():
    return np.concatenate([np.arange(h * HEAD_DIM, (h + 1) * HEAD_DIM) for h in GQA_HEAD_ORDER])


def kernel(x, mem, rel_bias, mem_norm, final_norm, even_norm, even_w_in, even_sink, even_q_norm, even_k_norm, even_w_mem_kv, even_w_out, odd_norm, odd_w_in, odd_lambda_q1, odd_lambda_k1, odd_lambda_q2, odd_lambda_k2, odd_subln, odd_w_mem_kv, odd_w_out):
    batch, seq, d = x.shape
    mem_len = mem.shape[1]
    tokens = batch * seq
    nb = seq // BLOCK
    xw = X_HEADS * HEAD_DIM
    gw = A_HEADS * HEAD_DIM

    gq = _gqa_cols()
    aq0, ak0, av0 = 0, gw, gw + 128
    bq0 = av0 + 128
    bk0, bv0 = bq0 + gw, bq0 + gw + 128
    xq0 = bv0 + 128
    gate0 = xq0 + xw
    even_cols = np.concatenate([aq0 + gq, np.arange(ak0, bq0), bq0 + gq, np.arange(bk0, gate0),
                                gate0 + gq, gate0 + gw + gq,
                                np.arange(gate0 + 2 * gw, gate0 + D_MODEL)])
    mix_rows = np.concatenate([gq, gw + gq, np.arange(2 * gw, D_MODEL)])
    w_in0 = even_w_in[0][:, even_cols].astype(BF16)
    w_out0 = even_w_out[0][mix_rows, :].astype(BF16)
    w_in1 = odd_w_in[0].astype(BF16)
    w_out1 = odd_w_out[0].astype(BF16)

    x2 = x.reshape(tokens, d)
    mem2 = mem.reshape(batch * mem_len, d)

    bias_full, bias_win = _bias_tiles(rel_bias, nb)
    cos, sin_signed = _rope_tables(seq)
    sink_rows = jnp.repeat(even_sink[0] * LOG2E, BLOCK).reshape(A_HEADS * BLOCK, 1)

    mk0, mv0 = _norm_proj(mem2, mem_norm, even_w_mem_kv[0].astype(BF16), (xw, xw), 512)
    aq, ak, av, bq, bk, bv, xq, gate = _norm_proj(
        x2, even_norm[0], w_in0, (gw, 128, 128, gw, 128, 128, xw, D_MODEL), 512)
    y_a = _window_attn(aq, ak, av, bias_win, sink_rows, batch, seq)
    y_b = _dense_attn(bq, bk, bv, cos, sin_signed,
                      jnp.tile(even_q_norm[0], 2).reshape(1, LANES),
                      jnp.tile(even_k_norm[0], 2).reshape(1, LANES), batch, seq, 256)
    y_x = _cross_attn(xq, mk0, mv0, batch, seq, 512)
    h1 = _gate_out([y_a, y_b, y_x], gate, w_out0, x2, None, 512)

    lam_init = 0.8 - 0.6 * math.exp(-0.3 * 1)
    mk1, mv1 = _norm_proj(mem2, mem_norm, odd_w_mem_kv[0].astype(BF16), (xw, xw), 512)
    cw = C_HEADS * HEAD_DIM
    q1, q2, k1, k2, v, xq1, gate1 = _norm_proj(
        h1, odd_norm[0], w_in1, (cw, cw, cw, cw, C_HEADS * C_V_DIM, xw, D_MODEL), 512)
    lam_vecs = jnp.stack([odd_lambda_q1[0], odd_lambda_k1[0], odd_lambda_q2[0], odd_lambda_k2[0]])
    bias_max = jnp.broadcast_to((jnp.max(rel_bias, axis=0) * LOG2E).reshape(C_HEADS, 1, 1),
                                (C_HEADS, 1, LANES))
    y_c = _diff_attn(q1, q2, k1, k2, v, bias_full, bias_max, lam_vecs, odd_subln[0], lam_init,
                     batch, seq, 256)
    y_x1 = _cross_attn(xq1, mk1, mv1, batch, seq, 512)
    out = _gate_out([y_c, y_x1], gate1, w_out1, h1, final_norm, 512)
    return out.reshape(batch, seq, d)
```

```python
import functools
import math

import numpy as np
import jax
import jax.numpy as jnp
from jax import lax
from jax.experimental import pallas as pl
from jax.experimental.pallas import tpu as pltpu

D_MODEL = 1024
HEAD_DIM = 64
BLOCK = 128
WINDOW = 128
GRID_W = 64
A_HEADS = 6
B_HEADS = 6
C_HEADS = 6
C_V_DIM = 128
X_HEADS = 4
REL_BUCKETS = 32
REL_MAX_DIST = 128
ROPE_THETA = 10000.0
EPS = 1e-6
NEG_INF = -1e30
LOG2E = math.log2(math.e)
LANES = 128
ONES_ROWS = 16
SUM_FLOOR = 2.0 ** -64
VMEM_LIMIT = 56 * 1024 * 1024

F32 = jnp.float32
BF16 = jnp.bfloat16

GQA_HEAD_ORDER = (0, 3, 1, 4, 2, 5)


def _cparams(n_axes, flags=None):
    return pltpu.CompilerParams(dimension_semantics=("arbitrary",) * n_axes,
                                vmem_limit_bytes=VMEM_LIMIT, flags=flags)


def _lane_lo(shape):
    return lax.broadcasted_iota(jnp.int32, shape, len(shape) - 1) < HEAD_DIM


def _rmsnorm_rows(x, g):
    ms = jnp.mean(x * x, axis=-1, keepdims=True)
    return x * lax.rsqrt(ms + EPS) * g


def _nt_dot(a, b):
    return lax.dot_general(a, b, (((1,), (1,)), ((), ())), preferred_element_type=F32)


def _norm_proj_kernel(x_ref, g_ref, w_ref, *out_refs):
    xn = _rmsnorm_rows(x_ref[...], g_ref[...]).astype(BF16)
    c0 = 0
    for o_ref in out_refs:
        width = o_ref.shape[1]
        o_ref[...] = jnp.dot(xn, w_ref[:, c0:c0 + width],
                             preferred_element_type=F32).astype(o_ref.dtype)
        c0 += width


def _norm_proj(x, g, w_bf16, splits, tm):
    rows, d = x.shape
    n = w_bf16.shape[1]
    assert sum(splits) == n and rows % tm == 0
    return pl.pallas_call(
        _norm_proj_kernel,
        grid=(rows // tm,),
        in_specs=[pl.BlockSpec((tm, d), lambda i: (i, 0)),
                  pl.BlockSpec((1, d), lambda i: (0, 0)),
                  pl.BlockSpec((d, n), lambda i: (0, 0))],
        out_specs=[pl.BlockSpec((tm, s), lambda i: (i, 0)) for s in splits],
        out_shape=[jax.ShapeDtypeStruct((rows, s), BF16) for s in splits],
        compiler_params=_cparams(1),
        name="norm_proj",
    )(x, g.reshape(1, d), w_bf16)


def _gate_out_kernel(*refs, n_parts, final):
    y_refs = refs[:n_parts]
    gate_ref, w_ref, res_ref = refs[n_parts:n_parts + 3]
    o_ref = refs[-1]
    acc = res_ref[...]
    c0 = 0
    for y_ref in y_refs:
        width = y_ref.shape[1]
        g = gate_ref[:, c0:c0 + width].astype(F32)
        silu = g * (1.0 / (1.0 + jnp.exp(-g)))
        yy = (y_ref[...].astype(F32) * silu).astype(BF16)
        acc = acc + jnp.dot(yy, w_ref[c0:c0 + width, :], preferred_element_type=F32)
        c0 += width
    if final:
        acc = _rmsnorm_rows(acc, refs[n_parts + 3][...])
    o_ref[...] = acc


def _gate_out(y_parts, gate, w_bf16, resid, final_g, tm):
    rows, d = resid.shape
    mix = w_bf16.shape[0]
    final = final_g is not None
    in_specs = [pl.BlockSpec((tm, y.shape[1]), lambda i: (i, 0)) for y in y_parts]
    in_specs += [pl.BlockSpec((tm, mix), lambda i: (i, 0)),
                 pl.BlockSpec((mix, d), lambda i: (0, 0)),
                 pl.BlockSpec((tm, d), lambda i: (i, 0))]
    args = list(y_parts) + [gate, w_bf16, resid]
    if final:
        in_specs.append(pl.BlockSpec((1, d), lambda i: (0, 0)))
        args.append(final_g.reshape(1, d))
    return pl.pallas_call(
        functools.partial(_gate_out_kernel, n_parts=len(y_parts), final=final),
        grid=(rows // tm,),
        in_specs=in_specs,
        out_specs=pl.BlockSpec((tm, d), lambda i: (i, 0)),
        out_shape=jax.ShapeDtypeStruct((rows, d), F32),
        compiler_params=_cparams(1),
        name="gate_out",
    )(*args)


def _bucket_thresholds():
    nb = REL_BUCKETS // 2
    max_exact = nb // 2
    n = np.arange(0, 4 * REL_MAX_DIST)
    nf = np.maximum(n, 1).astype(np.float32)
    large = max_exact + (np.log(nf / np.float32(max_exact))
                         / np.float32(math.log(REL_MAX_DIST / max_exact))
                         * np.float32(nb - max_exact)).astype(np.int32)
    bucket = np.where(n < max_exact, n, np.minimum(large, nb - 1))
    assert np.all(np.diff(bucket) >= 0) and bucket[0] == 0 and bucket[-1] == nb - 1
    thr = [int(np.argmax(bucket >= k)) for k in range(1, nb)]
    assert thr[-1] < REL_MAX_DIST
    return thr


def _bias_tiles_kernel(tab_ref, full_ref, win_ref, *, n_tiles, n_far):
    h = pl.program_id(0)
    thr = _bucket_thresholds()
    nb = REL_BUCKETS // 2
    row = lax.broadcasted_iota(jnp.int32, (BLOCK, BLOCK), 0)
    col = lax.broadcasted_iota(jnp.int32, (BLOCK, BLOCK), 1)
    def tile_of(rel):
        n = jnp.abs(rel)
        neg = jnp.full((BLOCK, BLOCK), tab_ref[0, h], F32)
        pos = jnp.full((BLOCK, BLOCK), tab_ref[nb, h], F32)
        for b in range(1, nb):
            ge = n >= thr[b - 1]
            neg = jnp.where(ge, tab_ref[b, h], neg)
            pos = jnp.where(ge, tab_ref[nb + b, h], pos)
        return jnp.where(rel > 0, pos, neg) * LOG2E

    for k in range(n_tiles):
        rel = (k - n_far) * BLOCK + row - col
        tile = tile_of(rel)
        full_ref[0, k] = tile
        j = k - (n_far - 1)
        if 0 <= j < 3:
            win_ref[j * BLOCK:(j + 1) * BLOCK, :] = jnp.where(jnp.abs(rel) <= WINDOW, tile, NEG_INF)


def _bias_tiles(rel_bias, n_blocks):
    n_far = n_blocks - 1
    n_tiles = 2 * n_blocks - 1
    n_heads = rel_bias.shape[1]
    return pl.pallas_call(
        functools.partial(_bias_tiles_kernel, n_tiles=n_tiles, n_far=n_far),
        grid=(n_heads,),
        in_specs=[pl.BlockSpec(memory_space=pltpu.SMEM)],
        out_specs=[pl.BlockSpec((1, n_tiles, BLOCK, BLOCK), lambda h: (h, 0, 0, 0)),
                   pl.BlockSpec((3 * BLOCK, BLOCK), lambda h: (0, h))],
        out_shape=[jax.ShapeDtypeStruct((n_heads, n_tiles, BLOCK, BLOCK), F32),
                   jax.ShapeDtypeStruct((3 * BLOCK, n_heads * BLOCK), F32)],
        compiler_params=_cparams(1),
        name="bias_tiles",
    )(rel_bias)


def _stack_gqa_heads(q_tiles):
    lo = _lane_lo(q_tiles[0].shape)
    return jnp.concatenate([jnp.where(lo, q, 0.0).astype(BF16) for q in q_tiles]
                           + [jnp.where(lo, 0.0, q).astype(BF16) for q in q_tiles], axis=0)


def _stack_pair_heads(q):
    lo = _lane_lo(q.shape)
    return jnp.concatenate([jnp.where(lo, q, 0.0).astype(BF16),
                            jnp.where(lo, 0.0, q).astype(BF16)], axis=0)


def _v_with_ones(v_pair):
    lane = lax.broadcasted_iota(jnp.int32, v_pair.shape, 1)
    vf = v_pair.astype(F32)
    v_lo = jnp.where(lane < HEAD_DIM, vf, jnp.where(lane == HEAD_DIM, 1.0, 0.0))
    v_hi = jnp.where(lane >= HEAD_DIM, vf, jnp.where(lane == 0, 1.0, 0.0))
    return v_lo.astype(BF16), v_hi.astype(BF16)


def _normalise_pair(o_lo, o_hi, extra_lo=None, extra_hi=None):
    lane = lax.broadcasted_iota(jnp.int32, o_lo.shape, 1)
    l_lo = jnp.sum(jnp.where(lane == HEAD_DIM, o_lo, 0.0), axis=-1, keepdims=True)
    l_hi = jnp.sum(jnp.where(lane == 0, o_hi, 0.0), axis=-1, keepdims=True)
    if extra_lo is not None:
        l_lo = l_lo + extra_lo
        l_hi = l_hi + extra_hi
    return jnp.where(lane < HEAD_DIM, o_lo * (1.0 / l_lo), o_hi * (1.0 / l_hi))


def _window_attn_kernel(q_ref, k_ref, v_ref, bias_ref, bmax_ref, sink_col_ref, sink_row_ref,
                        o_ref, kpad_scr, vtpad_scr, kmax_scr, *, n_blocks, group):
    i = pl.program_id(1)
    seq = k_ref.shape[0]
    half = 3 * BLOCK

    @pl.when(i == 0)
    def _():
        lane = lax.broadcasted_iota(jnp.int32, (seq, LANES), 1)
        k_aug = jnp.concatenate([k_ref[...], jnp.where(lane == 0, 1.0, 0.0).astype(BF16)], axis=1)
        zero_blk = jnp.zeros((1, BLOCK, 2 * LANES), BF16)
        kpad_scr[0:1] = zero_blk
        kpad_scr[n_blocks + 1:n_blocks + 2] = zero_blk
        kpad_scr[1:n_blocks + 1] = k_aug.reshape(n_blocks, BLOCK, 2 * LANES)
        kmax_lo, kmax_hi = _pair_key_norm_max(k_ref[...])
        kmax_scr[0:1, :] = kmax_lo
        kmax_scr[1:2, :] = kmax_hi
        v_t = v_ref[...].astype(F32).T.astype(BF16)
        ones = jnp.ones((ONES_ROWS, BLOCK), BF16)
        zero_v = jnp.zeros((HEAD_DIM + ONES_ROWS, BLOCK), BF16)
        for g in range(2):
            vtpad_scr[g, 0] = zero_v
            vtpad_scr[g, n_blocks + 1] = zero_v
            for blk in range(n_blocks):
                vtpad_scr[g, blk + 1, :HEAD_DIM, :] = v_t[g * HEAD_DIM:(g + 1) * HEAD_DIM,
                                                           blk * BLOCK:(blk + 1) * BLOCK]
                vtpad_scr[g, blk + 1, HEAD_DIM:, :] = ones

    probe = (lax.broadcasted_iota(jnp.int32, (8, 2 * LANES), 1) == LANES).astype(BF16)
    sink_row = sink_row_ref[...]

    def query_block(u, exact):
        n = i * group + u
        qf = q_ref[u * BLOCK:(u + 1) * BLOCK, :].astype(F32) * (HEAD_DIM ** -0.5 * LOG2E)
        lhs = _stack_gqa_heads([qf[:, j * LANES:(j + 1) * LANES] for j in range(3)])
        k_win = kpad_scr[pl.ds(n, 3)].reshape(3 * BLOCK, 2 * LANES)
        if exact:
            s = _nt_dot(k_win[:, :LANES], lhs)
        else:
            aug = _shift_column(lhs, kmax_scr[0:1, :], kmax_scr[1:2, :], bmax_ref[...], bmax_ref[...],
                                floor=sink_col_ref[...])
            lhs_aug = jnp.concatenate([lhs, aug], axis=1)
            s = _nt_dot(k_win, lhs_aug)
            neg_shift = _nt_dot(probe, lhs_aug)[0:1]
        s = s + bias_ref[...]
        s = jnp.concatenate([jnp.where(n > 0, s[:BLOCK], NEG_INF), s[BLOCK:2 * BLOCK],
                             jnp.where(n < n_blocks - 1, s[2 * BLOCK:], NEG_INF)], axis=0)
        if exact:
            m = jnp.maximum(jnp.max(s, axis=0, keepdims=True), sink_row)
            p = jnp.exp2(s - m).astype(BF16)
            p_sink = jnp.exp2(sink_row - m)
        else:
            p = jnp.exp2(s).astype(BF16)
            p_sink = jnp.exp2(sink_row + neg_shift)
        outs, sums = [], []
        for g in range(2):
            acc = None
            for j in range(3):
                pv = jnp.dot(vtpad_scr[g, n + j], p[j * BLOCK:(j + 1) * BLOCK, g * half:(g + 1) * half],
                             preferred_element_type=F32)
                acc = pv if acc is None else acc + pv
            total = acc[HEAD_DIM:HEAD_DIM + 1] + p_sink[:, g * half:(g + 1) * half]
            outs.append(acc[:HEAD_DIM] * (1.0 / total))
            sums.append(total)
        for j in range(3):
            cols = slice(j * BLOCK, (j + 1) * BLOCK)
            pair_t = jnp.concatenate([outs[0][:, cols], outs[1][:, cols]], axis=0)
            o_ref[u * BLOCK:(u + 1) * BLOCK, j * LANES:(j + 1) * LANES] = pair_t.T.astype(o_ref.dtype)
        return sums

    sums = [s for u in range(group) for s in query_block(u, exact=False)]

    @pl.when(jnp.logical_not(jnp.min(jnp.concatenate(sums, axis=0)) >= SUM_FLOOR))
    def _():
        for u in range(group):
            query_block(u, exact=True)


def _window_attn(q, k, v, bias_win_t, bias_max, sink, batch, seq, group):
    nb = seq // BLOCK
    steps = nb // group
    rows = A_HEADS * BLOCK
    bmax_col = jnp.broadcast_to(jnp.repeat(bias_max, BLOCK).reshape(rows, 1), (rows, LANES))
    sink_col = jnp.broadcast_to(jnp.repeat(sink, BLOCK).reshape(rows, 1), (rows, LANES))
    sink_row = jnp.repeat(sink, BLOCK).reshape(1, rows)
    const = lambda shape: pl.BlockSpec(shape, lambda b, i: (0,) * len(shape))
    return pl.pallas_call(
        functools.partial(_window_attn_kernel, n_blocks=nb, group=group),
        grid=(batch, steps),
        in_specs=[pl.BlockSpec((group * BLOCK, 3 * LANES), lambda b, i: (b * steps + i, 0)),
                  pl.BlockSpec((seq, LANES), lambda b, i: (b, 0)),
                  pl.BlockSpec((seq, LANES), lambda b, i: (b, 0)),
                  const(bias_win_t.shape), const((rows, LANES)), const((rows, LANES)),
                  const((1, rows))],
        out_specs=pl.BlockSpec((group * BLOCK, 3 * LANES), lambda b, i: (b * steps + i, 0)),
        out_shape=jax.ShapeDtypeStruct(q.shape, BF16),
        scratch_shapes=[pltpu.VMEM((nb + 2, BLOCK, 2 * LANES), BF16),
                        pltpu.VMEM((2, nb + 2, HEAD_DIM + ONES_ROWS, BLOCK), BF16),
                        pltpu.VMEM((2, LANES), F32)],
        compiler_params=_cparams(2),
        name="window_attn",
    )(q, k, v, bias_win_t, bmax_col, sink_col, sink_row)


def _norm_rope_pair(x, g, cos, sin_signed):
    lo = _lane_lo(x.shape)
    x2 = x * x
    ss_lo = jnp.sum(jnp.where(lo, x2, 0.0), axis=-1, keepdims=True)
    ss_hi = jnp.sum(jnp.where(lo, 0.0, x2), axis=-1, keepdims=True)
    ms = jnp.where(lo, ss_lo, ss_hi) * (1.0 / HEAD_DIM)
    y = x * lax.rsqrt(ms + EPS) * g
    lane = lax.broadcasted_iota(jnp.int32, x.shape, 1)
    quarter = HEAD_DIM // 4
    first = (lane & quarter) == 0
    partner = jnp.where(first, pltpu.roll(y, LANES - quarter, 1), pltpu.roll(y, quarter, 1))
    return y * cos + partner * sin_signed


def _dense_attn_kernel(q_ref, k_ref, v_ref, cos_ref, sin_ref, gq_ref, gk_ref, o_ref,
                       ka_scr, vt_scr, kmax_scr, *, tq):
    i = pl.program_id(1)
    seq = k_ref.shape[0]
    kc = 2 * BLOCK
    n_chunks = seq // kc
    grp = 3 * tq

    @pl.when(i == 0)
    def _():
        k = _norm_rope_pair(k_ref[...].astype(F32), gk_ref[...],
                            cos_ref[...], sin_ref[...]).astype(BF16)
        lane = lax.broadcasted_iota(jnp.int32, (seq, LANES), 1)
        ka_scr[:, :LANES] = k
        ka_scr[:, LANES:] = jnp.where(lane == 0, 1.0, 0.0).astype(BF16)
        kmax_lo, kmax_hi = _pair_key_norm_max(k)
        kmax_scr[0:1, :] = kmax_lo
        kmax_scr[1:2, :] = kmax_hi
        v_t = v_ref[...].astype(F32).T.astype(BF16)
        for g in range(2):
            vt_scr[g, :HEAD_DIM, :] = v_t[g * HEAD_DIM:(g + 1) * HEAD_DIM]
            vt_scr[g, HEAD_DIM:, :] = jnp.ones((ONES_ROWS, seq), BF16)

    rows = pl.ds(pl.multiple_of(i * tq, tq), tq)
    cos = cos_ref[rows, :]
    sin = sin_ref[rows, :]
    q_tiles = []
    for j in range(3):
        qj = _norm_rope_pair(q_ref[:, j * LANES:(j + 1) * LANES].astype(F32),
                             gq_ref[...], cos, sin)
        q_tiles.append(qj * (HEAD_DIM ** -0.5 * LOG2E))
    lhs = _stack_gqa_heads(q_tiles)

    def write(acc):
        outs = [a[:HEAD_DIM] * (1.0 / a[HEAD_DIM:HEAD_DIM + 1]) for a in acc]
        for j in range(3):
            cols = slice(j * tq, (j + 1) * tq)
            pair_t = jnp.concatenate([outs[0][:, cols], outs[1][:, cols]], axis=0)
            o_ref[:, j * LANES:(j + 1) * LANES] = pair_t.T.astype(o_ref.dtype)

    aug = _shift_column(lhs, kmax_scr[0:1, :], kmax_scr[1:2, :], 0.0, 0.0)
    s_all = _nt_dot(ka_scr[...], jnp.concatenate([lhs, aug], axis=1))
    p = jnp.exp2(s_all).astype(BF16)
    acc = [jnp.dot(vt_scr[g], p[:, g * grp:(g + 1) * grp], preferred_element_type=F32)
           for g in range(2)]
    write(acc)
    sums = jnp.concatenate([a[HEAD_DIM:HEAD_DIM + 1] for a in acc], axis=0)

    @pl.when(jnp.logical_not(jnp.min(sums) >= SUM_FLOOR))
    def _():
        s_exact = _nt_dot(ka_scr[:, :LANES], lhs)
        m = None
        acc2 = [None, None]
        for c in range(n_chunks):
            s = s_exact[c * kc:(c + 1) * kc, :]
            m_c = jnp.max(s, axis=0, keepdims=True)
            m_new = m_c if m is None else jnp.maximum(m, m_c)
            pc = jnp.exp2(s - m_new).astype(BF16)
            if m is not None:
                alpha = jnp.exp2(m - m_new)
            for g in range(2):
                cols = slice(g * grp, (g + 1) * grp)
                pv = jnp.dot(vt_scr[g, :, c * kc:(c + 1) * kc], pc[:, cols],
                             preferred_element_type=F32)
                acc2[g] = pv if m is None else acc2[g] * alpha[:, cols] + pv
            m = m_new
        write(acc2)


def _dense_attn(q, k, v, cos, sin_signed, gq, gk, batch, seq, tq):
    nq = seq // tq
    return pl.pallas_call(
        functools.partial(_dense_attn_kernel, tq=tq),
        grid=(batch, nq),
        in_specs=[pl.BlockSpec((tq, 3 * LANES), lambda b, i: (b * nq + i, 0)),
                  pl.BlockSpec((seq, LANES), lambda b, i: (b, 0)),
                  pl.BlockSpec((seq, LANES), lambda b, i: (b, 0)),
                  pl.BlockSpec((seq, LANES), lambda b, i: (0, 0)),
                  pl.BlockSpec((seq, LANES), lambda b, i: (0, 0)),
                  pl.BlockSpec((1, LANES), lambda b, i: (0, 0)),
                  pl.BlockSpec((1, LANES), lambda b, i: (0, 0))],
        out_specs=pl.BlockSpec((tq, 3 * LANES), lambda b, i: (b * nq + i, 0)),
        out_shape=jax.ShapeDtypeStruct(q.shape, BF16),
        scratch_shapes=[pltpu.VMEM((seq, 2 * LANES), BF16),
                        pltpu.VMEM((2, HEAD_DIM + ONES_ROWS, seq), BF16),
                        pltpu.VMEM((2, LANES), F32)],
        compiler_params=_cparams(2),
        name="dense_attn",
    )(q, k, v, cos, sin_signed, gq, gk)


def _cross_attn_kernel(q_ref, mk_ref, mv_ref, o_ref, *, tq):
    n_pairs = q_ref.shape[1] // LANES
    for j in range(n_pairs):
        cols = slice(j * LANES, (j + 1) * LANES)
        lhs = _stack_pair_heads(q_ref[:, cols].astype(F32) * (HEAD_DIM ** -0.5 * LOG2E))
        s = _nt_dot(lhs, mk_ref[:, cols])
        m = jnp.max(s, axis=-1, keepdims=True)
        p = jnp.exp2(s - m).astype(BF16)
        v_lo, v_hi = _v_with_ones(mv_ref[:, cols])
        o_lo = jnp.dot(p[:tq], v_lo, preferred_element_type=F32)
        o_hi = jnp.dot(p[tq:], v_hi, preferred_element_type=F32)
        o_ref[:, cols] = _normalise_pair(o_lo, o_hi).astype(o_ref.dtype)


def _cross_attn(q, mk, mv, batch, seq, tq):
    nq = seq // tq
    mem_len = mk.shape[0] // batch
    width = q.shape[1]
    return pl.pallas_call(
        functools.partial(_cross_attn_kernel, tq=tq),
        grid=(batch, nq),
        in_specs=[pl.BlockSpec((tq, width), lambda b, i: (b * nq + i, 0)),
                  pl.BlockSpec((mem_len, width), lambda b, i: (b, 0)),
                  pl.BlockSpec((mem_len, width), lambda b, i: (b, 0))],
        out_specs=pl.BlockSpec((tq, width), lambda b, i: (b * nq + i, 0)),
        out_shape=jax.ShapeDtypeStruct(q.shape, BF16),
        compiler_params=_cparams(2),
        name="cross_attn",
    )(q, mk, mv)


def _shift_column(lhs, kmax_lo, kmax_hi, bias_max_lo, bias_max_hi, floor=None):
    half = lhs.shape[0] // 2
    lf = lhs.astype(F32)
    q_sq = jnp.sum(lf * lf, axis=-1, keepdims=True)
    q_norm = q_sq * lax.rsqrt(jnp.maximum(q_sq, 1e-30))
    row = lax.broadcasted_iota(jnp.int32, lhs.shape, 0)
    lane = lax.broadcasted_iota(jnp.int32, lhs.shape, 1)
    shift = (q_norm * jnp.where(row < half, kmax_lo, kmax_hi)
             + jnp.where(row < half, bias_max_lo, bias_max_hi))
    if floor is not None:
        shift = jnp.maximum(shift, floor)
    shift = shift + jnp.abs(shift) * 2.0 ** -7
    return jnp.where(lane == 0, -shift, 0.0).astype(BF16)


def _pair_key_norm_max(k_pair):
    kf = k_pair.astype(F32)
    k2 = kf * kf
    lo = _lane_lo(k2.shape)
    n_lo = jnp.max(jnp.sum(jnp.where(lo, k2, 0.0), axis=-1, keepdims=True), axis=0, keepdims=True)
    n_hi = jnp.max(jnp.sum(jnp.where(lo, 0.0, k2), axis=-1, keepdims=True), axis=0, keepdims=True)
    return (jnp.broadcast_to(jnp.sqrt(n_lo), (1, LANES)),
            jnp.broadcast_to(jnp.sqrt(n_hi), (1, LANES)))


def _diff_attn_kernel(q1_ref, q2_ref, k1_ref, k2_ref, v_ref, bias_ref, bmax_ref, lam_ref, g_ref,
                      o_ref, vt_scr, ka_scr, kmax_scr, *, tq, n_blocks, lam_init):
    i = pl.program_id(2)
    sub = tq // BLOCK
    seq = k1_ref.shape[0]
    kc = 2 * BLOCK
    n_chunks = seq // kc
    k_refs = (k1_ref, k2_ref)

    @pl.when(i == 0)
    def _():
        for head in range(2):
            v_t = v_ref[:, head * C_V_DIM:(head + 1) * C_V_DIM].astype(F32).T
            vt_scr[head, :C_V_DIM, :] = v_t.astype(BF16)
            vt_scr[head, C_V_DIM:, :] = jnp.ones((ONES_ROWS, seq), BF16)
        lane = lax.broadcasted_iota(jnp.int32, (seq, LANES), 1)
        ones_col = jnp.where(lane == 0, 1.0, 0.0).astype(BF16)
        for t in range(2):
            ka_scr[t, :, :LANES] = k_refs[t][...]
            ka_scr[t, :, LANES:] = ones_col
            kmax_lo, kmax_hi = _pair_key_norm_max(k_refs[t][...])
            kmax_scr[t, 0:1, :] = kmax_lo
            kmax_scr[t, 1:2, :] = kmax_hi

    lam_vec = lam_ref[...]
    lam = (jnp.exp(jnp.sum(lam_vec[0:1] * lam_vec[1:2], axis=-1, keepdims=True))
           - jnp.exp(jnp.sum(lam_vec[2:3] * lam_vec[3:4], axis=-1, keepdims=True)) + lam_init)
    lhs = [_stack_pair_heads(q_ref[...].astype(F32) * (HEAD_DIM ** -0.5 * LOG2E))
           for q_ref in (q1_ref, q2_ref)]

    def bias_chunk(c):
        return jnp.concatenate(
            [jnp.concatenate([bias_ref[head, (n_blocks - 1) - (i * sub + u) + 2 * c + e]
                              for head in range(2) for u in range(sub)], axis=1)
             for e in range(2)], axis=0)

    def shifted_scores(t):
        aug = _shift_column(lhs[t], kmax_scr[t, 0:1, :], kmax_scr[t, 1:2, :],
                            bmax_ref[0], bmax_ref[1])
        return _nt_dot(ka_scr[t], jnp.concatenate([lhs[t], aug], axis=1))

    def pv_of_shifted(s_all):
        p = jnp.concatenate([jnp.exp2(s_all[c * kc:(c + 1) * kc, :] + bias_chunk(c)).astype(BF16)
                             for c in range(n_chunks)], axis=0)
        return [jnp.dot(vt_scr[head], p[:, head * tq:(head + 1) * tq],
                        preferred_element_type=F32) for head in range(2)]

    def pv_running_max(s_all):
        m = None
        acc = [None, None]
        for c in range(n_chunks):
            s = s_all[c * kc:(c + 1) * kc, :] + bias_chunk(c)
            m_c = jnp.max(s, axis=0, keepdims=True)
            m_new = m_c if m is None else jnp.maximum(m, m_c)
            p = jnp.exp2(s - m_new).astype(BF16)
            if m is not None:
                alpha = jnp.exp2(m - m_new)
            for head in range(2):
                cols = slice(head * tq, (head + 1) * tq)
                pv = jnp.dot(vt_scr[head, :, c * kc:(c + 1) * kc], p[:, cols],
                             preferred_element_type=F32)
                acc[head] = pv if m is None else acc[head] * alpha[:, cols] + pv
            m = m_new
        return acc

    def write(acc1, acc2):
        g = g_ref[...] * (1.0 - lam_init)
        for head in range(2):
            o1 = acc1[head][:C_V_DIM] * (1.0 / acc1[head][C_V_DIM:C_V_DIM + 1])
            o2 = acc2[head][:C_V_DIM] * (1.0 / acc2[head][C_V_DIM:C_V_DIM + 1])
            out = o1 - lam * o2
            ms = jnp.mean(out * out, axis=0, keepdims=True)
            y = out * lax.rsqrt(ms + EPS) * g
            o_ref[:, head * C_V_DIM:(head + 1) * C_V_DIM] = y.T.astype(o_ref.dtype)

    s1 = shifted_scores(0)
    s2 = shifted_scores(1)
    acc1 = pv_of_shifted(s1)
    acc2 = pv_of_shifted(s2)
    write(acc1, acc2)
    sums = jnp.concatenate([a[C_V_DIM:C_V_DIM + 1] for a in acc1 + acc2], axis=0)

    @pl.when(jnp.logical_not(jnp.min(sums) >= SUM_FLOOR))
    def _():
        write(pv_running_max(_nt_dot(k1_ref[...], lhs[0])),
              pv_running_max(_nt_dot(k2_ref[...], lhs[1])))


def _diff_attn(q1, q2, k1, k2, v, bias_t, bias_max, lam_vecs, subln_g, lam_init, batch, seq, tq):
    nq = seq // tq
    nb = seq // BLOCK
    n_pairs = C_HEADS // 2
    n_tiles = bias_t.shape[1]
    q_spec = pl.BlockSpec((tq, LANES), lambda b, p, i: (b * nq + i, p))
    k_spec = pl.BlockSpec((seq, LANES), lambda b, p, i: (b, p))
    g_cols = jnp.broadcast_to(subln_g.reshape(C_V_DIM, 1), (C_V_DIM, tq))
    return pl.pallas_call(
        functools.partial(_diff_attn_kernel, tq=tq, n_blocks=nb, lam_init=lam_init),
        grid=(batch, n_pairs, nq),
        in_specs=[q_spec, q_spec, k_spec, k_spec,
                  pl.BlockSpec((seq, 2 * C_V_DIM), lambda b, p, i: (b, p)),
                  pl.BlockSpec((2, n_tiles, BLOCK, BLOCK), lambda b, p, i: (p, 0, 0, 0)),
                  pl.BlockSpec((2, 1, LANES), lambda b, p, i: (p, 0, 0)),
                  pl.BlockSpec(lam_vecs.shape, lambda b, p, i: (0, 0)),
                  pl.BlockSpec((C_V_DIM, tq), lambda b, p, i: (0, 0))],
        out_specs=pl.BlockSpec((tq, 2 * C_V_DIM), lambda b, p, i: (b * nq + i, p)),
        out_shape=jax.ShapeDtypeStruct(v.shape, BF16),
        scratch_shapes=[pltpu.VMEM((2, C_V_DIM + ONES_ROWS, seq), BF16),
                        pltpu.VMEM((2, seq, 2 * LANES), BF16),
                        pltpu.VMEM((2, 2, LANES), F32)],
        compiler_params=_cparams(3),
        name="diff_attn",
    )(q1, q2, k1, k2, v, bias_t, bias_max, lam_vecs, g_cols)


def _rope_tables(seq):
    rows = seq // GRID_W
    row = jnp.broadcast_to(jnp.arange(rows)[:, None], (rows, GRID_W)).reshape(-1)
    col = jnp.broadcast_to(jnp.arange(GRID_W)[None, :], (rows, GRID_W)).reshape(-1)
    half = HEAD_DIM // 2
    inv = 1.0 / (ROPE_THETA ** (jnp.arange(0, half, 2, dtype=F32) / half))
    ang_row = row.astype(F32)[:, None] * inv
    ang_col = col.astype(F32)[:, None] * inv
    cos = jnp.concatenate([jnp.cos(ang_row)] * 2 + [jnp.cos(ang_col)] * 2, axis=-1)
    sin = jnp.concatenate([-jnp.sin(ang_row), jnp.sin(ang_row),
                           -jnp.sin(ang_col), jnp.sin(ang_col)], axis=-1)
    return jnp.tile(cos, (1, 2)), jnp.tile(sin, (1, 2))


def _gqa_cols():
    return np.concatenate([np.arange(h * HEAD_DIM, (h + 1) * HEAD_DIM) for h in GQA_HEAD_ORDER])


def kernel(x, mem, rel_bias, mem_norm, final_norm, even_norm, even_w_in, even_sink, even_q_norm, even_k_norm, even_w_mem_kv, even_w_out, odd_norm, odd_w_in, odd_lambda_q1, odd_lambda_k1, odd_lambda_q2, odd_lambda_k2, odd_subln, odd_w_mem_kv, odd_w_out):
    batch, seq, d = x.shape
    mem_len = mem.shape[1]
    tokens = batch * seq
    nb = seq // BLOCK
    xw = X_HEADS * HEAD_DIM
    gw = A_HEADS * HEAD_DIM

    gq = _gqa_cols()
    aq0, ak0, av0 = 0, gw, gw + 128
    bq0 = av0 + 128
    bk0, bv0 = bq0 + gw, bq0 + gw + 128
    xq0 = bv0 + 128
    gate0 = xq0 + xw
    even_cols = np.concatenate([aq0 + gq, np.arange(ak0, bq0), bq0 + gq, np.arange(bk0, gate0),
                                gate0 + gq, gate0 + gw + gq,
                                np.arange(gate0 + 2 * gw, gate0 + D_MODEL)])
    mix_rows = np.concatenate([gq, gw + gq, np.arange(2 * gw, D_MODEL)])
    w_in0 = even_w_in[0][:, even_cols].astype(BF16)
    w_out0 = even_w_out[0][mix_rows, :].astype(BF16)
    w_in1 = odd_w_in[0].astype(BF16)
    w_out1 = odd_w_out[0].astype(BF16)

    x2 = x.reshape(tokens, d)
    mem2 = mem.reshape(batch * mem_len, d)

    bias_full, bias_win = _bias_tiles(rel_bias, nb)
    cos, sin_signed = _rope_tables(seq)
    head_bias_max = jnp.max(rel_bias, axis=0) * LOG2E

    mk0, mv0 = _norm_proj(mem2, mem_norm, even_w_mem_kv[0].astype(BF16), (xw, xw), 512)
    aq, ak, av, bq, bk, bv, xq, gate = _norm_proj(
        x2, even_norm[0], w_in0, (gw, 128, 128, gw, 128, 128, xw, D_MODEL), 512)
    y_a = _window_attn(aq, ak, av, bias_win, head_bias_max, even_sink[0] * LOG2E, batch, seq, 4)
    y_b = _dense_attn(bq, bk, bv, cos, sin_signed,
                      jnp.tile(even_q_norm[0], 2).reshape(1, LANES),
                      jnp.tile(even_k_norm[0], 2).reshape(1, LANES), batch, seq, 256)
    y_x = _cross_attn(xq, mk0, mv0, batch, seq, 512)
    h1 = _gate_out([y_a, y_b, y_x], gate, w_out0, x2, None, 512)

    lam_init = 0.8 - 0.6 * math.exp(-0.3 * 1)
    mk1, mv1 = _norm_proj(mem2, mem_norm, odd_w_mem_kv[0].astype(BF16), (xw, xw), 512)
    cw = C_HEADS * HEAD_DIM
    q1, q2, k1, k2, v, xq1, gate1 = _norm_proj(
        h1, odd_norm[0], w_in1, (cw, cw, cw, cw, C_HEADS * C_V_DIM, xw, D_MODEL), 512)
    lam_vecs = jnp.stack([odd_lambda_q1[0], odd_lambda_k1[0], odd_lambda_q2[0], odd_lambda_k2[0]])
    bias_max = jnp.broadcast_to(head_bias_max.reshape(C_HEADS, 1, 1), (C_HEADS, 1, LANES))
    y_c = _diff_attn(q1, q2, k1, k2, v, bias_full, bias_max, lam_vecs, odd_subln[0], lam_init,
                     batch, seq, 256)
    y_x1 = _cross_attn(xq1, mk1, mv1, batch, seq, 512)
    out = _gate_out([y_c, y_x1], gate1, w_out1, h1, final_norm, 512)
    return out.reshape(batch, seq, d)
```

```python
import functools
import math

import numpy as np
import jax
import jax.numpy as jnp
from jax import lax
from jax.experimental import pallas as pl
from jax.experimental.pallas import tpu as pltpu

D_MODEL = 1024
HEAD_DIM = 64
BLOCK = 128
WINDOW = 128
GRID_W = 64
A_HEADS = 6
B_HEADS = 6
C_HEADS = 6
C_V_DIM = 128
X_HEADS = 4
REL_BUCKETS = 32
REL_MAX_DIST = 128
ROPE_THETA = 10000.0
EPS = 1e-6
NEG_INF = -1e30
LOG2E = math.log2(math.e)
LANES = 128
ONES_ROWS = 16
PROJ_SUB_ROWS = 256
TM_PROJ = 512
TQ_DENSE = 256
TQ_DIFF = 512
TQ_CROSS = 1024
WINDOW_GROUP = 4
SUM_FLOOR = 2.0 ** -64
VMEM_LIMIT = 56 * 1024 * 1024

F32 = jnp.float32
BF16 = jnp.bfloat16

GQA_HEAD_ORDER = (0, 3, 1, 4, 2, 5)


def _cparams(n_axes, flags=None):
    return pltpu.CompilerParams(dimension_semantics=("arbitrary",) * n_axes,
                                vmem_limit_bytes=VMEM_LIMIT, flags=flags)


def _lane_lo(shape):
    return lax.broadcasted_iota(jnp.int32, shape, len(shape) - 1) < HEAD_DIM


def _rmsnorm_rows(x, g):
    ms = jnp.mean(x * x, axis=-1, keepdims=True)
    return x * lax.rsqrt(ms + EPS) * g


def _nt_dot(a, b):
    return lax.dot_general(a, b, (((1,), (1,)), ((), ())), preferred_element_type=F32)


def _norm_proj_kernel(*refs, qk_rope):
    x_ref, g_ref, w_ref = refs[:3]
    if qk_rope is None:
        out_refs = refs[3:]
    else:
        cos_ref, sin_ref, gq_ref, gk_ref = refs[3:7]
        out_refs = refs[7:]
    tm = x_ref.shape[0]
    for r0 in range(0, tm, PROJ_SUB_ROWS):
        rows = slice(r0, min(r0 + PROJ_SUB_ROWS, tm))
        xn = _rmsnorm_rows(x_ref[rows, :], g_ref[...]).astype(BF16)
        y_all = jnp.dot(xn, w_ref[...], preferred_element_type=F32)
        c0 = 0
        for j, o_ref in enumerate(out_refs):
            width = o_ref.shape[1]
            y = y_all[:, c0:c0 + width]
            if qk_rope is not None and j in qk_rope:
                gain, scale = ((gq_ref, HEAD_DIM ** -0.5 * LOG2E) if j == qk_rope[0]
                               else (gk_ref, 1.0))
                y = jnp.concatenate(
                    [_norm_rope_pair(y[:, t * LANES:(t + 1) * LANES], gain[...],
                                     cos_ref[rows, :], sin_ref[rows, :]) * scale
                     for t in range(width // LANES)], axis=1)
            o_ref[rows, :] = y.astype(o_ref.dtype)
            c0 += width


def _norm_proj(x, g, w_bf16, splits, tm, qk_rope=None, rope_args=(), seq=None):
    rows, d = x.shape
    n = w_bf16.shape[1]
    assert sum(splits) == n and rows % tm == 0
    in_specs = [pl.BlockSpec((tm, d), lambda i: (i, 0)),
                pl.BlockSpec((1, d), lambda i: (0, 0)),
                pl.BlockSpec((d, n), lambda i: (0, 0))]
    if qk_rope is not None:
        pos_blocks = seq // tm
        in_specs += [pl.BlockSpec((tm, LANES), lambda i: (i % pos_blocks, 0)),
                     pl.BlockSpec((tm, LANES), lambda i: (i % pos_blocks, 0)),
                     pl.BlockSpec((1, LANES), lambda i: (0, 0)),
                     pl.BlockSpec((1, LANES), lambda i: (0, 0))]
    return pl.pallas_call(
        functools.partial(_norm_proj_kernel, qk_rope=qk_rope),
        grid=(rows // tm,),
        in_specs=in_specs,
        out_specs=[pl.BlockSpec((tm, s), lambda i: (i, 0)) for s in splits],
        out_shape=[jax.ShapeDtypeStruct((rows, s), BF16) for s in splits],
        compiler_params=_cparams(1),
        name="norm_proj",
    )(x, g.reshape(1, d), w_bf16, *rope_args)


def _gate_out_kernel(*refs, n_parts, final):
    y_refs = refs[:n_parts]
    gate_ref, w_ref, res_ref = refs[n_parts:n_parts + 3]
    o_ref = refs[-1]
    parts = []
    c0 = 0
    for y_ref in y_refs:
        width = y_ref.shape[1]
        g = gate_ref[:, c0:c0 + width].astype(F32)
        silu = g * (1.0 / (1.0 + jnp.exp(-g)))
        parts.append((y_ref[...].astype(F32) * silu).astype(BF16))
        c0 += width
    acc = res_ref[...] + jnp.dot(jnp.concatenate(parts, axis=1), w_ref[...],
                                 preferred_element_type=F32)
    if final:
        acc = _rmsnorm_rows(acc, refs[n_parts + 3][...])
    o_ref[...] = acc


def _gate_out(y_parts, gate, w_bf16, resid, final_g, tm):
    rows, d = resid.shape
    mix = w_bf16.shape[0]
    final = final_g is not None
    in_specs = [pl.BlockSpec((tm, y.shape[1]), lambda i: (i, 0)) for y in y_parts]
    in_specs += [pl.BlockSpec((tm, mix), lambda i: (i, 0)),
                 pl.BlockSpec((mix, d), lambda i: (0, 0)),
                 pl.BlockSpec((tm, d), lambda i: (i, 0))]
    args = list(y_parts) + [gate, w_bf16, resid]
    if final:
        in_specs.append(pl.BlockSpec((1, d), lambda i: (0, 0)))
        args.append(final_g.reshape(1, d))
    return pl.pallas_call(
        functools.partial(_gate_out_kernel, n_parts=len(y_parts), final=final),
        grid=(rows // tm,),
        in_specs=in_specs,
        out_specs=pl.BlockSpec((tm, d), lambda i: (i, 0)),
        out_shape=jax.ShapeDtypeStruct((rows, d), F32),
        compiler_params=_cparams(1),
        name="gate_out",
    )(*args)


def _bucket_thresholds():
    nb = REL_BUCKETS // 2
    max_exact = nb // 2
    n = np.arange(0, 4 * REL_MAX_DIST)
    nf = np.maximum(n, 1).astype(np.float32)
    large = max_exact + (np.log(nf / np.float32(max_exact))
                         / np.float32(math.log(REL_MAX_DIST / max_exact))
                         * np.float32(nb - max_exact)).astype(np.int32)
    bucket = np.where(n < max_exact, n, np.minimum(large, nb - 1))
    assert np.all(np.diff(bucket) >= 0) and bucket[0] == 0 and bucket[-1] == nb - 1
    thr = [int(np.argmax(bucket >= k)) for k in range(1, nb)]
    assert thr[-1] < REL_MAX_DIST
    return thr


def _bias_tiles_kernel(tab_ref, full_ref, win_ref, *, n_tiles, n_far):
    h = pl.program_id(0)
    thr = _bucket_thresholds()
    nb = REL_BUCKETS // 2
    row = lax.broadcasted_iota(jnp.int32, (BLOCK, BLOCK), 0)
    col = lax.broadcasted_iota(jnp.int32, (BLOCK, BLOCK), 1)
    def tile_of(rel):
        n = jnp.abs(rel)
        neg = jnp.full((BLOCK, BLOCK), tab_ref[0, h], F32)
        pos = jnp.full((BLOCK, BLOCK), tab_ref[nb, h], F32)
        for b in range(1, nb):
            ge = n >= thr[b - 1]
            neg = jnp.where(ge, tab_ref[b, h], neg)
            pos = jnp.where(ge, tab_ref[nb + b, h], pos)
        return jnp.where(rel > 0, pos, neg) * LOG2E

    for k in range(n_tiles):
        rel = (k - n_far) * BLOCK + row - col
        tile = tile_of(rel)
        full_ref[0, k] = tile
        j = k - (n_far - 1)
        if 0 <= j < 3:
            win_ref[j * BLOCK:(j + 1) * BLOCK, :] = jnp.where(jnp.abs(rel) <= WINDOW, tile, NEG_INF)


def _bias_tiles(rel_bias, n_blocks):
    n_far = n_blocks - 1
    n_tiles = 2 * n_blocks - 1
    n_heads = rel_bias.shape[1]
    return pl.pallas_call(
        functools.partial(_bias_tiles_kernel, n_tiles=n_tiles, n_far=n_far),
        grid=(n_heads,),
        in_specs=[pl.BlockSpec(memory_space=pltpu.SMEM)],
        out_specs=[pl.BlockSpec((1, n_tiles, BLOCK, BLOCK), lambda h: (h, 0, 0, 0)),
                   pl.BlockSpec((3 * BLOCK, BLOCK), lambda h: (0, h))],
        out_shape=[jax.ShapeDtypeStruct((n_heads, n_tiles, BLOCK, BLOCK), F32),
                   jax.ShapeDtypeStruct((3 * BLOCK, n_heads * BLOCK), F32)],
        compiler_params=_cparams(1),
        name="bias_tiles",
    )(rel_bias)


def _stack_gqa_heads(q_tiles):
    lo = _lane_lo(q_tiles[0].shape)
    return jnp.concatenate([jnp.where(lo, q, 0.0).astype(BF16) for q in q_tiles]
                           + [jnp.where(lo, 0.0, q).astype(BF16) for q in q_tiles], axis=0)


def _stack_pair_heads(q):
    lo = _lane_lo(q.shape)
    return jnp.concatenate([jnp.where(lo, q, 0.0).astype(BF16),
                            jnp.where(lo, 0.0, q).astype(BF16)], axis=0)


def _v_with_ones(v_pair):
    lane = lax.broadcasted_iota(jnp.int32, v_pair.shape, 1)
    vf = v_pair.astype(F32)
    v_lo = jnp.where(lane < HEAD_DIM, vf, jnp.where(lane == HEAD_DIM, 1.0, 0.0))
    v_hi = jnp.where(lane >= HEAD_DIM, vf, jnp.where(lane == 0, 1.0, 0.0))
    return v_lo.astype(BF16), v_hi.astype(BF16)


def _normalise_pair(o_lo, o_hi, extra_lo=None, extra_hi=None):
    lane = lax.broadcasted_iota(jnp.int32, o_lo.shape, 1)
    l_lo = jnp.sum(jnp.where(lane == HEAD_DIM, o_lo, 0.0), axis=-1, keepdims=True)
    l_hi = jnp.sum(jnp.where(lane == 0, o_hi, 0.0), axis=-1, keepdims=True)
    if extra_lo is not None:
        l_lo = l_lo + extra_lo
        l_hi = l_hi + extra_hi
    return jnp.where(lane < HEAD_DIM, o_lo * (1.0 / l_lo), o_hi * (1.0 / l_hi))


def _window_attn_kernel(q_ref, k_ref, v_ref, bias_ref, bmax_ref, sink_col_ref, sink_row_ref,
                        o_ref, kpad_scr, vtpad_scr, kmax_scr, *, n_blocks, group):
    i = pl.program_id(1)
    seq = k_ref.shape[0]
    half = 3 * BLOCK

    @pl.when(i == 0)
    def _():
        lane = lax.broadcasted_iota(jnp.int32, (seq, LANES), 1)
        k_aug = jnp.concatenate([k_ref[...], jnp.where(lane == 0, 1.0, 0.0).astype(BF16)], axis=1)
        zero_blk = jnp.zeros((1, BLOCK, 2 * LANES), BF16)
        kpad_scr[0:1] = zero_blk
        kpad_scr[n_blocks + 1:n_blocks + 2] = zero_blk
        kpad_scr[1:n_blocks + 1] = k_aug.reshape(n_blocks, BLOCK, 2 * LANES)
        kmax_lo, kmax_hi = _pair_key_norm_max(k_ref[...])
        kmax_scr[0:1, :] = kmax_lo
        kmax_scr[1:2, :] = kmax_hi
        v_t = v_ref[...].astype(F32).T.astype(BF16)
        ones = jnp.ones((ONES_ROWS, BLOCK), BF16)
        zero_v = jnp.zeros((HEAD_DIM + ONES_ROWS, BLOCK), BF16)
        for g in range(2):
            vtpad_scr[g, 0] = zero_v
            vtpad_scr[g, n_blocks + 1] = zero_v
            for blk in range(n_blocks):
                vtpad_scr[g, blk + 1, :HEAD_DIM, :] = v_t[g * HEAD_DIM:(g + 1) * HEAD_DIM,
                                                           blk * BLOCK:(blk + 1) * BLOCK]
                vtpad_scr[g, blk + 1, HEAD_DIM:, :] = ones

    probe = (lax.broadcasted_iota(jnp.int32, (8, 2 * LANES), 1) == LANES).astype(BF16)
    sink_row = sink_row_ref[...]

    def query_block(u, exact):
        n = i * group + u
        qf = q_ref[u * BLOCK:(u + 1) * BLOCK, :].astype(F32) * (HEAD_DIM ** -0.5 * LOG2E)
        lhs = _stack_gqa_heads([qf[:, j * LANES:(j + 1) * LANES] for j in range(3)])
        k_win = kpad_scr[pl.ds(n, 3)].reshape(3 * BLOCK, 2 * LANES)
        if exact:
            s = _nt_dot(k_win[:, :LANES], lhs)
        else:
            aug = _shift_column(lhs, kmax_scr[0:1, :], kmax_scr[1:2, :], bmax_ref[...], bmax_ref[...],
                                floor=sink_col_ref[...])
            lhs_aug = jnp.concatenate([lhs, aug], axis=1)
            s = _nt_dot(k_win, lhs_aug)
            neg_shift = _nt_dot(probe, lhs_aug)[0:1]
        s = s + bias_ref[...]
        s = jnp.concatenate([jnp.where(n > 0, s[:BLOCK], NEG_INF), s[BLOCK:2 * BLOCK],
                             jnp.where(n < n_blocks - 1, s[2 * BLOCK:], NEG_INF)], axis=0)
        if exact:
            m = jnp.maximum(jnp.max(s, axis=0, keepdims=True), sink_row)
            p = jnp.exp2(s - m).astype(BF16)
            p_sink = jnp.exp2(sink_row - m)
        else:
            p = jnp.exp2(s).astype(BF16)
            p_sink = jnp.exp2(sink_row + neg_shift)
        outs, sums = [], []
        for g in range(2):
            acc = None
            for j in range(3):
                pv = jnp.dot(vtpad_scr[g, n + j], p[j * BLOCK:(j + 1) * BLOCK, g * half:(g + 1) * half],
                             preferred_element_type=F32)
                acc = pv if acc is None else acc + pv
            total = acc[HEAD_DIM:HEAD_DIM + 1] + p_sink[:, g * half:(g + 1) * half]
            outs.append(acc[:HEAD_DIM] * (1.0 / total))
            sums.append(total)
        for j in range(3):
            cols = slice(j * BLOCK, (j + 1) * BLOCK)
            pair_t = jnp.concatenate([outs[0][:, cols], outs[1][:, cols]], axis=0)
            o_ref[u * BLOCK:(u + 1) * BLOCK, j * LANES:(j + 1) * LANES] = pair_t.T.astype(o_ref.dtype)
        return sums

    sums = [s for u in range(group) for s in query_block(u, exact=False)]

    @pl.when(jnp.logical_not(jnp.min(jnp.concatenate(sums, axis=0)) >= SUM_FLOOR))
    def _():
        for u in range(group):
            query_block(u, exact=True)


def _window_attn(q, k, v, bias_win_t, bias_max, sink, batch, seq, group):
    nb = seq // BLOCK
    steps = nb // group
    rows = A_HEADS * BLOCK
    bmax_col = jnp.broadcast_to(jnp.repeat(bias_max, BLOCK).reshape(rows, 1), (rows, LANES))
    sink_col = jnp.broadcast_to(jnp.repeat(sink, BLOCK).reshape(rows, 1), (rows, LANES))
    sink_row = jnp.repeat(sink, BLOCK).reshape(1, rows)
    const = lambda shape: pl.BlockSpec(shape, lambda b, i: (0,) * len(shape))
    return pl.pallas_call(
        functools.partial(_window_attn_kernel, n_blocks=nb, group=group),
        grid=(batch, steps),
        in_specs=[pl.BlockSpec((group * BLOCK, 3 * LANES), lambda b, i: (b * steps + i, 0)),
                  pl.BlockSpec((seq, LANES), lambda b, i: (b, 0)),
                  pl.BlockSpec((seq, LANES), lambda b, i: (b, 0)),
                  const(bias_win_t.shape), const((rows, LANES)), const((rows, LANES)),
                  const((1, rows))],
        out_specs=pl.BlockSpec((group * BLOCK, 3 * LANES), lambda b, i: (b * steps + i, 0)),
        out_shape=jax.ShapeDtypeStruct(q.shape, BF16),
        scratch_shapes=[pltpu.VMEM((nb + 2, BLOCK, 2 * LANES), BF16),
                        pltpu.VMEM((2, nb + 2, HEAD_DIM + ONES_ROWS, BLOCK), BF16),
                        pltpu.VMEM((2, LANES), F32)],
        compiler_params=_cparams(2),
        name="window_attn",
    )(q, k, v, bias_win_t, bmax_col, sink_col, sink_row)


def _norm_rope_pair(x, g, cos, sin_signed):
    lo = _lane_lo(x.shape)
    x2 = x * x
    ss_lo = jnp.sum(jnp.where(lo, x2, 0.0), axis=-1, keepdims=True)
    ss_hi = jnp.sum(jnp.where(lo, 0.0, x2), axis=-1, keepdims=True)
    ms = jnp.where(lo, ss_lo, ss_hi) * (1.0 / HEAD_DIM)
    y = x * lax.rsqrt(ms + EPS) * g
    lane = lax.broadcasted_iota(jnp.int32, x.shape, 1)
    quarter = HEAD_DIM // 4
    first = (lane & quarter) == 0
    partner = jnp.where(first, pltpu.roll(y, LANES - quarter, 1), pltpu.roll(y, quarter, 1))
    return y * cos + partner * sin_signed


def _dense_attn_kernel(q_ref, k_ref, v_ref, o_ref, ka_scr, vt_scr, kmax_scr, *, tq):
    i = pl.program_id(1)
    seq = k_ref.shape[0]
    kc = 2 * BLOCK
    n_chunks = seq // kc
    grp = 3 * tq

    @pl.when(i == 0)
    def _():
        lane = lax.broadcasted_iota(jnp.int32, (seq, LANES), 1)
        ka_scr[:, :LANES] = k_ref[...]
        ka_scr[:, LANES:] = jnp.where(lane == 0, 1.0, 0.0).astype(BF16)
        kmax_lo, kmax_hi = _pair_key_norm_max(k_ref[...])
        kmax_scr[0:1, :] = kmax_lo
        kmax_scr[1:2, :] = kmax_hi
        v_t = v_ref[...].astype(F32).T.astype(BF16)
        for g in range(2):
            vt_scr[g, :HEAD_DIM, :] = v_t[g * HEAD_DIM:(g + 1) * HEAD_DIM]
            vt_scr[g, HEAD_DIM:, :] = jnp.ones((ONES_ROWS, seq), BF16)

    lhs = _stack_gqa_heads([q_ref[:, j * LANES:(j + 1) * LANES].astype(F32)
                            for j in range(3)])

    def write(acc):
        outs = [a[:HEAD_DIM] * (1.0 / a[HEAD_DIM:HEAD_DIM + 1]) for a in acc]
        for j in range(3):
            cols = slice(j * tq, (j + 1) * tq)
            pair_t = jnp.concatenate([outs[0][:, cols], outs[1][:, cols]], axis=0)
            o_ref[:, j * LANES:(j + 1) * LANES] = pair_t.T.astype(o_ref.dtype)

    aug = _shift_column(lhs, kmax_scr[0:1, :], kmax_scr[1:2, :], 0.0, 0.0)
    s_all = _nt_dot(ka_scr[...], jnp.concatenate([lhs, aug], axis=1))
    p = jnp.exp2(s_all).astype(BF16)
    acc = [jnp.dot(vt_scr[g], p[:, g * grp:(g + 1) * grp], preferred_element_type=F32)
           for g in range(2)]
    write(acc)
    sums = jnp.concatenate([a[HEAD_DIM:HEAD_DIM + 1] for a in acc], axis=0)

    @pl.when(jnp.logical_not(jnp.min(sums) >= SUM_FLOOR))
    def _():
        s_exact = _nt_dot(ka_scr[:, :LANES], lhs)
        m = None
        acc2 = [None, None]
        for c in range(n_chunks):
            s = s_exact[c * kc:(c + 1) * kc, :]
            m_c = jnp.max(s, axis=0, keepdims=True)
            m_new = m_c if m is None else jnp.maximum(m, m_c)
            pc = jnp.exp2(s - m_new).astype(BF16)
            if m is not None:
                alpha = jnp.exp2(m - m_new)
            for g in range(2):
                cols = slice(g * grp, (g + 1) * grp)
                pv = jnp.dot(vt_scr[g, :, c * kc:(c + 1) * kc], pc[:, cols],
                             preferred_element_type=F32)
                acc2[g] = pv if m is None else acc2[g] * alpha[:, cols] + pv
            m = m_new
        write(acc2)


def _dense_attn(q, k, v, batch, seq, tq):
    nq = seq // tq
    return pl.pallas_call(
        functools.partial(_dense_attn_kernel, tq=tq),
        grid=(batch, nq),
        in_specs=[pl.BlockSpec((tq, 3 * LANES), lambda b, i: (b * nq + i, 0)),
                  pl.BlockSpec((seq, LANES), lambda b, i: (b, 0)),
                  pl.BlockSpec((seq, LANES), lambda b, i: (b, 0))],
        out_specs=pl.BlockSpec((tq, 3 * LANES), lambda b, i: (b * nq + i, 0)),
        out_shape=jax.ShapeDtypeStruct(q.shape, BF16),
        scratch_shapes=[pltpu.VMEM((seq, 2 * LANES), BF16),
                        pltpu.VMEM((2, HEAD_DIM + ONES_ROWS, seq), BF16),
                        pltpu.VMEM((2, LANES), F32)],
        compiler_params=_cparams(2),
        name="dense_attn",
    )(q, k, v)


def _cross_attn_kernel(q_ref, mk_ref, mv_ref, o_ref, ka_scr, vt_scr, kmax_scr, *, tq):
    i = pl.program_id(1)
    n_pairs = q_ref.shape[1] // LANES
    mem_len = mk_ref.shape[0]

    @pl.when(i == 0)
    def _():
        lane = lax.broadcasted_iota(jnp.int32, (mem_len, LANES), 1)
        ones_col = jnp.where(lane == 0, 1.0, 0.0).astype(BF16)
        v_t = mv_ref[...].astype(F32).T.astype(BF16)
        for j in range(n_pairs):
            k_pair = mk_ref[:, j * LANES:(j + 1) * LANES]
            ka_scr[j, :, :LANES] = k_pair
            ka_scr[j, :, LANES:] = ones_col
            kmax_lo, kmax_hi = _pair_key_norm_max(k_pair)
            kmax_scr[j, 0:1, :] = kmax_lo
            kmax_scr[j, 1:2, :] = kmax_hi
        for h in range(2 * n_pairs):
            vt_scr[h, :HEAD_DIM, :] = v_t[h * HEAD_DIM:(h + 1) * HEAD_DIM]
            vt_scr[h, HEAD_DIM:, :] = jnp.ones((ONES_ROWS, mem_len), BF16)

    lhs = [_stack_pair_heads(q_ref[:, j * LANES:(j + 1) * LANES].astype(F32)
                             * (HEAD_DIM ** -0.5 * LOG2E)) for j in range(n_pairs)]

    def attend(j, exact):
        if exact:
            s = _nt_dot(ka_scr[j, :, :LANES], lhs[j])
            s = s - jnp.max(s, axis=0, keepdims=True)
        else:
            aug = _shift_column(lhs[j], kmax_scr[j, 0:1, :], kmax_scr[j, 1:2, :], 0.0, 0.0)
            s = _nt_dot(ka_scr[j], jnp.concatenate([lhs[j], aug], axis=1))
        p = jnp.exp2(s).astype(BF16)
        acc = [jnp.dot(vt_scr[2 * j + h], p[:, h * tq:(h + 1) * tq], preferred_element_type=F32)
               for h in range(2)]
        pair_t = jnp.concatenate([a[:HEAD_DIM] * (1.0 / a[HEAD_DIM:HEAD_DIM + 1]) for a in acc],
                                 axis=0)
        o_ref[:, j * LANES:(j + 1) * LANES] = pair_t.T.astype(o_ref.dtype)
        return [a[HEAD_DIM:HEAD_DIM + 1] for a in acc]

    sums = [s for j in range(n_pairs) for s in attend(j, exact=False)]

    @pl.when(jnp.logical_not(jnp.min(jnp.concatenate(sums, axis=0)) >= SUM_FLOOR))
    def _():
        for j in range(n_pairs):
            attend(j, exact=True)


def _cross_attn(q, mk, mv, batch, seq, tq):
    nq = seq // tq
    mem_len = mk.shape[0] // batch
    width = q.shape[1]
    n_pairs = width // LANES
    return pl.pallas_call(
        functools.partial(_cross_attn_kernel, tq=tq),
        grid=(batch, nq),
        in_specs=[pl.BlockSpec((tq, width), lambda b, i: (b * nq + i, 0)),
                  pl.BlockSpec((mem_len, width), lambda b, i: (b, 0)),
                  pl.BlockSpec((mem_len, width), lambda b, i: (b, 0))],
        out_specs=pl.BlockSpec((tq, width), lambda b, i: (b * nq + i, 0)),
        out_shape=jax.ShapeDtypeStruct(q.shape, BF16),
        scratch_shapes=[pltpu.VMEM((n_pairs, mem_len, 2 * LANES), BF16),
                        pltpu.VMEM((2 * n_pairs, HEAD_DIM + ONES_ROWS, mem_len), BF16),
                        pltpu.VMEM((n_pairs, 2, LANES), F32)],
        compiler_params=_cparams(2),
        name="cross_attn",
    )(q, mk, mv)


def _shift_column(lhs, kmax_lo, kmax_hi, bias_max_lo, bias_max_hi, floor=None):
    half = lhs.shape[0] // 2
    lf = lhs.astype(F32)
    q_sq = jnp.sum(lf * lf, axis=-1, keepdims=True)
    q_norm = q_sq * lax.rsqrt(jnp.maximum(q_sq, 1e-30))
    row = lax.broadcasted_iota(jnp.int32, lhs.shape, 0)
    lane = lax.broadcasted_iota(jnp.int32, lhs.shape, 1)
    shift = (q_norm * jnp.where(row < half, kmax_lo, kmax_hi)
             + jnp.where(row < half, bias_max_lo, bias_max_hi))
    if floor is not None:
        shift = jnp.maximum(shift, floor)
    shift = shift + jnp.abs(shift) * 2.0 ** -7
    return jnp.where(lane == 0, -shift, 0.0).astype(BF16)


def _pair_key_norm_max(k_pair):
    kf = k_pair.astype(F32)
    k2 = kf * kf
    lo = _lane_lo(k2.shape)
    n_lo = jnp.max(jnp.sum(jnp.where(lo, k2, 0.0), axis=-1, keepdims=True), axis=0, keepdims=True)
    n_hi = jnp.max(jnp.sum(jnp.where(lo, 0.0, k2), axis=-1, keepdims=True), axis=0, keepdims=True)
    return (jnp.broadcast_to(jnp.sqrt(n_lo), (1, LANES)),
            jnp.broadcast_to(jnp.sqrt(n_hi), (1, LANES)))


def _diff_attn_kernel(q1_ref, q2_ref, k1_ref, k2_ref, v_ref, bias_ref, bmax_ref, lam_ref, g_ref,
                      o_ref, vt_scr, ka_scr, kmax_scr, *, tq, n_blocks, lam_init):
    i = pl.program_id(2)
    sub = tq // BLOCK
    seq = k1_ref.shape[0]
    kc = 2 * BLOCK
    n_chunks = seq // kc
    k_refs = (k1_ref, k2_ref)

    @pl.when(i == 0)
    def _():
        for head in range(2):
            v_t = v_ref[:, head * C_V_DIM:(head + 1) * C_V_DIM].astype(F32).T
            vt_scr[head, :C_V_DIM, :] = v_t.astype(BF16)
            vt_scr[head, C_V_DIM:, :] = jnp.ones((ONES_ROWS, seq), BF16)
        lane = lax.broadcasted_iota(jnp.int32, (seq, LANES), 1)
        ones_col = jnp.where(lane == 0, 1.0, 0.0).astype(BF16)
        for t in range(2):
            ka_scr[t, :, :LANES] = k_refs[t][...]
            ka_scr[t, :, LANES:] = ones_col
            kmax_lo, kmax_hi = _pair_key_norm_max(k_refs[t][...])
            kmax_scr[t, 0:1, :] = kmax_lo
            kmax_scr[t, 1:2, :] = kmax_hi

    lam_vec = lam_ref[...]
    lam = (jnp.exp(jnp.sum(lam_vec[0:1] * lam_vec[1:2], axis=-1, keepdims=True))
           - jnp.exp(jnp.sum(lam_vec[2:3] * lam_vec[3:4], axis=-1, keepdims=True)) + lam_init)
    lhs = [_stack_pair_heads(q_ref[...].astype(F32) * (HEAD_DIM ** -0.5 * LOG2E))
           for q_ref in (q1_ref, q2_ref)]

    def bias_chunk(c):
        return jnp.concatenate(
            [jnp.concatenate([bias_ref[head, (n_blocks - 1) - (i * sub + u) + 2 * c + e]
                              for head in range(2) for u in range(sub)], axis=1)
             for e in range(2)], axis=0)

    def shifted_scores(t):
        aug = _shift_column(lhs[t], kmax_scr[t, 0:1, :], kmax_scr[t, 1:2, :],
                            bmax_ref[0], bmax_ref[1])
        return _nt_dot(ka_scr[t], jnp.concatenate([lhs[t], aug], axis=1))

    def pv_of_shifted(s_all):
        p = jnp.concatenate([jnp.exp2(s_all[c * kc:(c + 1) * kc, :] + bias_chunk(c)).astype(BF16)
                             for c in range(n_chunks)], axis=0)
        return [jnp.dot(vt_scr[head], p[:, head * tq:(head + 1) * tq],
                        preferred_element_type=F32) for head in range(2)]

    def pv_running_max(s_all):
        m = None
        acc = [None, None]
        for c in range(n_chunks):
            s = s_all[c * kc:(c + 1) * kc, :] + bias_chunk(c)
            m_c = jnp.max(s, axis=0, keepdims=True)
            m_new = m_c if m is None else jnp.maximum(m, m_c)
            p = jnp.exp2(s - m_new).astype(BF16)
            if m is not None:
                alpha = jnp.exp2(m - m_new)
            for head in range(2):
                cols = slice(head * tq, (head + 1) * tq)
                pv = jnp.dot(vt_scr[head, :, c * kc:(c + 1) * kc], p[:, cols],
                             preferred_element_type=F32)
                acc[head] = pv if m is None else acc[head] * alpha[:, cols] + pv
            m = m_new
        return acc

    def write(acc1, acc2):
        g = g_ref[...] * (1.0 - lam_init)
        for head in range(2):
            o1 = acc1[head][:C_V_DIM] * (1.0 / acc1[head][C_V_DIM:C_V_DIM + 1])
            o2 = acc2[head][:C_V_DIM] * (1.0 / acc2[head][C_V_DIM:C_V_DIM + 1])
            out = o1 - lam * o2
            ms = jnp.mean(out * out, axis=0, keepdims=True)
            y = out * lax.rsqrt(ms + EPS) * g
            o_ref[:, head * C_V_DIM:(head + 1) * C_V_DIM] = y.T.astype(o_ref.dtype)

    s1 = shifted_scores(0)
    s2 = shifted_scores(1)
    acc1 = pv_of_shifted(s1)
    acc2 = pv_of_shifted(s2)
    write(acc1, acc2)
    sums = jnp.concatenate([a[C_V_DIM:C_V_DIM + 1] for a in acc1 + acc2], axis=0)

    @pl.when(jnp.logical_not(jnp.min(sums) >= SUM_FLOOR))
    def _():
        write(pv_running_max(_nt_dot(k1_ref[...], lhs[0])),
              pv_running_max(_nt_dot(k2_ref[...], lhs[1])))


def _diff_attn(q1, q2, k1, k2, v, bias_t, bias_max, lam_vecs, subln_g, lam_init, batch, seq, tq):
    nq = seq // tq
    nb = seq // BLOCK
    n_pairs = C_HEADS // 2
    n_tiles = bias_t.shape[1]
    q_spec = pl.BlockSpec((tq, LANES), lambda b, p, i: (b * nq + i, p))
    k_spec = pl.BlockSpec((seq, LANES), lambda b, p, i: (b, p))
    g_cols = jnp.broadcast_to(subln_g.reshape(C_V_DIM, 1), (C_V_DIM, tq))
    return pl.pallas_call(
        functools.partial(_diff_attn_kernel, tq=tq, n_blocks=nb, lam_init=lam_init),
        grid=(batch, n_pairs, nq),
        in_specs=[q_spec, q_spec, k_spec, k_spec,
                  pl.BlockSpec((seq, 2 * C_V_DIM), lambda b, p, i: (b, p)),
                  pl.BlockSpec((2, n_tiles, BLOCK, BLOCK), lambda b, p, i: (p, 0, 0, 0)),
                  pl.BlockSpec((2, 1, LANES), lambda b, p, i: (p, 0, 0)),
                  pl.BlockSpec(lam_vecs.shape, lambda b, p, i: (0, 0)),
                  pl.BlockSpec((C_V_DIM, tq), lambda b, p, i: (0, 0))],
        out_specs=pl.BlockSpec((tq, 2 * C_V_DIM), lambda b, p, i: (b * nq + i, p)),
        out_shape=jax.ShapeDtypeStruct(v.shape, BF16),
        scratch_shapes=[pltpu.VMEM((2, C_V_DIM + ONES_ROWS, seq), BF16),
                        pltpu.VMEM((2, seq, 2 * LANES), BF16),
                        pltpu.VMEM((2, 2, LANES), F32)],
        compiler_params=_cparams(3),
        name="diff_attn",
    )(q1, q2, k1, k2, v, bias_t, bias_max, lam_vecs, g_cols)


def _rope_tables(seq):
    rows = seq // GRID_W
    row = jnp.broadcast_to(jnp.arange(rows)[:, None], (rows, GRID_W)).reshape(-1)
    col = jnp.broadcast_to(jnp.arange(GRID_W)[None, :], (rows, GRID_W)).reshape(-1)
    half = HEAD_DIM // 2
    inv = 1.0 / (ROPE_THETA ** (jnp.arange(0, half, 2, dtype=F32) / half))
    ang_row = row.astype(F32)[:, None] * inv
    ang_col = col.astype(F32)[:, None] * inv
    cos = jnp.concatenate([jnp.cos(ang_row)] * 2 + [jnp.cos(ang_col)] * 2, axis=-1)
    sin = jnp.concatenate([-jnp.sin(ang_row), jnp.sin(ang_row),
                           -jnp.sin(ang_col), jnp.sin(ang_col)], axis=-1)
    return jnp.tile(cos, (1, 2)), jnp.tile(sin, (1, 2))


def _gqa_cols():
    return np.concatenate([np.arange(h * HEAD_DIM, (h + 1) * HEAD_DIM) for h in GQA_HEAD_ORDER])


def kernel(x, mem, rel_bias, mem_norm, final_norm, even_norm, even_w_in, even_sink, even_q_norm, even_k_norm, even_w_mem_kv, even_w_out, odd_norm, odd_w_in, odd_lambda_q1, odd_lambda_k1, odd_lambda_q2, odd_lambda_k2, odd_subln, odd_w_mem_kv, odd_w_out):
    batch, seq, d = x.shape
    mem_len = mem.shape[1]
    tokens = batch * seq
    nb = seq // BLOCK
    xw = X_HEADS * HEAD_DIM
    gw = A_HEADS * HEAD_DIM

    gq = _gqa_cols()
    aq0, ak0, av0 = 0, gw, gw + 128
    bq0 = av0 + 128
    bk0, bv0 = bq0 + gw, bq0 + gw + 128
    xq0 = bv0 + 128
    gate0 = xq0 + xw
    even_cols = np.concatenate([aq0 + gq, np.arange(ak0, bq0), bq0 + gq, np.arange(bk0, gate0),
                                gate0 + gq, gate0 + gw + gq,
                                np.arange(gate0 + 2 * gw, gate0 + D_MODEL)])
    mix_rows = np.concatenate([gq, gw + gq, np.arange(2 * gw, D_MODEL)])
    w_in0 = even_w_in[0][:, even_cols].astype(BF16)
    w_out0 = even_w_out[0][mix_rows, :].astype(BF16)
    w_in1 = odd_w_in[0].astype(BF16)
    w_out1 = odd_w_out[0].astype(BF16)

    x2 = x.reshape(tokens, d)
    mem2 = mem.reshape(batch * mem_len, d)

    bias_full, bias_win = _bias_tiles(rel_bias, nb)
    cos, sin_signed = _rope_tables(seq)
    head_bias_max = jnp.max(rel_bias, axis=0) * LOG2E

    w_mem = jnp.concatenate([even_w_mem_kv[0], odd_w_mem_kv[0]], axis=1).astype(BF16)
    mk0, mv0, mk1, mv1 = _norm_proj(mem2, mem_norm, w_mem, (xw,) * 4, TM_PROJ)

    rope_args = (cos, sin_signed, jnp.tile(even_q_norm[0], 2).reshape(1, LANES),
                 jnp.tile(even_k_norm[0], 2).reshape(1, LANES))
    aq, ak, av, bq, bk, bv, xq, gate = _norm_proj(
        x2, even_norm[0], w_in0, (gw, 128, 128, gw, 128, 128, xw, D_MODEL), TM_PROJ,
        qk_rope=(3, 4), rope_args=rope_args, seq=seq)
    y_a = _window_attn(aq, ak, av, bias_win, head_bias_max, even_sink[0] * LOG2E, batch, seq,
                       WINDOW_GROUP)
    y_b = _dense_attn(bq, bk, bv, batch, seq, TQ_DENSE)
    y_x = _cross_attn(xq, mk0, mv0, batch, seq, TQ_CROSS)
    h1 = _gate_out([y_a, y_b, y_x], gate, w_out0, x2, None, TM_PROJ)

    lam_init = 0.8 - 0.6 * math.exp(-0.3 * 1)
    cw = C_HEADS * HEAD_DIM
    q1, q2, k1, k2, v, xq1, gate1 = _norm_proj(
        h1, odd_norm[0], w_in1, (cw, cw, cw, cw, C_HEADS * C_V_DIM, xw, D_MODEL), TM_PROJ)
    lam_vecs = jnp.stack([odd_lambda_q1[0], odd_lambda_k1[0], odd_lambda_q2[0], odd_lambda_k2[0]])
    bias_max = jnp.broadcast_to(head_bias_max.reshape(C_HEADS, 1, 1), (C_HEADS, 1, LANES))
    y_c = _diff_attn(q1, q2, k1, k2, v, bias_full, bias_max, lam_vecs, odd_subln[0], lam_init,
                     batch, seq, TQ_DIFF)
    y_x1 = _cross_attn(xq1, mk1, mv1, batch, seq, TQ_CROSS)
    out = _gate_out([y_c, y_x1], gate1, w_out1, h1, final_norm, TM_PROJ)
    return out.reshape(batch, seq, d)
```

```python
import functools
import math

import numpy as np
import jax
import jax.numpy as jnp
from jax import lax
from jax.experimental import pallas as pl
from jax.experimental.pallas import tpu as pltpu

D_MODEL = 1024
HEAD_DIM = 64
BLOCK = 128
WINDOW = 128
GRID_W = 64
A_HEADS = 6
B_HEADS = 6
C_HEADS = 6
C_V_DIM = 128
X_HEADS = 4
REL_BUCKETS = 32
REL_MAX_DIST = 128
ROPE_THETA = 10000.0
EPS = 1e-6
NEG_INF = -1e30
LOG2E = math.log2(math.e)
LANES = 128
ONES_ROWS = 16
PROJ_SUB_ROWS = 256
TM_PROJ = 1024
TQ_DENSE = 256
TQ_DIFF = 512
TQ_CROSS = 2048
WINDOW_GROUP = 8
SUM_FLOOR = 2.0 ** -64
VMEM_LIMIT = 56 * 1024 * 1024

F32 = jnp.float32
BF16 = jnp.bfloat16

GQA_HEAD_ORDER = (0, 3, 1, 4, 2, 5)


def _cparams(n_axes, flags=None):
    return pltpu.CompilerParams(dimension_semantics=("arbitrary",) * n_axes,
                                vmem_limit_bytes=VMEM_LIMIT, flags=flags)


def _lane_lo(shape):
    return lax.broadcasted_iota(jnp.int32, shape, len(shape) - 1) < HEAD_DIM


def _rmsnorm_rows(x, g):
    ms = jnp.mean(x * x, axis=-1, keepdims=True)
    return x * lax.rsqrt(ms + EPS) * g


def _nt_dot(a, b):
    return lax.dot_general(a, b, (((1,), (1,)), ((), ())), preferred_element_type=F32)


def _norm_proj_kernel(*refs, qk_rope):
    x_ref, g_ref, w_ref = refs[:3]
    if qk_rope is None:
        out_refs = refs[3:]
    else:
        cos_ref, sin_ref, gq_ref, gk_ref = refs[3:7]
        out_refs = refs[7:]
    tm = x_ref.shape[0]
    for r0 in range(0, tm, PROJ_SUB_ROWS):
        rows = slice(r0, min(r0 + PROJ_SUB_ROWS, tm))
        xn = _rmsnorm_rows(x_ref[rows, :], g_ref[...]).astype(BF16)
        y_all = jnp.dot(xn, w_ref[...], preferred_element_type=F32)
        c0 = 0
        for j, o_ref in enumerate(out_refs):
            width = o_ref.shape[1]
            y = y_all[:, c0:c0 + width]
            if qk_rope is not None and j in qk_rope:
                gain, scale = ((gq_ref, HEAD_DIM ** -0.5 * LOG2E) if j == qk_rope[0]
                               else (gk_ref, 1.0))
                y = jnp.concatenate(
                    [_norm_rope_pair(y[:, t * LANES:(t + 1) * LANES], gain[...],
                                     cos_ref[rows, :], sin_ref[rows, :]) * scale
                     for t in range(width // LANES)], axis=1)
            o_ref[rows, :] = y.astype(o_ref.dtype)
            c0 += width


def _norm_proj(x, g, w_bf16, splits, tm, qk_rope=None, rope_args=(), seq=None):
    rows, d = x.shape
    n = w_bf16.shape[1]
    assert sum(splits) == n and rows % tm == 0
    in_specs = [pl.BlockSpec((tm, d), lambda i: (i, 0)),
                pl.BlockSpec((1, d), lambda i: (0, 0)),
                pl.BlockSpec((d, n), lambda i: (0, 0))]
    if qk_rope is not None:
        pos_blocks = seq // tm
        in_specs += [pl.BlockSpec((tm, LANES), lambda i: (i % pos_blocks, 0)),
                     pl.BlockSpec((tm, LANES), lambda i: (i % pos_blocks, 0)),
                     pl.BlockSpec((1, LANES), lambda i: (0, 0)),
                     pl.BlockSpec((1, LANES), lambda i: (0, 0))]
    return pl.pallas_call(
        functools.partial(_norm_proj_kernel, qk_rope=qk_rope),
        grid=(rows // tm,),
        in_specs=in_specs,
        out_specs=[pl.BlockSpec((tm, s), lambda i: (i, 0)) for s in splits],
        out_shape=[jax.ShapeDtypeStruct((rows, s), BF16) for s in splits],
        compiler_params=_cparams(1),
        name="norm_proj",
    )(x, g.reshape(1, d), w_bf16, *rope_args)


def _gate_out_kernel(*refs, n_parts, final):
    y_refs = refs[:n_parts]
    gate_ref, w_ref, res_ref = refs[n_parts:n_parts + 3]
    o_ref = refs[-1]
    parts = []
    c0 = 0
    for y_ref in y_refs:
        width = y_ref.shape[1]
        g = gate_ref[:, c0:c0 + width].astype(F32)
        silu = g * (1.0 / (1.0 + jnp.exp(-g)))
        parts.append((y_ref[...].astype(F32) * silu).astype(BF16))
        c0 += width
    acc = res_ref[...] + jnp.dot(jnp.concatenate(parts, axis=1), w_ref[...],
                                 preferred_element_type=F32)
    if final:
        acc = _rmsnorm_rows(acc, refs[n_parts + 3][...])
    o_ref[...] = acc


def _gate_out(y_parts, gate, w_bf16, resid, final_g, tm):
    rows, d = resid.shape
    mix = w_bf16.shape[0]
    final = final_g is not None
    in_specs = [pl.BlockSpec((tm, y.shape[1]), lambda i: (i, 0)) for y in y_parts]
    in_specs += [pl.BlockSpec((tm, mix), lambda i: (i, 0)),
                 pl.BlockSpec((mix, d), lambda i: (0, 0)),
                 pl.BlockSpec((tm, d), lambda i: (i, 0))]
    args = list(y_parts) + [gate, w_bf16, resid]
    if final:
        in_specs.append(pl.BlockSpec((1, d), lambda i: (0, 0)))
        args.append(final_g.reshape(1, d))
    return pl.pallas_call(
        functools.partial(_gate_out_kernel, n_parts=len(y_parts), final=final),
        grid=(rows // tm,),
        in_specs=in_specs,
        out_specs=pl.BlockSpec((tm, d), lambda i: (i, 0)),
        out_shape=jax.ShapeDtypeStruct((rows, d), F32),
        compiler_params=_cparams(1),
        name="gate_out",
    )(*args)


def _bucket_thresholds():
    nb = REL_BUCKETS // 2
    max_exact = nb // 2
    n = np.arange(0, 4 * REL_MAX_DIST)
    nf = np.maximum(n, 1).astype(np.float32)
    large = max_exact + (np.log(nf / np.float32(max_exact))
                         / np.float32(math.log(REL_MAX_DIST / max_exact))
                         * np.float32(nb - max_exact)).astype(np.int32)
    bucket = np.where(n < max_exact, n, np.minimum(large, nb - 1))
    assert np.all(np.diff(bucket) >= 0) and bucket[0] == 0 and bucket[-1] == nb - 1
    thr = [int(np.argmax(bucket >= k)) for k in range(1, nb)]
    assert thr[-1] < REL_MAX_DIST
    return thr


def _bias_tiles_kernel(tab_ref, full_ref, win_ref, *, n_tiles, n_far):
    h = pl.program_id(0)
    thr = _bucket_thresholds()
    nb = REL_BUCKETS // 2
    row = lax.broadcasted_iota(jnp.int32, (BLOCK, BLOCK), 0)
    col = lax.broadcasted_iota(jnp.int32, (BLOCK, BLOCK), 1)
    def tile_of(rel):
        n = jnp.abs(rel)
        neg = jnp.full((BLOCK, BLOCK), tab_ref[0, h], F32)
        pos = jnp.full((BLOCK, BLOCK), tab_ref[nb, h], F32)
        for b in range(1, nb):
            ge = n >= thr[b - 1]
            neg = jnp.where(ge, tab_ref[b, h], neg)
            pos = jnp.where(ge, tab_ref[nb + b, h], pos)
        return jnp.where(rel > 0, pos, neg) * LOG2E

    for k in range(n_tiles):
        rel = (k - n_far) * BLOCK + row - col
        tile = tile_of(rel)
        full_ref[0, k] = tile
        j = k - (n_far - 1)
        if 0 <= j < 3:
            win_ref[j * BLOCK:(j + 1) * BLOCK, :] = jnp.where(jnp.abs(rel) <= WINDOW, tile, NEG_INF)


def _bias_tiles(rel_bias, n_blocks):
    n_far = n_blocks - 1
    n_tiles = 2 * n_blocks - 1
    n_heads = rel_bias.shape[1]
    return pl.pallas_call(
        functools.partial(_bias_tiles_kernel, n_tiles=n_tiles, n_far=n_far),
        grid=(n_heads,),
        in_specs=[pl.BlockSpec(memory_space=pltpu.SMEM)],
        out_specs=[pl.BlockSpec((1, n_tiles, BLOCK, BLOCK), lambda h: (h, 0, 0, 0)),
                   pl.BlockSpec((3 * BLOCK, BLOCK), lambda h: (0, h))],
        out_shape=[jax.ShapeDtypeStruct((n_heads, n_tiles, BLOCK, BLOCK), F32),
                   jax.ShapeDtypeStruct((3 * BLOCK, n_heads * BLOCK), F32)],
        compiler_params=_cparams(1),
        name="bias_tiles",
    )(rel_bias)


def _stack_gqa_heads(q_tiles):
    lo = _lane_lo(q_tiles[0].shape)
    return jnp.concatenate([jnp.where(lo, q, 0.0).astype(BF16) for q in q_tiles]
                           + [jnp.where(lo, 0.0, q).astype(BF16) for q in q_tiles], axis=0)


def _stack_pair_heads(q):
    lo = _lane_lo(q.shape)
    return jnp.concatenate([jnp.where(lo, q, 0.0).astype(BF16),
                            jnp.where(lo, 0.0, q).astype(BF16)], axis=0)


def _v_with_ones(v_pair):
    lane = lax.broadcasted_iota(jnp.int32, v_pair.shape, 1)
    vf = v_pair.astype(F32)
    v_lo = jnp.where(lane < HEAD_DIM, vf, jnp.where(lane == HEAD_DIM, 1.0, 0.0))
    v_hi = jnp.where(lane >= HEAD_DIM, vf, jnp.where(lane == 0, 1.0, 0.0))
    return v_lo.astype(BF16), v_hi.astype(BF16)


def _normalise_pair(o_lo, o_hi, extra_lo=None, extra_hi=None):
    lane = lax.broadcasted_iota(jnp.int32, o_lo.shape, 1)
    l_lo = jnp.sum(jnp.where(lane == HEAD_DIM, o_lo, 0.0), axis=-1, keepdims=True)
    l_hi = jnp.sum(jnp.where(lane == 0, o_hi, 0.0), axis=-1, keepdims=True)
    if extra_lo is not None:
        l_lo = l_lo + extra_lo
        l_hi = l_hi + extra_hi
    return jnp.where(lane < HEAD_DIM, o_lo * (1.0 / l_lo), o_hi * (1.0 / l_hi))


def _window_attn_kernel(q_ref, k_ref, v_ref, bias_ref, bmax_ref, sink_col_ref, sink_row_ref,
                        o_ref, kpad_scr, vtpad_scr, kmax_scr, *, n_blocks, group):
    i = pl.program_id(1)
    seq = k_ref.shape[0]
    half = 3 * BLOCK

    @pl.when(i == 0)
    def _():
        lane = lax.broadcasted_iota(jnp.int32, (seq, LANES), 1)
        k_aug = jnp.concatenate([k_ref[...], jnp.where(lane == 0, 1.0, 0.0).astype(BF16)], axis=1)
        zero_blk = jnp.zeros((1, BLOCK, 2 * LANES), BF16)
        kpad_scr[0:1] = zero_blk
        kpad_scr[n_blocks + 1:n_blocks + 2] = zero_blk
        kpad_scr[1:n_blocks + 1] = k_aug.reshape(n_blocks, BLOCK, 2 * LANES)
        kmax_lo, kmax_hi = _pair_key_norm_max(k_ref[...])
        kmax_scr[0:1, :] = kmax_lo
        kmax_scr[1:2, :] = kmax_hi
        v_t = v_ref[...].astype(F32).T.astype(BF16)
        ones = jnp.ones((ONES_ROWS, BLOCK), BF16)
        zero_v = jnp.zeros((HEAD_DIM + ONES_ROWS, BLOCK), BF16)
        for g in range(2):
            vtpad_scr[g, 0] = zero_v
            vtpad_scr[g, n_blocks + 1] = zero_v
            for blk in range(n_blocks):
                vtpad_scr[g, blk + 1, :HEAD_DIM, :] = v_t[g * HEAD_DIM:(g + 1) * HEAD_DIM,
                                                           blk * BLOCK:(blk + 1) * BLOCK]
                vtpad_scr[g, blk + 1, HEAD_DIM:, :] = ones

    probe = (lax.broadcasted_iota(jnp.int32, (8, 2 * LANES), 1) == LANES).astype(BF16)
    sink_row = sink_row_ref[...]

    def query_block(u, exact):
        n = i * group + u
        qf = q_ref[u * BLOCK:(u + 1) * BLOCK, :].astype(F32) * (HEAD_DIM ** -0.5 * LOG2E)
        lhs = _stack_gqa_heads([qf[:, j * LANES:(j + 1) * LANES] for j in range(3)])
        k_win = kpad_scr[pl.ds(n, 3)].reshape(3 * BLOCK, 2 * LANES)
        if exact:
            s = _nt_dot(k_win[:, :LANES], lhs)
        else:
            aug = _shift_column(lhs, kmax_scr[0:1, :], kmax_scr[1:2, :], bmax_ref[...], bmax_ref[...],
                                floor=sink_col_ref[...])
            lhs_aug = jnp.concatenate([lhs, aug], axis=1)
            s = _nt_dot(k_win, lhs_aug)
            neg_shift = _nt_dot(probe, lhs_aug)[0:1]
        s = s + bias_ref[...]
        s = jnp.concatenate([jnp.where(n > 0, s[:BLOCK], NEG_INF), s[BLOCK:2 * BLOCK],
                             jnp.where(n < n_blocks - 1, s[2 * BLOCK:], NEG_INF)], axis=0)
        if exact:
            m = jnp.maximum(jnp.max(s, axis=0, keepdims=True), sink_row)
            p = jnp.exp2(s - m).astype(BF16)
            p_sink = jnp.exp2(sink_row - m)
        else:
            p = jnp.exp2(s).astype(BF16)
            p_sink = jnp.exp2(sink_row + neg_shift)
        outs, sums = [], []
        for g in range(2):
            acc = None
            for j in range(3):
                pv = jnp.dot(vtpad_scr[g, n + j], p[j * BLOCK:(j + 1) * BLOCK, g * half:(g + 1) * half],
                             preferred_element_type=F32)
                acc = pv if acc is None else acc + pv
            total = acc[HEAD_DIM:HEAD_DIM + 1] + p_sink[:, g * half:(g + 1) * half]
            outs.append(acc[:HEAD_DIM] * (1.0 / total))
            sums.append(total)
        for j in range(3):
            cols = slice(j * BLOCK, (j + 1) * BLOCK)
            pair_t = jnp.concatenate([outs[0][:, cols], outs[1][:, cols]], axis=0)
            o_ref[u * BLOCK:(u + 1) * BLOCK, j * LANES:(j + 1) * LANES] = pair_t.T.astype(o_ref.dtype)
        return sums

    sums = [s for u in range(group) for s in query_block(u, exact=False)]

    @pl.when(jnp.logical_not(jnp.min(jnp.concatenate(sums, axis=0)) >= SUM_FLOOR))
    def _():
        for u in range(group):
            query_block(u, exact=True)


def _window_attn(q, k, v, bias_win_t, bias_max, sink, batch, seq, group):
    nb = seq // BLOCK
    steps = nb // group
    rows = A_HEADS * BLOCK
    bmax_col = jnp.broadcast_to(jnp.repeat(bias_max, BLOCK).reshape(rows, 1), (rows, LANES))
    sink_col = jnp.broadcast_to(jnp.repeat(sink, BLOCK).reshape(rows, 1), (rows, LANES))
    sink_row = jnp.repeat(sink, BLOCK).reshape(1, rows)
    const = lambda shape: pl.BlockSpec(shape, lambda b, i: (0,) * len(shape))
    return pl.pallas_call(
        functools.partial(_window_attn_kernel, n_blocks=nb, group=group),
        grid=(batch, steps),
        in_specs=[pl.BlockSpec((group * BLOCK, 3 * LANES), lambda b, i: (b * steps + i, 0)),
                  pl.BlockSpec((seq, LANES), lambda b, i: (b, 0)),
                  pl.BlockSpec((seq, LANES), lambda b, i: (b, 0)),
                  const(bias_win_t.shape), const((rows, LANES)), const((rows, LANES)),
                  const((1, rows))],
        out_specs=pl.BlockSpec((group * BLOCK, 3 * LANES), lambda b, i: (b * steps + i, 0)),
        out_shape=jax.ShapeDtypeStruct(q.shape, BF16),
        scratch_shapes=[pltpu.VMEM((nb + 2, BLOCK, 2 * LANES), BF16),
                        pltpu.VMEM((2, nb + 2, HEAD_DIM + ONES_ROWS, BLOCK), BF16),
                        pltpu.VMEM((2, LANES), F32)],
        compiler_params=_cparams(2),
        name="window_attn",
    )(q, k, v, bias_win_t, bmax_col, sink_col, sink_row)


def _norm_rope_pair(x, g, cos, sin_signed):
    lo = _lane_lo(x.shape)
    x2 = x * x
    ss_lo = jnp.sum(jnp.where(lo, x2, 0.0), axis=-1, keepdims=True)
    ss_hi = jnp.sum(jnp.where(lo, 0.0, x2), axis=-1, keepdims=True)
    ms = jnp.where(lo, ss_lo, ss_hi) * (1.0 / HEAD_DIM)
    y = x * lax.rsqrt(ms + EPS) * g
    lane = lax.broadcasted_iota(jnp.int32, x.shape, 1)
    quarter = HEAD_DIM // 4
    first = (lane & quarter) == 0
    partner = jnp.where(first, pltpu.roll(y, LANES - quarter, 1), pltpu.roll(y, quarter, 1))
    return y * cos + partner * sin_signed


def _dense_attn_kernel(q_ref, k_ref, v_ref, o_ref, ka_scr, vt_scr, kmax_scr, *, tq):
    i = pl.program_id(1)
    seq = k_ref.shape[0]
    kc = 2 * BLOCK
    n_chunks = seq // kc
    grp = 3 * tq

    @pl.when(i == 0)
    def _():
        lane = lax.broadcasted_iota(jnp.int32, (seq, LANES), 1)
        ka_scr[:, :LANES] = k_ref[...]
        ka_scr[:, LANES:] = jnp.where(lane == 0, 1.0, 0.0).astype(BF16)
        kmax_lo, kmax_hi = _pair_key_norm_max(k_ref[...])
        kmax_scr[0:1, :] = kmax_lo
        kmax_scr[1:2, :] = kmax_hi
        v_t = v_ref[...].astype(F32).T.astype(BF16)
        for g in range(2):
            vt_scr[g, :HEAD_DIM, :] = v_t[g * HEAD_DIM:(g + 1) * HEAD_DIM]
            vt_scr[g, HEAD_DIM:, :] = jnp.ones((ONES_ROWS, seq), BF16)

    lhs = _stack_gqa_heads([q_ref[:, j * LANES:(j + 1) * LANES].astype(F32)
                            for j in range(3)])

    def write(acc):
        outs = [a[:HEAD_DIM] * (1.0 / a[HEAD_DIM:HEAD_DIM + 1]) for a in acc]
        for j in range(3):
            cols = slice(j * tq, (j + 1) * tq)
            pair_t = jnp.concatenate([outs[0][:, cols], outs[1][:, cols]], axis=0)
            o_ref[:, j * LANES:(j + 1) * LANES] = pair_t.T.astype(o_ref.dtype)

    aug = _shift_column(lhs, kmax_scr[0:1, :], kmax_scr[1:2, :], 0.0, 0.0)
    s_all = _nt_dot(ka_scr[...], jnp.concatenate([lhs, aug], axis=1))
    p = jnp.exp2(s_all).astype(BF16)
    acc = [jnp.dot(vt_scr[g], p[:, g * grp:(g + 1) * grp], preferred_element_type=F32)
           for g in range(2)]
    write(acc)
    sums = jnp.concatenate([a[HEAD_DIM:HEAD_DIM + 1] for a in acc], axis=0)

    @pl.when(jnp.logical_not(jnp.min(sums) >= SUM_FLOOR))
    def _():
        s_exact = _nt_dot(ka_scr[:, :LANES], lhs)
        m = None
        acc2 = [None, None]
        for c in range(n_chunks):
            s = s_exact[c * kc:(c + 1) * kc, :]
            m_c = jnp.max(s, axis=0, keepdims=True)
            m_new = m_c if m is None else jnp.maximum(m, m_c)
            pc = jnp.exp2(s - m_new).astype(BF16)
            if m is not None:
                alpha = jnp.exp2(m - m_new)
            for g in range(2):
                cols = slice(g * grp, (g + 1) * grp)
                pv = jnp.dot(vt_scr[g, :, c * kc:(c + 1) * kc], pc[:, cols],
                             preferred_element_type=F32)
                acc2[g] = pv if m is None else acc2[g] * alpha[:, cols] + pv
            m = m_new
        write(acc2)


def _dense_attn(q, k, v, batch, seq, tq):
    nq = seq // tq
    return pl.pallas_call(
        functools.partial(_dense_attn_kernel, tq=tq),
        grid=(batch, nq),
        in_specs=[pl.BlockSpec((tq, 3 * LANES), lambda b, i: (b * nq + i, 0)),
                  pl.BlockSpec((seq, LANES), lambda b, i: (b, 0)),
                  pl.BlockSpec((seq, LANES), lambda b, i: (b, 0))],
        out_specs=pl.BlockSpec((tq, 3 * LANES), lambda b, i: (b * nq + i, 0)),
        out_shape=jax.ShapeDtypeStruct(q.shape, BF16),
        scratch_shapes=[pltpu.VMEM((seq, 2 * LANES), BF16),
                        pltpu.VMEM((2, HEAD_DIM + ONES_ROWS, seq), BF16),
                        pltpu.VMEM((2, LANES), F32)],
        compiler_params=_cparams(2),
        name="dense_attn",
    )(q, k, v)


def _cross_attn_kernel(q_ref, mk_ref, mv_ref, o_ref, ka_scr, vt_scr, kmax_scr, *, tq):
    i = pl.program_id(1)
    n_pairs = q_ref.shape[1] // LANES
    mem_len = mk_ref.shape[0]

    @pl.when(i == 0)
    def _():
        lane = lax.broadcasted_iota(jnp.int32, (mem_len, LANES), 1)
        ones_col = jnp.where(lane == 0, 1.0, 0.0).astype(BF16)
        v_t = mv_ref[...].astype(F32).T.astype(BF16)
        for j in range(n_pairs):
            k_pair = mk_ref[:, j * LANES:(j + 1) * LANES]
            ka_scr[j, :, :LANES] = k_pair
            ka_scr[j, :, LANES:] = ones_col
            kmax_lo, kmax_hi = _pair_key_norm_max(k_pair)
            kmax_scr[j, 0:1, :] = kmax_lo
            kmax_scr[j, 1:2, :] = kmax_hi
        for h in range(2 * n_pairs):
            vt_scr[h, :HEAD_DIM, :] = v_t[h * HEAD_DIM:(h + 1) * HEAD_DIM]
            vt_scr[h, HEAD_DIM:, :] = jnp.ones((ONES_ROWS, mem_len), BF16)

    lhs = [_stack_pair_heads(q_ref[:, j * LANES:(j + 1) * LANES].astype(F32)
                             * (HEAD_DIM ** -0.5 * LOG2E)) for j in range(n_pairs)]

    def attend(j, exact):
        if exact:
            s = _nt_dot(ka_scr[j, :, :LANES], lhs[j])
            s = s - jnp.max(s, axis=0, keepdims=True)
        else:
            aug = _shift_column(lhs[j], kmax_scr[j, 0:1, :], kmax_scr[j, 1:2, :], 0.0, 0.0)
            s = _nt_dot(ka_scr[j], jnp.concatenate([lhs[j], aug], axis=1))
        p = jnp.exp2(s).astype(BF16)
        acc = [jnp.dot(vt_scr[2 * j + h], p[:, h * tq:(h + 1) * tq], preferred_element_type=F32)
               for h in range(2)]
        pair_t = jnp.concatenate([a[:HEAD_DIM] * (1.0 / a[HEAD_DIM:HEAD_DIM + 1]) for a in acc],
                                 axis=0)
        o_ref[:, j * LANES:(j + 1) * LANES] = pair_t.T.astype(o_ref.dtype)
        return [a[HEAD_DIM:HEAD_DIM + 1] for a in acc]

    sums = [s for j in range(n_pairs) for s in attend(j, exact=False)]

    @pl.when(jnp.logical_not(jnp.min(jnp.concatenate(sums, axis=0)) >= SUM_FLOOR))
    def _():
        for j in range(n_pairs):
            attend(j, exact=True)


def _cross_attn(q, mk, mv, batch, seq, tq):
    nq = seq // tq
    mem_len = mk.shape[0] // batch
    width = q.shape[1]
    n_pairs = width // LANES
    return pl.pallas_call(
        functools.partial(_cross_attn_kernel, tq=tq),
        grid=(batch, nq),
        in_specs=[pl.BlockSpec((tq, width), lambda b, i: (b * nq + i, 0)),
                  pl.BlockSpec((mem_len, width), lambda b, i: (b, 0)),
                  pl.BlockSpec((mem_len, width), lambda b, i: (b, 0))],
        out_specs=pl.BlockSpec((tq, width), lambda b, i: (b * nq + i, 0)),
        out_shape=jax.ShapeDtypeStruct(q.shape, BF16),
        scratch_shapes=[pltpu.VMEM((n_pairs, mem_len, 2 * LANES), BF16),
                        pltpu.VMEM((2 * n_pairs, HEAD_DIM + ONES_ROWS, mem_len), BF16),
                        pltpu.VMEM((n_pairs, 2, LANES), F32)],
        compiler_params=_cparams(2),
        name="cross_attn",
    )(q, mk, mv)


def _shift_column(lhs, kmax_lo, kmax_hi, bias_max_lo, bias_max_hi, floor=None):
    half = lhs.shape[0] // 2
    lf = lhs.astype(F32)
    q_sq = jnp.sum(lf * lf, axis=-1, keepdims=True)
    q_norm = q_sq * lax.rsqrt(jnp.maximum(q_sq, 1e-30))
    row = lax.broadcasted_iota(jnp.int32, lhs.shape, 0)
    lane = lax.broadcasted_iota(jnp.int32, lhs.shape, 1)
    shift = (q_norm * jnp.where(row < half, kmax_lo, kmax_hi)
             + jnp.where(row < half, bias_max_lo, bias_max_hi))
    if floor is not None:
        shift = jnp.maximum(shift, floor)
    shift = shift + jnp.abs(shift) * 2.0 ** -7
    return jnp.where(lane == 0, -shift, 0.0).astype(BF16)


def _pair_key_norm_max(k_pair):
    kf = k_pair.astype(F32)
    row = lax.broadcasted_iota(jnp.int32, (LANES, 2 * LANES), 0)
    col = lax.broadcasted_iota(jnp.int32, (LANES, 2 * LANES), 1)
    half_sel = jnp.where((row < HEAD_DIM) == (col < LANES), 1.0, 0.0).astype(BF16)
    sq = jnp.dot((kf * kf).astype(BF16), half_sel, preferred_element_type=F32)
    norm_max = jnp.sqrt(jnp.max(sq, axis=0, keepdims=True))
    return norm_max[:, :LANES], norm_max[:, LANES:]


def _diff_attn_kernel(q1_ref, q2_ref, k1_ref, k2_ref, v_ref, bias_ref, bmax_ref, lam_ref, g_ref,
                      o_ref, vt_scr, ka_scr, kmax_scr, *, tq, n_blocks, lam_init):
    i = pl.program_id(2)
    sub = tq // BLOCK
    seq = k1_ref.shape[0]
    kc = 2 * BLOCK
    n_chunks = seq // kc
    k_refs = (k1_ref, k2_ref)

    @pl.when(i == 0)
    def _():
        for head in range(2):
            v_t = v_ref[:, head * C_V_DIM:(head + 1) * C_V_DIM].astype(F32).T
            vt_scr[head, :C_V_DIM, :] = v_t.astype(BF16)
            vt_scr[head, C_V_DIM:, :] = jnp.ones((ONES_ROWS, seq), BF16)
        lane = lax.broadcasted_iota(jnp.int32, (seq, LANES), 1)
        ones_col = jnp.where(lane == 0, 1.0, 0.0).astype(BF16)
        for t in range(2):
            ka_scr[t, :, :LANES] = k_refs[t][...]
            ka_scr[t, :, LANES:] = ones_col
            kmax_lo, kmax_hi = _pair_key_norm_max(k_refs[t][...])
            kmax_scr[t, 0:1, :] = kmax_lo
            kmax_scr[t, 1:2, :] = kmax_hi

    lam_vec = lam_ref[...]
    lam = (jnp.exp(jnp.sum(lam_vec[0:1] * lam_vec[1:2], axis=-1, keepdims=True))
           - jnp.exp(jnp.sum(lam_vec[2:3] * lam_vec[3:4], axis=-1, keepdims=True)) + lam_init)
    lhs = [_stack_pair_heads(q_ref[...].astype(F32) * (HEAD_DIM ** -0.5 * LOG2E))
           for q_ref in (q1_ref, q2_ref)]

    def bias_chunk(c):
        return jnp.concatenate(
            [jnp.concatenate([bias_ref[head, (n_blocks - 1) - (i * sub + u) + 2 * c + e]
                              for head in range(2) for u in range(sub)], axis=1)
             for e in range(2)], axis=0)

    def shifted_scores(t):
        aug = _shift_column(lhs[t], kmax_scr[t, 0:1, :], kmax_scr[t, 1:2, :],
                            bmax_ref[0], bmax_ref[1])
        return _nt_dot(ka_scr[t], jnp.concatenate([lhs[t], aug], axis=1))

    def pv_of_shifted(s_all):
        p = jnp.concatenate([jnp.exp2(s_all[c * kc:(c + 1) * kc, :] + bias_chunk(c)).astype(BF16)
                             for c in range(n_chunks)], axis=0)
        return [jnp.dot(vt_scr[head], p[:, head * tq:(head + 1) * tq],
                        preferred_element_type=F32) for head in range(2)]

    def pv_running_max(s_all):
        m = None
        acc = [None, None]
        for c in range(n_chunks):
            s = s_all[c * kc:(c + 1) * kc, :] + bias_chunk(c)
            m_c = jnp.max(s, axis=0, keepdims=True)
            m_new = m_c if m is None else jnp.maximum(m, m_c)
            p = jnp.exp2(s - m_new).astype(BF16)
            if m is not None:
                alpha = jnp.exp2(m - m_new)
            for head in range(2):
                cols = slice(head * tq, (head + 1) * tq)
                pv = jnp.dot(vt_scr[head, :, c * kc:(c + 1) * kc], p[:, cols],
                             preferred_element_type=F32)
                acc[head] = pv if m is None else acc[head] * alpha[:, cols] + pv
            m = m_new
        return acc

    def write(acc1, acc2):
        g = g_ref[...] * (1.0 - lam_init)
        for head in range(2):
            o1 = acc1[head][:C_V_DIM] * (1.0 / acc1[head][C_V_DIM:C_V_DIM + 1])
            o2 = acc2[head][:C_V_DIM] * (1.0 / acc2[head][C_V_DIM:C_V_DIM + 1])
            out = o1 - lam * o2
            ms = jnp.mean(out * out, axis=0, keepdims=True)
            y = out * lax.rsqrt(ms + EPS) * g
            o_ref[:, head * C_V_DIM:(head + 1) * C_V_DIM] = y.T.astype(o_ref.dtype)

    s1 = shifted_scores(0)
    s2 = shifted_scores(1)
    acc1 = pv_of_shifted(s1)
    acc2 = pv_of_shifted(s2)
    write(acc1, acc2)
    sums = jnp.concatenate([a[C_V_DIM:C_V_DIM + 1] for a in acc1 + acc2], axis=0)

    @pl.when(jnp.logical_not(jnp.min(sums) >= SUM_FLOOR))
    def _():
        write(pv_running_max(_nt_dot(k1_ref[...], lhs[0])),
              pv_running_max(_nt_dot(k2_ref[...], lhs[1])))


def _diff_attn(q1, q2, k1, k2, v, bias_t, bias_max, lam_vecs, subln_g, lam_init, batch, seq, tq):
    nq = seq // tq
    nb = seq // BLOCK
    n_pairs = C_HEADS // 2
    n_tiles = bias_t.shape[1]
    q_spec = pl.BlockSpec((tq, LANES), lambda b, p, i: (b * nq + i, p))
    k_spec = pl.BlockSpec((seq, LANES), lambda b, p, i: (b, p))
    g_cols = jnp.broadcast_to(subln_g.reshape(C_V_DIM, 1), (C_V_DIM, tq))
    return pl.pallas_call(
        functools.partial(_diff_attn_kernel, tq=tq, n_blocks=nb, lam_init=lam_init),
        grid=(batch, n_pairs, nq),
        in_specs=[q_spec, q_spec, k_spec, k_spec,
                  pl.BlockSpec((seq, 2 * C_V_DIM), lambda b, p, i: (b, p)),
                  pl.BlockSpec((2, n_tiles, BLOCK, BLOCK), lambda b, p, i: (p, 0, 0, 0)),
                  pl.BlockSpec((2, 1, LANES), lambda b, p, i: (p, 0, 0)),
                  pl.BlockSpec(lam_vecs.shape, lambda b, p, i: (0, 0)),
                  pl.BlockSpec((C_V_DIM, tq), lambda b, p, i: (0, 0))],
        out_specs=pl.BlockSpec((tq, 2 * C_V_DIM), lambda b, p, i: (b * nq + i, p)),
        out_shape=jax.ShapeDtypeStruct(v.shape, BF16),
        scratch_shapes=[pltpu.VMEM((2, C_V_DIM + ONES_ROWS, seq), BF16),
                        pltpu.VMEM((2, seq, 2 * LANES), BF16),
                        pltpu.VMEM((2, 2, LANES), F32)],
        compiler_params=_cparams(3),
        name="diff_attn",
    )(q1, q2, k1, k2, v, bias_t, bias_max, lam_vecs, g_cols)


def _rope_tables(seq):
    rows = seq // GRID_W
    row = jnp.broadcast_to(jnp.arange(rows)[:, None], (rows, GRID_W)).reshape(-1)
    col = jnp.broadcast_to(jnp.arange(GRID_W)[None, :], (rows, GRID_W)).reshape(-1)
    half = HEAD_DIM // 2
    inv = 1.0 / (ROPE_THETA ** (jnp.arange(0, half, 2, dtype=F32) / half))
    ang_row = row.astype(F32)[:, None] * inv
    ang_col = col.astype(F32)[:, None] * inv
    cos = jnp.concatenate([jnp.cos(ang_row)] * 2 + [jnp.cos(ang_col)] * 2, axis=-1)
    sin = jnp.concatenate([-jnp.sin(ang_row), jnp.sin(ang_row),
                           -jnp.sin(ang_col), jnp.sin(ang_col)], axis=-1)
    return jnp.tile(cos, (1, 2)), jnp.tile(sin, (1, 2))


def _gqa_cols():
    return np.concatenate([np.arange(h * HEAD_DIM, (h + 1) * HEAD_DIM) for h in GQA_HEAD_ORDER])


def kernel(x, mem, rel_bias, mem_norm, final_norm, even_norm, even_w_in, even_sink, even_q_norm, even_k_norm, even_w_mem_kv, even_w_out, odd_norm, odd_w_in, odd_lambda_q1, odd_lambda_k1, odd_lambda_q2, odd_lambda_k2, odd_subln, odd_w_mem_kv, odd_w_out):
    batch, seq, d = x.shape
    mem_len = mem.shape[1]
    tokens = batch * seq
    nb = seq // BLOCK
    xw = X_HEADS * HEAD_DIM
    gw = A_HEADS * HEAD_DIM

    gq = _gqa_cols()
    aq0, ak0, av0 = 0, gw, gw + 128
    bq0 = av0 + 128
    bk0, bv0 = bq0 + gw, bq0 + gw + 128
    xq0 = bv0 + 128
    gate0 = xq0 + xw
    even_cols = np.concatenate([aq0 + gq, np.arange(ak0, bq0), bq0 + gq, np.arange(bk0, gate0),
                                gate0 + gq, gate0 + gw + gq,
                                np.arange(gate0 + 2 * gw, gate0 + D_MODEL)])
    mix_rows = np.concatenate([gq, gw + gq, np.arange(2 * gw, D_MODEL)])
    w_in0 = even_w_in[0].astype(BF16)[:, even_cols]
    w_out0 = even_w_out[0].astype(BF16)[mix_rows, :]
    w_in1 = odd_w_in[0].astype(BF16)
    w_out1 = odd_w_out[0].astype(BF16)

    x2 = x.reshape(tokens, d)
    mem2 = mem.reshape(batch * mem_len, d)

    bias_full, bias_win = _bias_tiles(rel_bias, nb)
    cos, sin_signed = _rope_tables(seq)
    head_bias_max = jnp.max(rel_bias, axis=0) * LOG2E

    w_mem = jnp.concatenate([even_w_mem_kv[0], odd_w_mem_kv[0]], axis=1).astype(BF16)
    mk0, mv0, mk1, mv1 = _norm_proj(mem2, mem_norm, w_mem, (xw,) * 4, TM_PROJ)

    rope_args = (cos, sin_signed, jnp.tile(even_q_norm[0], 2).reshape(1, LANES),
                 jnp.tile(even_k_norm[0], 2).reshape(1, LANES))
    aq, ak, av, bq, bk, bv, xq, gate = _norm_proj(
        x2, even_norm[0], w_in0, (gw, 128, 128, gw, 128, 128, xw, D_MODEL), TM_PROJ,
        qk_rope=(3, 4), rope_args=rope_args, seq=seq)
    y_a = _window_attn(aq, ak, av, bias_win, head_bias_max, even_sink[0] * LOG2E, batch, seq,
                       WINDOW_GROUP)
    y_b = _dense_attn(bq, bk, bv, batch, seq, TQ_DENSE)
    y_x = _cross_attn(xq, mk0, mv0, batch, seq, TQ_CROSS)
    h1 = _gate_out([y_a, y_b, y_x], gate, w_out0, x2, None, TM_PROJ)

    lam_init = 0.8 - 0.6 * math.exp(-0.3 * 1)
    cw = C_HEADS * HEAD_DIM
    q1, q2, k1, k2, v, xq1, gate1 = _norm_proj(
        h1, odd_norm[0], w_in1, (cw, cw, cw, cw, C_HEADS * C_V_DIM, xw, D_MODEL), TM_PROJ)
    lam_vecs = jnp.stack([odd_lambda_q1[0], odd_lambda_k1[0], odd_lambda_q2[0], odd_lambda_k2[0]])
    bias_max = jnp.broadcast_to(head_bias_max.reshape(C_HEADS, 1, 1), (C_HEADS, 1, LANES))
    y_c = _diff_attn(q1, q2, k1, k2, v, bias_full, bias_max, lam_vecs, odd_subln[0], lam_init,
                     batch, seq, TQ_DIFF)
    y_x1 = _cross_attn(xq1, mk1, mv1, batch, seq, TQ_CROSS)
    out = _gate_out([y_c, y_x1], gate1, w_out1, h1, final_norm, TM_PROJ)
    return out.reshape(batch, seq, d)
```

```python
import functools
import math

import numpy as np
import jax
import jax.numpy as jnp
from jax import lax
from jax.experimental import pallas as pl
from jax.experimental.pallas import tpu as pltpu

D_MODEL = 1024
HEAD_DIM = 64
BLOCK = 128
WINDOW = 128
GRID_W = 64
A_HEADS = 6
B_HEADS = 6
C_HEADS = 6
C_V_DIM = 128
X_HEADS = 4
REL_BUCKETS = 32
REL_MAX_DIST = 128
ROPE_THETA = 10000.0
EPS = 1e-6
NEG_INF = -1e30
LOG2E = math.log2(math.e)
LANES = 128
ONES_ROWS = 16
PROJ_SUB_ROWS = 256
TM_PROJ = 1024
TQ_DENSE = 256
TQ_DIFF = 512
TQ_CROSS = 2048
WINDOW_GROUP = 8
SUM_FLOOR = 2.0 ** -64
VMEM_LIMIT = 56 * 1024 * 1024

F32 = jnp.float32
BF16 = jnp.bfloat16

GQA_HEAD_ORDER = (0, 3, 1, 4, 2, 5)


def _cparams(n_axes, flags=None):
    return pltpu.CompilerParams(dimension_semantics=("arbitrary",) * n_axes,
                                vmem_limit_bytes=VMEM_LIMIT, flags=flags)


def _lane_lo(shape):
    return lax.broadcasted_iota(jnp.int32, shape, len(shape) - 1) < HEAD_DIM


def _rmsnorm_rows(x, g):
    ms = jnp.mean(x * x, axis=-1, keepdims=True)
    return x * lax.rsqrt(ms + EPS) * g


def _nt_dot(a, b):
    return lax.dot_general(a, b, (((1,), (1,)), ((), ())), preferred_element_type=F32)


def _norm_proj_kernel(*refs, qk_rope):
    x_ref, g_ref, w_ref = refs[:3]
    if qk_rope is None:
        out_refs = refs[3:]
    else:
        cos_ref, sin_ref, gq_ref, gk_ref = refs[3:7]
        out_refs = refs[7:]
    tm = x_ref.shape[0]
    for r0 in range(0, tm, PROJ_SUB_ROWS):
        rows = slice(r0, min(r0 + PROJ_SUB_ROWS, tm))
        xn = _rmsnorm_rows(x_ref[rows, :], g_ref[...]).astype(BF16)
        y_all = jnp.dot(xn, w_ref[...], preferred_element_type=F32)
        c0 = 0
        for j, o_ref in enumerate(out_refs):
            width = o_ref.shape[1]
            y = y_all[:, c0:c0 + width]
            if qk_rope is not None and j in qk_rope:
                gain, scale = ((gq_ref, HEAD_DIM ** -0.5 * LOG2E) if j == qk_rope[0]
                               else (gk_ref, 1.0))
                y = jnp.concatenate(
                    [_norm_rope_pair(y[:, t * LANES:(t + 1) * LANES], gain[...],
                                     cos_ref[rows, :], sin_ref[rows, :]) * scale
                     for t in range(width // LANES)], axis=1)
            o_ref[rows, :] = y.astype(o_ref.dtype)
            c0 += width


def _norm_proj(x, g, w_bf16, splits, tm, qk_rope=None, rope_args=(), seq=None):
    rows, d = x.shape
    n = w_bf16.shape[1]
    assert sum(splits) == n and rows % tm == 0
    in_specs = [pl.BlockSpec((tm, d), lambda i: (i, 0)),
                pl.BlockSpec((1, d), lambda i: (0, 0)),
                pl.BlockSpec((d, n), lambda i: (0, 0))]
    if qk_rope is not None:
        pos_blocks = seq // tm
        in_specs += [pl.BlockSpec((tm, LANES), lambda i: (i % pos_blocks, 0)),
                     pl.BlockSpec((tm, LANES), lambda i: (i % pos_blocks, 0)),
                     pl.BlockSpec((1, LANES), lambda i: (0, 0)),
                     pl.BlockSpec((1, LANES), lambda i: (0, 0))]
    return pl.pallas_call(
        functools.partial(_norm_proj_kernel, qk_rope=qk_rope),
        grid=(rows // tm,),
        in_specs=in_specs,
        out_specs=[pl.BlockSpec((tm, s), lambda i: (i, 0)) for s in splits],
        out_shape=[jax.ShapeDtypeStruct((rows, s), BF16) for s in splits],
        compiler_params=_cparams(1),
        name="norm_proj",
    )(x, g.reshape(1, d), w_bf16, *rope_args)


def _gate_out_kernel(*refs, n_parts, after):
    y_refs = refs[:n_parts]
    gate_ref, w_ref, res_ref, g_ref = refs[n_parts:n_parts + 4]
    if after == "proj":
        w_next_ref, h_ref = refs[n_parts + 4:n_parts + 6]
        out_refs = refs[n_parts + 6:]
    else:
        h_ref = refs[n_parts + 4]
    tm = res_ref.shape[0]
    for r0 in range(0, tm, PROJ_SUB_ROWS):
        rows = slice(r0, min(r0 + PROJ_SUB_ROWS, tm))
        parts = []
        c0 = 0
        for y_ref in y_refs:
            width = y_ref.shape[1]
            g = gate_ref[rows, c0:c0 + width].astype(F32)
            silu = g * (1.0 / (1.0 + jnp.exp(-g)))
            parts.append((y_ref[rows, :].astype(F32) * silu).astype(BF16))
            c0 += width
        h = res_ref[rows, :] + jnp.dot(jnp.concatenate(parts, axis=1), w_ref[...],
                                       preferred_element_type=F32)
        if after == "norm":
            h_ref[rows, :] = _rmsnorm_rows(h, g_ref[...])
            continue
        h_ref[rows, :] = h
        y_all = jnp.dot(_rmsnorm_rows(h, g_ref[...]).astype(BF16), w_next_ref[...],
                        preferred_element_type=F32)
        c0 = 0
        for o_ref in out_refs:
            width = o_ref.shape[1]
            o_ref[rows, :] = y_all[:, c0:c0 + width].astype(o_ref.dtype)
            c0 += width


def _gate_out(y_parts, gate, w_bf16, resid, norm_g, tm, w_next=None, splits=()):
    rows, d = resid.shape
    mix = w_bf16.shape[0]
    after = "norm" if w_next is None else "proj"
    row_block = lambda width: pl.BlockSpec((tm, width), lambda i: (i, 0))
    whole = lambda a: pl.BlockSpec(a.shape, lambda i: (0, 0))
    in_specs = [row_block(y.shape[1]) for y in y_parts]
    in_specs += [row_block(mix), whole(w_bf16), row_block(d), pl.BlockSpec((1, d), lambda i: (0, 0))]
    args = list(y_parts) + [gate, w_bf16, resid, norm_g.reshape(1, d)]
    out_specs = [row_block(d)]
    out_shape = [jax.ShapeDtypeStruct((rows, d), F32)]
    if after == "proj":
        assert sum(splits) == w_next.shape[1]
        in_specs.append(whole(w_next))
        args.append(w_next)
        out_specs += [row_block(s) for s in splits]
        out_shape += [jax.ShapeDtypeStruct((rows, s), BF16) for s in splits]
    return pl.pallas_call(
        functools.partial(_gate_out_kernel, n_parts=len(y_parts), after=after),
        grid=(rows // tm,),
        in_specs=in_specs,
        out_specs=out_specs,
        out_shape=out_shape,
        compiler_params=_cparams(1),
        name="gate_out",
    )(*args)


def _bucket_thresholds():
    nb = REL_BUCKETS // 2
    max_exact = nb // 2
    n = np.arange(0, 4 * REL_MAX_DIST)
    nf = np.maximum(n, 1).astype(np.float32)
    large = max_exact + (np.log(nf / np.float32(max_exact))
                         / np.float32(math.log(REL_MAX_DIST / max_exact))
                         * np.float32(nb - max_exact)).astype(np.int32)
    bucket = np.where(n < max_exact, n, np.minimum(large, nb - 1))
    assert np.all(np.diff(bucket) >= 0) and bucket[0] == 0 and bucket[-1] == nb - 1
    thr = [int(np.argmax(bucket >= k)) for k in range(1, nb)]
    assert thr[-1] < REL_MAX_DIST
    return thr


def _bias_tiles_kernel(tab_ref, full_ref, win_ref, *, n_tiles, n_far):
    h = pl.program_id(0)
    thr = _bucket_thresholds()
    nb = REL_BUCKETS // 2
    row = lax.broadcasted_iota(jnp.int32, (BLOCK, BLOCK), 0)
    col = lax.broadcasted_iota(jnp.int32, (BLOCK, BLOCK), 1)
    def tile_of(rel):
        n = jnp.abs(rel)
        neg = jnp.full((BLOCK, BLOCK), tab_ref[0, h], F32)
        pos = jnp.full((BLOCK, BLOCK), tab_ref[nb, h], F32)
        for b in range(1, nb):
            ge = n >= thr[b - 1]
            neg = jnp.where(ge, tab_ref[b, h], neg)
            pos = jnp.where(ge, tab_ref[nb + b, h], pos)
        return jnp.where(rel > 0, pos, neg) * LOG2E

    for k in range(n_tiles):
        rel = (k - n_far) * BLOCK + row - col
        tile = tile_of(rel)
        full_ref[0, k] = tile
        j = k - (n_far - 1)
        if 0 <= j < 3:
            win_ref[j * BLOCK:(j + 1) * BLOCK, :] = jnp.where(jnp.abs(rel) <= WINDOW, tile, NEG_INF)


def _bias_tiles(rel_bias, n_blocks):
    n_far = n_blocks - 1
    n_tiles = 2 * n_blocks - 1
    n_heads = rel_bias.shape[1]
    return pl.pallas_call(
        functools.partial(_bias_tiles_kernel, n_tiles=n_tiles, n_far=n_far),
        grid=(n_heads,),
        in_specs=[pl.BlockSpec(memory_space=pltpu.SMEM)],
        out_specs=[pl.BlockSpec((1, n_tiles, BLOCK, BLOCK), lambda h: (h, 0, 0, 0)),
                   pl.BlockSpec((3 * BLOCK, BLOCK), lambda h: (0, h))],
        out_shape=[jax.ShapeDtypeStruct((n_heads, n_tiles, BLOCK, BLOCK), F32),
                   jax.ShapeDtypeStruct((3 * BLOCK, n_heads * BLOCK), F32)],
        compiler_params=_cparams(1),
        name="bias_tiles",
    )(rel_bias)


def _stack_gqa_heads(q_tiles):
    lo = _lane_lo(q_tiles[0].shape)
    return jnp.concatenate([jnp.where(lo, q, 0.0).astype(BF16) for q in q_tiles]
                           + [jnp.where(lo, 0.0, q).astype(BF16) for q in q_tiles], axis=0)


def _stack_pair_heads(q):
    lo = _lane_lo(q.shape)
    return jnp.concatenate([jnp.where(lo, q, 0.0).astype(BF16),
                            jnp.where(lo, 0.0, q).astype(BF16)], axis=0)


def _window_attn_kernel(q_ref, k_ref, v_ref, bias_ref, sink_row_ref,
                        o_ref, kpad_scr, vtpad_scr, *, n_blocks, group):
    i = pl.program_id(1)
    half = 3 * BLOCK

    @pl.when(i == 0)
    def _():
        zero_blk = jnp.zeros((1, BLOCK, LANES), BF16)
        kpad_scr[0:1] = zero_blk
        kpad_scr[n_blocks + 1:n_blocks + 2] = zero_blk
        kpad_scr[1:n_blocks + 1] = k_ref[...].reshape(n_blocks, BLOCK, LANES)
        v_t = v_ref[...].astype(F32).T.astype(BF16)
        ones = jnp.ones((ONES_ROWS, BLOCK), BF16)
        zero_v = jnp.zeros((HEAD_DIM + ONES_ROWS, BLOCK), BF16)
        for g in range(2):
            vtpad_scr[g, 0] = zero_v
            vtpad_scr[g, n_blocks + 1] = zero_v
            for blk in range(n_blocks):
                vtpad_scr[g, blk + 1, :HEAD_DIM, :] = v_t[g * HEAD_DIM:(g + 1) * HEAD_DIM,
                                                           blk * BLOCK:(blk + 1) * BLOCK]
                vtpad_scr[g, blk + 1, HEAD_DIM:, :] = ones

    sink_row = sink_row_ref[...]

    for u in range(group):
        n = i * group + u
        qf = q_ref[u * BLOCK:(u + 1) * BLOCK, :].astype(F32) * (HEAD_DIM ** -0.5 * LOG2E)
        lhs = _stack_gqa_heads([qf[:, j * LANES:(j + 1) * LANES] for j in range(3)])
        k_win = kpad_scr[pl.ds(n, 3)].reshape(3 * BLOCK, LANES)
        s = _nt_dot(k_win, lhs) + bias_ref[...]
        s = jnp.concatenate([jnp.where(n > 0, s[:BLOCK], NEG_INF), s[BLOCK:2 * BLOCK],
                             jnp.where(n < n_blocks - 1, s[2 * BLOCK:], NEG_INF)], axis=0)
        m = jnp.maximum(jnp.max(s, axis=0, keepdims=True), sink_row)
        p = jnp.exp2(s - m).astype(BF16)
        p_sink = jnp.exp2(sink_row - m)
        outs = []
        for g in range(2):
            acc = None
            for j in range(3):
                pv = jnp.dot(vtpad_scr[g, n + j], p[j * BLOCK:(j + 1) * BLOCK, g * half:(g + 1) * half],
                             preferred_element_type=F32)
                acc = pv if acc is None else acc + pv
            total = acc[HEAD_DIM:HEAD_DIM + 1] + p_sink[:, g * half:(g + 1) * half]
            outs.append(acc[:HEAD_DIM] * (1.0 / total))
        for j in range(3):
            cols = slice(j * BLOCK, (j + 1) * BLOCK)
            pair_t = jnp.concatenate([outs[0][:, cols], outs[1][:, cols]], axis=0)
            o_ref[u * BLOCK:(u + 1) * BLOCK, j * LANES:(j + 1) * LANES] = pair_t.T.astype(o_ref.dtype)


def _window_attn(q, k, v, bias_win_t, sink, batch, seq, group):
    nb = seq // BLOCK
    steps = nb // group
    rows = A_HEADS * BLOCK
    sink_row = jnp.repeat(sink, BLOCK).reshape(1, rows)
    const = lambda shape: pl.BlockSpec(shape, lambda b, i: (0,) * len(shape))
    return pl.pallas_call(
        functools.partial(_window_attn_kernel, n_blocks=nb, group=group),
        grid=(batch, steps),
        in_specs=[pl.BlockSpec((group * BLOCK, 3 * LANES), lambda b, i: (b * steps + i, 0)),
                  pl.BlockSpec((seq, LANES), lambda b, i: (b, 0)),
                  pl.BlockSpec((seq, LANES), lambda b, i: (b, 0)),
                  const(bias_win_t.shape), const((1, rows))],
        out_specs=pl.BlockSpec((group * BLOCK, 3 * LANES), lambda b, i: (b * steps + i, 0)),
        out_shape=jax.ShapeDtypeStruct(q.shape, BF16),
        scratch_shapes=[pltpu.VMEM((nb + 2, BLOCK, LANES), BF16),
                        pltpu.VMEM((2, nb + 2, HEAD_DIM + ONES_ROWS, BLOCK), BF16)],
        compiler_params=_cparams(2),
        name="window_attn",
    )(q, k, v, bias_win_t, sink_row)


def _norm_rope_pair(x, g, cos, sin_signed):
    lo = _lane_lo(x.shape)
    x2 = x * x
    ss_lo = jnp.sum(jnp.where(lo, x2, 0.0), axis=-1, keepdims=True)
    ss_hi = jnp.sum(jnp.where(lo, 0.0, x2), axis=-1, keepdims=True)
    ms = jnp.where(lo, ss_lo, ss_hi) * (1.0 / HEAD_DIM)
    y = x * lax.rsqrt(ms + EPS) * g
    lane = lax.broadcasted_iota(jnp.int32, x.shape, 1)
    quarter = HEAD_DIM // 4
    first = (lane & quarter) == 0
    partner = jnp.where(first, pltpu.roll(y, LANES - quarter, 1), pltpu.roll(y, quarter, 1))
    return y * cos + partner * sin_signed


def _dense_attn_kernel(q_ref, k_ref, v_ref, o_ref, ka_scr, vt_scr, kmax_scr, *, tq):
    i = pl.program_id(1)
    seq = k_ref.shape[0]
    kc = 2 * BLOCK
    n_chunks = seq // kc
    grp = 3 * tq

    @pl.when(i == 0)
    def _():
        lane = lax.broadcasted_iota(jnp.int32, (seq, LANES), 1)
        ka_scr[:, :LANES] = k_ref[...]
        ka_scr[:, LANES:] = jnp.where(lane == 0, 1.0, 0.0).astype(BF16)
        kmax_lo, kmax_hi = _pair_key_norm_max(k_ref[...])
        kmax_scr[0:1, :] = kmax_lo
        kmax_scr[1:2, :] = kmax_hi
        v_t = v_ref[...].astype(F32).T.astype(BF16)
        for g in range(2):
            vt_scr[g, :HEAD_DIM, :] = v_t[g * HEAD_DIM:(g + 1) * HEAD_DIM]
            vt_scr[g, HEAD_DIM:, :] = jnp.ones((ONES_ROWS, seq), BF16)

    lhs = _stack_gqa_heads([q_ref[:, j * LANES:(j + 1) * LANES].astype(F32)
                            for j in range(3)])

    def write(acc):
        outs = [a[:HEAD_DIM] * (1.0 / a[HEAD_DIM:HEAD_DIM + 1]) for a in acc]
        for j in range(3):
            cols = slice(j * tq, (j + 1) * tq)
            pair_t = jnp.concatenate([outs[0][:, cols], outs[1][:, cols]], axis=0)
            o_ref[:, j * LANES:(j + 1) * LANES] = pair_t.T.astype(o_ref.dtype)

    aug = _shift_column(lhs, kmax_scr[0:1, :], kmax_scr[1:2, :], 0.0, 0.0)
    s_all = _nt_dot(ka_scr[...], jnp.concatenate([lhs, aug], axis=1))
    p = jnp.exp2(s_all).astype(BF16)
    acc = [jnp.dot(vt_scr[g], p[:, g * grp:(g + 1) * grp], preferred_element_type=F32)
           for g in range(2)]
    write(acc)
    sums = jnp.concatenate([a[HEAD_DIM:HEAD_DIM + 1] for a in acc], axis=0)

    @pl.when(jnp.logical_not(jnp.min(sums) >= SUM_FLOOR))
    def _():
        s_exact = _nt_dot(ka_scr[:, :LANES], lhs)
        m = None
        acc2 = [None, None]
        for c in range(n_chunks):
            s = s_exact[c * kc:(c + 1) * kc, :]
            m_c = jnp.max(s, axis=0, keepdims=True)
            m_new = m_c if m is None else jnp.maximum(m, m_c)
            pc = jnp.exp2(s - m_new).astype(BF16)
            if m is not None:
                alpha = jnp.exp2(m - m_new)
            for g in range(2):
                cols = slice(g * grp, (g + 1) * grp)
                pv = jnp.dot(vt_scr[g, :, c * kc:(c + 1) * kc], pc[:, cols],
                             preferred_element_type=F32)
                acc2[g] = pv if m is None else acc2[g] * alpha[:, cols] + pv
            m = m_new
        write(acc2)


def _dense_attn(q, k, v, batch, seq, tq):
    nq = seq // tq
    return pl.pallas_call(
        functools.partial(_dense_attn_kernel, tq=tq),
        grid=(batch, nq),
        in_specs=[pl.BlockSpec((tq, 3 * LANES), lambda b, i: (b * nq + i, 0)),
                  pl.BlockSpec((seq, LANES), lambda b, i: (b, 0)),
                  pl.BlockSpec((seq, LANES), lambda b, i: (b, 0))],
        out_specs=pl.BlockSpec((tq, 3 * LANES), lambda b, i: (b * nq + i, 0)),
        out_shape=jax.ShapeDtypeStruct(q.shape, BF16),
        scratch_shapes=[pltpu.VMEM((seq, 2 * LANES), BF16),
                        pltpu.VMEM((2, HEAD_DIM + ONES_ROWS, seq), BF16),
                        pltpu.VMEM((2, LANES), F32)],
        compiler_params=_cparams(2),
        name="dense_attn",
    )(q, k, v)


def _cross_attn_kernel(q_ref, mk_ref, mv_ref, o_ref, vt_scr, *, tq):
    i = pl.program_id(1)
    n_pairs = q_ref.shape[1] // LANES
    mem_len = mk_ref.shape[0]

    @pl.when(i == 0)
    def _():
        v_t = mv_ref[...].astype(F32).T.astype(BF16)
        for h in range(2 * n_pairs):
            vt_scr[h, :HEAD_DIM, :] = v_t[h * HEAD_DIM:(h + 1) * HEAD_DIM]
            vt_scr[h, HEAD_DIM:, :] = jnp.ones((ONES_ROWS, mem_len), BF16)

    for j in range(n_pairs):
        cols = slice(j * LANES, (j + 1) * LANES)
        lhs = _stack_pair_heads(q_ref[:, cols].astype(F32) * (HEAD_DIM ** -0.5 * LOG2E))
        s = _nt_dot(mk_ref[:, cols], lhs)
        p = jnp.exp2(s - jnp.max(s, axis=0, keepdims=True)).astype(BF16)
        acc = [jnp.dot(vt_scr[2 * j + h], p[:, h * tq:(h + 1) * tq], preferred_element_type=F32)
               for h in range(2)]
        pair_t = jnp.concatenate([a[:HEAD_DIM] * (1.0 / a[HEAD_DIM:HEAD_DIM + 1]) for a in acc],
                                 axis=0)
        o_ref[:, cols] = pair_t.T.astype(o_ref.dtype)


def _cross_attn(q, mk, mv, batch, seq, tq):
    nq = seq // tq
    mem_len = mk.shape[0] // batch
    width = q.shape[1]
    n_pairs = width // LANES
    return pl.pallas_call(
        functools.partial(_cross_attn_kernel, tq=tq),
        grid=(batch, nq),
        in_specs=[pl.BlockSpec((tq, width), lambda b, i: (b * nq + i, 0)),
                  pl.BlockSpec((mem_len, width), lambda b, i: (b, 0)),
                  pl.BlockSpec((mem_len, width), lambda b, i: (b, 0))],
        out_specs=pl.BlockSpec((tq, width), lambda b, i: (b * nq + i, 0)),
        out_shape=jax.ShapeDtypeStruct(q.shape, BF16),
        scratch_shapes=[pltpu.VMEM((2 * n_pairs, HEAD_DIM + ONES_ROWS, mem_len), BF16)],
        compiler_params=_cparams(2),
        name="cross_attn",
    )(q, mk, mv)


def _shift_column(lhs, kmax_lo, kmax_hi, bias_max_lo, bias_max_hi):
    half = lhs.shape[0] // 2
    lf = lhs.astype(F32)
    q_sq = jnp.sum(lf * lf, axis=-1, keepdims=True)
    q_norm = q_sq * lax.rsqrt(jnp.maximum(q_sq, 1e-30))
    row = lax.broadcasted_iota(jnp.int32, lhs.shape, 0)
    lane = lax.broadcasted_iota(jnp.int32, lhs.shape, 1)
    shift = (q_norm * jnp.where(row < half, kmax_lo, kmax_hi)
             + jnp.where(row < half, bias_max_lo, bias_max_hi))
    shift = shift + jnp.abs(shift) * 2.0 ** -7
    return jnp.where(lane == 0, -shift, 0.0).astype(BF16)


def _pair_key_norm_max(k_pair):
    kf = k_pair.astype(F32)
    row = lax.broadcasted_iota(jnp.int32, (LANES, 2 * LANES), 0)
    col = lax.broadcasted_iota(jnp.int32, (LANES, 2 * LANES), 1)
    half_sel = jnp.where((row < HEAD_DIM) == (col < LANES), 1.0, 0.0).astype(BF16)
    sq = jnp.dot((kf * kf).astype(BF16), half_sel, preferred_element_type=F32)
    norm_max = jnp.sqrt(jnp.max(sq, axis=0, keepdims=True))
    return norm_max[:, :LANES], norm_max[:, LANES:]


def _diff_attn_kernel(q1_ref, q2_ref, k1_ref, k2_ref, v_ref, bias_ref, bmax_ref, lam_ref, g_ref,
                      o_ref, vt_scr, ka_scr, kmax_scr, *, tq, n_blocks, lam_init):
    i = pl.program_id(2)
    sub = tq // BLOCK
    seq = k1_ref.shape[0]
    kc = 2 * BLOCK
    n_chunks = seq // kc
    k_refs = (k1_ref, k2_ref)

    @pl.when(i == 0)
    def _():
        for head in range(2):
            v_t = v_ref[:, head * C_V_DIM:(head + 1) * C_V_DIM].astype(F32).T
            vt_scr[head, :C_V_DIM, :] = v_t.astype(BF16)
            vt_scr[head, C_V_DIM:, :] = jnp.ones((ONES_ROWS, seq), BF16)
        lane = lax.broadcasted_iota(jnp.int32, (seq, LANES), 1)
        ones_col = jnp.where(lane == 0, 1.0, 0.0).astype(BF16)
        for t in range(2):
            ka_scr[t, :, :LANES] = k_refs[t][...]
            ka_scr[t, :, LANES:] = ones_col
            kmax_lo, kmax_hi = _pair_key_norm_max(k_refs[t][...])
            kmax_scr[t, 0:1, :] = kmax_lo
            kmax_scr[t, 1:2, :] = kmax_hi

    lam_vec = lam_ref[...]
    lam = (jnp.exp(jnp.sum(lam_vec[0:1] * lam_vec[1:2], axis=-1, keepdims=True))
           - jnp.exp(jnp.sum(lam_vec[2:3] * lam_vec[3:4], axis=-1, keepdims=True)) + lam_init)
    lhs = [_stack_pair_heads(q_ref[...].astype(F32) * (HEAD_DIM ** -0.5 * LOG2E))
           for q_ref in (q1_ref, q2_ref)]

    def bias_chunk(c):
        return jnp.concatenate(
            [jnp.concatenate([bias_ref[head, (n_blocks - 1) - (i * sub + u) + 2 * c + e]
                              for head in range(2) for u in range(sub)], axis=1)
             for e in range(2)], axis=0)

    def shifted_scores(t):
        aug = _shift_column(lhs[t], kmax_scr[t, 0:1, :], kmax_scr[t, 1:2, :],
                            bmax_ref[0], bmax_ref[1])
        return _nt_dot(ka_scr[t], jnp.concatenate([lhs[t], aug], axis=1))

    def pv_of_shifted(s_all):
        p = jnp.concatenate([jnp.exp2(s_all[c * kc:(c + 1) * kc, :] + bias_chunk(c)).astype(BF16)
                             for c in range(n_chunks)], axis=0)
        return [jnp.dot(vt_scr[head], p[:, head * tq:(head + 1) * tq],
                        preferred_element_type=F32) for head in range(2)]

    def pv_running_max(s_all):
        m = None
        acc = [None, None]
        for c in range(n_chunks):
            s = s_all[c * kc:(c + 1) * kc, :] + bias_chunk(c)
            m_c = jnp.max(s, axis=0, keepdims=True)
            m_new = m_c if m is None else jnp.maximum(m, m_c)
            p = jnp.exp2(s - m_new).astype(BF16)
            if m is not None:
                alpha = jnp.exp2(m - m_new)
            for head in range(2):
                cols = slice(head * tq, (head + 1) * tq)
                pv = jnp.dot(vt_scr[head, :, c * kc:(c + 1) * kc], p[:, cols],
                             preferred_element_type=F32)
                acc[head] = pv if m is None else acc[head] * alpha[:, cols] + pv
            m = m_new
        return acc

    def write(acc1, acc2):
        g = g_ref[...] * (1.0 - lam_init)
        for head in range(2):
            o1 = acc1[head][:C_V_DIM] * (1.0 / acc1[head][C_V_DIM:C_V_DIM + 1])
            o2 = acc2[head][:C_V_DIM] * (1.0 / acc2[head][C_V_DIM:C_V_DIM + 1])
            out = o1 - lam * o2
            ms = jnp.mean(out * out, axis=0, keepdims=True)
            y = out * lax.rsqrt(ms + EPS) * g
            o_ref[:, head * C_V_DIM:(head + 1) * C_V_DIM] = y.T.astype(o_ref.dtype)

    s1 = shifted_scores(0)
    s2 = shifted_scores(1)
    acc1 = pv_of_shifted(s1)
    acc2 = pv_of_shifted(s2)
    write(acc1, acc2)
    sums = jnp.concatenate([a[C_V_DIM:C_V_DIM + 1] for a in acc1 + acc2], axis=0)

    @pl.when(jnp.logical_not(jnp.min(sums) >= SUM_FLOOR))
    def _():
        write(pv_running_max(_nt_dot(k1_ref[...], lhs[0])),
              pv_running_max(_nt_dot(k2_ref[...], lhs[1])))


def _diff_attn(q1, q2, k1, k2, v, bias_t, bias_max, lam_vecs, subln_g, lam_init, batch, seq, tq):
    nq = seq // tq
    nb = seq // BLOCK
    n_pairs = C_HEADS // 2
    n_tiles = bias_t.shape[1]
    q_spec = pl.BlockSpec((tq, LANES), lambda b, p, i: (b * nq + i, p))
    k_spec = pl.BlockSpec((seq, LANES), lambda b, p, i: (b, p))
    g_cols = jnp.broadcast_to(subln_g.reshape(C_V_DIM, 1), (C_V_DIM, tq))
    return pl.pallas_call(
        functools.partial(_diff_attn_kernel, tq=tq, n_blocks=nb, lam_init=lam_init),
        grid=(batch, n_pairs, nq),
        in_specs=[q_spec, q_spec, k_spec, k_spec,
                  pl.BlockSpec((seq, 2 * C_V_DIM), lambda b, p, i: (b, p)),
                  pl.BlockSpec((2, n_tiles, BLOCK, BLOCK), lambda b, p, i: (p, 0, 0, 0)),
                  pl.BlockSpec((2, 1, LANES), lambda b, p, i: (p, 0, 0)),
                  pl.BlockSpec(lam_vecs.shape, lambda b, p, i: (0, 0)),
                  pl.BlockSpec((C_V_DIM, tq), lambda b, p, i: (0, 0))],
        out_specs=pl.BlockSpec((tq, 2 * C_V_DIM), lambda b, p, i: (b * nq + i, p)),
        out_shape=jax.ShapeDtypeStruct(v.shape, BF16),
        scratch_shapes=[pltpu.VMEM((2, C_V_DIM + ONES_ROWS, seq), BF16),
                        pltpu.VMEM((2, seq, 2 * LANES), BF16),
                        pltpu.VMEM((2, 2, LANES), F32)],
        compiler_params=_cparams(3),
        name="diff_attn",
    )(q1, q2, k1, k2, v, bias_t, bias_max, lam_vecs, g_cols)


def _rope_tables(seq):
    rows = seq // GRID_W
    row = jnp.broadcast_to(jnp.arange(rows)[:, None], (rows, GRID_W)).reshape(-1)
    col = jnp.broadcast_to(jnp.arange(GRID_W)[None, :], (rows, GRID_W)).reshape(-1)
    half = HEAD_DIM // 2
    inv = 1.0 / (ROPE_THETA ** (jnp.arange(0, half, 2, dtype=F32) / half))
    ang_row = row.astype(F32)[:, None] * inv
    ang_col = col.astype(F32)[:, None] * inv
    cos = jnp.concatenate([jnp.cos(ang_row)] * 2 + [jnp.cos(ang_col)] * 2, axis=-1)
    sin = jnp.concatenate([-jnp.sin(ang_row), jnp.sin(ang_row),
                           -jnp.sin(ang_col), jnp.sin(ang_col)], axis=-1)
    return jnp.tile(cos, (1, 2)), jnp.tile(sin, (1, 2))


def _gqa_cols():
    return np.concatenate([np.arange(h * HEAD_DIM, (h + 1) * HEAD_DIM) for h in GQA_HEAD_ORDER])


def kernel(x, mem, rel_bias, mem_norm, final_norm, even_norm, even_w_in, even_sink, even_q_norm, even_k_norm, even_w_mem_kv, even_w_out, odd_norm, odd_w_in, odd_lambda_q1, odd_lambda_k1, odd_lambda_q2, odd_lambda_k2, odd_subln, odd_w_mem_kv, odd_w_out):
    batch, seq, d = x.shape
    mem_len = mem.shape[1]
    tokens = batch * seq
    nb = seq // BLOCK
    xw = X_HEADS * HEAD_DIM
    gw = A_HEADS * HEAD_DIM

    gq = _gqa_cols()
    aq0, ak0, av0 = 0, gw, gw + 128
    bq0 = av0 + 128
    bk0, bv0 = bq0 + gw, bq0 + gw + 128
    xq0 = bv0 + 128
    gate0 = xq0 + xw
    even_cols = np.concatenate([aq0 + gq, np.arange(ak0, bq0), bq0 + gq, np.arange(bk0, gate0),
                                gate0 + gq, gate0 + gw + gq,
                                np.arange(gate0 + 2 * gw, gate0 + D_MODEL)])
    mix_rows = np.concatenate([gq, gw + gq, np.arange(2 * gw, D_MODEL)])
    w_in0 = even_w_in[0].astype(BF16)[:, even_cols]
    w_out0 = even_w_out[0].astype(BF16)[mix_rows, :]
    w_in1 = odd_w_in[0].astype(BF16)
    w_out1 = odd_w_out[0].astype(BF16)

    x2 = x.reshape(tokens, d)
    mem2 = mem.reshape(batch * mem_len, d)

    bias_full, bias_win = _bias_tiles(rel_bias, nb)
    cos, sin_signed = _rope_tables(seq)
    head_bias_max = jnp.max(rel_bias, axis=0) * LOG2E

    w_mem = jnp.concatenate([even_w_mem_kv[0], odd_w_mem_kv[0]], axis=1).astype(BF16)
    mk0, mv0, mk1, mv1 = _norm_proj(mem2, mem_norm, w_mem, (xw,) * 4, TM_PROJ)

    rope_args = (cos, sin_signed, jnp.tile(even_q_norm[0], 2).reshape(1, LANES),
                 jnp.tile(even_k_norm[0], 2).reshape(1, LANES))
    aq, ak, av, bq, bk, bv, xq, gate = _norm_proj(
        x2, even_norm[0], w_in0, (gw, 128, 128, gw, 128, 128, xw, D_MODEL), TM_PROJ,
        qk_rope=(3, 4), rope_args=rope_args, seq=seq)
    y_a = _window_attn(aq, ak, av, bias_win, even_sink[0] * LOG2E, batch, seq, WINDOW_GROUP)
    y_b = _dense_attn(bq, bk, bv, batch, seq, TQ_DENSE)
    y_x = _cross_attn(xq, mk0, mv0, batch, seq, TQ_CROSS)
    lam_init = 0.8 - 0.6 * math.exp(-0.3 * 1)
    cw = C_HEADS * HEAD_DIM
    h1, q1, q2, k1, k2, v, xq1, gate1 = _gate_out(
        [y_a, y_b, y_x], gate, w_out0, x2, odd_norm[0], TM_PROJ, w_next=w_in1,
        splits=(cw, cw, cw, cw, C_HEADS * C_V_DIM, xw, D_MODEL))
    lam_vecs = jnp.stack([odd_lambda_q1[0], odd_lambda_k1[0], odd_lambda_q2[0], odd_lambda_k2[0]])
    bias_max = jnp.broadcast_to(head_bias_max.reshape(C_HEADS, 1, 1), (C_HEADS, 1, LANES))
    y_c = _diff_attn(q1, q2, k1, k2, v, bias_full, bias_max, lam_vecs, odd_subln[0], lam_init,
                     batch, seq, TQ_DIFF)
    y_x1 = _cross_attn(xq1, mk1, mv1, batch, seq, TQ_CROSS)
    (out,) = _gate_out([y_c, y_x1], gate1, w_out1, h1, final_norm, TM_PROJ)
    return out.reshape(batch, seq, d)
```

```python
import functools
import math

import numpy as np
import jax
import jax.numpy as jnp
from jax import lax
from jax.experimental import pallas as pl
from jax.experimental.pallas import tpu as pltpu

D_MODEL = 1024
HEAD_DIM = 64
BLOCK = 128
WINDOW = 128
GRID_W = 64
A_HEADS = 6
B_HEADS = 6
C_HEADS = 6
C_V_DIM = 128
X_HEADS = 4
REL_BUCKETS = 32
REL_MAX_DIST = 128
ROPE_THETA = 10000.0
EPS = 1e-6
NEG_INF = -1e30
LOG2E = math.log2(math.e)
LANES = 128
ONES_ROWS = 16
PROJ_SUB_ROWS = 256
TM_PROJ = 1024
TQ_DENSE = 256
TQ_DIFF = 512
TQ_CROSS = 2048
WINDOW_GROUP = 8
SUM_FLOOR = 2.0 ** -64
VMEM_LIMIT = 56 * 1024 * 1024

F32 = jnp.float32
BF16 = jnp.bfloat16

GQA_HEAD_ORDER = (0, 3, 1, 4, 2, 5)


def _cparams(n_axes, flags=None):
    return pltpu.CompilerParams(dimension_semantics=("arbitrary",) * n_axes,
                                vmem_limit_bytes=VMEM_LIMIT, flags=flags)


def _lane_lo(shape):
    return lax.broadcasted_iota(jnp.int32, shape, len(shape) - 1) < HEAD_DIM


def _rmsnorm_rows(x, g):
    ms = jnp.mean(x * x, axis=-1, keepdims=True)
    return x * lax.rsqrt(ms + EPS) * g


def _nt_dot(a, b):
    return lax.dot_general(a, b, (((1,), (1,)), ((), ())), preferred_element_type=F32)


def _norm_proj_kernel(*refs, qk_rope):
    x_ref, g_ref, w_ref = refs[:3]
    if qk_rope is None:
        out_refs = refs[3:]
    else:
        cos_ref, sin_ref, gq_ref, gk_ref = refs[3:7]
        out_refs = refs[7:]
    tm = x_ref.shape[0]
    for r0 in range(0, tm, PROJ_SUB_ROWS):
        rows = slice(r0, min(r0 + PROJ_SUB_ROWS, tm))
        xn = _rmsnorm_rows(x_ref[rows, :], g_ref[...]).astype(BF16)
        y_all = jnp.dot(xn, w_ref[...], preferred_element_type=F32)
        c0 = 0
        for j, o_ref in enumerate(out_refs):
            width = o_ref.shape[1]
            y = y_all[:, c0:c0 + width]
            if qk_rope is not None and j in qk_rope:
                gain, scale = ((gq_ref, HEAD_DIM ** -0.5 * LOG2E) if j == qk_rope[0]
                               else (gk_ref, 1.0))
                y = jnp.concatenate(
                    [_norm_rope_pair(y[:, t * LANES:(t + 1) * LANES], gain[...],
                                     cos_ref[rows, :], sin_ref[rows, :]) * scale
                     for t in range(width // LANES)], axis=1)
            o_ref[rows, :] = y.astype(o_ref.dtype)
            c0 += width


def _norm_proj(x, g, w_bf16, splits, tm, qk_rope=None, rope_args=(), seq=None):
    rows, d = x.shape
    n = w_bf16.shape[1]
    assert sum(splits) == n and rows % tm == 0
    in_specs = [pl.BlockSpec((tm, d), lambda i: (i, 0)),
                pl.BlockSpec((1, d), lambda i: (0, 0)),
                pl.BlockSpec((d, n), lambda i: (0, 0))]
    if qk_rope is not None:
        pos_blocks = seq // tm
        in_specs += [pl.BlockSpec((tm, LANES), lambda i: (i % pos_blocks, 0)),
                     pl.BlockSpec((tm, LANES), lambda i: (i % pos_blocks, 0)),
                     pl.BlockSpec((1, LANES), lambda i: (0, 0)),
                     pl.BlockSpec((1, LANES), lambda i: (0, 0))]
    return pl.pallas_call(
        functools.partial(_norm_proj_kernel, qk_rope=qk_rope),
        grid=(rows // tm,),
        in_specs=in_specs,
        out_specs=[pl.BlockSpec((tm, s), lambda i: (i, 0)) for s in splits],
        out_shape=[jax.ShapeDtypeStruct((rows, s), BF16) for s in splits],
        compiler_params=_cparams(1),
        name="norm_proj",
    )(x, g.reshape(1, d), w_bf16, *rope_args)


def _gate_out_kernel(*refs, n_parts, after):
    y_refs = refs[:n_parts]
    gate_ref, w_ref, res_ref, g_ref = refs[n_parts:n_parts + 4]
    if after == "proj":
        w_next_ref, h_ref = refs[n_parts + 4:n_parts + 6]
        out_refs = refs[n_parts + 6:]
    else:
        h_ref = refs[n_parts + 4]
    tm = res_ref.shape[0]
    for r0 in range(0, tm, PROJ_SUB_ROWS):
        rows = slice(r0, min(r0 + PROJ_SUB_ROWS, tm))
        parts = []
        c0 = 0
        for y_ref in y_refs:
            width = y_ref.shape[1]
            g = gate_ref[rows, c0:c0 + width].astype(F32)
            silu = g * (1.0 / (1.0 + jnp.exp(-g)))
            parts.append((y_ref[rows, :].astype(F32) * silu).astype(BF16))
            c0 += width
        h = res_ref[rows, :] + jnp.dot(jnp.concatenate(parts, axis=1), w_ref[...],
                                       preferred_element_type=F32)
        if after == "norm":
            h_ref[rows, :] = _rmsnorm_rows(h, g_ref[...])
            continue
        h_ref[rows, :] = h
        y_all = jnp.dot(_rmsnorm_rows(h, g_ref[...]).astype(BF16), w_next_ref[...],
                        preferred_element_type=F32)
        c0 = 0
        for o_ref in out_refs:
            width = o_ref.shape[1]
            o_ref[rows, :] = y_all[:, c0:c0 + width].astype(o_ref.dtype)
            c0 += width


def _gate_out(y_parts, gate, w_bf16, resid, norm_g, tm, w_next=None, splits=()):
    rows, d = resid.shape
    mix = w_bf16.shape[0]
    after = "norm" if w_next is None else "proj"
    row_block = lambda width: pl.BlockSpec((tm, width), lambda i: (i, 0))
    whole = lambda a: pl.BlockSpec(a.shape, lambda i: (0, 0))
    in_specs = [row_block(y.shape[1]) for y in y_parts]
    in_specs += [row_block(mix), whole(w_bf16), row_block(d), pl.BlockSpec((1, d), lambda i: (0, 0))]
    args = list(y_parts) + [gate, w_bf16, resid, norm_g.reshape(1, d)]
    out_specs = [row_block(d)]
    out_shape = [jax.ShapeDtypeStruct((rows, d), F32)]
    if after == "proj":
        assert sum(splits) == w_next.shape[1]
        in_specs.append(whole(w_next))
        args.append(w_next)
        out_specs += [row_block(s) for s in splits]
        out_shape += [jax.ShapeDtypeStruct((rows, s), BF16) for s in splits]
    return pl.pallas_call(
        functools.partial(_gate_out_kernel, n_parts=len(y_parts), after=after),
        grid=(rows // tm,),
        in_specs=in_specs,
        out_specs=out_specs,
        out_shape=out_shape,
        compiler_params=_cparams(1),
        name="gate_out",
    )(*args)


def _bucket_thresholds():
    nb = REL_BUCKETS // 2
    max_exact = nb // 2
    n = np.arange(0, 4 * REL_MAX_DIST)
    nf = np.maximum(n, 1).astype(np.float32)
    large = max_exact + (np.log(nf / np.float32(max_exact))
                         / np.float32(math.log(REL_MAX_DIST / max_exact))
                         * np.float32(nb - max_exact)).astype(np.int32)
    bucket = np.where(n < max_exact, n, np.minimum(large, nb - 1))
    assert np.all(np.diff(bucket) >= 0) and bucket[0] == 0 and bucket[-1] == nb - 1
    thr = [int(np.argmax(bucket >= k)) for k in range(1, nb)]
    assert thr[-1] < REL_MAX_DIST
    return thr


def _bias_tiles_kernel(tab_ref, full_ref, win_ref, *, n_tiles, n_far):
    h = pl.program_id(0)
    thr = _bucket_thresholds()
    nb = REL_BUCKETS // 2
    row = lax.broadcasted_iota(jnp.int32, (BLOCK, BLOCK), 0)
    col = lax.broadcasted_iota(jnp.int32, (BLOCK, BLOCK), 1)
    def tile_of(rel):
        n = jnp.abs(rel)
        neg = jnp.full((BLOCK, BLOCK), tab_ref[0, h], F32)
        pos = jnp.full((BLOCK, BLOCK), tab_ref[nb, h], F32)
        for b in range(1, nb):
            ge = n >= thr[b - 1]
            neg = jnp.where(ge, tab_ref[b, h], neg)
            pos = jnp.where(ge, tab_ref[nb + b, h], pos)
        return jnp.where(rel > 0, pos, neg) * LOG2E

    for k in range(n_tiles):
        rel = (k - n_far) * BLOCK + row - col
        tile = tile_of(rel)
        full_ref[0, k] = tile
        j = k - (n_far - 1)
        if 0 <= j < 3:
            win_ref[j * BLOCK:(j + 1) * BLOCK, :] = jnp.where(jnp.abs(rel) <= WINDOW, tile, NEG_INF)


def _bias_tiles(rel_bias, n_blocks):
    n_far = n_blocks - 1
    n_tiles = 2 * n_blocks - 1
    n_heads = rel_bias.shape[1]
    return pl.pallas_call(
        functools.partial(_bias_tiles_kernel, n_tiles=n_tiles, n_far=n_far),
        grid=(n_heads,),
        in_specs=[pl.BlockSpec(memory_space=pltpu.SMEM)],
        out_specs=[pl.BlockSpec((1, n_tiles, BLOCK, BLOCK), lambda h: (h, 0, 0, 0)),
                   pl.BlockSpec((3 * BLOCK, BLOCK), lambda h: (0, h))],
        out_shape=[jax.ShapeDtypeStruct((n_heads, n_tiles, BLOCK, BLOCK), F32),
                   jax.ShapeDtypeStruct((3 * BLOCK, n_heads * BLOCK), F32)],
        compiler_params=_cparams(1),
        name="bias_tiles",
    )(rel_bias)


def _stack_gqa_heads(q_tiles):
    lo = _lane_lo(q_tiles[0].shape)
    return jnp.concatenate([jnp.where(lo, q, 0.0).astype(BF16) for q in q_tiles]
                           + [jnp.where(lo, 0.0, q).astype(BF16) for q in q_tiles], axis=0)


def _stack_pair_heads(q):
    lo = _lane_lo(q.shape)
    return jnp.concatenate([jnp.where(lo, q, 0.0).astype(BF16),
                            jnp.where(lo, 0.0, q).astype(BF16)], axis=0)


def _window_attn_kernel(q_ref, k_ref, v_ref, bias_ref, sink_row_ref,
                        o_ref, kpad_scr, vtpad_scr, *, n_blocks, group):
    i = pl.program_id(1)
    half = 3 * BLOCK

    @pl.when(i == 0)
    def _():
        zero_blk = jnp.zeros((1, BLOCK, LANES), BF16)
        kpad_scr[0:1] = zero_blk
        kpad_scr[n_blocks + 1:n_blocks + 2] = zero_blk
        kpad_scr[1:n_blocks + 1] = k_ref[...].reshape(n_blocks, BLOCK, LANES)
        v_t = v_ref[...].astype(F32).T.astype(BF16)
        ones = jnp.ones((ONES_ROWS, BLOCK), BF16)
        zero_v = jnp.zeros((HEAD_DIM + ONES_ROWS, BLOCK), BF16)
        for g in range(2):
            vtpad_scr[g, 0] = zero_v
            vtpad_scr[g, n_blocks + 1] = zero_v
            for blk in range(n_blocks):
                vtpad_scr[g, blk + 1, :HEAD_DIM, :] = v_t[g * HEAD_DIM:(g + 1) * HEAD_DIM,
                                                           blk * BLOCK:(blk + 1) * BLOCK]
                vtpad_scr[g, blk + 1, HEAD_DIM:, :] = ones

    sink_row = sink_row_ref[...]

    for u in range(group):
        n = i * group + u
        qf = q_ref[u * BLOCK:(u + 1) * BLOCK, :].astype(F32) * (HEAD_DIM ** -0.5 * LOG2E)
        lhs = _stack_gqa_heads([qf[:, j * LANES:(j + 1) * LANES] for j in range(3)])
        k_win = kpad_scr[pl.ds(n, 3)].reshape(3 * BLOCK, LANES)
        s = _nt_dot(k_win, lhs) + bias_ref[...]
        s = jnp.concatenate([jnp.where(n > 0, s[:BLOCK], NEG_INF), s[BLOCK:2 * BLOCK],
                             jnp.where(n < n_blocks - 1, s[2 * BLOCK:], NEG_INF)], axis=0)
        m = jnp.maximum(jnp.max(s, axis=0, keepdims=True), sink_row)
        p = jnp.exp2(s - m).astype(BF16)
        p_sink = jnp.exp2(sink_row - m)
        outs = []
        for g in range(2):
            acc = None
            for j in range(3):
                pv = jnp.dot(vtpad_scr[g, n + j], p[j * BLOCK:(j + 1) * BLOCK, g * half:(g + 1) * half],
                             preferred_element_type=F32)
                acc = pv if acc is None else acc + pv
            total = acc[HEAD_DIM:HEAD_DIM + 1] + p_sink[:, g * half:(g + 1) * half]
            outs.append(acc[:HEAD_DIM] * (1.0 / total))
        for j in range(3):
            cols = slice(j * BLOCK, (j + 1) * BLOCK)
            pair_t = jnp.concatenate([outs[0][:, cols], outs[1][:, cols]], axis=0)
            o_ref[u * BLOCK:(u + 1) * BLOCK, j * LANES:(j + 1) * LANES] = pair_t.T.astype(o_ref.dtype)


def _window_attn(q, k, v, bias_win_t, sink, batch, seq, group):
    nb = seq // BLOCK
    steps = nb // group
    rows = A_HEADS * BLOCK
    sink_row = jnp.repeat(sink, BLOCK).reshape(1, rows)
    const = lambda shape: pl.BlockSpec(shape, lambda b, i: (0,) * len(shape))
    return pl.pallas_call(
        functools.partial(_window_attn_kernel, n_blocks=nb, group=group),
        grid=(batch, steps),
        in_specs=[pl.BlockSpec((group * BLOCK, 3 * LANES), lambda b, i: (b * steps + i, 0)),
                  pl.BlockSpec((seq, LANES), lambda b, i: (b, 0)),
                  pl.BlockSpec((seq, LANES), lambda b, i: (b, 0)),
                  const(bias_win_t.shape), const((1, rows))],
        out_specs=pl.BlockSpec((group * BLOCK, 3 * LANES), lambda b, i: (b * steps + i, 0)),
        out_shape=jax.ShapeDtypeStruct(q.shape, BF16),
        scratch_shapes=[pltpu.VMEM((nb + 2, BLOCK, LANES), BF16),
                        pltpu.VMEM((2, nb + 2, HEAD_DIM + ONES_ROWS, BLOCK), BF16)],
        compiler_params=_cparams(2),
        name="window_attn",
    )(q, k, v, bias_win_t, sink_row)


def _norm_rope_pair(x, g, cos, sin_signed):
    lo = _lane_lo(x.shape)
    x2 = x * x
    ss_lo = jnp.sum(jnp.where(lo, x2, 0.0), axis=-1, keepdims=True)
    ss_hi = jnp.sum(jnp.where(lo, 0.0, x2), axis=-1, keepdims=True)
    ms = jnp.where(lo, ss_lo, ss_hi) * (1.0 / HEAD_DIM)
    y = x * lax.rsqrt(ms + EPS) * g
    lane = lax.broadcasted_iota(jnp.int32, x.shape, 1)
    quarter = HEAD_DIM // 4
    first = (lane & quarter) == 0
    partner = jnp.where(first, pltpu.roll(y, LANES - quarter, 1), pltpu.roll(y, quarter, 1))
    return y * cos + partner * sin_signed


def _dense_attn_kernel(q_ref, k_ref, v_ref, o_ref, ka_scr, vt_scr, kmax_scr, *, tq):
    seq = k_ref.shape[0]
    kc = 2 * BLOCK
    n_chunks = seq // kc
    grp = 3 * tq

    lane = lax.broadcasted_iota(jnp.int32, (seq, LANES), 1)
    ka_scr[:, :LANES] = k_ref[...]
    ka_scr[:, LANES:] = jnp.where(lane == 0, 1.0, 0.0).astype(BF16)
    kmax_lo, kmax_hi = _pair_key_norm_max(k_ref[...])
    kmax_scr[0:1, :] = kmax_lo
    kmax_scr[1:2, :] = kmax_hi
    v_t = v_ref[...].astype(F32).T.astype(BF16)
    for g in range(2):
        vt_scr[g, :HEAD_DIM, :] = v_t[g * HEAD_DIM:(g + 1) * HEAD_DIM]
        vt_scr[g, HEAD_DIM:, :] = jnp.ones((ONES_ROWS, seq), BF16)

    def query_tile(i, carry):
        rows = pl.ds(pl.multiple_of(i * tq, tq), tq)
        lhs = _stack_gqa_heads([q_ref[rows, j * LANES:(j + 1) * LANES].astype(F32)
                                for j in range(3)])

        def write(acc):
            outs = [a[:HEAD_DIM] * (1.0 / a[HEAD_DIM:HEAD_DIM + 1]) for a in acc]
            for j in range(3):
                cols = slice(j * tq, (j + 1) * tq)
                pair_t = jnp.concatenate([outs[0][:, cols], outs[1][:, cols]], axis=0)
                o_ref[rows, j * LANES:(j + 1) * LANES] = pair_t.T.astype(o_ref.dtype)

        aug = _shift_column(lhs, kmax_scr[0:1, :], kmax_scr[1:2, :], 0.0, 0.0)
        s_all = _nt_dot(ka_scr[...], jnp.concatenate([lhs, aug], axis=1))
        p = jnp.exp2(s_all).astype(BF16)
        acc = [jnp.dot(vt_scr[g], p[:, g * grp:(g + 1) * grp], preferred_element_type=F32)
               for g in range(2)]
        write(acc)
        sums = jnp.concatenate([a[HEAD_DIM:HEAD_DIM + 1] for a in acc], axis=0)

        @pl.when(jnp.logical_not(jnp.min(sums) >= SUM_FLOOR))
        def _():
            s_exact = _nt_dot(ka_scr[:, :LANES], lhs)
            m = None
            acc2 = [None, None]
            for c in range(n_chunks):
                s = s_exact[c * kc:(c + 1) * kc, :]
                m_c = jnp.max(s, axis=0, keepdims=True)
                m_new = m_c if m is None else jnp.maximum(m, m_c)
                pc = jnp.exp2(s - m_new).astype(BF16)
                if m is not None:
                    alpha = jnp.exp2(m - m_new)
                for g in range(2):
                    cols = slice(g * grp, (g + 1) * grp)
                    pv = jnp.dot(vt_scr[g, :, c * kc:(c + 1) * kc], pc[:, cols],
                                 preferred_element_type=F32)
                    acc2[g] = pv if m is None else acc2[g] * alpha[:, cols] + pv
                m = m_new
            write(acc2)

        return carry

    lax.fori_loop(0, seq // tq, query_tile, 0)


def _dense_attn(q, k, v, batch, seq, tq):
    return pl.pallas_call(
        functools.partial(_dense_attn_kernel, tq=tq),
        grid=(batch,),
        in_specs=[pl.BlockSpec((seq, 3 * LANES), lambda b: (b, 0)),
                  pl.BlockSpec((seq, LANES), lambda b: (b, 0)),
                  pl.BlockSpec((seq, LANES), lambda b: (b, 0))],
        out_specs=pl.BlockSpec((seq, 3 * LANES), lambda b: (b, 0)),
        out_shape=jax.ShapeDtypeStruct(q.shape, BF16),
        scratch_shapes=[pltpu.VMEM((seq, 2 * LANES), BF16),
                        pltpu.VMEM((2, HEAD_DIM + ONES_ROWS, seq), BF16),
                        pltpu.VMEM((2, LANES), F32)],
        compiler_params=_cparams(1),
        name="dense_attn",
    )(q, k, v)


def _cross_attn_kernel(q_ref, mk_ref, mv_ref, o_ref, vt_scr, *, tq):
    i = pl.program_id(1)
    n_pairs = q_ref.shape[1] // LANES
    mem_len = mk_ref.shape[0]

    @pl.when(i == 0)
    def _():
        v_t = mv_ref[...].astype(F32).T.astype(BF16)
        for h in range(2 * n_pairs):
            vt_scr[h, :HEAD_DIM, :] = v_t[h * HEAD_DIM:(h + 1) * HEAD_DIM]
            vt_scr[h, HEAD_DIM:, :] = jnp.ones((ONES_ROWS, mem_len), BF16)

    for j in range(n_pairs):
        cols = slice(j * LANES, (j + 1) * LANES)
        lhs = _stack_pair_heads(q_ref[:, cols].astype(F32) * (HEAD_DIM ** -0.5 * LOG2E))
        s = _nt_dot(mk_ref[:, cols], lhs)
        p = jnp.exp2(s - jnp.max(s, axis=0, keepdims=True)).astype(BF16)
        acc = [jnp.dot(vt_scr[2 * j + h], p[:, h * tq:(h + 1) * tq], preferred_element_type=F32)
               for h in range(2)]
        pair_t = jnp.concatenate([a[:HEAD_DIM] * (1.0 / a[HEAD_DIM:HEAD_DIM + 1]) for a in acc],
                                 axis=0)
        o_ref[:, cols] = pair_t.T.astype(o_ref.dtype)


def _cross_attn(q, mk, mv, batch, seq, tq):
    nq = seq // tq
    mem_len = mk.shape[0] // batch
    width = q.shape[1]
    n_pairs = width // LANES
    return pl.pallas_call(
        functools.partial(_cross_attn_kernel, tq=tq),
        grid=(batch, nq),
        in_specs=[pl.BlockSpec((tq, width), lambda b, i: (b * nq + i, 0)),
                  pl.BlockSpec((mem_len, width), lambda b, i: (b, 0)),
                  pl.BlockSpec((mem_len, width), lambda b, i: (b, 0))],
        out_specs=pl.BlockSpec((tq, width), lambda b, i: (b * nq + i, 0)),
        out_shape=jax.ShapeDtypeStruct(q.shape, BF16),
        scratch_shapes=[pltpu.VMEM((2 * n_pairs, HEAD_DIM + ONES_ROWS, mem_len), BF16)],
        compiler_params=_cparams(2),
        name="cross_attn",
    )(q, mk, mv)


def _shift_column(lhs, kmax_lo, kmax_hi, bias_max_lo, bias_max_hi):
    half = lhs.shape[0] // 2
    lf = lhs.astype(F32)
    q_sq = jnp.sum(lf * lf, axis=-1, keepdims=True)
    q_norm = q_sq * lax.rsqrt(jnp.maximum(q_sq, 1e-30))
    row = lax.broadcasted_iota(jnp.int32, lhs.shape, 0)
    lane = lax.broadcasted_iota(jnp.int32, lhs.shape, 1)
    shift = (q_norm * jnp.where(row < half, kmax_lo, kmax_hi)
             + jnp.where(row < half, bias_max_lo, bias_max_hi))
    shift = shift + jnp.abs(shift) * 2.0 ** -7
    return jnp.where(lane == 0, -shift, 0.0).astype(BF16)


def _pair_key_norm_max(k_pair):
    kf = k_pair.astype(F32)
    row = lax.broadcasted_iota(jnp.int32, (LANES, 2 * LANES), 0)
    col = lax.broadcasted_iota(jnp.int32, (LANES, 2 * LANES), 1)
    half_sel = jnp.where((row < HEAD_DIM) == (col < LANES), 1.0, 0.0).astype(BF16)
    sq = jnp.dot((kf * kf).astype(BF16), half_sel, preferred_element_type=F32)
    norm_max = jnp.sqrt(jnp.max(sq, axis=0, keepdims=True))
    return norm_max[:, :LANES], norm_max[:, LANES:]


def _diff_attn_kernel(q1_ref, q2_ref, k1_ref, k2_ref, v_ref, bias_ref, bmax_ref, lam_ref, g_ref,
                      o_ref, vt_scr, ka_scr, kmax_scr, *, tq, n_blocks, lam_init):
    sub = tq // BLOCK
    seq = k1_ref.shape[0]
    kc = 2 * BLOCK
    n_chunks = seq // kc
    k_refs = (k1_ref, k2_ref)

    for head in range(2):
        v_t = v_ref[:, head * C_V_DIM:(head + 1) * C_V_DIM].astype(F32).T
        vt_scr[head, :C_V_DIM, :] = v_t.astype(BF16)
        vt_scr[head, C_V_DIM:, :] = jnp.ones((ONES_ROWS, seq), BF16)
    lane = lax.broadcasted_iota(jnp.int32, (seq, LANES), 1)
    ones_col = jnp.where(lane == 0, 1.0, 0.0).astype(BF16)
    for t in range(2):
        ka_scr[t, :, :LANES] = k_refs[t][...]
        ka_scr[t, :, LANES:] = ones_col
        kmax_lo, kmax_hi = _pair_key_norm_max(k_refs[t][...])
        kmax_scr[t, 0:1, :] = kmax_lo
        kmax_scr[t, 1:2, :] = kmax_hi

    lam_vec = lam_ref[...]
    lam = (jnp.exp(jnp.sum(lam_vec[0:1] * lam_vec[1:2], axis=-1, keepdims=True))
           - jnp.exp(jnp.sum(lam_vec[2:3] * lam_vec[3:4], axis=-1, keepdims=True)) + lam_init)

    def query_tile(i, carry):
        rows = pl.ds(pl.multiple_of(i * tq, tq), tq)
        lhs = [_stack_pair_heads(q_ref[rows, :].astype(F32) * (HEAD_DIM ** -0.5 * LOG2E))
               for q_ref in (q1_ref, q2_ref)]
        _diff_attn_tile(i, rows, lhs, lam, bias_ref, bmax_ref, g_ref, o_ref, vt_scr, ka_scr,
                        kmax_scr, k_refs, tq=tq, sub=sub, kc=kc, n_chunks=n_chunks,
                        n_blocks=n_blocks, lam_init=lam_init)
        return carry

    lax.fori_loop(0, seq // tq, query_tile, 0)


def _diff_attn_tile(i, rows, lhs, lam, bias_ref, bmax_ref, g_ref, o_ref, vt_scr, ka_scr, kmax_scr,
                    k_refs, *, tq, sub, kc, n_chunks, n_blocks, lam_init):
    def bias_chunk(c):
        return jnp.concatenate(
            [jnp.concatenate([bias_ref[head, (n_blocks - 1) - (i * sub + u) + 2 * c + e]
                              for head in range(2) for u in range(sub)], axis=1)
             for e in range(2)], axis=0)

    def shifted_scores(t):
        aug = _shift_column(lhs[t], kmax_scr[t, 0:1, :], kmax_scr[t, 1:2, :],
                            bmax_ref[0], bmax_ref[1])
        return _nt_dot(ka_scr[t], jnp.concatenate([lhs[t], aug], axis=1))

    def pv_of_shifted(s_all):
        p = jnp.concatenate([jnp.exp2(s_all[c * kc:(c + 1) * kc, :] + bias_chunk(c)).astype(BF16)
                             for c in range(n_chunks)], axis=0)
        return [jnp.dot(vt_scr[head], p[:, head * tq:(head + 1) * tq],
                        preferred_element_type=F32) for head in range(2)]

    def pv_running_max(s_all):
        m = None
        acc = [None, None]
        for c in range(n_chunks):
            s = s_all[c * kc:(c + 1) * kc, :] + bias_chunk(c)
            m_c = jnp.max(s, axis=0, keepdims=True)
            m_new = m_c if m is None else jnp.maximum(m, m_c)
            p = jnp.exp2(s - m_new).astype(BF16)
            if m is not None:
                alpha = jnp.exp2(m - m_new)
            for head in range(2):
                cols = slice(head * tq, (head + 1) * tq)
                pv = jnp.dot(vt_scr[head, :, c * kc:(c + 1) * kc], p[:, cols],
                             preferred_element_type=F32)
                acc[head] = pv if m is None else acc[head] * alpha[:, cols] + pv
            m = m_new
        return acc

    def write(acc1, acc2):
        g = g_ref[...] * (1.0 - lam_init)
        for head in range(2):
            o1 = acc1[head][:C_V_DIM] * (1.0 / acc1[head][C_V_DIM:C_V_DIM + 1])
            o2 = acc2[head][:C_V_DIM] * (1.0 / acc2[head][C_V_DIM:C_V_DIM + 1])
            out = o1 - lam * o2
            ms = jnp.mean(out * out, axis=0, keepdims=True)
            y = out * lax.rsqrt(ms + EPS) * g
            o_ref[rows, head * C_V_DIM:(head + 1) * C_V_DIM] = y.T.astype(o_ref.dtype)

    s1 = shifted_scores(0)
    s2 = shifted_scores(1)
    acc1 = pv_of_shifted(s1)
    acc2 = pv_of_shifted(s2)
    write(acc1, acc2)
    sums = jnp.concatenate([a[C_V_DIM:C_V_DIM + 1] for a in acc1 + acc2], axis=0)

    @pl.when(jnp.logical_not(jnp.min(sums) >= SUM_FLOOR))
    def _():
        write(pv_running_max(_nt_dot(k_refs[0][...], lhs[0])),
              pv_running_max(_nt_dot(k_refs[1][...], lhs[1])))


def _diff_attn(q1, q2, k1, k2, v, bias_t, bias_max, lam_vecs, subln_g, lam_init, batch, seq, tq):
    nb = seq // BLOCK
    n_pairs = C_HEADS // 2
    n_tiles = bias_t.shape[1]
    seq_spec = pl.BlockSpec((seq, LANES), lambda b, p: (b, p))
    g_cols = jnp.broadcast_to(subln_g.reshape(C_V_DIM, 1), (C_V_DIM, tq))
    return pl.pallas_call(
        functools.partial(_diff_attn_kernel, tq=tq, n_blocks=nb, lam_init=lam_init),
        grid=(batch, n_pairs),
        in_specs=[seq_spec, seq_spec, seq_spec, seq_spec,
                  pl.BlockSpec((seq, 2 * C_V_DIM), lambda b, p: (b, p)),
                  pl.BlockSpec((2, n_tiles, BLOCK, BLOCK), lambda b, p: (p, 0, 0, 0)),
                  pl.BlockSpec((2, 1, LANES), lambda b, p: (p, 0, 0)),
                  pl.BlockSpec(lam_vecs.shape, lambda b, p: (0, 0)),
                  pl.BlockSpec((C_V_DIM, tq), lambda b, p: (0, 0))],
        out_specs=pl.BlockSpec((seq, 2 * C_V_DIM), lambda b, p: (b, p)),
        out_shape=jax.ShapeDtypeStruct(v.shape, BF16),
        scratch_shapes=[pltpu.VMEM((2, C_V_DIM + ONES_ROWS, seq), BF16),
                        pltpu.VMEM((2, seq, 2 * LANES), BF16),
                        pltpu.VMEM((2, 2, LANES), F32)],
        compiler_params=_cparams(2),
        name="diff_attn",
    )(q1, q2, k1, k2, v, bias_t, bias_max, lam_vecs, g_cols)


def _rope_tables(seq):
    rows = seq // GRID_W
    row = jnp.broadcast_to(jnp.arange(rows)[:, None], (rows, GRID_W)).reshape(-1)
    col = jnp.broadcast_to(jnp.arange(GRID_W)[None, :], (rows, GRID_W)).reshape(-1)
    half = HEAD_DIM // 2
    inv = 1.0 / (ROPE_THETA ** (jnp.arange(0, half, 2, dtype=F32) / half))
    ang_row = row.astype(F32)[:, None] * inv
    ang_col = col.astype(F32)[:, None] * inv
    cos = jnp.concatenate([jnp.cos(ang_row)] * 2 + [jnp.cos(ang_col)] * 2, axis=-1)
    sin = jnp.concatenate([-jnp.sin(ang_row), jnp.sin(ang_row),
                           -jnp.sin(ang_col), jnp.sin(ang_col)], axis=-1)
    return jnp.tile(cos, (1, 2)), jnp.tile(sin, (1, 2))


def _gqa_cols():
    return np.concatenate([np.arange(h * HEAD_DIM, (h + 1) * HEAD_DIM) for h in GQA_HEAD_ORDER])


def kernel(x, mem, rel_bias, mem_norm, final_norm, even_norm, even_w_in, even_sink, even_q_norm, even_k_norm, even_w_mem_kv, even_w_out, odd_norm, odd_w_in, odd_lambda_q1, odd_lambda_k1, odd_lambda_q2, odd_lambda_k2, odd_subln, odd_w_mem_kv, odd_w_out):
    batch, seq, d = x.shape
    mem_len = mem.shape[1]
    tokens = batch * seq
    nb = seq // BLOCK
    xw = X_HEADS * HEAD_DIM
    gw = A_HEADS * HEAD_DIM

    gq = _gqa_cols()
    aq0, ak0, av0 = 0, gw, gw + 128
    bq0 = av0 + 128
    bk0, bv0 = bq0 + gw, bq0 + gw + 128
    xq0 = bv0 + 128
    gate0 = xq0 + xw
    even_cols = np.concatenate([aq0 + gq, np.arange(ak0, bq0), bq0 + gq, np.arange(bk0, gate0),
                                gate0 + gq, gate0 + gw + gq,
                                np.arange(gate0 + 2 * gw, gate0 + D_MODEL)])
    mix_rows = np.concatenate([gq, gw + gq, np.arange(2 * gw, D_MODEL)])
    w_in0 = even_w_in[0].astype(BF16)[:, even_cols]
    w_out0 = even_w_out[0].astype(BF16)[mix_rows, :]
    w_in1 = odd_w_in[0].astype(BF16)
    w_out1 = odd_w_out[0].astype(BF16)

    x2 = x.reshape(tokens, d)
    mem2 = mem.reshape(batch * mem_len, d)

    bias_full, bias_win = _bias_tiles(rel_bias, nb)
    cos, sin_signed = _rope_tables(seq)
    head_bias_max = jnp.max(rel_bias, axis=0) * LOG2E

    w_mem = jnp.concatenate([even_w_mem_kv[0], odd_w_mem_kv[0]], axis=1).astype(BF16)
    mk0, mv0, mk1, mv1 = _norm_proj(mem2, mem_norm, w_mem, (xw,) * 4, TM_PROJ)

    rope_args = (cos, sin_signed, jnp.tile(even_q_norm[0], 2).reshape(1, LANES),
                 jnp.tile(even_k_norm[0], 2).reshape(1, LANES))
    aq, ak, av, bq, bk, bv, xq, gate = _norm_proj(
        x2, even_norm[0], w_in0, (gw, 128, 128, gw, 128, 128, xw, D_MODEL), TM_PROJ,
        qk_rope=(3, 4), rope_args=rope_args, seq=seq)
    y_a = _window_attn(aq, ak, av, bias_win, even_sink[0] * LOG2E, batch, seq, WINDOW_GROUP)
    y_b = _dense_attn(bq, bk, bv, batch, seq, TQ_DENSE)
    y_x = _cross_attn(xq, mk0, mv0, batch, seq, TQ_CROSS)
    lam_init = 0.8 - 0.6 * math.exp(-0.3 * 1)
    cw = C_HEADS * HEAD_DIM
    h1, q1, q2, k1, k2, v, xq1, gate1 = _gate_out(
        [y_a, y_b, y_x], gate, w_out0, x2, odd_norm[0], TM_PROJ, w_next=w_in1,
        splits=(cw, cw, cw, cw, C_HEADS * C_V_DIM, xw, D_MODEL))
    lam_vecs = jnp.stack([odd_lambda_q1[0], odd_lambda_k1[0], odd_lambda_q2[0], odd_lambda_k2[0]])
    bias_max = jnp.broadcast_to(head_bias_max.reshape(C_HEADS, 1, 1), (C_HEADS, 1, LANES))
    y_c = _diff_attn(q1, q2, k1, k2, v, bias_full, bias_max, lam_vecs, odd_subln[0], lam_init,
                     batch, seq, TQ_DIFF)
    y_x1 = _cross_attn(xq1, mk1, mv1, batch, seq, TQ_CROSS)
    (out,) = _gate_out([y_c, y_x1], gate1, w_out1, h1, final_norm, TM_PROJ)
    return out.reshape(batch, seq, d)
```

```python
import functools
import math

import numpy as np
import jax
import jax.numpy as jnp
from jax import lax
from jax.experimental import pallas as pl
from jax.experimental.pallas import tpu as pltpu

D_MODEL = 1024
HEAD_DIM = 64
BLOCK = 128
WINDOW = 128
GRID_W = 64
A_HEADS = 6
B_HEADS = 6
C_HEADS = 6
C_V_DIM = 128
X_HEADS = 4
REL_BUCKETS = 32
REL_MAX_DIST = 128
ROPE_THETA = 10000.0
EPS = 1e-6
NEG_INF = -1e30
LOG2E = math.log2(math.e)
LANES = 128
ONES_ROWS = 16
PROJ_SUB_ROWS = 256
TM_PROJ = 1024
TQ_DENSE = 256
TQ_DIFF = 512
TQ_CROSS = 2048
WINDOW_GROUP = 8
SUM_FLOOR = 2.0 ** -64
BIAS_FAR = 2
VMEM_LIMIT = 56 * 1024 * 1024

F32 = jnp.float32
BF16 = jnp.bfloat16


def _cparams(n_axes, flags=None):
    return pltpu.CompilerParams(dimension_semantics=("arbitrary",) * n_axes,
                                vmem_limit_bytes=VMEM_LIMIT, flags=flags)


def _lane_lo(shape):
    return lax.broadcasted_iota(jnp.int32, shape, len(shape) - 1) < HEAD_DIM


def _rmsnorm_rows(x, g):
    ms = jnp.mean(x * x, axis=-1, keepdims=True)
    return x * lax.rsqrt(ms + EPS) * g


def _nt_dot(a, b):
    return lax.dot_general(a, b, (((1,), (1,)), ((), ())), preferred_element_type=F32)


def _norm_proj_kernel(*refs, qk_rope, gate_last):
    x_ref, g_ref, w_ref = refs[:3]
    if qk_rope is None:
        out_refs = refs[3:]
    else:
        cos_ref, sin_ref, gq_ref, gk_ref = refs[3:7]
        out_refs = refs[7:]
    tm = x_ref.shape[0]
    for r0 in range(0, tm, PROJ_SUB_ROWS):
        rows = slice(r0, min(r0 + PROJ_SUB_ROWS, tm))
        xn = _rmsnorm_rows(x_ref[rows, :], g_ref[...]).astype(BF16)
        y_all = jnp.dot(xn, w_ref[...], preferred_element_type=F32)
        c0 = 0
        for j, o_ref in enumerate(out_refs):
            width = o_ref.shape[1]
            y = y_all[:, c0:c0 + width]
            if qk_rope is not None and j in qk_rope:
                gain, scale = ((gq_ref, HEAD_DIM ** -0.5 * LOG2E) if j == qk_rope[0]
                               else (gk_ref, 1.0))
                y = jnp.concatenate(
                    [_norm_rope_pair(y[:, t * LANES:(t + 1) * LANES], gain[...],
                                     cos_ref[rows, :], sin_ref[rows, :]) * scale
                     for t in range(width // LANES)], axis=1)
            if gate_last and j == len(out_refs) - 1:
                y = _silu(y)
            o_ref[rows, :] = y.astype(o_ref.dtype)
            c0 += width


def _silu(y):
    return y * (1.0 / (1.0 + jnp.exp(-y)))


def _norm_proj(x, g, w_bf16, splits, tm, gate_last, qk_rope=None, rope_args=(), seq=None):
    rows, d = x.shape
    n = w_bf16.shape[1]
    assert sum(splits) == n and rows % tm == 0
    in_specs = [pl.BlockSpec((tm, d), lambda i: (i, 0)),
                pl.BlockSpec((1, d), lambda i: (0, 0)),
                pl.BlockSpec((d, n), lambda i: (0, 0))]
    if qk_rope is not None:
        pos_blocks = seq // tm
        in_specs += [pl.BlockSpec((tm, LANES), lambda i: (i % pos_blocks, 0)),
                     pl.BlockSpec((tm, LANES), lambda i: (i % pos_blocks, 0)),
                     pl.BlockSpec((1, LANES), lambda i: (0, 0)),
                     pl.BlockSpec((1, LANES), lambda i: (0, 0))]
    return pl.pallas_call(
        functools.partial(_norm_proj_kernel, qk_rope=qk_rope, gate_last=gate_last),
        grid=(rows // tm,),
        in_specs=in_specs,
        out_specs=[pl.BlockSpec((tm, s), lambda i: (i, 0)) for s in splits],
        out_shape=[jax.ShapeDtypeStruct((rows, s), BF16) for s in splits],
        compiler_params=_cparams(1),
        name="norm_proj",
    )(x, g.reshape(1, d), w_bf16, *rope_args)


def _gate_out_kernel(*refs, n_parts, after):
    y_refs = refs[:n_parts]
    gate_ref, w_ref, res_ref, g_ref = refs[n_parts:n_parts + 4]
    if after == "proj":
        w_next_ref, h_ref = refs[n_parts + 4:n_parts + 6]
        out_refs = refs[n_parts + 6:]
    else:
        h_ref = refs[n_parts + 4]
    tm = res_ref.shape[0]
    for r0 in range(0, tm, PROJ_SUB_ROWS):
        rows = slice(r0, min(r0 + PROJ_SUB_ROWS, tm))
        parts = []
        c0 = 0
        for y_ref in y_refs:
            width = y_ref.shape[1]
            parts.append(y_ref[rows, :] * gate_ref[rows, c0:c0 + width])
            c0 += width
        h = res_ref[rows, :] + jnp.dot(jnp.concatenate(parts, axis=1), w_ref[...],
                                       preferred_element_type=F32)
        if after == "norm":
            h_ref[rows, :] = _rmsnorm_rows(h, g_ref[...])
            continue
        h_ref[rows, :] = h
        y_all = jnp.dot(_rmsnorm_rows(h, g_ref[...]).astype(BF16), w_next_ref[...],
                        preferred_element_type=F32)
        c0 = 0
        for j, o_ref in enumerate(out_refs):
            width = o_ref.shape[1]
            y = y_all[:, c0:c0 + width]
            if j == len(out_refs) - 1:
                y = _silu(y)
            o_ref[rows, :] = y.astype(o_ref.dtype)
            c0 += width


def _gate_out(y_parts, gate, w_bf16, resid, norm_g, tm, w_next=None, splits=()):
    rows, d = resid.shape
    mix = w_bf16.shape[0]
    after = "norm" if w_next is None else "proj"
    row_block = lambda width: pl.BlockSpec((tm, width), lambda i: (i, 0))
    whole = lambda a: pl.BlockSpec(a.shape, lambda i: (0, 0))
    in_specs = [row_block(y.shape[1]) for y in y_parts]
    in_specs += [row_block(mix), whole(w_bf16), row_block(d), pl.BlockSpec((1, d), lambda i: (0, 0))]
    args = list(y_parts) + [gate, w_bf16, resid, norm_g.reshape(1, d)]
    out_specs = [row_block(d)]
    out_shape = [jax.ShapeDtypeStruct((rows, d), F32)]
    if after == "proj":
        assert sum(splits) == w_next.shape[1]
        in_specs.append(whole(w_next))
        args.append(w_next)
        out_specs += [row_block(s) for s in splits]
        out_shape += [jax.ShapeDtypeStruct((rows, s), BF16) for s in splits]
    return pl.pallas_call(
        functools.partial(_gate_out_kernel, n_parts=len(y_parts), after=after),
        grid=(rows // tm,),
        in_specs=in_specs,
        out_specs=out_specs,
        out_shape=out_shape,
        compiler_params=_cparams(1),
        name="gate_out",
    )(*args)


def _bucket_thresholds():
    nb = REL_BUCKETS // 2
    max_exact = nb // 2
    n = np.arange(0, 4 * REL_MAX_DIST)
    nf = np.maximum(n, 1).astype(np.float32)
    large = max_exact + (np.log(nf / np.float32(max_exact))
                         / np.float32(math.log(REL_MAX_DIST / max_exact))
                         * np.float32(nb - max_exact)).astype(np.int32)
    bucket = np.where(n < max_exact, n, np.minimum(large, nb - 1))
    assert np.all(np.diff(bucket) >= 0) and bucket[0] == 0 and bucket[-1] == nb - 1
    thr = [int(np.argmax(bucket >= k)) for k in range(1, nb)]
    assert thr[-1] < REL_MAX_DIST
    return thr


def _bias_tiles_kernel(tab_ref, near_ref, win_ref):
    h = pl.program_id(0)
    thr = _bucket_thresholds()
    nb = REL_BUCKETS // 2
    row = lax.broadcasted_iota(jnp.int32, (BLOCK, BLOCK), 0)
    col = lax.broadcasted_iota(jnp.int32, (BLOCK, BLOCK), 1)
    def tile_of(rel):
        n = jnp.abs(rel)
        neg = jnp.full((BLOCK, BLOCK), tab_ref[0, h], F32)
        pos = jnp.full((BLOCK, BLOCK), tab_ref[nb, h], F32)
        for b in range(1, nb):
            ge = n >= thr[b - 1]
            neg = jnp.where(ge, tab_ref[b, h], neg)
            pos = jnp.where(ge, tab_ref[nb + b, h], pos)
        return jnp.where(rel > 0, pos, neg) * LOG2E

    for k in range(2 * BIAS_FAR + 1):
        rel = (k - BIAS_FAR) * BLOCK + row - col
        tile = tile_of(rel)
        near_ref[0, k] = tile
        if abs(k - BIAS_FAR) <= 1:
            j = k - BIAS_FAR + 1
            win_ref[j * BLOCK:(j + 1) * BLOCK, :] = jnp.where(jnp.abs(rel) <= WINDOW, tile, NEG_INF)


def _bias_tiles(rel_bias):
    assert _bucket_thresholds()[-1] <= (BIAS_FAR - 1) * BLOCK + 1
    n_tiles = 2 * BIAS_FAR + 1
    n_heads = rel_bias.shape[1]
    return pl.pallas_call(
        _bias_tiles_kernel,
        grid=(n_heads,),
        in_specs=[pl.BlockSpec(memory_space=pltpu.SMEM)],
        out_specs=[pl.BlockSpec((1, n_tiles, BLOCK, BLOCK), lambda h: (h, 0, 0, 0)),
                   pl.BlockSpec((3 * BLOCK, BLOCK), lambda h: (0, h))],
        out_shape=[jax.ShapeDtypeStruct((n_heads, n_tiles, BLOCK, BLOCK), F32),
                   jax.ShapeDtypeStruct((3 * BLOCK, n_heads * BLOCK), F32)],
        compiler_params=_cparams(1),
        name="bias_tiles",
    )(rel_bias)


def _stack_gqa_heads(q_tiles):
    lo = _lane_lo(q_tiles[0].shape)
    swap = lambda q: pltpu.roll(q, HEAD_DIM, 1)
    heads_lo = [q_tiles[0], swap(q_tiles[0]), q_tiles[1]]
    heads_hi = [q_tiles[1], swap(q_tiles[2]), q_tiles[2]]
    return jnp.concatenate([jnp.where(lo, q, 0.0).astype(BF16) for q in heads_lo]
                           + [jnp.where(lo, 0.0, q).astype(BF16) for q in heads_hi], axis=0)


def _unstack_gqa_heads(out_lo, out_hi, width):
    blk = lambda o, j: o[:, j * width:(j + 1) * width]
    return [jnp.concatenate([blk(out_lo, 0), blk(out_lo, 1)], axis=0),
            jnp.concatenate([blk(out_lo, 2), blk(out_hi, 0)], axis=0),
            jnp.concatenate([blk(out_hi, 1), blk(out_hi, 2)], axis=0)]


def _stack_pair_heads(q):
    lo = _lane_lo(q.shape)
    return jnp.concatenate([jnp.where(lo, q, 0.0).astype(BF16),
                            jnp.where(lo, 0.0, q).astype(BF16)], axis=0)


def _window_attn_kernel(q_ref, k_ref, v_ref, bias_ref, sink_row_ref,
                        o_ref, kpad_scr, vtpad_scr, *, n_blocks, group):
    i = pl.program_id(1)
    half = 3 * BLOCK

    @pl.when(i == 0)
    def _():
        zero_blk = jnp.zeros((1, BLOCK, LANES), BF16)
        kpad_scr[0:1] = zero_blk
        kpad_scr[n_blocks + 1:n_blocks + 2] = zero_blk
        kpad_scr[1:n_blocks + 1] = k_ref[...].reshape(n_blocks, BLOCK, LANES)
        v_t = v_ref[...].astype(F32).T.astype(BF16)
        ones = jnp.ones((ONES_ROWS, BLOCK), BF16)
        zero_v = jnp.zeros((HEAD_DIM + ONES_ROWS, BLOCK), BF16)
        for g in range(2):
            vtpad_scr[g, 0] = zero_v
            vtpad_scr[g, n_blocks + 1] = zero_v
            for blk in range(n_blocks):
                vtpad_scr[g, blk + 1, :HEAD_DIM, :] = v_t[g * HEAD_DIM:(g + 1) * HEAD_DIM,
                                                           blk * BLOCK:(blk + 1) * BLOCK]
                vtpad_scr[g, blk + 1, HEAD_DIM:, :] = ones

    sink_row = sink_row_ref[...]

    for u in range(group):
        n = i * group + u
        qf = q_ref[u * BLOCK:(u + 1) * BLOCK, :].astype(F32) * (HEAD_DIM ** -0.5 * LOG2E)
        lhs = _stack_gqa_heads([qf[:, j * LANES:(j + 1) * LANES] for j in range(3)])
        k_win = kpad_scr[pl.ds(n, 3)].reshape(3 * BLOCK, LANES)
        s = _nt_dot(k_win, lhs) + bias_ref[...]
        s = jnp.concatenate([jnp.where(n > 0, s[:BLOCK], NEG_INF), s[BLOCK:2 * BLOCK],
                             jnp.where(n < n_blocks - 1, s[2 * BLOCK:], NEG_INF)], axis=0)
        m = jnp.maximum(jnp.max(s, axis=0, keepdims=True), sink_row)
        p = jnp.exp2(s - m).astype(BF16)
        p_sink = jnp.exp2(sink_row - m)
        outs = []
        for g in range(2):
            acc = None
            for j in range(3):
                pv = jnp.dot(vtpad_scr[g, n + j], p[j * BLOCK:(j + 1) * BLOCK, g * half:(g + 1) * half],
                             preferred_element_type=F32)
                acc = pv if acc is None else acc + pv
            total = acc[HEAD_DIM:HEAD_DIM + 1] + p_sink[:, g * half:(g + 1) * half]
            outs.append(acc[:HEAD_DIM] * (1.0 / total))
        for j, pair_t in enumerate(_unstack_gqa_heads(outs[0], outs[1], BLOCK)):
            o_ref[u * BLOCK:(u + 1) * BLOCK, j * LANES:(j + 1) * LANES] = pair_t.T.astype(o_ref.dtype)


def _window_attn(q, k, v, bias_win_t, sink, batch, seq, group):
    nb = seq // BLOCK
    steps = nb // group
    rows = A_HEADS * BLOCK
    sink_row = jnp.repeat(sink, BLOCK).reshape(1, rows)
    const = lambda shape: pl.BlockSpec(shape, lambda b, i: (0,) * len(shape))
    return pl.pallas_call(
        functools.partial(_window_attn_kernel, n_blocks=nb, group=group),
        grid=(batch, steps),
        in_specs=[pl.BlockSpec((group * BLOCK, 3 * LANES), lambda b, i: (b * steps + i, 0)),
                  pl.BlockSpec((seq, LANES), lambda b, i: (b, 0)),
                  pl.BlockSpec((seq, LANES), lambda b, i: (b, 0)),
                  const(bias_win_t.shape), const((1, rows))],
        out_specs=pl.BlockSpec((group * BLOCK, 3 * LANES), lambda b, i: (b * steps + i, 0)),
        out_shape=jax.ShapeDtypeStruct(q.shape, BF16),
        scratch_shapes=[pltpu.VMEM((nb + 2, BLOCK, LANES), BF16),
                        pltpu.VMEM((2, nb + 2, HEAD_DIM + ONES_ROWS, BLOCK), BF16)],
        compiler_params=_cparams(2),
        name="window_attn",
    )(q, k, v, bias_win_t, sink_row)


def _norm_rope_pair(x, g, cos, sin_signed):
    lo = _lane_lo(x.shape)
    x2 = x * x
    ss_lo = jnp.sum(jnp.where(lo, x2, 0.0), axis=-1, keepdims=True)
    ss_hi = jnp.sum(jnp.where(lo, 0.0, x2), axis=-1, keepdims=True)
    ms = jnp.where(lo, ss_lo, ss_hi) * (1.0 / HEAD_DIM)
    y = x * lax.rsqrt(ms + EPS) * g
    lane = lax.broadcasted_iota(jnp.int32, x.shape, 1)
    quarter = HEAD_DIM // 4
    first = (lane & quarter) == 0
    partner = jnp.where(first, pltpu.roll(y, LANES - quarter, 1), pltpu.roll(y, quarter, 1))
    return y * cos + partner * sin_signed


def _dense_attn_kernel(q_ref, k_ref, v_ref, o_ref, ka_scr, vt_scr, kmax_scr, *, tq):
    seq = k_ref.shape[0]
    kc = 2 * BLOCK
    n_chunks = seq // kc
    grp = 3 * tq

    lane = lax.broadcasted_iota(jnp.int32, (seq, LANES), 1)
    ka_scr[:, :LANES] = k_ref[...]
    ka_scr[:, LANES:] = jnp.where(lane == 0, 1.0, 0.0).astype(BF16)
    kmax_lo, kmax_hi = _pair_key_norm_max(k_ref[...])
    kmax_scr[0:1, :] = kmax_lo
    kmax_scr[1:2, :] = kmax_hi
    v_t = v_ref[...].astype(F32).T.astype(BF16)
    for g in range(2):
        vt_scr[g, :HEAD_DIM, :] = v_t[g * HEAD_DIM:(g + 1) * HEAD_DIM]
        vt_scr[g, HEAD_DIM:, :] = jnp.ones((ONES_ROWS, seq), BF16)

    def query_tile(i, carry):
        rows = pl.ds(pl.multiple_of(i * tq, tq), tq)
        lhs = _stack_gqa_heads([q_ref[rows, j * LANES:(j + 1) * LANES].astype(F32)
                                for j in range(3)])

        def write(acc):
            outs = [a[:HEAD_DIM] * (1.0 / a[HEAD_DIM:HEAD_DIM + 1]) for a in acc]
            for j, pair_t in enumerate(_unstack_gqa_heads(outs[0], outs[1], tq)):
                o_ref[rows, j * LANES:(j + 1) * LANES] = pair_t.T.astype(o_ref.dtype)

        aug = _shift_column(lhs, kmax_scr[0:1, :], kmax_scr[1:2, :], 0.0, 0.0)
        s_all = _nt_dot(ka_scr[...], jnp.concatenate([lhs, aug], axis=1))
        p = jnp.exp2(s_all).astype(BF16)
        acc = [jnp.dot(vt_scr[g], p[:, g * grp:(g + 1) * grp], preferred_element_type=F32)
               for g in range(2)]
        write(acc)
        sums = jnp.concatenate([a[HEAD_DIM:HEAD_DIM + 1] for a in acc], axis=0)

        @pl.when(jnp.logical_not(jnp.min(sums) >= SUM_FLOOR))
        def _():
            s_exact = _nt_dot(ka_scr[:, :LANES], lhs)
            m = None
            acc2 = [None, None]
            for c in range(n_chunks):
                s = s_exact[c * kc:(c + 1) * kc, :]
                m_c = jnp.max(s, axis=0, keepdims=True)
                m_new = m_c if m is None else jnp.maximum(m, m_c)
                pc = jnp.exp2(s - m_new).astype(BF16)
                if m is not None:
                    alpha = jnp.exp2(m - m_new)
                for g in range(2):
                    cols = slice(g * grp, (g + 1) * grp)
                    pv = jnp.dot(vt_scr[g, :, c * kc:(c + 1) * kc], pc[:, cols],
                                 preferred_element_type=F32)
                    acc2[g] = pv if m is None else acc2[g] * alpha[:, cols] + pv
                m = m_new
            write(acc2)

        return carry

    lax.fori_loop(0, seq // tq, query_tile, 0)


def _dense_attn(q, k, v, batch, seq, tq):
    return pl.pallas_call(
        functools.partial(_dense_attn_kernel, tq=tq),
        grid=(batch,),
        in_specs=[pl.BlockSpec((seq, 3 * LANES), lambda b: (b, 0)),
                  pl.BlockSpec((seq, LANES), lambda b: (b, 0)),
                  pl.BlockSpec((seq, LANES), lambda b: (b, 0))],
        out_specs=pl.BlockSpec((seq, 3 * LANES), lambda b: (b, 0)),
        out_shape=jax.ShapeDtypeStruct(q.shape, BF16),
        scratch_shapes=[pltpu.VMEM((seq, 2 * LANES), BF16),
                        pltpu.VMEM((2, HEAD_DIM + ONES_ROWS, seq), BF16),
                        pltpu.VMEM((2, LANES), F32)],
        compiler_params=_cparams(1),
        name="dense_attn",
    )(q, k, v)


def _cross_attn_kernel(q_ref, mk_ref, mv_ref, o_ref, vt_scr, *, tq):
    i = pl.program_id(1)
    n_pairs = q_ref.shape[1] // LANES
    mem_len = mk_ref.shape[0]

    @pl.when(i == 0)
    def _():
        v_t = mv_ref[...].astype(F32).T.astype(BF16)
        for h in range(2 * n_pairs):
            vt_scr[h, :HEAD_DIM, :] = v_t[h * HEAD_DIM:(h + 1) * HEAD_DIM]
            vt_scr[h, HEAD_DIM:, :] = jnp.ones((ONES_ROWS, mem_len), BF16)

    for j in range(n_pairs):
        cols = slice(j * LANES, (j + 1) * LANES)
        lhs = _stack_pair_heads(q_ref[:, cols].astype(F32) * (HEAD_DIM ** -0.5 * LOG2E))
        s = _nt_dot(mk_ref[:, cols], lhs)
        p = jnp.exp2(s - jnp.max(s, axis=0, keepdims=True)).astype(BF16)
        acc = [jnp.dot(vt_scr[2 * j + h], p[:, h * tq:(h + 1) * tq], preferred_element_type=F32)
               for h in range(2)]
        pair_t = jnp.concatenate([a[:HEAD_DIM] * (1.0 / a[HEAD_DIM:HEAD_DIM + 1]) for a in acc],
                                 axis=0)
        o_ref[:, cols] = pair_t.T.astype(o_ref.dtype)


def _cross_attn(q, mk, mv, batch, seq, tq):
    nq = seq // tq
    mem_len = mk.shape[0] // batch
    width = q.shape[1]
    n_pairs = width // LANES
    return pl.pallas_call(
        functools.partial(_cross_attn_kernel, tq=tq),
        grid=(batch, nq),
        in_specs=[pl.BlockSpec((tq, width), lambda b, i: (b * nq + i, 0)),
                  pl.BlockSpec((mem_len, width), lambda b, i: (b, 0)),
                  pl.BlockSpec((mem_len, width), lambda b, i: (b, 0))],
        out_specs=pl.BlockSpec((tq, width), lambda b, i: (b * nq + i, 0)),
        out_shape=jax.ShapeDtypeStruct(q.shape, BF16),
        scratch_shapes=[pltpu.VMEM((2 * n_pairs, HEAD_DIM + ONES_ROWS, mem_len), BF16)],
        compiler_params=_cparams(2),
        name="cross_attn",
    )(q, mk, mv)


def _shift_column(lhs, kmax_lo, kmax_hi, bias_max_lo, bias_max_hi):
    half = lhs.shape[0] // 2
    lf = lhs.astype(F32)
    q_sq = jnp.sum(lf * lf, axis=-1, keepdims=True)
    q_norm = q_sq * lax.rsqrt(jnp.maximum(q_sq, 1e-30))
    row = lax.broadcasted_iota(jnp.int32, lhs.shape, 0)
    lane = lax.broadcasted_iota(jnp.int32, lhs.shape, 1)
    shift = (q_norm * jnp.where(row < half, kmax_lo, kmax_hi)
             + jnp.where(row < half, bias_max_lo, bias_max_hi))
    shift = shift + jnp.abs(shift) * 2.0 ** -7
    return jnp.where(lane == 0, -shift, 0.0).astype(BF16)


def _pair_key_norm_max(k_pair):
    kf = k_pair.astype(F32)
    row = lax.broadcasted_iota(jnp.int32, (LANES, 2 * LANES), 0)
    col = lax.broadcasted_iota(jnp.int32, (LANES, 2 * LANES), 1)
    half_sel = jnp.where((row < HEAD_DIM) == (col < LANES), 1.0, 0.0).astype(BF16)
    sq = jnp.dot((kf * kf).astype(BF16), half_sel, preferred_element_type=F32)
    norm_max = jnp.sqrt(jnp.max(sq, axis=0, keepdims=True))
    return norm_max[:, :LANES], norm_max[:, LANES:]


def _diff_attn_kernel(q1_ref, q2_ref, k1_ref, k2_ref, v_ref, bias_ref, bmax_ref, lam_ref, g_ref,
                      o_ref, vt_scr, ka_scr, kmax_scr, *, tq, lam_init):
    sub = tq // BLOCK
    seq = k1_ref.shape[0]
    kc = 2 * BLOCK
    n_chunks = seq // kc
    k_refs = (k1_ref, k2_ref)

    for head in range(2):
        v_t = v_ref[:, head * C_V_DIM:(head + 1) * C_V_DIM].astype(F32).T
        vt_scr[head, :C_V_DIM, :] = v_t.astype(BF16)
        vt_scr[head, C_V_DIM:, :] = jnp.ones((ONES_ROWS, seq), BF16)
    lane = lax.broadcasted_iota(jnp.int32, (seq, LANES), 1)
    ones_col = jnp.where(lane == 0, 1.0, 0.0).astype(BF16)
    for t in range(2):
        ka_scr[t, :, :LANES] = k_refs[t][...]
        ka_scr[t, :, LANES:] = ones_col
        kmax_lo, kmax_hi = _pair_key_norm_max(k_refs[t][...])
        kmax_scr[t, 0:1, :] = kmax_lo
        kmax_scr[t, 1:2, :] = kmax_hi

    lam_vec = lam_ref[...]
    lam = (jnp.exp(jnp.sum(lam_vec[0:1] * lam_vec[1:2], axis=-1, keepdims=True))
           - jnp.exp(jnp.sum(lam_vec[2:3] * lam_vec[3:4], axis=-1, keepdims=True)) + lam_init)

    def query_tile(i, carry):
        rows = pl.ds(pl.multiple_of(i * tq, tq), tq)
        lhs = [_stack_pair_heads(q_ref[rows, :].astype(F32) * (HEAD_DIM ** -0.5 * LOG2E))
               for q_ref in (q1_ref, q2_ref)]
        _diff_attn_tile(i, rows, lhs, lam, bias_ref, bmax_ref, g_ref, o_ref, vt_scr, ka_scr,
                        kmax_scr, k_refs, tq=tq, sub=sub, kc=kc, n_chunks=n_chunks,
                        lam_init=lam_init)
        return carry

    lax.fori_loop(0, seq // tq, query_tile, 0)


def _diff_attn_tile(i, rows, lhs, lam, bias_ref, bmax_ref, g_ref, o_ref, vt_scr, ka_scr, kmax_scr,
                    k_refs, *, tq, sub, kc, n_chunks, lam_init):
    def bias_chunk(c):
        def tile(head, u, e):
            offset = (2 * c + e) - (i * sub + u)
            return bias_ref[head, jnp.clip(offset, -BIAS_FAR, BIAS_FAR) + BIAS_FAR]
        return jnp.concatenate(
            [jnp.concatenate([tile(head, u, e) for head in range(2) for u in range(sub)], axis=1)
             for e in range(2)], axis=0)

    def shifted_scores(t):
        aug = _shift_column(lhs[t], kmax_scr[t, 0:1, :], kmax_scr[t, 1:2, :],
                            bmax_ref[0], bmax_ref[1])
        return _nt_dot(ka_scr[t], jnp.concatenate([lhs[t], aug], axis=1))

    def pv_of_shifted(s_all):
        p = jnp.concatenate([jnp.exp2(s_all[c * kc:(c + 1) * kc, :] + bias_chunk(c)).astype(BF16)
                             for c in range(n_chunks)], axis=0)
        return [jnp.dot(vt_scr[head], p[:, head * tq:(head + 1) * tq],
                        preferred_element_type=F32) for head in range(2)]

    def pv_running_max(s_all):
        m = None
        acc = [None, None]
        for c in range(n_chunks):
            s = s_all[c * kc:(c + 1) * kc, :] + bias_chunk(c)
            m_c = jnp.max(s, axis=0, keepdims=True)
            m_new = m_c if m is None else jnp.maximum(m, m_c)
            p = jnp.exp2(s - m_new).astype(BF16)
            if m is not None:
                alpha = jnp.exp2(m - m_new)
            for head in range(2):
                cols = slice(head * tq, (head + 1) * tq)
                pv = jnp.dot(vt_scr[head, :, c * kc:(c + 1) * kc], p[:, cols],
                             preferred_element_type=F32)
                acc[head] = pv if m is None else acc[head] * alpha[:, cols] + pv
            m = m_new
        return acc

    def write(acc1, acc2):
        g = g_ref[...] * (1.0 - lam_init)
        for head in range(2):
            o1 = acc1[head][:C_V_DIM] * (1.0 / acc1[head][C_V_DIM:C_V_DIM + 1])
            o2 = acc2[head][:C_V_DIM] * (1.0 / acc2[head][C_V_DIM:C_V_DIM + 1])
            out = o1 - lam * o2
            ms = jnp.mean(out * out, axis=0, keepdims=True)
            y = out * lax.rsqrt(ms + EPS) * g
            o_ref[rows, head * C_V_DIM:(head + 1) * C_V_DIM] = y.T.astype(o_ref.dtype)

    s1 = shifted_scores(0)
    s2 = shifted_scores(1)
    acc1 = pv_of_shifted(s1)
    acc2 = pv_of_shifted(s2)
    write(acc1, acc2)
    sums = jnp.concatenate([a[C_V_DIM:C_V_DIM + 1] for a in acc1 + acc2], axis=0)

    @pl.when(jnp.logical_not(jnp.min(sums) >= SUM_FLOOR))
    def _():
        write(pv_running_max(_nt_dot(k_refs[0][...], lhs[0])),
              pv_running_max(_nt_dot(k_refs[1][...], lhs[1])))


def _diff_attn(q1, q2, k1, k2, v, bias_t, bias_max, lam_vecs, subln_g, lam_init, batch, seq, tq):
    n_pairs = C_HEADS // 2
    n_tiles = bias_t.shape[1]
    seq_spec = pl.BlockSpec((seq, LANES), lambda b, p: (b, p))
    g_cols = jnp.broadcast_to(subln_g.reshape(C_V_DIM, 1), (C_V_DIM, tq))
    return pl.pallas_call(
        functools.partial(_diff_attn_kernel, tq=tq, lam_init=lam_init),
        grid=(batch, n_pairs),
        in_specs=[seq_spec, seq_spec, seq_spec, seq_spec,
                  pl.BlockSpec((seq, 2 * C_V_DIM), lambda b, p: (b, p)),
                  pl.BlockSpec((2, n_tiles, BLOCK, BLOCK), lambda b, p: (p, 0, 0, 0)),
                  pl.BlockSpec((2, 1, LANES), lambda b, p: (p, 0, 0)),
                  pl.BlockSpec(lam_vecs.shape, lambda b, p: (0, 0)),
                  pl.BlockSpec((C_V_DIM, tq), lambda b, p: (0, 0))],
        out_specs=pl.BlockSpec((seq, 2 * C_V_DIM), lambda b, p: (b, p)),
        out_shape=jax.ShapeDtypeStruct(v.shape, BF16),
        scratch_shapes=[pltpu.VMEM((2, C_V_DIM + ONES_ROWS, seq), BF16),
                        pltpu.VMEM((2, seq, 2 * LANES), BF16),
                        pltpu.VMEM((2, 2, LANES), F32)],
        compiler_params=_cparams(2),
        name="diff_attn",
    )(q1, q2, k1, k2, v, bias_t, bias_max, lam_vecs, g_cols)


def _rope_tables(seq):
    rows = seq // GRID_W
    row = jnp.broadcast_to(jnp.arange(rows)[:, None], (rows, GRID_W)).reshape(-1)
    col = jnp.broadcast_to(jnp.arange(GRID_W)[None, :], (rows, GRID_W)).reshape(-1)
    half = HEAD_DIM // 2
    inv = 1.0 / (ROPE_THETA ** (jnp.arange(0, half, 2, dtype=F32) / half))
    ang_row = row.astype(F32)[:, None] * inv
    ang_col = col.astype(F32)[:, None] * inv
    cos = jnp.concatenate([jnp.cos(ang_row)] * 2 + [jnp.cos(ang_col)] * 2, axis=-1)
    sin = jnp.concatenate([-jnp.sin(ang_row), jnp.sin(ang_row),
                           -jnp.sin(ang_col), jnp.sin(ang_col)], axis=-1)
    return jnp.tile(cos, (1, 2)), jnp.tile(sin, (1, 2))


def kernel(x, mem, rel_bias, mem_norm, final_norm, even_norm, even_w_in, even_sink, even_q_norm, even_k_norm, even_w_mem_kv, even_w_out, odd_norm, odd_w_in, odd_lambda_q1, odd_lambda_k1, odd_lambda_q2, odd_lambda_k2, odd_subln, odd_w_mem_kv, odd_w_out):
    batch, seq, d = x.shape
    mem_len = mem.shape[1]
    tokens = batch * seq
    xw = X_HEADS * HEAD_DIM
    gw = A_HEADS * HEAD_DIM

    w_in0 = even_w_in[0].astype(BF16)
    w_out0 = even_w_out[0].astype(BF16)
    w_in1 = odd_w_in[0].astype(BF16)
    w_out1 = odd_w_out[0].astype(BF16)

    x2 = x.reshape(tokens, d)
    mem2 = mem.reshape(batch * mem_len, d)

    bias_near, bias_win = _bias_tiles(rel_bias)
    cos, sin_signed = _rope_tables(seq)
    head_bias_max = jnp.max(rel_bias, axis=0) * LOG2E

    w_mem = jnp.concatenate([even_w_mem_kv[0], odd_w_mem_kv[0]], axis=1).astype(BF16)
    mk0, mv0, mk1, mv1 = _norm_proj(mem2, mem_norm, w_mem, (xw,) * 4, TM_PROJ, gate_last=False)

    rope_args = (cos, sin_signed, jnp.tile(even_q_norm[0], 2).reshape(1, LANES),
                 jnp.tile(even_k_norm[0], 2).reshape(1, LANES))
    aq, ak, av, bq, bk, bv, xq, gate = _norm_proj(
        x2, even_norm[0], w_in0, (gw, 128, 128, gw, 128, 128, xw, D_MODEL), TM_PROJ,
        gate_last=True, qk_rope=(3, 4), rope_args=rope_args, seq=seq)
    y_a = _window_attn(aq, ak, av, bias_win, even_sink[0] * LOG2E, batch, seq, WINDOW_GROUP)
    y_b = _dense_attn(bq, bk, bv, batch, seq, TQ_DENSE)
    y_x = _cross_attn(xq, mk0, mv0, batch, seq, TQ_CROSS)
    lam_init = 0.8 - 0.6 * math.exp(-0.3 * 1)
    cw = C_HEADS * HEAD_DIM
    h1, q1, q2, k1, k2, v, xq1, gate1 = _gate_out(
        [y_a, y_b, y_x], gate, w_out0, x2, odd_norm[0], TM_PROJ, w_next=w_in1,
        splits=(cw, cw, cw, cw, C_HEADS * C_V_DIM, xw, D_MODEL))
    lam_vecs = jnp.stack([odd_lambda_q1[0], odd_lambda_k1[0], odd_lambda_q2[0], odd_lambda_k2[0]])
    bias_max = jnp.broadcast_to(head_bias_max.reshape(C_HEADS, 1, 1), (C_HEADS, 1, LANES))
    y_c = _diff_attn(q1, q2, k1, k2, v, bias_near, bias_max, lam_vecs, odd_subln[0], lam_init,
                     batch, seq, TQ_DIFF)
    y_x1 = _cross_attn(xq1, mk1, mv1, batch, seq, TQ_CROSS)
    (out,) = _gate_out([y_c, y_x1], gate1, w_out1, h1, final_norm, TM_PROJ)
    return out.reshape(batch, seq, d)
```

```python
import functools
import math

import numpy as np
import jax
import jax.numpy as jnp
from jax import lax
from jax.experimental import pallas as pl
from jax.experimental.pallas import tpu as pltpu

D_MODEL = 1024
HEAD_DIM = 64
BLOCK = 128
WINDOW = 128
GRID_W = 64
A_HEADS = 6
B_HEADS = 6
C_HEADS = 6
C_V_DIM = 128
X_HEADS = 4
REL_BUCKETS = 32
REL_MAX_DIST = 128
ROPE_THETA = 10000.0
EPS = 1e-6
NEG_INF = -1e30
LOG2E = math.log2(math.e)
LANES = 128
ONES_ROWS = 16
PROJ_SUB_ROWS = 256
TM_PROJ = 1024
TQ_DENSE = 256
TQ_DIFF = 512
TQ_CROSS = 2048
WINDOW_GROUP = 8
SUM_FLOOR = 2.0 ** -64
BIAS_FAR = 2
VMEM_LIMIT = 56 * 1024 * 1024

F32 = jnp.float32
BF16 = jnp.bfloat16


def _cparams(n_axes, flags=None):
    return pltpu.CompilerParams(dimension_semantics=("arbitrary",) * n_axes,
                                vmem_limit_bytes=VMEM_LIMIT, flags=flags)


def _lane_lo(shape):
    return lax.broadcasted_iota(jnp.int32, shape, len(shape) - 1) < HEAD_DIM


def _rmsnorm_rows(x, g):
    ms = jnp.mean(x * x, axis=-1, keepdims=True)
    return x * lax.rsqrt(ms + EPS) * g


def _nt_dot(a, b):
    return lax.dot_general(a, b, (((1,), (1,)), ((), ())), preferred_element_type=F32)


def _norm_proj_kernel(*refs, qk_rope, gate_last):
    x_ref, g_ref, w_ref = refs[:3]
    if qk_rope is None:
        out_refs = refs[3:]
    else:
        cos_ref, sin_ref, gq_ref, gk_ref = refs[3:7]
        out_refs = refs[7:]
    tm = x_ref.shape[0]
    for r0 in range(0, tm, PROJ_SUB_ROWS):
        rows = slice(r0, min(r0 + PROJ_SUB_ROWS, tm))
        xn = _rmsnorm_rows(x_ref[rows, :], g_ref[...]).astype(BF16)
        y_all = jnp.dot(xn, w_ref[...], preferred_element_type=F32)
        c0 = 0
        for j, o_ref in enumerate(out_refs):
            width = o_ref.shape[1]
            y = y_all[:, c0:c0 + width]
            if qk_rope is not None and j in qk_rope:
                gain, scale = ((gq_ref, HEAD_DIM ** -0.5 * LOG2E) if j == qk_rope[0]
                               else (gk_ref, 1.0))
                y = jnp.concatenate(
                    [_norm_rope_pair(y[:, t * LANES:(t + 1) * LANES], gain[...],
                                     cos_ref[rows, :], sin_ref[rows, :]) * scale
                     for t in range(width // LANES)], axis=1)
            if gate_last and j == len(out_refs) - 1:
                y = _silu(y)
            o_ref[rows, :] = y.astype(o_ref.dtype)
            c0 += width


def _silu(y):
    return y * (1.0 / (1.0 + jnp.exp(-y)))


def _norm_proj(x, g, w_bf16, splits, tm, gate_last, qk_rope=None, rope_args=(), seq=None):
    rows, d = x.shape
    n = w_bf16.shape[1]
    assert sum(splits) == n and rows % tm == 0
    in_specs = [pl.BlockSpec((tm, d), lambda i: (i, 0)),
                pl.BlockSpec((1, d), lambda i: (0, 0)),
                pl.BlockSpec((d, n), lambda i: (0, 0))]
    if qk_rope is not None:
        pos_blocks = seq // tm
        in_specs += [pl.BlockSpec((tm, LANES), lambda i: (i % pos_blocks, 0)),
                     pl.BlockSpec((tm, LANES), lambda i: (i % pos_blocks, 0)),
                     pl.BlockSpec((1, LANES), lambda i: (0, 0)),
                     pl.BlockSpec((1, LANES), lambda i: (0, 0))]
    return pl.pallas_call(
        functools.partial(_norm_proj_kernel, qk_rope=qk_rope, gate_last=gate_last),
        grid=(rows // tm,),
        in_specs=in_specs,
        out_specs=[pl.BlockSpec((tm, s), lambda i: (i, 0)) for s in splits],
        out_shape=[jax.ShapeDtypeStruct((rows, s), BF16) for s in splits],
        compiler_params=_cparams(1),
        name="norm_proj",
    )(x, g.reshape(1, d), w_bf16, *rope_args)


def _gate_out_kernel(*refs, n_parts, after):
    y_refs = refs[:n_parts]
    gate_ref, w_ref, res_ref, g_ref = refs[n_parts:n_parts + 4]
    if after == "proj":
        w_next_ref, h_ref = refs[n_parts + 4:n_parts + 6]
        out_refs = refs[n_parts + 6:]
    else:
        h_ref = refs[n_parts + 4]
    tm = res_ref.shape[0]
    for r0 in range(0, tm, PROJ_SUB_ROWS):
        rows = slice(r0, min(r0 + PROJ_SUB_ROWS, tm))
        parts = []
        c0 = 0
        for y_ref in y_refs:
            width = y_ref.shape[1]
            parts.append(y_ref[rows, :] * gate_ref[rows, c0:c0 + width])
            c0 += width
        h = res_ref[rows, :] + jnp.dot(jnp.concatenate(parts, axis=1), w_ref[...],
                                       preferred_element_type=F32)
        if after == "norm":
            h_ref[rows, :] = _rmsnorm_rows(h, g_ref[...])
            continue
        h_ref[rows, :] = h
        y_all = jnp.dot(_rmsnorm_rows(h, g_ref[...]).astype(BF16), w_next_ref[...],
                        preferred_element_type=F32)
        c0 = 0
        for j, o_ref in enumerate(out_refs):
            width = o_ref.shape[1]
            y = y_all[:, c0:c0 + width]
            if j == len(out_refs) - 1:
                y = _silu(y)
            o_ref[rows, :] = y.astype(o_ref.dtype)
            c0 += width


def _gate_out(y_parts, gate, w_bf16, resid, norm_g, tm, w_next=None, splits=()):
    rows, d = resid.shape
    mix = w_bf16.shape[0]
    after = "norm" if w_next is None else "proj"
    row_block = lambda width: pl.BlockSpec((tm, width), lambda i: (i, 0))
    whole = lambda a: pl.BlockSpec(a.shape, lambda i: (0, 0))
    in_specs = [row_block(y.shape[1]) for y in y_parts]
    in_specs += [row_block(mix), whole(w_bf16), row_block(d), pl.BlockSpec((1, d), lambda i: (0, 0))]
    args = list(y_parts) + [gate, w_bf16, resid, norm_g.reshape(1, d)]
    out_specs = [row_block(d)]
    out_shape = [jax.ShapeDtypeStruct((rows, d), F32)]
    if after == "proj":
        assert sum(splits) == w_next.shape[1]
        in_specs.append(whole(w_next))
        args.append(w_next)
        out_specs += [row_block(s) for s in splits]
        out_shape += [jax.ShapeDtypeStruct((rows, s), BF16) for s in splits]
    return pl.pallas_call(
        functools.partial(_gate_out_kernel, n_parts=len(y_parts), after=after),
        grid=(rows // tm,),
        in_specs=in_specs,
        out_specs=out_specs,
        out_shape=out_shape,
        compiler_params=_cparams(1),
        name="gate_out",
    )(*args)


def _bucket_thresholds():
    nb = REL_BUCKETS // 2
    max_exact = nb // 2
    n = np.arange(0, 4 * REL_MAX_DIST)
    nf = np.maximum(n, 1).astype(np.float32)
    large = max_exact + (np.log(nf / np.float32(max_exact))
                         / np.float32(math.log(REL_MAX_DIST / max_exact))
                         * np.float32(nb - max_exact)).astype(np.int32)
    bucket = np.where(n < max_exact, n, np.minimum(large, nb - 1))
    assert np.all(np.diff(bucket) >= 0) and bucket[0] == 0 and bucket[-1] == nb - 1
    thr = [int(np.argmax(bucket >= k)) for k in range(1, nb)]
    assert thr[-1] < REL_MAX_DIST
    return thr


def _bias_tiles_kernel(tab_ref, near_ref, win_ref):
    h = pl.program_id(0)
    thr = _bucket_thresholds()
    nb = REL_BUCKETS // 2
    row = lax.broadcasted_iota(jnp.int32, (BLOCK, BLOCK), 0)
    col = lax.broadcasted_iota(jnp.int32, (BLOCK, BLOCK), 1)
    def tile_of(rel):
        n = jnp.abs(rel)
        neg = jnp.full((BLOCK, BLOCK), tab_ref[0, h], F32)
        pos = jnp.full((BLOCK, BLOCK), tab_ref[nb, h], F32)
        for b in range(1, nb):
            ge = n >= thr[b - 1]
            neg = jnp.where(ge, tab_ref[b, h], neg)
            pos = jnp.where(ge, tab_ref[nb + b, h], pos)
        return jnp.where(rel > 0, pos, neg) * LOG2E

    for k in range(2 * BIAS_FAR + 1):
        rel = (k - BIAS_FAR) * BLOCK + row - col
        tile = tile_of(rel)
        near_ref[0, k] = tile
        if abs(k - BIAS_FAR) <= 1:
            j = k - BIAS_FAR + 1
            win_ref[j * BLOCK:(j + 1) * BLOCK, :] = jnp.where(jnp.abs(rel) <= WINDOW, tile, NEG_INF)


def _bias_tiles(rel_bias):
    assert _bucket_thresholds()[-1] <= (BIAS_FAR - 1) * BLOCK + 1
    n_tiles = 2 * BIAS_FAR + 1
    n_heads = rel_bias.shape[1]
    return pl.pallas_call(
        _bias_tiles_kernel,
        grid=(n_heads,),
        in_specs=[pl.BlockSpec(memory_space=pltpu.SMEM)],
        out_specs=[pl.BlockSpec((1, n_tiles, BLOCK, BLOCK), lambda h: (h, 0, 0, 0)),
                   pl.BlockSpec((3 * BLOCK, BLOCK), lambda h: (0, h))],
        out_shape=[jax.ShapeDtypeStruct((n_heads, n_tiles, BLOCK, BLOCK), F32),
                   jax.ShapeDtypeStruct((3 * BLOCK, n_heads * BLOCK), F32)],
        compiler_params=_cparams(1),
        name="bias_tiles",
    )(rel_bias)


def _stack_gqa_heads(q_tiles):
    lo = _lane_lo(q_tiles[0].shape)
    swap = lambda q: pltpu.roll(q, HEAD_DIM, 1)
    heads_lo = [q_tiles[0], swap(q_tiles[0]), q_tiles[1]]
    heads_hi = [q_tiles[1], swap(q_tiles[2]), q_tiles[2]]
    return jnp.concatenate([jnp.where(lo, q, 0.0).astype(BF16) for q in heads_lo]
                           + [jnp.where(lo, 0.0, q).astype(BF16) for q in heads_hi], axis=0)


def _unstack_gqa_heads(out_lo, out_hi, width):
    blk = lambda o, j: o[:, j * width:(j + 1) * width]
    return [jnp.concatenate([blk(out_lo, 0), blk(out_lo, 1)], axis=0),
            jnp.concatenate([blk(out_lo, 2), blk(out_hi, 0)], axis=0),
            jnp.concatenate([blk(out_hi, 1), blk(out_hi, 2)], axis=0)]


def _stack_pair_heads(q):
    lo = _lane_lo(q.shape)
    return jnp.concatenate([jnp.where(lo, q, 0.0).astype(BF16),
                            jnp.where(lo, 0.0, q).astype(BF16)], axis=0)


def _window_attn_kernel(q_ref, k_ref, v_ref, bias_ref, sink_row_ref,
                        o_ref, kpad_scr, vtpad_scr, *, n_blocks, group):
    i = pl.program_id(1)
    half = 3 * BLOCK

    @pl.when(i == 0)
    def _():
        zero_blk = jnp.zeros((1, BLOCK, LANES), BF16)
        kpad_scr[0:1] = zero_blk
        kpad_scr[n_blocks + 1:n_blocks + 2] = zero_blk
        kpad_scr[1:n_blocks + 1] = k_ref[...].reshape(n_blocks, BLOCK, LANES)
        v_t = v_ref[...].astype(F32).T.astype(BF16)
        ones = jnp.ones((ONES_ROWS, BLOCK), BF16)
        zero_v = jnp.zeros((HEAD_DIM + ONES_ROWS, BLOCK), BF16)
        for g in range(2):
            vtpad_scr[g, 0] = zero_v
            vtpad_scr[g, n_blocks + 1] = zero_v
            for blk in range(n_blocks):
                vtpad_scr[g, blk + 1, :HEAD_DIM, :] = v_t[g * HEAD_DIM:(g + 1) * HEAD_DIM,
                                                           blk * BLOCK:(blk + 1) * BLOCK]
                vtpad_scr[g, blk + 1, HEAD_DIM:, :] = ones

    sink_row = sink_row_ref[...]

    for u in range(group):
        n = i * group + u
        qf = q_ref[u * BLOCK:(u + 1) * BLOCK, :].astype(F32) * (HEAD_DIM ** -0.5 * LOG2E)
        lhs = _stack_gqa_heads([qf[:, j * LANES:(j + 1) * LANES] for j in range(3)])
        k_win = kpad_scr[pl.ds(n, 3)].reshape(3 * BLOCK, LANES)
        s = _nt_dot(k_win, lhs) + bias_ref[...]
        s = jnp.concatenate([jnp.where(n > 0, s[:BLOCK], NEG_INF), s[BLOCK:2 * BLOCK],
                             jnp.where(n < n_blocks - 1, s[2 * BLOCK:], NEG_INF)], axis=0)
        m = jnp.maximum(jnp.max(s, axis=0, keepdims=True), sink_row)
        p = jnp.exp2(s - m).astype(BF16)
        p_sink = jnp.exp2(sink_row - m)
        outs = []
        for g in range(2):
            acc = None
            for j in range(3):
                pv = jnp.dot(vtpad_scr[g, n + j], p[j * BLOCK:(j + 1) * BLOCK, g * half:(g + 1) * half],
                             preferred_element_type=F32)
                acc = pv if acc is None else acc + pv
            total = acc[HEAD_DIM:HEAD_DIM + 1] + p_sink[:, g * half:(g + 1) * half]
            outs.append(acc[:HEAD_DIM] * (1.0 / total))
        for j, pair_t in enumerate(_unstack_gqa_heads(outs[0], outs[1], BLOCK)):
            o_ref[u * BLOCK:(u + 1) * BLOCK, j * LANES:(j + 1) * LANES] = pair_t.T.astype(o_ref.dtype)


def _window_attn(q, k, v, bias_win_t, sink, batch, seq, group):
    nb = seq // BLOCK
    steps = nb // group
    rows = A_HEADS * BLOCK
    sink_row = jnp.repeat(sink, BLOCK).reshape(1, rows)
    const = lambda shape: pl.BlockSpec(shape, lambda b, i: (0,) * len(shape))
    return pl.pallas_call(
        functools.partial(_window_attn_kernel, n_blocks=nb, group=group),
        grid=(batch, steps),
        in_specs=[pl.BlockSpec((group * BLOCK, 3 * LANES), lambda b, i: (b * steps + i, 0)),
                  pl.BlockSpec((seq, LANES), lambda b, i: (b, 0)),
                  pl.BlockSpec((seq, LANES), lambda b, i: (b, 0)),
                  const(bias_win_t.shape), const((1, rows))],
        out_specs=pl.BlockSpec((group * BLOCK, 3 * LANES), lambda b, i: (b * steps + i, 0)),
        out_shape=jax.ShapeDtypeStruct(q.shape, BF16),
        scratch_shapes=[pltpu.VMEM((nb + 2, BLOCK, LANES), BF16),
                        pltpu.VMEM((2, nb + 2, HEAD_DIM + ONES_ROWS, BLOCK), BF16)],
        compiler_params=_cparams(2),
        name="window_attn",
    )(q, k, v, bias_win_t, sink_row)


def _norm_rope_pair(x, g, cos, sin_signed):
    lo = _lane_lo(x.shape)
    x2 = x * x
    ss_lo = jnp.sum(jnp.where(lo, x2, 0.0), axis=-1, keepdims=True)
    ss_hi = jnp.sum(jnp.where(lo, 0.0, x2), axis=-1, keepdims=True)
    ms = jnp.where(lo, ss_lo, ss_hi) * (1.0 / HEAD_DIM)
    y = x * lax.rsqrt(ms + EPS) * g
    lane = lax.broadcasted_iota(jnp.int32, x.shape, 1)
    quarter = HEAD_DIM // 4
    first = (lane & quarter) == 0
    partner = jnp.where(first, pltpu.roll(y, LANES - quarter, 1), pltpu.roll(y, quarter, 1))
    return y * cos + partner * sin_signed


def _dense_attn_kernel(q_ref, k_ref, v_ref, o_ref, ka_scr, vt_scr, kmax_scr, *, tq):
    seq = k_ref.shape[0]
    kc = 2 * BLOCK
    n_chunks = seq // kc
    grp = 3 * tq

    lane = lax.broadcasted_iota(jnp.int32, (seq, LANES), 1)
    ka_scr[:, :LANES] = k_ref[...]
    ka_scr[:, LANES:] = jnp.where(lane == 0, 1.0, 0.0).astype(BF16)
    kmax_lo, kmax_hi = _pair_key_norm_max(k_ref[...])
    kmax_scr[0:1, :] = kmax_lo
    kmax_scr[1:2, :] = kmax_hi
    v_t = v_ref[...].astype(F32).T.astype(BF16)
    for g in range(2):
        vt_scr[g, :HEAD_DIM, :] = v_t[g * HEAD_DIM:(g + 1) * HEAD_DIM]
        vt_scr[g, HEAD_DIM:, :] = jnp.ones((ONES_ROWS, seq), BF16)

    def query_tile(i, carry):
        rows = pl.ds(pl.multiple_of(i * tq, tq), tq)
        lhs = _stack_gqa_heads([q_ref[rows, j * LANES:(j + 1) * LANES].astype(F32)
                                for j in range(3)])

        def write(acc):
            outs = [a[:HEAD_DIM] * (1.0 / a[HEAD_DIM:HEAD_DIM + 1]) for a in acc]
            for j, pair_t in enumerate(_unstack_gqa_heads(outs[0], outs[1], tq)):
                o_ref[rows, j * LANES:(j + 1) * LANES] = pair_t.T.astype(o_ref.dtype)

        aug = _shift_column(lhs, kmax_scr[0:1, :], kmax_scr[1:2, :], 0.0, 0.0)
        s_all = _nt_dot(ka_scr[...], jnp.concatenate([lhs, aug], axis=1))
        p = jnp.exp2(s_all).astype(BF16)
        acc = [jnp.dot(vt_scr[g], p[:, g * grp:(g + 1) * grp], preferred_element_type=F32)
               for g in range(2)]
        write(acc)
        sums = jnp.concatenate([a[HEAD_DIM:HEAD_DIM + 1] for a in acc], axis=0)

        @pl.when(jnp.logical_not(jnp.min(sums) >= SUM_FLOOR))
        def _():
            s_exact = _nt_dot(ka_scr[:, :LANES], lhs)
            m = None
            acc2 = [None, None]
            for c in range(n_chunks):
                s = s_exact[c * kc:(c + 1) * kc, :]
                m_c = jnp.max(s, axis=0, keepdims=True)
                m_new = m_c if m is None else jnp.maximum(m, m_c)
                pc = jnp.exp2(s - m_new).astype(BF16)
                if m is not None:
                    alpha = jnp.exp2(m - m_new)
                for g in range(2):
                    cols = slice(g * grp, (g + 1) * grp)
                    pv = jnp.dot(vt_scr[g, :, c * kc:(c + 1) * kc], pc[:, cols],
                                 preferred_element_type=F32)
                    acc2[g] = pv if m is None else acc2[g] * alpha[:, cols] + pv
                m = m_new
            write(acc2)

        return carry

    lax.fori_loop(0, seq // tq, query_tile, 0)


def _dense_attn(q, k, v, batch, seq, tq):
    return pl.pallas_call(
        functools.partial(_dense_attn_kernel, tq=tq),
        grid=(batch,),
        in_specs=[pl.BlockSpec((seq, 3 * LANES), lambda b: (b, 0)),
                  pl.BlockSpec((seq, LANES), lambda b: (b, 0)),
                  pl.BlockSpec((seq, LANES), lambda b: (b, 0))],
        out_specs=pl.BlockSpec((seq, 3 * LANES), lambda b: (b, 0)),
        out_shape=jax.ShapeDtypeStruct(q.shape, BF16),
        scratch_shapes=[pltpu.VMEM((seq, 2 * LANES), BF16),
                        pltpu.VMEM((2, HEAD_DIM + ONES_ROWS, seq), BF16),
                        pltpu.VMEM((2, LANES), F32)],
        compiler_params=_cparams(1),
        name="dense_attn",
    )(q, k, v)


def _cross_attn_kernel(q_ref, mk_ref, mv_ref, o_ref, vt_scr, *, tq):
    i = pl.program_id(1)
    n_pairs = q_ref.shape[1] // LANES
    mem_len = mk_ref.shape[0]

    @pl.when(i == 0)
    def _():
        v_t = mv_ref[...].astype(F32).T.astype(BF16)
        for h in range(2 * n_pairs):
            vt_scr[h, :HEAD_DIM, :] = v_t[h * HEAD_DIM:(h + 1) * HEAD_DIM]
            vt_scr[h, HEAD_DIM:, :] = jnp.ones((ONES_ROWS, mem_len), BF16)

    for j in range(n_pairs):
        cols = slice(j * LANES, (j + 1) * LANES)
        lhs = _stack_pair_heads(q_ref[:, cols].astype(F32) * (HEAD_DIM ** -0.5 * LOG2E))
        s = _nt_dot(mk_ref[:, cols], lhs)
        p = jnp.exp2(s - jnp.max(s, axis=0, keepdims=True)).astype(BF16)
        acc = [jnp.dot(vt_scr[2 * j + h], p[:, h * tq:(h + 1) * tq], preferred_element_type=F32)
               for h in range(2)]
        pair_t = jnp.concatenate([a[:HEAD_DIM] * (1.0 / a[HEAD_DIM:HEAD_DIM + 1]) for a in acc],
                                 axis=0)
        o_ref[:, cols] = pair_t.T.astype(o_ref.dtype)


def _cross_attn(q, mk, mv, batch, seq, tq):
    nq = seq // tq
    mem_len = mk.shape[0] // batch
    width = q.shape[1]
    n_pairs = width // LANES
    return pl.pallas_call(
        functools.partial(_cross_attn_kernel, tq=tq),
        grid=(batch, nq),
        in_specs=[pl.BlockSpec((tq, width), lambda b, i: (b * nq + i, 0)),
                  pl.BlockSpec((mem_len, width), lambda b, i: (b, 0)),
                  pl.BlockSpec((mem_len, width), lambda b, i: (b, 0))],
        out_specs=pl.BlockSpec((tq, width), lambda b, i: (b * nq + i, 0)),
        out_shape=jax.ShapeDtypeStruct(q.shape, BF16),
        scratch_shapes=[pltpu.VMEM((2 * n_pairs, HEAD_DIM + ONES_ROWS, mem_len), BF16)],
        compiler_params=_cparams(2),
        name="cross_attn",
    )(q, mk, mv)


def _shift_column(lhs, kmax_lo, kmax_hi, bias_max_lo, bias_max_hi):
    half = lhs.shape[0] // 2
    lf = lhs.astype(F32)
    q_sq = jnp.sum(lf * lf, axis=-1, keepdims=True)
    q_norm = q_sq * lax.rsqrt(jnp.maximum(q_sq, 1e-30))
    row = lax.broadcasted_iota(jnp.int32, lhs.shape, 0)
    lane = lax.broadcasted_iota(jnp.int32, lhs.shape, 1)
    shift = (q_norm * jnp.where(row < half, kmax_lo, kmax_hi)
             + jnp.where(row < half, bias_max_lo, bias_max_hi))
    shift = shift + jnp.abs(shift) * 2.0 ** -7
    return jnp.where(lane == 0, -shift, 0.0).astype(BF16)


def _pair_key_norm_max(k_pair):
    kf = k_pair.astype(F32)
    row = lax.broadcasted_iota(jnp.int32, (LANES, 2 * LANES), 0)
    col = lax.broadcasted_iota(jnp.int32, (LANES, 2 * LANES), 1)
    half_sel = jnp.where((row < HEAD_DIM) == (col < LANES), 1.0, 0.0).astype(BF16)
    sq = jnp.dot((kf * kf).astype(BF16), half_sel, preferred_element_type=F32)
    norm_max = jnp.sqrt(jnp.max(sq, axis=0, keepdims=True))
    return norm_max[:, :LANES], norm_max[:, LANES:]


def _diff_attn_kernel(q1_ref, q2_ref, k1_ref, k2_ref, v_ref, bias_ref, bmax_ref, lam_ref, g_ref,
                      o_ref, vt_scr, ka_scr, kmax_scr, lhs_scr, acc_scr, *, tq, lam_init):
    sub = tq // BLOCK
    seq = k1_ref.shape[0]
    kc = 2 * BLOCK
    n_chunks = seq // kc
    k_refs = (k1_ref, k2_ref)

    for head in range(2):
        v_t = v_ref[:, head * C_V_DIM:(head + 1) * C_V_DIM].astype(F32).T
        vt_scr[head, :C_V_DIM, :] = v_t.astype(BF16)
        vt_scr[head, C_V_DIM:, :] = jnp.ones((ONES_ROWS, seq), BF16)
    lane = lax.broadcasted_iota(jnp.int32, (seq, LANES), 1)
    ones_col = jnp.where(lane == 0, 1.0, 0.0).astype(BF16)
    for t in range(2):
        ka_scr[t, :, :LANES] = k_refs[t][...]
        ka_scr[t, :, LANES:] = ones_col
        kmax_lo, kmax_hi = _pair_key_norm_max(k_refs[t][...])
        kmax_scr[t, 0:1, :] = kmax_lo
        kmax_scr[t, 1:2, :] = kmax_hi

    lam_vec = lam_ref[...]
    lam = (jnp.exp(jnp.sum(lam_vec[0:1] * lam_vec[1:2], axis=-1, keepdims=True))
           - jnp.exp(jnp.sum(lam_vec[2:3] * lam_vec[3:4], axis=-1, keepdims=True)) + lam_init)

    n_q = seq // tq
    tile_rows = lambda tile: pl.ds(pl.multiple_of(tile * tq, tq), tq)

    def stage_queries(tile):
        for t, q_ref in enumerate((q1_ref, q2_ref)):
            lhs = _stack_pair_heads(q_ref[tile_rows(tile), :].astype(F32)
                                    * (HEAD_DIM ** -0.5 * LOG2E))
            lhs_scr[t, :, :LANES] = lhs
            lhs_scr[t, :, LANES:] = _shift_column(lhs, kmax_scr[t, 0:1, :], kmax_scr[t, 1:2, :],
                                                  bmax_ref[0], bmax_ref[1])

    def finish(tile):
        g = g_ref[...] * (1.0 - lam_init)
        for head in range(2):
            a1, a2 = acc_scr[head], acc_scr[2 + head]
            o1 = a1[:C_V_DIM] * (1.0 / a1[C_V_DIM:C_V_DIM + 1])
            o2 = a2[:C_V_DIM] * (1.0 / a2[C_V_DIM:C_V_DIM + 1])
            out = o1 - lam * o2
            ms = jnp.mean(out * out, axis=0, keepdims=True)
            y = out * lax.rsqrt(ms + EPS) * g
            o_ref[tile_rows(tile), head * C_V_DIM:(head + 1) * C_V_DIM] = y.T.astype(o_ref.dtype)

    def bias_chunk(i, c):
        def tile(head, u, e):
            offset = (2 * c + e) - (i * sub + u)
            return bias_ref[head, jnp.clip(offset, -BIAS_FAR, BIAS_FAR) + BIAS_FAR]
        return jnp.concatenate(
            [jnp.concatenate([tile(head, u, e) for head in range(2) for u in range(sub)], axis=1)
             for e in range(2)], axis=0)

    def pv_of_shifted(i, s_all):
        p = jnp.concatenate([jnp.exp2(s_all[c * kc:(c + 1) * kc, :] + bias_chunk(i, c)).astype(BF16)
                             for c in range(n_chunks)], axis=0)
        return [jnp.dot(vt_scr[head], p[:, head * tq:(head + 1) * tq],
                        preferred_element_type=F32) for head in range(2)]

    def pv_running_max(i, s_all):
        m = None
        acc = [None, None]
        for c in range(n_chunks):
            s = s_all[c * kc:(c + 1) * kc, :] + bias_chunk(i, c)
            m_c = jnp.max(s, axis=0, keepdims=True)
            m_new = m_c if m is None else jnp.maximum(m, m_c)
            p = jnp.exp2(s - m_new).astype(BF16)
            if m is not None:
                alpha = jnp.exp2(m - m_new)
            for head in range(2):
                cols = slice(head * tq, (head + 1) * tq)
                pv = jnp.dot(vt_scr[head, :, c * kc:(c + 1) * kc], p[:, cols],
                             preferred_element_type=F32)
                acc[head] = pv if m is None else acc[head] * alpha[:, cols] + pv
            m = m_new
        return acc

    def store_acc(acc):
        for j, a in enumerate(acc):
            acc_scr[j] = a

    stage_queries(0)
    acc_scr[...] = jnp.ones(acc_scr.shape, F32)

    def query_tile(i, carry):
        finish(jnp.maximum(i - 1, 0))
        lhs_aug = [lhs_scr[t] for t in range(2)]
        s1 = _nt_dot(ka_scr[0], lhs_aug[0])
        s2 = _nt_dot(ka_scr[1], lhs_aug[1])
        acc = pv_of_shifted(i, s1) + pv_of_shifted(i, s2)
        store_acc(acc)
        sums = jnp.concatenate([a[C_V_DIM:C_V_DIM + 1] for a in acc], axis=0)
        stage_queries(jnp.minimum(i + 1, n_q - 1))

        @pl.when(jnp.logical_not(jnp.min(sums) >= SUM_FLOOR))
        def _():
            store_acc(pv_running_max(i, _nt_dot(k_refs[0][...], lhs_aug[0][:, :LANES]))
                      + pv_running_max(i, _nt_dot(k_refs[1][...], lhs_aug[1][:, :LANES])))

        return carry

    lax.fori_loop(0, n_q, query_tile, 0)
    finish(n_q - 1)


def _diff_attn(q1, q2, k1, k2, v, bias_t, bias_max, lam_vecs, subln_g, lam_init, batch, seq, tq):
    n_pairs = C_HEADS // 2
    n_tiles = bias_t.shape[1]
    seq_spec = pl.BlockSpec((seq, LANES), lambda b, p: (b, p))
    g_cols = jnp.broadcast_to(subln_g.reshape(C_V_DIM, 1), (C_V_DIM, tq))
    return pl.pallas_call(
        functools.partial(_diff_attn_kernel, tq=tq, lam_init=lam_init),
        grid=(batch, n_pairs),
        in_specs=[seq_spec, seq_spec, seq_spec, seq_spec,
                  pl.BlockSpec((seq, 2 * C_V_DIM), lambda b, p: (b, p)),
                  pl.BlockSpec((2, n_tiles, BLOCK, BLOCK), lambda b, p: (p, 0, 0, 0)),
                  pl.BlockSpec((2, 1, LANES), lambda b, p: (p, 0, 0)),
                  pl.BlockSpec(lam_vecs.shape, lambda b, p: (0, 0)),
                  pl.BlockSpec((C_V_DIM, tq), lambda b, p: (0, 0))],
        out_specs=pl.BlockSpec((seq, 2 * C_V_DIM), lambda b, p: (b, p)),
        out_shape=jax.ShapeDtypeStruct(v.shape, BF16),
        scratch_shapes=[pltpu.VMEM((2, C_V_DIM + ONES_ROWS, seq), BF16),
                        pltpu.VMEM((2, seq, 2 * LANES), BF16),
                        pltpu.VMEM((2, 2, LANES), F32),
                        pltpu.VMEM((2, 2 * tq, 2 * LANES), BF16),
                        pltpu.VMEM((4, C_V_DIM + ONES_ROWS, tq), F32)],
        compiler_params=_cparams(2),
        name="diff_attn",
    )(q1, q2, k1, k2, v, bias_t, bias_max, lam_vecs, g_cols)


def _rope_tables(seq):
    rows = seq // GRID_W
    row = jnp.broadcast_to(jnp.arange(rows)[:, None], (rows, GRID_W)).reshape(-1)
    col = jnp.broadcast_to(jnp.arange(GRID_W)[None, :], (rows, GRID_W)).reshape(-1)
    half = HEAD_DIM // 2
    inv = 1.0 / (ROPE_THETA ** (jnp.arange(0, half, 2, dtype=F32) / half))
    ang_row = row.astype(F32)[:, None] * inv
    ang_col = col.astype(F32)[:, None] * inv
    cos = jnp.concatenate([jnp.cos(ang_row)] * 2 + [jnp.cos(ang_col)] * 2, axis=-1)
    sin = jnp.concatenate([-jnp.sin(ang_row), jnp.sin(ang_row),
                           -jnp.sin(ang_col), jnp.sin(ang_col)], axis=-1)
    return jnp.tile(cos, (1, 2)), jnp.tile(sin, (1, 2))


def kernel(x, mem, rel_bias, mem_norm, final_norm, even_norm, even_w_in, even_sink, even_q_norm, even_k_norm, even_w_mem_kv, even_w_out, odd_norm, odd_w_in, odd_lambda_q1, odd_lambda_k1, odd_lambda_q2, odd_lambda_k2, odd_subln, odd_w_mem_kv, odd_w_out):
    batch, seq, d = x.shape
    mem_len = mem.shape[1]
    tokens = batch * seq
    xw = X_HEADS * HEAD_DIM
    gw = A_HEADS * HEAD_DIM

    w_in0 = even_w_in[0].astype(BF16)
    w_out0 = even_w_out[0].astype(BF16)
    w_in1 = odd_w_in[0].astype(BF16)
    w_out1 = odd_w_out[0].astype(BF16)

    x2 = x.reshape(tokens, d)
    mem2 = mem.reshape(batch * mem_len, d)

    bias_near, bias_win = _bias_tiles(rel_bias)
    cos, sin_signed = _rope_tables(seq)
    head_bias_max = jnp.max(rel_bias, axis=0) * LOG2E

    w_mem = jnp.concatenate([even_w_mem_kv[0], odd_w_mem_kv[0]], axis=1).astype(BF16)
    mk0, mv0, mk1, mv1 = _norm_proj(mem2, mem_norm, w_mem, (xw,) * 4, TM_PROJ, gate_last=False)

    rope_args = (cos, sin_signed, jnp.tile(even_q_norm[0], 2).reshape(1, LANES),
                 jnp.tile(even_k_norm[0], 2).reshape(1, LANES))
    aq, ak, av, bq, bk, bv, xq, gate = _norm_proj(
        x2, even_norm[0], w_in0, (gw, 128, 128, gw, 128, 128, xw, D_MODEL), TM_PROJ,
        gate_last=True, qk_rope=(3, 4), rope_args=rope_args, seq=seq)
    y_a = _window_attn(aq, ak, av, bias_win, even_sink[0] * LOG2E, batch, seq, WINDOW_GROUP)
    y_b = _dense_attn(bq, bk, bv, batch, seq, TQ_DENSE)
    y_x = _cross_attn(xq, mk0, mv0, batch, seq, TQ_CROSS)
    lam_init = 0.8 - 0.6 * math.exp(-0.3 * 1)
    cw = C_HEADS * HEAD_DIM
    h1, q1, q2, k1, k2, v, xq1, gate1 = _gate_out(
        [y_a, y_b, y_x], gate, w_out0, x2, odd_norm[0], TM_PROJ, w_next=w_in1,
        splits=(cw, cw, cw, cw, C_HEADS * C_V_DIM, xw, D_MODEL))
    lam_vecs = jnp.stack([odd_lambda_q1[0], odd_lambda_k1[0], odd_lambda_q2[0], odd_lambda_k2[0]])
    bias_max = jnp.broadcast_to(head_bias_max.reshape(C_HEADS, 1, 1), (C_HEADS, 1, LANES))
    y_c = _diff_attn(q1, q2, k1, k2, v, bias_near, bias_max, lam_vecs, odd_subln[0], lam_init,
                     batch, seq, TQ_DIFF)
    y_x1 = _cross_attn(xq1, mk1, mv1, batch, seq, TQ_CROSS)
    (out,) = _gate_out([y_c, y_x1], gate1, w_out1, h1, final_norm, TM_PROJ)
    return out.reshape(batch, seq, d)
```

```python
import functools
import math

import numpy as np
import jax
import jax.numpy as jnp
from jax import lax
from jax.experimental import pallas as pl
from jax.experimental.pallas import tpu as pltpu

D_MODEL = 1024
HEAD_DIM = 64
BLOCK = 128
WINDOW = 128
GRID_W = 64
A_HEADS = 6
B_HEADS = 6
C_HEADS = 6
C_V_DIM = 128
X_HEADS = 4
REL_BUCKETS = 32
REL_MAX_DIST = 128
ROPE_THETA = 10000.0
EPS = 1e-6
NEG_INF = -1e30
LOG2E = math.log2(math.e)
LANES = 128
ONES_ROWS = 16
PROJ_SUB_ROWS = 256
TM_PROJ = 1024
TQ_DENSE = 256
TQ_DIFF = 512
TQ_CROSS = 2048
WINDOW_GROUP = 8
SUM_FLOOR = 2.0 ** -64
BIAS_FAR = 2
VMEM_LIMIT = 56 * 1024 * 1024

F32 = jnp.float32
BF16 = jnp.bfloat16


def _cparams(n_axes, flags=None):
    return pltpu.CompilerParams(dimension_semantics=("arbitrary",) * n_axes,
                                vmem_limit_bytes=VMEM_LIMIT, flags=flags)


def _lane_lo(shape):
    return lax.broadcasted_iota(jnp.int32, shape, len(shape) - 1) < HEAD_DIM


def _rmsnorm_rows(x, g):
    ms = jnp.mean(x * x, axis=-1, keepdims=True)
    return x * lax.rsqrt(ms + EPS) * g


def _nt_dot(a, b):
    return lax.dot_general(a, b, (((1,), (1,)), ((), ())), preferred_element_type=F32)


def _norm_proj_kernel(*refs, qk_rope, gate_last):
    x_ref, g_ref, w_ref = refs[:3]
    if qk_rope is None:
        out_refs = refs[3:]
    else:
        cos_ref, sin_ref, gq_ref, gk_ref = refs[3:7]
        out_refs = refs[7:]
    tm = x_ref.shape[0]
    for r0 in range(0, tm, PROJ_SUB_ROWS):
        rows = slice(r0, min(r0 + PROJ_SUB_ROWS, tm))
        xn = _rmsnorm_rows(x_ref[rows, :], g_ref[...]).astype(BF16)
        y_all = jnp.dot(xn, w_ref[...], preferred_element_type=F32)
        c0 = 0
        for j, o_ref in enumerate(out_refs):
            width = o_ref.shape[1]
            y = y_all[:, c0:c0 + width]
            if qk_rope is not None and j in qk_rope:
                gain, scale = ((gq_ref, HEAD_DIM ** -0.5 * LOG2E) if j == qk_rope[0]
                               else (gk_ref, 1.0))
                y = jnp.concatenate(
                    [_norm_rope_pair(y[:, t * LANES:(t + 1) * LANES], gain[...],
                                     cos_ref[rows, :], sin_ref[rows, :]) * scale
                     for t in range(width // LANES)], axis=1)
            if gate_last and j == len(out_refs) - 1:
                y = _silu(y)
            o_ref[rows, :] = y.astype(o_ref.dtype)
            c0 += width


def _silu(y):
    return y * (1.0 / (1.0 + jnp.exp(-y)))


def _norm_proj(x, g, w_bf16, splits, tm, gate_last, qk_rope=None, rope_args=(), seq=None):
    rows, d = x.shape
    n = w_bf16.shape[1]
    assert sum(splits) == n and rows % tm == 0
    in_specs = [pl.BlockSpec((tm, d), lambda i: (i, 0)),
                pl.BlockSpec((1, d), lambda i: (0, 0)),
                pl.BlockSpec((d, n), lambda i: (0, 0))]
    if qk_rope is not None:
        pos_blocks = seq // tm
        in_specs += [pl.BlockSpec((tm, LANES), lambda i: (i % pos_blocks, 0)),
                     pl.BlockSpec((tm, LANES), lambda i: (i % pos_blocks, 0)),
                     pl.BlockSpec((1, LANES), lambda i: (0, 0)),
                     pl.BlockSpec((1, LANES), lambda i: (0, 0))]
    return pl.pallas_call(
        functools.partial(_norm_proj_kernel, qk_rope=qk_rope, gate_last=gate_last),
        grid=(rows // tm,),
        in_specs=in_specs,
        out_specs=[pl.BlockSpec((tm, s), lambda i: (i, 0)) for s in splits],
        out_shape=[jax.ShapeDtypeStruct((rows, s), BF16) for s in splits],
        compiler_params=_cparams(1),
        name="norm_proj",
    )(x, g.reshape(1, d), w_bf16, *rope_args)


def _gate_out_kernel(*refs, n_parts, after):
    y_refs = refs[:n_parts]
    gate_ref, w_ref, res_ref, g_ref = refs[n_parts:n_parts + 4]
    if after == "proj":
        w_next_ref, h_ref = refs[n_parts + 4:n_parts + 6]
        out_refs = refs[n_parts + 6:]
    else:
        h_ref = refs[n_parts + 4]
    tm = res_ref.shape[0]
    for r0 in range(0, tm, PROJ_SUB_ROWS):
        rows = slice(r0, min(r0 + PROJ_SUB_ROWS, tm))
        parts = []
        c0 = 0
        for y_ref in y_refs:
            width = y_ref.shape[1]
            parts.append(y_ref[rows, :] * gate_ref[rows, c0:c0 + width])
            c0 += width
        h = res_ref[rows, :] + jnp.dot(jnp.concatenate(parts, axis=1), w_ref[...],
                                       preferred_element_type=F32)
        if after == "norm":
            h_ref[rows, :] = _rmsnorm_rows(h, g_ref[...])
            continue
        h_ref[rows, :] = h
        y_all = jnp.dot(_rmsnorm_rows(h, g_ref[...]).astype(BF16), w_next_ref[...],
                        preferred_element_type=F32)
        c0 = 0
        for j, o_ref in enumerate(out_refs):
            width = o_ref.shape[1]
            y = y_all[:, c0:c0 + width]
            if j == len(out_refs) - 1:
                y = _silu(y)
            o_ref[rows, :] = y.astype(o_ref.dtype)
            c0 += width


def _gate_out(y_parts, gate, w_bf16, resid, norm_g, tm, w_next=None, splits=()):
    rows, d = resid.shape
    mix = w_bf16.shape[0]
    after = "norm" if w_next is None else "proj"
    row_block = lambda width: pl.BlockSpec((tm, width), lambda i: (i, 0))
    whole = lambda a: pl.BlockSpec(a.shape, lambda i: (0, 0))
    in_specs = [row_block(y.shape[1]) for y in y_parts]
    in_specs += [row_block(mix), whole(w_bf16), row_block(d), pl.BlockSpec((1, d), lambda i: (0, 0))]
    args = list(y_parts) + [gate, w_bf16, resid, norm_g.reshape(1, d)]
    out_specs = [row_block(d)]
    out_shape = [jax.ShapeDtypeStruct((rows, d), F32)]
    if after == "proj":
        assert sum(splits) == w_next.shape[1]
        in_specs.append(whole(w_next))
        args.append(w_next)
        out_specs += [row_block(s) for s in splits]
        out_shape += [jax.ShapeDtypeStruct((rows, s), BF16) for s in splits]
    return pl.pallas_call(
        functools.partial(_gate_out_kernel, n_parts=len(y_parts), after=after),
        grid=(rows // tm,),
        in_specs=in_specs,
        out_specs=out_specs,
        out_shape=out_shape,
        compiler_params=_cparams(1),
        name="gate_out",
    )(*args)


def _bucket_thresholds():
    nb = REL_BUCKETS // 2
    max_exact = nb // 2
    n = np.arange(0, 4 * REL_MAX_DIST)
    nf = np.maximum(n, 1).astype(np.float32)
    large = max_exact + (np.log(nf / np.float32(max_exact))
                         / np.float32(math.log(REL_MAX_DIST / max_exact))
                         * np.float32(nb - max_exact)).astype(np.int32)
    bucket = np.where(n < max_exact, n, np.minimum(large, nb - 1))
    assert np.all(np.diff(bucket) >= 0) and bucket[0] == 0 and bucket[-1] == nb - 1
    thr = [int(np.argmax(bucket >= k)) for k in range(1, nb)]
    assert thr[-1] < REL_MAX_DIST
    return thr


def _bias_tiles_kernel(tab_ref, near_ref, win_ref):
    h = pl.program_id(0)
    thr = _bucket_thresholds()
    nb = REL_BUCKETS // 2
    row = lax.broadcasted_iota(jnp.int32, (BLOCK, BLOCK), 0)
    col = lax.broadcasted_iota(jnp.int32, (BLOCK, BLOCK), 1)
    def tile_of(rel):
        n = jnp.abs(rel)
        neg = jnp.full((BLOCK, BLOCK), tab_ref[0, h], F32)
        pos = jnp.full((BLOCK, BLOCK), tab_ref[nb, h], F32)
        for b in range(1, nb):
            ge = n >= thr[b - 1]
            neg = jnp.where(ge, tab_ref[b, h], neg)
            pos = jnp.where(ge, tab_ref[nb + b, h], pos)
        return jnp.where(rel > 0, pos, neg) * LOG2E

    for k in range(2 * BIAS_FAR + 1):
        rel = (k - BIAS_FAR) * BLOCK + row - col
        tile = tile_of(rel)
        near_ref[0, k] = tile
        if abs(k - BIAS_FAR) <= 1:
            j = k - BIAS_FAR + 1
            win_ref[j * BLOCK:(j + 1) * BLOCK, :] = jnp.where(jnp.abs(rel) <= WINDOW, tile, NEG_INF)


def _bias_tiles(rel_bias):
    assert _bucket_thresholds()[-1] <= (BIAS_FAR - 1) * BLOCK + 1
    n_tiles = 2 * BIAS_FAR + 1
    n_heads = rel_bias.shape[1]
    return pl.pallas_call(
        _bias_tiles_kernel,
        grid=(n_heads,),
        in_specs=[pl.BlockSpec(memory_space=pltpu.SMEM)],
        out_specs=[pl.BlockSpec((1, n_tiles, BLOCK, BLOCK), lambda h: (h, 0, 0, 0)),
                   pl.BlockSpec((3 * BLOCK, BLOCK), lambda h: (0, h))],
        out_shape=[jax.ShapeDtypeStruct((n_heads, n_tiles, BLOCK, BLOCK), F32),
                   jax.ShapeDtypeStruct((3 * BLOCK, n_heads * BLOCK), F32)],
        compiler_params=_cparams(1),
        name="bias_tiles",
    )(rel_bias)


def _stack_gqa_heads(q_tiles):
    lo = _lane_lo(q_tiles[0].shape)
    swap = lambda q: pltpu.roll(q, HEAD_DIM, 1)
    heads_lo = [q_tiles[0], swap(q_tiles[0]), q_tiles[1]]
    heads_hi = [q_tiles[1], swap(q_tiles[2]), q_tiles[2]]
    return jnp.concatenate([jnp.where(lo, q, 0.0).astype(BF16) for q in heads_lo]
                           + [jnp.where(lo, 0.0, q).astype(BF16) for q in heads_hi], axis=0)


def _unstack_gqa_heads(out_lo, out_hi, width):
    blk = lambda o, j: o[:, j * width:(j + 1) * width]
    return [jnp.concatenate([blk(out_lo, 0), blk(out_lo, 1)], axis=0),
            jnp.concatenate([blk(out_lo, 2), blk(out_hi, 0)], axis=0),
            jnp.concatenate([blk(out_hi, 1), blk(out_hi, 2)], axis=0)]


def _stack_pair_heads(q):
    lo = _lane_lo(q.shape)
    return jnp.concatenate([jnp.where(lo, q, 0.0).astype(BF16),
                            jnp.where(lo, 0.0, q).astype(BF16)], axis=0)


def _with_shift_lane(lhs, kmax_lo, kmax_hi, bias_max_lo, bias_max_hi):
    half = lhs.shape[0] // 2
    lf = lhs.astype(F32)
    q_sq = jnp.sum(lf * lf, axis=-1, keepdims=True)
    q_norm = q_sq * lax.rsqrt(jnp.maximum(q_sq, 1e-30))
    row = lax.broadcasted_iota(jnp.int32, lhs.shape, 0)
    lane = lax.broadcasted_iota(jnp.int32, lhs.shape, 1)
    lo_row = row < half
    shift = (q_norm * jnp.where(lo_row, kmax_lo, kmax_hi)
             + jnp.where(lo_row, bias_max_lo, bias_max_hi))
    shift = shift + jnp.abs(shift) * 2.0 ** -7
    slot_lane = jnp.where(lo_row, HEAD_DIM, 0)
    return jnp.where(lane == slot_lane, -shift, lf).astype(BF16)


def _keys_with_ones(k_pair):
    lane = lax.broadcasted_iota(jnp.int32, k_pair.shape, 1)
    kf = k_pair.astype(F32)
    k_lo = jnp.where(lane < HEAD_DIM, kf, jnp.where(lane == HEAD_DIM, 1.0, 0.0))
    k_hi = jnp.where(lane >= HEAD_DIM, kf, jnp.where(lane == 0, 1.0, 0.0))
    return k_lo.astype(BF16), k_hi.astype(BF16)


def _shifted_scores(ka_lo, ka_hi, lhs_shifted):
    half = lhs_shifted.shape[0] // 2
    return jnp.concatenate([_nt_dot(ka_lo, lhs_shifted[:half]), _nt_dot(ka_hi, lhs_shifted[half:])],
                           axis=1)


def _pair_key_norm_max(k_pair):
    kf = k_pair.astype(F32)
    row = lax.broadcasted_iota(jnp.int32, (LANES, 2 * LANES), 0)
    col = lax.broadcasted_iota(jnp.int32, (LANES, 2 * LANES), 1)
    half_sel = jnp.where((row < HEAD_DIM) == (col < LANES), 1.0, 0.0).astype(BF16)
    sq = jnp.dot((kf * kf).astype(BF16), half_sel, preferred_element_type=F32)
    norm_max = jnp.sqrt(jnp.max(sq, axis=0, keepdims=True))
    return norm_max[:, :LANES], norm_max[:, LANES:]


def _window_attn_kernel(q_ref, k_ref, v_ref, bias_ref, sink_row_ref,
                        o_ref, kpad_scr, vtpad_scr, *, n_blocks, group):
    i = pl.program_id(1)
    half = 3 * BLOCK

    @pl.when(i == 0)
    def _():
        zero_blk = jnp.zeros((1, BLOCK, LANES), BF16)
        kpad_scr[0:1] = zero_blk
        kpad_scr[n_blocks + 1:n_blocks + 2] = zero_blk
        kpad_scr[1:n_blocks + 1] = k_ref[...].reshape(n_blocks, BLOCK, LANES)
        v_t = v_ref[...].astype(F32).T.astype(BF16)
        ones = jnp.ones((ONES_ROWS, BLOCK), BF16)
        zero_v = jnp.zeros((HEAD_DIM + ONES_ROWS, BLOCK), BF16)
        for g in range(2):
            vtpad_scr[g, 0] = zero_v
            vtpad_scr[g, n_blocks + 1] = zero_v
            for blk in range(n_blocks):
                vtpad_scr[g, blk + 1, :HEAD_DIM, :] = v_t[g * HEAD_DIM:(g + 1) * HEAD_DIM,
                                                           blk * BLOCK:(blk + 1) * BLOCK]
                vtpad_scr[g, blk + 1, HEAD_DIM:, :] = ones

    sink_row = sink_row_ref[...]

    for u in range(group):
        n = i * group + u
        qf = q_ref[u * BLOCK:(u + 1) * BLOCK, :].astype(F32) * (HEAD_DIM ** -0.5 * LOG2E)
        lhs = _stack_gqa_heads([qf[:, j * LANES:(j + 1) * LANES] for j in range(3)])
        k_win = kpad_scr[pl.ds(n, 3)].reshape(3 * BLOCK, LANES)
        s = _nt_dot(k_win, lhs) + bias_ref[...]
        s = jnp.concatenate([jnp.where(n > 0, s[:BLOCK], NEG_INF), s[BLOCK:2 * BLOCK],
                             jnp.where(n < n_blocks - 1, s[2 * BLOCK:], NEG_INF)], axis=0)
        m = jnp.maximum(jnp.max(s, axis=0, keepdims=True), sink_row)
        p = jnp.exp2(s - m).astype(BF16)
        p_sink = jnp.exp2(sink_row - m)
        outs = []
        for g in range(2):
            acc = None
            for j in range(3):
                pv = jnp.dot(vtpad_scr[g, n + j], p[j * BLOCK:(j + 1) * BLOCK, g * half:(g + 1) * half],
                             preferred_element_type=F32)
                acc = pv if acc is None else acc + pv
            total = acc[HEAD_DIM:HEAD_DIM + 1] + p_sink[:, g * half:(g + 1) * half]
            outs.append(acc[:HEAD_DIM] * (1.0 / total))
        for j, pair_t in enumerate(_unstack_gqa_heads(outs[0], outs[1], BLOCK)):
            o_ref[u * BLOCK:(u + 1) * BLOCK, j * LANES:(j + 1) * LANES] = pair_t.T.astype(o_ref.dtype)


def _window_attn(q, k, v, bias_win_t, sink, batch, seq, group):
    nb = seq // BLOCK
    steps = nb // group
    rows = A_HEADS * BLOCK
    sink_row = jnp.repeat(sink, BLOCK).reshape(1, rows)
    const = lambda shape: pl.BlockSpec(shape, lambda b, i: (0,) * len(shape))
    return pl.pallas_call(
        functools.partial(_window_attn_kernel, n_blocks=nb, group=group),
        grid=(batch, steps),
        in_specs=[pl.BlockSpec((group * BLOCK, 3 * LANES), lambda b, i: (b * steps + i, 0)),
                  pl.BlockSpec((seq, LANES), lambda b, i: (b, 0)),
                  pl.BlockSpec((seq, LANES), lambda b, i: (b, 0)),
                  const(bias_win_t.shape), const((1, rows))],
        out_specs=pl.BlockSpec((group * BLOCK, 3 * LANES), lambda b, i: (b * steps + i, 0)),
        out_shape=jax.ShapeDtypeStruct(q.shape, BF16),
        scratch_shapes=[pltpu.VMEM((nb + 2, BLOCK, LANES), BF16),
                        pltpu.VMEM((2, nb + 2, HEAD_DIM + ONES_ROWS, BLOCK), BF16)],
        compiler_params=_cparams(2),
        name="window_attn",
    )(q, k, v, bias_win_t, sink_row)


def _norm_rope_pair(x, g, cos, sin_signed):
    lo = _lane_lo(x.shape)
    x2 = x * x
    ss_lo = jnp.sum(jnp.where(lo, x2, 0.0), axis=-1, keepdims=True)
    ss_hi = jnp.sum(jnp.where(lo, 0.0, x2), axis=-1, keepdims=True)
    ms = jnp.where(lo, ss_lo, ss_hi) * (1.0 / HEAD_DIM)
    y = x * lax.rsqrt(ms + EPS) * g
    lane = lax.broadcasted_iota(jnp.int32, x.shape, 1)
    quarter = HEAD_DIM // 4
    first = (lane & quarter) == 0
    partner = jnp.where(first, pltpu.roll(y, LANES - quarter, 1), pltpu.roll(y, quarter, 1))
    return y * cos + partner * sin_signed


def _dense_attn_kernel(q_ref, k_ref, v_ref, o_ref, ka_scr, vt_scr, kmax_scr, *, tq):
    seq = k_ref.shape[0]
    kc = 2 * BLOCK
    n_chunks = seq // kc
    grp = 3 * tq

    ka_scr[0], ka_scr[1] = _keys_with_ones(k_ref[...])
    kmax_lo, kmax_hi = _pair_key_norm_max(k_ref[...])
    kmax_scr[0:1, :] = kmax_lo
    kmax_scr[1:2, :] = kmax_hi
    v_t = v_ref[...].astype(F32).T.astype(BF16)
    for g in range(2):
        vt_scr[g, :HEAD_DIM, :] = v_t[g * HEAD_DIM:(g + 1) * HEAD_DIM]
        vt_scr[g, HEAD_DIM:, :] = jnp.ones((ONES_ROWS, seq), BF16)

    def query_tile(i, carry):
        rows = pl.ds(pl.multiple_of(i * tq, tq), tq)
        lhs = _stack_gqa_heads([q_ref[rows, j * LANES:(j + 1) * LANES].astype(F32)
                                for j in range(3)])

        def write(acc):
            outs = [a[:HEAD_DIM] * (1.0 / a[HEAD_DIM:HEAD_DIM + 1]) for a in acc]
            for j, pair_t in enumerate(_unstack_gqa_heads(outs[0], outs[1], tq)):
                o_ref[rows, j * LANES:(j + 1) * LANES] = pair_t.T.astype(o_ref.dtype)

        s_all = _shifted_scores(ka_scr[0], ka_scr[1], _with_shift_lane(
            lhs, kmax_scr[0:1, :], kmax_scr[1:2, :], 0.0, 0.0))
        p = jnp.exp2(s_all).astype(BF16)
        acc = [jnp.dot(vt_scr[g], p[:, g * grp:(g + 1) * grp], preferred_element_type=F32)
               for g in range(2)]
        write(acc)
        sums = jnp.concatenate([a[HEAD_DIM:HEAD_DIM + 1] for a in acc], axis=0)

        @pl.when(jnp.logical_not(jnp.min(sums) >= SUM_FLOOR))
        def _():
            s_exact = _nt_dot(k_ref[...], lhs)
            m = None
            acc2 = [None, None]
            for c in range(n_chunks):
                s = s_exact[c * kc:(c + 1) * kc, :]
                m_c = jnp.max(s, axis=0, keepdims=True)
                m_new = m_c if m is None else jnp.maximum(m, m_c)
                pc = jnp.exp2(s - m_new).astype(BF16)
                if m is not None:
                    alpha = jnp.exp2(m - m_new)
                for g in range(2):
                    cols = slice(g * grp, (g + 1) * grp)
                    pv = jnp.dot(vt_scr[g, :, c * kc:(c + 1) * kc], pc[:, cols],
                                 preferred_element_type=F32)
                    acc2[g] = pv if m is None else acc2[g] * alpha[:, cols] + pv
                m = m_new
            write(acc2)

        return carry

    lax.fori_loop(0, seq // tq, query_tile, 0)


def _dense_attn(q, k, v, batch, seq, tq):
    return pl.pallas_call(
        functools.partial(_dense_attn_kernel, tq=tq),
        grid=(batch,),
        in_specs=[pl.BlockSpec((seq, 3 * LANES), lambda b: (b, 0)),
                  pl.BlockSpec((seq, LANES), lambda b: (b, 0)),
                  pl.BlockSpec((seq, LANES), lambda b: (b, 0))],
        out_specs=pl.BlockSpec((seq, 3 * LANES), lambda b: (b, 0)),
        out_shape=jax.ShapeDtypeStruct(q.shape, BF16),
        scratch_shapes=[pltpu.VMEM((2, seq, LANES), BF16),
                        pltpu.VMEM((2, HEAD_DIM + ONES_ROWS, seq), BF16),
                        pltpu.VMEM((2, LANES), F32)],
        compiler_params=_cparams(1),
        name="dense_attn",
    )(q, k, v)


def _cross_attn_kernel(q_ref, mk_ref, mv_ref, o_ref, vt_scr, *, tq):
    i = pl.program_id(1)
    n_pairs = q_ref.shape[1] // LANES
    mem_len = mk_ref.shape[0]

    @pl.when(i == 0)
    def _():
        v_t = mv_ref[...].astype(F32).T.astype(BF16)
        for h in range(2 * n_pairs):
            vt_scr[h, :HEAD_DIM, :] = v_t[h * HEAD_DIM:(h + 1) * HEAD_DIM]
            vt_scr[h, HEAD_DIM:, :] = jnp.ones((ONES_ROWS, mem_len), BF16)

    for j in range(n_pairs):
        cols = slice(j * LANES, (j + 1) * LANES)
        lhs = _stack_pair_heads(q_ref[:, cols].astype(F32) * (HEAD_DIM ** -0.5 * LOG2E))
        s = _nt_dot(mk_ref[:, cols], lhs)
        p = jnp.exp2(s - jnp.max(s, axis=0, keepdims=True)).astype(BF16)
        acc = [jnp.dot(vt_scr[2 * j + h], p[:, h * tq:(h + 1) * tq], preferred_element_type=F32)
               for h in range(2)]
        pair_t = jnp.concatenate([a[:HEAD_DIM] * (1.0 / a[HEAD_DIM:HEAD_DIM + 1]) for a in acc],
                                 axis=0)
        o_ref[:, cols] = pair_t.T.astype(o_ref.dtype)


def _cross_attn(q, mk, mv, batch, seq, tq):
    nq = seq // tq
    mem_len = mk.shape[0] // batch
    width = q.shape[1]
    n_pairs = width // LANES
    return pl.pallas_call(
        functools.partial(_cross_attn_kernel, tq=tq),
        grid=(batch, nq),
        in_specs=[pl.BlockSpec((tq, width), lambda b, i: (b * nq + i, 0)),
                  pl.BlockSpec((mem_len, width), lambda b, i: (b, 0)),
                  pl.BlockSpec((mem_len, width), lambda b, i: (b, 0))],
        out_specs=pl.BlockSpec((tq, width), lambda b, i: (b * nq + i, 0)),
        out_shape=jax.ShapeDtypeStruct(q.shape, BF16),
        scratch_shapes=[pltpu.VMEM((2 * n_pairs, HEAD_DIM + ONES_ROWS, mem_len), BF16)],
        compiler_params=_cparams(2),
        name="cross_attn",
    )(q, mk, mv)


def _diff_attn_kernel(q1_ref, q2_ref, k1_ref, k2_ref, v_ref, bias_ref, bmax_ref, lam_ref, g_ref,
                      o_ref, vt_scr, ka_scr, kmax_scr, *, tq, lam_init):
    sub = tq // BLOCK
    seq = k1_ref.shape[0]
    kc = 2 * BLOCK
    n_chunks = seq // kc
    k_refs = (k1_ref, k2_ref)

    for head in range(2):
        v_t = v_ref[:, head * C_V_DIM:(head + 1) * C_V_DIM].astype(F32).T
        vt_scr[head, :C_V_DIM, :] = v_t.astype(BF16)
        vt_scr[head, C_V_DIM:, :] = jnp.ones((ONES_ROWS, seq), BF16)
    for t in range(2):
        ka_scr[t, 0], ka_scr[t, 1] = _keys_with_ones(k_refs[t][...])
        kmax_lo, kmax_hi = _pair_key_norm_max(k_refs[t][...])
        kmax_scr[t, 0:1, :] = kmax_lo
        kmax_scr[t, 1:2, :] = kmax_hi

    lam_vec = lam_ref[...]
    lam = (jnp.exp(jnp.sum(lam_vec[0:1] * lam_vec[1:2], axis=-1, keepdims=True))
           - jnp.exp(jnp.sum(lam_vec[2:3] * lam_vec[3:4], axis=-1, keepdims=True)) + lam_init)

    def query_tile(i, carry):
        rows = pl.ds(pl.multiple_of(i * tq, tq), tq)
        lhs = [_stack_pair_heads(q_ref[rows, :].astype(F32) * (HEAD_DIM ** -0.5 * LOG2E))
               for q_ref in (q1_ref, q2_ref)]
        _diff_attn_tile(i, rows, lhs, lam, bias_ref, bmax_ref, g_ref, o_ref, vt_scr, ka_scr,
                        kmax_scr, k_refs, tq=tq, sub=sub, kc=kc, n_chunks=n_chunks,
                        lam_init=lam_init)
        return carry

    lax.fori_loop(0, seq // tq, query_tile, 0)


def _diff_attn_tile(i, rows, lhs, lam, bias_ref, bmax_ref, g_ref, o_ref, vt_scr, ka_scr, kmax_scr,
                    k_refs, *, tq, sub, kc, n_chunks, lam_init):
    def bias_chunk(c):
        def tile(head, u, e):
            offset = (2 * c + e) - (i * sub + u)
            return bias_ref[head, jnp.clip(offset, -BIAS_FAR, BIAS_FAR) + BIAS_FAR]
        return jnp.concatenate(
            [jnp.concatenate([tile(head, u, e) for head in range(2) for u in range(sub)], axis=1)
             for e in range(2)], axis=0)

    def shifted_scores(t):
        return _shifted_scores(ka_scr[t, 0], ka_scr[t, 1], _with_shift_lane(
            lhs[t], kmax_scr[t, 0:1, :], kmax_scr[t, 1:2, :], bmax_ref[0], bmax_ref[1]))

    def pv_of_shifted(s_all):
        p = jnp.concatenate([jnp.exp2(s_all[c * kc:(c + 1) * kc, :] + bias_chunk(c)).astype(BF16)
                             for c in range(n_chunks)], axis=0)
        return [jnp.dot(vt_scr[head], p[:, head * tq:(head + 1) * tq],
                        preferred_element_type=F32) for head in range(2)]

    def pv_running_max(s_all):
        m = None
        acc = [None, None]
        for c in range(n_chunks):
            s = s_all[c * kc:(c + 1) * kc, :] + bias_chunk(c)
            m_c = jnp.max(s, axis=0, keepdims=True)
            m_new = m_c if m is None else jnp.maximum(m, m_c)
            p = jnp.exp2(s - m_new).astype(BF16)
            if m is not None:
                alpha = jnp.exp2(m - m_new)
            for head in range(2):
                cols = slice(head * tq, (head + 1) * tq)
                pv = jnp.dot(vt_scr[head, :, c * kc:(c + 1) * kc], p[:, cols],
                             preferred_element_type=F32)
                acc[head] = pv if m is None else acc[head] * alpha[:, cols] + pv
            m = m_new
        return acc

    def write(acc1, acc2):
        g = g_ref[...] * (1.0 - lam_init)
        for head in range(2):
            o1 = acc1[head][:C_V_DIM] * (1.0 / acc1[head][C_V_DIM:C_V_DIM + 1])
            o2 = acc2[head][:C_V_DIM] * (1.0 / acc2[head][C_V_DIM:C_V_DIM + 1])
            out = o1 - lam * o2
            ms = jnp.mean(out * out, axis=0, keepdims=True)
            y = out * lax.rsqrt(ms + EPS) * g
            o_ref[rows, head * C_V_DIM:(head + 1) * C_V_DIM] = y.T.astype(o_ref.dtype)

    s1 = shifted_scores(0)
    s2 = shifted_scores(1)
    acc1 = pv_of_shifted(s1)
    acc2 = pv_of_shifted(s2)
    write(acc1, acc2)
    sums = jnp.concatenate([a[C_V_DIM:C_V_DIM + 1] for a in acc1 + acc2], axis=0)

    @pl.when(jnp.logical_not(jnp.min(sums) >= SUM_FLOOR))
    def _():
        write(pv_running_max(_nt_dot(k_refs[0][...], lhs[0])),
              pv_running_max(_nt_dot(k_refs[1][...], lhs[1])))


def _diff_attn(q1, q2, k1, k2, v, bias_t, bias_max, lam_vecs, subln_g, lam_init, batch, seq, tq):
    n_pairs = C_HEADS // 2
    n_tiles = bias_t.shape[1]
    seq_spec = pl.BlockSpec((seq, LANES), lambda b, p: (b, p))
    g_cols = jnp.broadcast_to(subln_g.reshape(C_V_DIM, 1), (C_V_DIM, tq))
    return pl.pallas_call(
        functools.partial(_diff_attn_kernel, tq=tq, lam_init=lam_init),
        grid=(batch, n_pairs),
        in_specs=[seq_spec, seq_spec, seq_spec, seq_spec,
                  pl.BlockSpec((seq, 2 * C_V_DIM), lambda b, p: (b, p)),
                  pl.BlockSpec((2, n_tiles, BLOCK, BLOCK), lambda b, p: (p, 0, 0, 0)),
                  pl.BlockSpec((2, 1, LANES), lambda b, p: (p, 0, 0)),
                  pl.BlockSpec(lam_vecs.shape, lambda b, p: (0, 0)),
                  pl.BlockSpec((C_V_DIM, tq), lambda b, p: (0, 0))],
        out_specs=pl.BlockSpec((seq, 2 * C_V_DIM), lambda b, p: (b, p)),
        out_shape=jax.ShapeDtypeStruct(v.shape, BF16),
        scratch_shapes=[pltpu.VMEM((2, C_V_DIM + ONES_ROWS, seq), BF16),
                        pltpu.VMEM((2, 2, seq, LANES), BF16),
                        pltpu.VMEM((2, 2, LANES), F32)],
        compiler_params=_cparams(2),
        name="diff_attn",
    )(q1, q2, k1, k2, v, bias_t, bias_max, lam_vecs, g_cols)


def _rope_tables(seq):
    rows = seq // GRID_W
    row = jnp.broadcast_to(jnp.arange(rows)[:, None], (rows, GRID_W)).reshape(-1)
    col = jnp.broadcast_to(jnp.arange(GRID_W)[None, :], (rows, GRID_W)).reshape(-1)
    half = HEAD_DIM // 2
    inv = 1.0 / (ROPE_THETA ** (jnp.arange(0, half, 2, dtype=F32) / half))
    ang_row = row.astype(F32)[:, None] * inv
    ang_col = col.astype(F32)[:, None] * inv
    cos = jnp.concatenate([jnp.cos(ang_row)] * 2 + [jnp.cos(ang_col)] * 2, axis=-1)
    sin = jnp.concatenate([-jnp.sin(ang_row), jnp.sin(ang_row),
                           -jnp.sin(ang_col), jnp.sin(ang_col)], axis=-1)
    return jnp.tile(cos, (1, 2)), jnp.tile(sin, (1, 2))


def kernel(x, mem, rel_bias, mem_norm, final_norm, even_norm, even_w_in, even_sink, even_q_norm, even_k_norm, even_w_mem_kv, even_w_out, odd_norm, odd_w_in, odd_lambda_q1, odd_lambda_k1, odd_lambda_q2, odd_lambda_k2, odd_subln, odd_w_mem_kv, odd_w_out):
    batch, seq, d = x.shape
    mem_len = mem.shape[1]
    tokens = batch * seq
    xw = X_HEADS * HEAD_DIM
    gw = A_HEADS * HEAD_DIM

    w_in0 = even_w_in[0].astype(BF16)
    w_out0 = even_w_out[0].astype(BF16)
    w_in1 = odd_w_in[0].astype(BF16)
    w_out1 = odd_w_out[0].astype(BF16)

    x2 = x.reshape(tokens, d)
    mem2 = mem.reshape(batch * mem_len, d)

    bias_near, bias_win = _bias_tiles(rel_bias)
    cos, sin_signed = _rope_tables(seq)
    head_bias_max = jnp.max(rel_bias, axis=0) * LOG2E

    w_mem = jnp.concatenate([even_w_mem_kv[0], odd_w_mem_kv[0]], axis=1).astype(BF16)
    mk0, mv0, mk1, mv1 = _norm_proj(mem2, mem_norm, w_mem, (xw,) * 4, TM_PROJ, gate_last=False)

    rope_args = (cos, sin_signed, jnp.tile(even_q_norm[0], 2).reshape(1, LANES),
                 jnp.tile(even_k_norm[0], 2).reshape(1, LANES))
    aq, ak, av, bq, bk, bv, xq, gate = _norm_proj(
        x2, even_norm[0], w_in0, (gw, 128, 128, gw, 128, 128, xw, D_MODEL), TM_PROJ,
        gate_last=True, qk_rope=(3, 4), rope_args=rope_args, seq=seq)
    y_a = _window_attn(aq, ak, av, bias_win, even_sink[0] * LOG2E, batch, seq, WINDOW_GROUP)
    y_b = _dense_attn(bq, bk, bv, batch, seq, TQ_DENSE)
    y_x = _cross_attn(xq, mk0, mv0, batch, seq, TQ_CROSS)
    lam_init = 0.8 - 0.6 * math.exp(-0.3 * 1)
    cw = C_HEADS * HEAD_DIM
    h1, q1, q2, k1, k2, v, xq1, gate1 = _gate_out(
        [y_a, y_b, y_x], gate, w_out0, x2, odd_norm[0], TM_PROJ, w_next=w_in1,
        splits=(cw, cw, cw, cw, C_HEADS * C_V_DIM, xw, D_MODEL))
    lam_vecs = jnp.stack([odd_lambda_q1[0], odd_lambda_k1[0], odd_lambda_q2[0], odd_lambda_k2[0]])
    bias_max = jnp.broadcast_to(head_bias_max.reshape(C_HEADS, 1, 1), (C_HEADS, 1, LANES))
    y_c = _diff_attn(q1, q2, k1, k2, v, bias_near, bias_max, lam_vecs, odd_subln[0], lam_init,
                     batch, seq, TQ_DIFF)
    y_x1 = _cross_attn(xq1, mk1, mv1, batch, seq, TQ_CROSS)
    (out,) = _gate_out([y_c, y_x1], gate1, w_out1, h1, final_norm, TM_PROJ)
    return out.reshape(batch, seq, d)
```

```python
import functools
import math

import numpy as np
import jax
import jax.numpy as jnp
from jax import lax
from jax.experimental import pallas as pl
from jax.experimental.pallas import tpu as pltpu

D_MODEL = 1024
HEAD_DIM = 64
BLOCK = 128
WINDOW = 128
GRID_W = 64
A_HEADS = 6
B_HEADS = 6
C_HEADS = 6
C_V_DIM = 128
X_HEADS = 4
REL_BUCKETS = 32
REL_MAX_DIST = 128
ROPE_THETA = 10000.0
EPS = 1e-6
NEG_INF = -1e30
LOG2E = math.log2(math.e)
LANES = 128
ONES_ROWS = 16
PROJ_SUB_ROWS = 256
TM_PROJ = 1024
TM_MEM = 256
TQ_DENSE = 256
TQ_DIFF = 512
TQ_CROSS = 2048
WINDOW_GROUP = 16
SUM_FLOOR = 2.0 ** -64
SHIFT_MARGIN = 2.0 ** -7
BIAS_FAR = 2
VMEM_LIMIT = 56 * 1024 * 1024

F32 = jnp.float32
BF16 = jnp.bfloat16


def _cparams(n_axes, flags=None):
    return pltpu.CompilerParams(dimension_semantics=("arbitrary",) * n_axes,
                                vmem_limit_bytes=VMEM_LIMIT, flags=flags)


def _lane_lo(shape):
    return lax.broadcasted_iota(jnp.int32, shape, len(shape) - 1) < HEAD_DIM


def _rmsnorm_rows(x, g):
    ms = jnp.mean(x * x, axis=-1, keepdims=True)
    return x * lax.rsqrt(ms + EPS) * g


def _nt_dot(a, b):
    return lax.dot_general(a, b, (((1,), (1,)), ((), ())), preferred_element_type=F32)


def _norm_proj_kernel(*refs, qk_rope, gate_last):
    x_ref, g_ref, w_ref = refs[:3]
    if qk_rope is None:
        out_refs = refs[3:]
    else:
        cos_ref, sin_ref, gq_ref, gk_ref = refs[3:7]
        out_refs = refs[7:]
    tm = x_ref.shape[0]
    for r0 in range(0, tm, PROJ_SUB_ROWS):
        rows = slice(r0, min(r0 + PROJ_SUB_ROWS, tm))
        xn = _rmsnorm_rows(x_ref[rows, :], g_ref[...]).astype(BF16)
        y_all = jnp.dot(xn, w_ref[...], preferred_element_type=F32)
        c0 = 0
        for j, o_ref in enumerate(out_refs):
            width = o_ref.shape[1]
            y = y_all[:, c0:c0 + width]
            if qk_rope is not None and j in qk_rope:
                gain, scale = ((gq_ref, HEAD_DIM ** -0.5 * LOG2E) if j == qk_rope[0]
                               else (gk_ref, 1.0))
                y = jnp.concatenate(
                    [_norm_rope_pair(y[:, t * LANES:(t + 1) * LANES], gain[...],
                                     cos_ref[rows, :], sin_ref[rows, :]) * scale
                     for t in range(width // LANES)], axis=1)
            if gate_last and j == len(out_refs) - 1:
                y = _silu(y)
            o_ref[rows, :] = y.astype(o_ref.dtype)
            c0 += width


def _silu(y):
    return y * (1.0 / (1.0 + jnp.exp(-y)))


def _norm_proj(x, g, w_bf16, splits, tm, gate_last, qk_rope=None, rope_args=(), seq=None):
    rows, d = x.shape
    n = w_bf16.shape[1]
    assert sum(splits) == n and rows % tm == 0
    in_specs = [pl.BlockSpec((tm, d), lambda i: (i, 0)),
                pl.BlockSpec((1, d), lambda i: (0, 0)),
                pl.BlockSpec((d, n), lambda i: (0, 0))]
    if qk_rope is not None:
        pos_blocks = seq // tm
        in_specs += [pl.BlockSpec((tm, LANES), lambda i: (i % pos_blocks, 0)),
                     pl.BlockSpec((tm, LANES), lambda i: (i % pos_blocks, 0)),
                     pl.BlockSpec((1, LANES), lambda i: (0, 0)),
                     pl.BlockSpec((1, LANES), lambda i: (0, 0))]
    return pl.pallas_call(
        functools.partial(_norm_proj_kernel, qk_rope=qk_rope, gate_last=gate_last),
        grid=(rows // tm,),
        in_specs=in_specs,
        out_specs=[pl.BlockSpec((tm, s), lambda i: (i, 0)) for s in splits],
        out_shape=[jax.ShapeDtypeStruct((rows, s), BF16) for s in splits],
        compiler_params=_cparams(1),
        name="norm_proj",
    )(x, g.reshape(1, d), w_bf16, *rope_args)


def _gate_out_kernel(*refs, n_parts, after):
    y_refs = refs[:n_parts]
    gate_ref, w_ref, res_ref, g_ref = refs[n_parts:n_parts + 4]
    if after == "proj":
        w_next_ref, h_ref = refs[n_parts + 4:n_parts + 6]
        out_refs = refs[n_parts + 6:]
    else:
        h_ref = refs[n_parts + 4]
    tm = res_ref.shape[0]
    for r0 in range(0, tm, PROJ_SUB_ROWS):
        rows = slice(r0, min(r0 + PROJ_SUB_ROWS, tm))
        parts = []
        c0 = 0
        for y_ref in y_refs:
            width = y_ref.shape[1]
            parts.append(y_ref[rows, :] * gate_ref[rows, c0:c0 + width])
            c0 += width
        h = res_ref[rows, :] + jnp.dot(jnp.concatenate(parts, axis=1), w_ref[...],
                                       preferred_element_type=F32)
        if after == "norm":
            h_ref[rows, :] = _rmsnorm_rows(h, g_ref[...])
            continue
        h_ref[rows, :] = h
        y_all = jnp.dot(_rmsnorm_rows(h, g_ref[...]).astype(BF16), w_next_ref[...],
                        preferred_element_type=F32)
        c0 = 0
        for j, o_ref in enumerate(out_refs):
            width = o_ref.shape[1]
            y = y_all[:, c0:c0 + width]
            if j == len(out_refs) - 1:
                y = _silu(y)
            o_ref[rows, :] = y.astype(o_ref.dtype)
            c0 += width


def _gate_out(y_parts, gate, w_bf16, resid, norm_g, tm, w_next=None, splits=()):
    rows, d = resid.shape
    mix = w_bf16.shape[0]
    after = "norm" if w_next is None else "proj"
    row_block = lambda width: pl.BlockSpec((tm, width), lambda i: (i, 0))
    whole = lambda a: pl.BlockSpec(a.shape, lambda i: (0, 0))
    in_specs = [row_block(y.shape[1]) for y in y_parts]
    in_specs += [row_block(mix), whole(w_bf16), row_block(d), pl.BlockSpec((1, d), lambda i: (0, 0))]
    args = list(y_parts) + [gate, w_bf16, resid, norm_g.reshape(1, d)]
    out_specs = [row_block(d)]
    out_shape = [jax.ShapeDtypeStruct((rows, d), F32)]
    if after == "proj":
        assert sum(splits) == w_next.shape[1]
        in_specs.append(whole(w_next))
        args.append(w_next)
        out_specs += [row_block(s) for s in splits]
        out_shape += [jax.ShapeDtypeStruct((rows, s), BF16) for s in splits]
    return pl.pallas_call(
        functools.partial(_gate_out_kernel, n_parts=len(y_parts), after=after),
        grid=(rows // tm,),
        in_specs=in_specs,
        out_specs=out_specs,
        out_shape=out_shape,
        compiler_params=_cparams(1),
        name="gate_out",
    )(*args)


def _bucket_thresholds():
    nb = REL_BUCKETS // 2
    max_exact = nb // 2
    n = np.arange(0, 4 * REL_MAX_DIST)
    nf = np.maximum(n, 1).astype(np.float32)
    large = max_exact + (np.log(nf / np.float32(max_exact))
                         / np.float32(math.log(REL_MAX_DIST / max_exact))
                         * np.float32(nb - max_exact)).astype(np.int32)
    bucket = np.where(n < max_exact, n, np.minimum(large, nb - 1))
    assert np.all(np.diff(bucket) >= 0) and bucket[0] == 0 and bucket[-1] == nb - 1
    thr = [int(np.argmax(bucket >= k)) for k in range(1, nb)]
    assert thr[-1] < REL_MAX_DIST
    return thr


def _bias_tiles_kernel(tab_ref, near_ref, win_ref):
    h = pl.program_id(0)
    thr = _bucket_thresholds()
    nb = REL_BUCKETS // 2
    row = lax.broadcasted_iota(jnp.int32, (BLOCK, BLOCK), 0)
    col = lax.broadcasted_iota(jnp.int32, (BLOCK, BLOCK), 1)
    def tile_of(rel):
        n = jnp.abs(rel)
        neg = jnp.full((BLOCK, BLOCK), tab_ref[0, h], F32)
        pos = jnp.full((BLOCK, BLOCK), tab_ref[nb, h], F32)
        for b in range(1, nb):
            ge = n >= thr[b - 1]
            neg = jnp.where(ge, tab_ref[b, h], neg)
            pos = jnp.where(ge, tab_ref[nb + b, h], pos)
        return jnp.where(rel > 0, pos, neg) * LOG2E

    for k in range(2 * BIAS_FAR + 1):
        rel = (k - BIAS_FAR) * BLOCK + row - col
        tile = tile_of(rel)
        near_ref[0, k] = tile
        if abs(k - BIAS_FAR) <= 1:
            j = k - BIAS_FAR + 1
            win_ref[j * BLOCK:(j + 1) * BLOCK, :] = jnp.where(jnp.abs(rel) <= WINDOW, tile, NEG_INF)


def _bias_tiles(rel_bias):
    assert _bucket_thresholds()[-1] <= (BIAS_FAR - 1) * BLOCK + 1
    n_tiles = 2 * BIAS_FAR + 1
    n_heads = rel_bias.shape[1]
    return pl.pallas_call(
        _bias_tiles_kernel,
        grid=(n_heads,),
        in_specs=[pl.BlockSpec(memory_space=pltpu.SMEM)],
        out_specs=[pl.BlockSpec((1, n_tiles, BLOCK, BLOCK), lambda h: (h, 0, 0, 0)),
                   pl.BlockSpec((3 * BLOCK, BLOCK), lambda h: (0, h))],
        out_shape=[jax.ShapeDtypeStruct((n_heads, n_tiles, BLOCK, BLOCK), F32),
                   jax.ShapeDtypeStruct((3 * BLOCK, n_heads * BLOCK), F32)],
        compiler_params=_cparams(1),
        name="bias_tiles",
    )(rel_bias)


def _stack_gqa_heads(q_tiles):
    lo = _lane_lo(q_tiles[0].shape)
    swap = lambda q: pltpu.roll(q, HEAD_DIM, 1)
    heads_lo = [q_tiles[0], swap(q_tiles[0]), q_tiles[1]]
    heads_hi = [q_tiles[1], swap(q_tiles[2]), q_tiles[2]]
    return jnp.concatenate([jnp.where(lo, q, 0.0).astype(BF16) for q in heads_lo]
                           + [jnp.where(lo, 0.0, q).astype(BF16) for q in heads_hi], axis=0)


def _unstack_gqa_heads(out_lo, out_hi, width):
    blk = lambda o, j: o[:, j * width:(j + 1) * width]
    return [jnp.concatenate([blk(out_lo, 0), blk(out_lo, 1)], axis=0),
            jnp.concatenate([blk(out_lo, 2), blk(out_hi, 0)], axis=0),
            jnp.concatenate([blk(out_hi, 1), blk(out_hi, 2)], axis=0)]


def _stack_pair_heads(q):
    lo = _lane_lo(q.shape)
    return jnp.concatenate([jnp.where(lo, q, 0.0).astype(BF16),
                            jnp.where(lo, 0.0, q).astype(BF16)], axis=0)


def _with_shift_lane(lhs, kmax_lo, kmax_hi, bias_max_lo, bias_max_hi):
    half = lhs.shape[0] // 2
    lf = lhs.astype(F32)
    q_sq = jnp.sum(lf * lf, axis=-1, keepdims=True)
    q_norm = q_sq * lax.rsqrt(jnp.maximum(q_sq, 1e-30))
    row = lax.broadcasted_iota(jnp.int32, lhs.shape, 0)
    lane = lax.broadcasted_iota(jnp.int32, lhs.shape, 1)
    lo_row = row < half
    shift = (q_norm * jnp.where(lo_row, kmax_lo, kmax_hi)
             + jnp.where(lo_row, bias_max_lo, bias_max_hi))
    shift = shift + jnp.abs(shift) * SHIFT_MARGIN
    slot_lane = jnp.where(lo_row, HEAD_DIM, 0)
    return jnp.where(lane == slot_lane, -shift, lf).astype(BF16)


def _keys_with_ones(k_pair):
    lane = lax.broadcasted_iota(jnp.int32, k_pair.shape, 1)
    kf = k_pair.astype(F32)
    k_lo = jnp.where(lane < HEAD_DIM, kf, jnp.where(lane == HEAD_DIM, 1.0, 0.0))
    k_hi = jnp.where(lane >= HEAD_DIM, kf, jnp.where(lane == 0, 1.0, 0.0))
    return k_lo.astype(BF16), k_hi.astype(BF16)


def _shifted_scores(ka_lo, ka_hi, lhs_shifted):
    half = lhs_shifted.shape[0] // 2
    return jnp.concatenate([_nt_dot(ka_lo, lhs_shifted[:half]), _nt_dot(ka_hi, lhs_shifted[half:])],
                           axis=1)


def _pair_key_norm_max(k_pair):
    kf = k_pair.astype(F32)
    row = lax.broadcasted_iota(jnp.int32, (LANES, 2 * LANES), 0)
    col = lax.broadcasted_iota(jnp.int32, (LANES, 2 * LANES), 1)
    half_sel = jnp.where((row < HEAD_DIM) == (col < LANES), 1.0, 0.0).astype(BF16)
    sq = jnp.dot((kf * kf).astype(BF16), half_sel, preferred_element_type=F32)
    norm_max = jnp.sqrt(jnp.max(sq, axis=0, keepdims=True))
    return norm_max[:, :LANES], norm_max[:, LANES:]


def _window_attn_kernel(q_ref, k_ref, v_ref, bias_ref, sink_row_ref,
                        o_ref, kpad_scr, vtpad_scr, *, n_blocks, group):
    i = pl.program_id(1)
    half = 3 * BLOCK

    @pl.when(i == 0)
    def _():
        zero_blk = jnp.zeros((1, BLOCK, LANES), BF16)
        kpad_scr[0:1] = zero_blk
        kpad_scr[n_blocks + 1:n_blocks + 2] = zero_blk
        kpad_scr[1:n_blocks + 1] = k_ref[...].reshape(n_blocks, BLOCK, LANES)
        v_t = v_ref[...].astype(F32).T.astype(BF16)
        ones = jnp.ones((ONES_ROWS, BLOCK), BF16)
        zero_v = jnp.zeros((HEAD_DIM + ONES_ROWS, BLOCK), BF16)
        for g in range(2):
            vtpad_scr[g, 0] = zero_v
            vtpad_scr[g, n_blocks + 1] = zero_v
            for blk in range(n_blocks):
                vtpad_scr[g, blk + 1, :HEAD_DIM, :] = v_t[g * HEAD_DIM:(g + 1) * HEAD_DIM,
                                                           blk * BLOCK:(blk + 1) * BLOCK]
                vtpad_scr[g, blk + 1, HEAD_DIM:, :] = ones

    sink_row = sink_row_ref[...]

    for u in range(group):
        n = i * group + u
        qf = q_ref[u * BLOCK:(u + 1) * BLOCK, :].astype(F32) * (HEAD_DIM ** -0.5 * LOG2E)
        lhs = _stack_gqa_heads([qf[:, j * LANES:(j + 1) * LANES] for j in range(3)])
        k_win = kpad_scr[pl.ds(n, 3)].reshape(3 * BLOCK, LANES)
        s = _nt_dot(k_win, lhs) + bias_ref[...]
        s = jnp.concatenate([jnp.where(n > 0, s[:BLOCK], NEG_INF), s[BLOCK:2 * BLOCK],
                             jnp.where(n < n_blocks - 1, s[2 * BLOCK:], NEG_INF)], axis=0)
        m = jnp.maximum(jnp.max(s, axis=0, keepdims=True), sink_row)
        p = jnp.exp2(s - m).astype(BF16)
        p_sink = jnp.exp2(sink_row - m)
        outs = []
        for g in range(2):
            acc = None
            for j in range(3):
                pv = jnp.dot(vtpad_scr[g, n + j], p[j * BLOCK:(j + 1) * BLOCK, g * half:(g + 1) * half],
                             preferred_element_type=F32)
                acc = pv if acc is None else acc + pv
            total = acc[HEAD_DIM:HEAD_DIM + 1] + p_sink[:, g * half:(g + 1) * half]
            outs.append(acc[:HEAD_DIM] * (1.0 / total))
        for j, pair_t in enumerate(_unstack_gqa_heads(outs[0], outs[1], BLOCK)):
            o_ref[u * BLOCK:(u + 1) * BLOCK, j * LANES:(j + 1) * LANES] = pair_t.T.astype(o_ref.dtype)


def _window_attn(q, k, v, bias_win_t, sink, batch, seq, group):
    nb = seq // BLOCK
    steps = nb // group
    rows = A_HEADS * BLOCK
    sink_row = jnp.repeat(sink, BLOCK).reshape(1, rows)
    const = lambda shape: pl.BlockSpec(shape, lambda b, i: (0,) * len(shape))
    return pl.pallas_call(
        functools.partial(_window_attn_kernel, n_blocks=nb, group=group),
        grid=(batch, steps),
        in_specs=[pl.BlockSpec((group * BLOCK, 3 * LANES), lambda b, i: (b * steps + i, 0)),
                  pl.BlockSpec((seq, LANES), lambda b, i: (b, 0)),
                  pl.BlockSpec((seq, LANES), lambda b, i: (b, 0)),
                  const(bias_win_t.shape), const((1, rows))],
        out_specs=pl.BlockSpec((group * BLOCK, 3 * LANES), lambda b, i: (b * steps + i, 0)),
        out_shape=jax.ShapeDtypeStruct(q.shape, BF16),
        scratch_shapes=[pltpu.VMEM((nb + 2, BLOCK, LANES), BF16),
                        pltpu.VMEM((2, nb + 2, HEAD_DIM + ONES_ROWS, BLOCK), BF16)],
        compiler_params=_cparams(2),
        name="window_attn",
    )(q, k, v, bias_win_t, sink_row)


def _norm_rope_pair(x, g, cos, sin_signed):
    lo = _lane_lo(x.shape)
    x2 = x * x
    ss_lo = jnp.sum(jnp.where(lo, x2, 0.0), axis=-1, keepdims=True)
    ss_hi = jnp.sum(jnp.where(lo, 0.0, x2), axis=-1, keepdims=True)
    ms = jnp.where(lo, ss_lo, ss_hi) * (1.0 / HEAD_DIM)
    y = x * lax.rsqrt(ms + EPS) * g
    lane = lax.broadcasted_iota(jnp.int32, x.shape, 1)
    quarter = HEAD_DIM // 4
    first = (lane & quarter) == 0
    partner = jnp.where(first, pltpu.roll(y, LANES - quarter, 1), pltpu.roll(y, quarter, 1))
    return y * cos + partner * sin_signed


def _dense_attn_kernel(q_ref, k_ref, v_ref, o_ref, ka_scr, vt_scr, kmax_scr, *, tq):
    seq = k_ref.shape[0]
    kc = 2 * BLOCK
    n_chunks = seq // kc
    grp = 3 * tq

    ka_scr[0], ka_scr[1] = _keys_with_ones(k_ref[...])
    kmax_lo, kmax_hi = _pair_key_norm_max(k_ref[...])
    kmax_scr[0:1, :] = kmax_lo
    kmax_scr[1:2, :] = kmax_hi
    v_t = v_ref[...].astype(F32).T.astype(BF16)
    for g in range(2):
        vt_scr[g, :HEAD_DIM, :] = v_t[g * HEAD_DIM:(g + 1) * HEAD_DIM]
        vt_scr[g, HEAD_DIM:, :] = jnp.ones((ONES_ROWS, seq), BF16)

    def query_tile(i, carry):
        rows = pl.ds(pl.multiple_of(i * tq, tq), tq)
        lhs = _stack_gqa_heads([q_ref[rows, j * LANES:(j + 1) * LANES].astype(F32)
                                for j in range(3)])

        def write(acc):
            outs = [a[:HEAD_DIM] * (1.0 / a[HEAD_DIM:HEAD_DIM + 1]) for a in acc]
            for j, pair_t in enumerate(_unstack_gqa_heads(outs[0], outs[1], tq)):
                o_ref[rows, j * LANES:(j + 1) * LANES] = pair_t.T.astype(o_ref.dtype)

        s_all = _shifted_scores(ka_scr[0], ka_scr[1], _with_shift_lane(
            lhs, kmax_scr[0:1, :], kmax_scr[1:2, :], 0.0, 0.0))
        p = jnp.exp2(s_all).astype(BF16)
        acc = [jnp.dot(vt_scr[g], p[:, g * grp:(g + 1) * grp], preferred_element_type=F32)
               for g in range(2)]
        write(acc)
        sums = jnp.concatenate([a[HEAD_DIM:HEAD_DIM + 1] for a in acc], axis=0)

        @pl.when(jnp.logical_not(jnp.min(sums) >= SUM_FLOOR))
        def _():
            s_exact = _nt_dot(k_ref[...], lhs)
            m = None
            acc2 = [None, None]
            for c in range(n_chunks):
                s = s_exact[c * kc:(c + 1) * kc, :]
                m_c = jnp.max(s, axis=0, keepdims=True)
                m_new = m_c if m is None else jnp.maximum(m, m_c)
                pc = jnp.exp2(s - m_new).astype(BF16)
                if m is not None:
                    alpha = jnp.exp2(m - m_new)
                for g in range(2):
                    cols = slice(g * grp, (g + 1) * grp)
                    pv = jnp.dot(vt_scr[g, :, c * kc:(c + 1) * kc], pc[:, cols],
                                 preferred_element_type=F32)
                    acc2[g] = pv if m is None else acc2[g] * alpha[:, cols] + pv
                m = m_new
            write(acc2)

        return carry

    lax.fori_loop(0, seq // tq, query_tile, 0)


def _dense_attn(q, k, v, batch, seq, tq):
    return pl.pallas_call(
        functools.partial(_dense_attn_kernel, tq=tq),
        grid=(batch,),
        in_specs=[pl.BlockSpec((seq, 3 * LANES), lambda b: (b, 0)),
                  pl.BlockSpec((seq, LANES), lambda b: (b, 0)),
                  pl.BlockSpec((seq, LANES), lambda b: (b, 0))],
        out_specs=pl.BlockSpec((seq, 3 * LANES), lambda b: (b, 0)),
        out_shape=jax.ShapeDtypeStruct(q.shape, BF16),
        scratch_shapes=[pltpu.VMEM((2, seq, LANES), BF16),
                        pltpu.VMEM((2, HEAD_DIM + ONES_ROWS, seq), BF16),
                        pltpu.VMEM((2, LANES), F32)],
        compiler_params=_cparams(1),
        name="dense_attn",
    )(q, k, v)


def _cross_attn_kernel(q_ref, mk_ref, mv_ref, o_ref, vt_scr, *, tq):
    i = pl.program_id(1)
    n_pairs = q_ref.shape[1] // LANES
    mem_len = mk_ref.shape[0]

    @pl.when(i == 0)
    def _():
        v_t = mv_ref[...].astype(F32).T.astype(BF16)
        for h in range(2 * n_pairs):
            vt_scr[h, :HEAD_DIM, :] = v_t[h * HEAD_DIM:(h + 1) * HEAD_DIM]
            vt_scr[h, HEAD_DIM:, :] = jnp.ones((ONES_ROWS, mem_len), BF16)

    for j in range(n_pairs):
        cols = slice(j * LANES, (j + 1) * LANES)
        lhs = _stack_pair_heads(q_ref[:, cols].astype(F32) * (HEAD_DIM ** -0.5 * LOG2E))
        s = _nt_dot(mk_ref[:, cols], lhs)
        p = jnp.exp2(s - jnp.max(s, axis=0, keepdims=True)).astype(BF16)
        acc = [jnp.dot(vt_scr[2 * j + h], p[:, h * tq:(h + 1) * tq], preferred_element_type=F32)
               for h in range(2)]
        pair_t = jnp.concatenate([a[:HEAD_DIM] * (1.0 / a[HEAD_DIM:HEAD_DIM + 1]) for a in acc],
                                 axis=0)
        o_ref[:, cols] = pair_t.T.astype(o_ref.dtype)


def _cross_attn(q, mk, mv, batch, seq, tq):
    nq = seq // tq
    mem_len = mk.shape[0] // batch
    width = q.shape[1]
    n_pairs = width // LANES
    return pl.pallas_call(
        functools.partial(_cross_attn_kernel, tq=tq),
        grid=(batch, nq),
        in_specs=[pl.BlockSpec((tq, width), lambda b, i: (b * nq + i, 0)),
                  pl.BlockSpec((mem_len, width), lambda b, i: (b, 0)),
                  pl.BlockSpec((mem_len, width), lambda b, i: (b, 0))],
        out_specs=pl.BlockSpec((tq, width), lambda b, i: (b * nq + i, 0)),
        out_shape=jax.ShapeDtypeStruct(q.shape, BF16),
        scratch_shapes=[pltpu.VMEM((2 * n_pairs, HEAD_DIM + ONES_ROWS, mem_len), BF16)],
        compiler_params=_cparams(2),
        name="cross_attn",
    )(q, mk, mv)


def _diff_attn_kernel(q1_ref, q2_ref, k1_ref, k2_ref, v_ref, bias_ref, bmax_ref, lam_ref, g_ref,
                      o_ref, vt_scr, ka_scr, kmax_scr, *, tq, lam_init):
    sub = tq // BLOCK
    seq = k1_ref.shape[0]
    kc = 2 * BLOCK
    n_chunks = seq // kc
    k_refs = (k1_ref, k2_ref)

    for head in range(2):
        v_t = v_ref[:, head * C_V_DIM:(head + 1) * C_V_DIM].astype(F32).T
        vt_scr[head, :C_V_DIM, :] = v_t.astype(BF16)
        vt_scr[head, C_V_DIM:, :] = jnp.ones((ONES_ROWS, seq), BF16)
    for t in range(2):
        ka_scr[t, 0], ka_scr[t, 1] = _keys_with_ones(k_refs[t][...])
        kmax_lo, kmax_hi = _pair_key_norm_max(k_refs[t][...])
        kmax_scr[t, 0:1, :] = kmax_lo
        kmax_scr[t, 1:2, :] = kmax_hi

    lam_vec = lam_ref[...]
    lam = (jnp.exp(jnp.sum(lam_vec[0:1] * lam_vec[1:2], axis=-1, keepdims=True))
           - jnp.exp(jnp.sum(lam_vec[2:3] * lam_vec[3:4], axis=-1, keepdims=True)) + lam_init)

    def query_tile(i, carry):
        rows = pl.ds(pl.multiple_of(i * tq, tq), tq)
        lhs = [_stack_pair_heads(q_ref[rows, :].astype(F32) * (HEAD_DIM ** -0.5 * LOG2E))
               for q_ref in (q1_ref, q2_ref)]
        _diff_attn_tile(i, rows, lhs, lam, bias_ref, bmax_ref, g_ref, o_ref, vt_scr, ka_scr,
                        kmax_scr, k_refs, tq=tq, sub=sub, kc=kc, n_chunks=n_chunks,
                        lam_init=lam_init)
        return carry

    lax.fori_loop(0, seq // tq, query_tile, 0)


def _diff_attn_tile(i, rows, lhs, lam, bias_ref, bmax_ref, g_ref, o_ref, vt_scr, ka_scr, kmax_scr,
                    k_refs, *, tq, sub, kc, n_chunks, lam_init):
    def bias_chunk(c):
        def tile(head, u, e):
            offset = (2 * c + e) - (i * sub + u)
            return bias_ref[head, jnp.clip(offset, -BIAS_FAR, BIAS_FAR) + BIAS_FAR]
        return jnp.concatenate(
            [jnp.concatenate([tile(head, u, e) for head in range(2) for u in range(sub)], axis=1)
             for e in range(2)], axis=0)

    def shifted_scores(t):
        return _shifted_scores(ka_scr[t, 0], ka_scr[t, 1], _with_shift_lane(
            lhs[t], kmax_scr[t, 0:1, :], kmax_scr[t, 1:2, :], bmax_ref[0], bmax_ref[1]))

    def pv_of_shifted(s_all):
        p = jnp.concatenate([jnp.exp2(s_all[c * kc:(c + 1) * kc, :] + bias_chunk(c)).astype(BF16)
                             for c in range(n_chunks)], axis=0)
        return [jnp.dot(vt_scr[head], p[:, head * tq:(head + 1) * tq],
                        preferred_element_type=F32) for head in range(2)]

    def pv_running_max(s_all):
        m = None
        acc = [None, None]
        for c in range(n_chunks):
            s = s_all[c * kc:(c + 1) * kc, :] + bias_chunk(c)
            m_c = jnp.max(s, axis=0, keepdims=True)
            m_new = m_c if m is None else jnp.maximum(m, m_c)
            p = jnp.exp2(s - m_new).astype(BF16)
            if m is not None:
                alpha = jnp.exp2(m - m_new)
            for head in range(2):
                cols = slice(head * tq, (head + 1) * tq)
                pv = jnp.dot(vt_scr[head, :, c * kc:(c + 1) * kc], p[:, cols],
                             preferred_element_type=F32)
                acc[head] = pv if m is None else acc[head] * alpha[:, cols] + pv
            m = m_new
        return acc

    def write(acc1, acc2):
        g = g_ref[...] * (1.0 - lam_init)
        for head in range(2):
            o1 = acc1[head][:C_V_DIM] * (1.0 / acc1[head][C_V_DIM:C_V_DIM + 1])
            o2 = acc2[head][:C_V_DIM] * (1.0 / acc2[head][C_V_DIM:C_V_DIM + 1])
            out = o1 - lam * o2
            ms = jnp.mean(out * out, axis=0, keepdims=True)
            y = out * lax.rsqrt(ms + EPS) * g
            o_ref[rows, head * C_V_DIM:(head + 1) * C_V_DIM] = y.T.astype(o_ref.dtype)

    s1 = shifted_scores(0)
    s2 = shifted_scores(1)
    acc1 = pv_of_shifted(s1)
    acc2 = pv_of_shifted(s2)
    write(acc1, acc2)
    sums = jnp.concatenate([a[C_V_DIM:C_V_DIM + 1] for a in acc1 + acc2], axis=0)

    @pl.when(jnp.logical_not(jnp.min(sums) >= SUM_FLOOR))
    def _():
        write(pv_running_max(_nt_dot(k_refs[0][...], lhs[0])),
              pv_running_max(_nt_dot(k_refs[1][...], lhs[1])))


def _diff_attn(q1, q2, k1, k2, v, bias_t, bias_max, lam_vecs, subln_g, lam_init, batch, seq, tq):
    n_pairs = C_HEADS // 2
    n_tiles = bias_t.shape[1]
    seq_spec = pl.BlockSpec((seq, LANES), lambda b, p: (b, p))
    g_cols = jnp.broadcast_to(subln_g.reshape(C_V_DIM, 1), (C_V_DIM, tq))
    return pl.pallas_call(
        functools.partial(_diff_attn_kernel, tq=tq, lam_init=lam_init),
        grid=(batch, n_pairs),
        in_specs=[seq_spec, seq_spec, seq_spec, seq_spec,
                  pl.BlockSpec((seq, 2 * C_V_DIM), lambda b, p: (b, p)),
                  pl.BlockSpec((2, n_tiles, BLOCK, BLOCK), lambda b, p: (p, 0, 0, 0)),
                  pl.BlockSpec((2, 1, LANES), lambda b, p: (p, 0, 0)),
                  pl.BlockSpec(lam_vecs.shape, lambda b, p: (0, 0)),
                  pl.BlockSpec((C_V_DIM, tq), lambda b, p: (0, 0))],
        out_specs=pl.BlockSpec((seq, 2 * C_V_DIM), lambda b, p: (b, p)),
        out_shape=jax.ShapeDtypeStruct(v.shape, BF16),
        scratch_shapes=[pltpu.VMEM((2, C_V_DIM + ONES_ROWS, seq), BF16),
                        pltpu.VMEM((2, 2, seq, LANES), BF16),
                        pltpu.VMEM((2, 2, LANES), F32)],
        compiler_params=_cparams(2),
        name="diff_attn",
    )(q1, q2, k1, k2, v, bias_t, bias_max, lam_vecs, g_cols)


def _rope_tables(seq):
    rows = seq // GRID_W
    row = jnp.broadcast_to(jnp.arange(rows)[:, None], (rows, GRID_W)).reshape(-1)
    col = jnp.broadcast_to(jnp.arange(GRID_W)[None, :], (rows, GRID_W)).reshape(-1)
    half = HEAD_DIM // 2
    inv = 1.0 / (ROPE_THETA ** (jnp.arange(0, half, 2, dtype=F32) / half))
    ang_row = row.astype(F32)[:, None] * inv
    ang_col = col.astype(F32)[:, None] * inv
    cos = jnp.concatenate([jnp.cos(ang_row)] * 2 + [jnp.cos(ang_col)] * 2, axis=-1)
    sin = jnp.concatenate([-jnp.sin(ang_row), jnp.sin(ang_row),
                           -jnp.sin(ang_col), jnp.sin(ang_col)], axis=-1)
    return jnp.tile(cos, (1, 2)), jnp.tile(sin, (1, 2))


def kernel(x, mem, rel_bias, mem_norm, final_norm, even_norm, even_w_in, even_sink, even_q_norm, even_k_norm, even_w_mem_kv, even_w_out, odd_norm, odd_w_in, odd_lambda_q1, odd_lambda_k1, odd_lambda_q2, odd_lambda_k2, odd_subln, odd_w_mem_kv, odd_w_out):
    batch, seq, d = x.shape
    mem_len = mem.shape[1]
    tokens = batch * seq
    xw = X_HEADS * HEAD_DIM
    gw = A_HEADS * HEAD_DIM

    w_in0 = even_w_in[0].astype(BF16)
    w_out0 = even_w_out[0].astype(BF16)
    w_in1 = odd_w_in[0].astype(BF16)
    w_out1 = odd_w_out[0].astype(BF16)

    x2 = x.reshape(tokens, d)
    mem2 = mem.reshape(batch * mem_len, d)

    bias_near, bias_win = _bias_tiles(rel_bias)
    cos, sin_signed = _rope_tables(seq)
    head_bias_max = jnp.max(rel_bias, axis=0) * LOG2E

    w_mem = jnp.concatenate([even_w_mem_kv[0], odd_w_mem_kv[0]], axis=1).astype(BF16)
    mk0, mv0, mk1, mv1 = _norm_proj(mem2, mem_norm, w_mem, (xw,) * 4, TM_MEM, gate_last=False)

    rope_args = (cos, sin_signed, jnp.tile(even_q_norm[0], 2).reshape(1, LANES),
                 jnp.tile(even_k_norm[0], 2).reshape(1, LANES))
    aq, ak, av, bq, bk, bv, xq, gate = _norm_proj(
        x2, even_norm[0], w_in0, (gw, 128, 128, gw, 128, 128, xw, D_MODEL), TM_PROJ,
        gate_last=True, qk_rope=(3, 4), rope_args=rope_args, seq=seq)
    y_a = _window_attn(aq, ak, av, bias_win, even_sink[0] * LOG2E, batch, seq, WINDOW_GROUP)
    y_b = _dense_attn(bq, bk, bv, batch, seq, TQ_DENSE)
    y_x = _cross_attn(xq, mk0, mv0, batch, seq, TQ_CROSS)
    lam_init = 0.8 - 0.6 * math.exp(-0.3 * 1)
    cw = C_HEADS * HEAD_DIM
    h1, q1, q2, k1, k2, v, xq1, gate1 = _gate_out(
        [y_a, y_b, y_x], gate, w_out0, x2, odd_norm[0], TM_PROJ, w_next=w_in1,
        splits=(cw, cw, cw, cw, C_HEADS * C_V_DIM, xw, D_MODEL))
    lam_vecs = jnp.stack([odd_lambda_q1[0], odd_lambda_k1[0], odd_lambda_q2[0], odd_lambda_k2[0]])
    bias_max = jnp.broadcast_to(head_bias_max.reshape(C_HEADS, 1, 1), (C_HEADS, 1, LANES))
    y_c = _diff_attn(q1, q2, k1, k2, v, bias_near, bias_max, lam_vecs, odd_subln[0], lam_init,
                     batch, seq, TQ_DIFF)
    y_x1 = _cross_attn(xq1, mk1, mv1, batch, seq, TQ_CROSS)
    (out,) = _gate_out([y_c, y_x1], gate1, w_out1, h1, final_norm, TM_PROJ)
    return out.reshape(batch, seq, d)
```

```python
import functools
import math

import numpy as np
import jax
import jax.numpy as jnp
from jax import lax
from jax.experimental import pallas as pl
from jax.experimental.pallas import tpu as pltpu

D_MODEL = 1024
HEAD_DIM = 64
BLOCK = 128
WINDOW = 128
GRID_W = 64
A_HEADS = 6
B_HEADS = 6
C_HEADS = 6
C_V_DIM = 128
X_HEADS = 4
REL_BUCKETS = 32
REL_MAX_DIST = 128
ROPE_THETA = 10000.0
EPS = 1e-6
NEG_INF = -1e30
LOG2E = math.log2(math.e)
LANES = 128
ONES_ROWS = 16
PROJ_SUB_ROWS = 512
TM_PROJ = 1024
TM_MEM = 256
TQ_DENSE = 256
TQ_DIFF = 512
TQ_CROSS = 2048
WINDOW_GROUP = 16
SUM_FLOOR = 2.0 ** -64
SHIFT_MARGIN = 2.0 ** -7
BIAS_FAR = 2
VMEM_LIMIT = 56 * 1024 * 1024

F32 = jnp.float32
BF16 = jnp.bfloat16


def _cparams(n_axes, flags=None):
    return pltpu.CompilerParams(dimension_semantics=("arbitrary",) * n_axes,
                                vmem_limit_bytes=VMEM_LIMIT, flags=flags)


def _lane_lo(shape):
    return lax.broadcasted_iota(jnp.int32, shape, len(shape) - 1) < HEAD_DIM


def _rmsnorm_rows(x, g):
    ms = jnp.mean(x * x, axis=-1, keepdims=True)
    return x * lax.rsqrt(ms + EPS) * g


def _nt_dot(a, b):
    return lax.dot_general(a, b, (((1,), (1,)), ((), ())), preferred_element_type=F32)


def _norm_proj_kernel(*refs, qk_rope, gate_last):
    x_ref, g_ref, w_ref = refs[:3]
    if qk_rope is None:
        out_refs = refs[3:]
    else:
        cos_ref, sin_ref, gq_ref, gk_ref = refs[3:7]
        out_refs = refs[7:]
    tm = x_ref.shape[0]
    for r0 in range(0, tm, PROJ_SUB_ROWS):
        rows = slice(r0, min(r0 + PROJ_SUB_ROWS, tm))
        xn = _rmsnorm_rows(x_ref[rows, :], g_ref[...]).astype(BF16)
        y_all = jnp.dot(xn, w_ref[...], preferred_element_type=F32)
        c0 = 0
        for j, o_ref in enumerate(out_refs):
            width = o_ref.shape[1]
            y = y_all[:, c0:c0 + width]
            if qk_rope is not None and j in qk_rope:
                gain, scale = ((gq_ref, HEAD_DIM ** -0.5 * LOG2E) if j == qk_rope[0]
                               else (gk_ref, 1.0))
                y = jnp.concatenate(
                    [_norm_rope_pair(y[:, t * LANES:(t + 1) * LANES], gain[...],
                                     cos_ref[rows, :], sin_ref[rows, :]) * scale
                     for t in range(width // LANES)], axis=1)
            if gate_last and j == len(out_refs) - 1:
                y = _silu(y)
            o_ref[rows, :] = y.astype(o_ref.dtype)
            c0 += width


def _silu(y):
    return y * (1.0 / (1.0 + jnp.exp(-y)))


def _norm_proj(x, g, w_bf16, splits, tm, gate_last, qk_rope=None, rope_args=(), seq=None):
    rows, d = x.shape
    n = w_bf16.shape[1]
    assert sum(splits) == n and rows % tm == 0
    in_specs = [pl.BlockSpec((tm, d), lambda i: (i, 0)),
                pl.BlockSpec((1, d), lambda i: (0, 0)),
                pl.BlockSpec((d, n), lambda i: (0, 0))]
    if qk_rope is not None:
        pos_blocks = seq // tm
        in_specs += [pl.BlockSpec((tm, LANES), lambda i: (i % pos_blocks, 0)),
                     pl.BlockSpec((tm, LANES), lambda i: (i % pos_blocks, 0)),
                     pl.BlockSpec((1, LANES), lambda i: (0, 0)),
                     pl.BlockSpec((1, LANES), lambda i: (0, 0))]
    return pl.pallas_call(
        functools.partial(_norm_proj_kernel, qk_rope=qk_rope, gate_last=gate_last),
        grid=(rows // tm,),
        in_specs=in_specs,
        out_specs=[pl.BlockSpec((tm, s), lambda i: (i, 0)) for s in splits],
        out_shape=[jax.ShapeDtypeStruct((rows, s), BF16) for s in splits],
        compiler_params=_cparams(1),
        name="norm_proj",
    )(x, g.reshape(1, d), w_bf16, *rope_args)


def _gate_out_kernel(*refs, n_parts, after):
    y_refs = refs[:n_parts]
    gate_ref, w_ref, res_ref, g_ref = refs[n_parts:n_parts + 4]
    if after == "proj":
        w_next_ref, h_ref = refs[n_parts + 4:n_parts + 6]
        out_refs = refs[n_parts + 6:]
    else:
        h_ref = refs[n_parts + 4]
    tm = res_ref.shape[0]
    for r0 in range(0, tm, PROJ_SUB_ROWS):
        rows = slice(r0, min(r0 + PROJ_SUB_ROWS, tm))
        parts = []
        c0 = 0
        for y_ref in y_refs:
            width = y_ref.shape[1]
            parts.append(y_ref[rows, :] * gate_ref[rows, c0:c0 + width])
            c0 += width
        h = res_ref[rows, :] + jnp.dot(jnp.concatenate(parts, axis=1), w_ref[...],
                                       preferred_element_type=F32)
        if after == "norm":
            h_ref[rows, :] = _rmsnorm_rows(h, g_ref[...])
            continue
        h_ref[rows, :] = h
        y_all = jnp.dot(_rmsnorm_rows(h, g_ref[...]).astype(BF16), w_next_ref[...],
                        preferred_element_type=F32)
        c0 = 0
        for j, o_ref in enumerate(out_refs):
            width = o_ref.shape[1]
            y = y_all[:, c0:c0 + width]
            if j == len(out_refs) - 1:
                y = _silu(y)
            o_ref[rows, :] = y.astype(o_ref.dtype)
            c0 += width


def _gate_out(y_parts, gate, w_bf16, resid, norm_g, tm, w_next=None, splits=()):
    rows, d = resid.shape
    mix = w_bf16.shape[0]
    after = "norm" if w_next is None else "proj"
    row_block = lambda width: pl.BlockSpec((tm, width), lambda i: (i, 0))
    whole = lambda a: pl.BlockSpec(a.shape, lambda i: (0, 0))
    in_specs = [row_block(y.shape[1]) for y in y_parts]
    in_specs += [row_block(mix), whole(w_bf16), row_block(d), pl.BlockSpec((1, d), lambda i: (0, 0))]
    args = list(y_parts) + [gate, w_bf16, resid, norm_g.reshape(1, d)]
    out_specs = [row_block(d)]
    out_shape = [jax.ShapeDtypeStruct((rows, d), F32)]
    if after == "proj":
        assert sum(splits) == w_next.shape[1]
        in_specs.append(whole(w_next))
        args.append(w_next)
        out_specs += [row_block(s) for s in splits]
        out_shape += [jax.ShapeDtypeStruct((rows, s), BF16) for s in splits]
    return pl.pallas_call(
        functools.partial(_gate_out_kernel, n_parts=len(y_parts), after=after),
        grid=(rows // tm,),
        in_specs=in_specs,
        out_specs=out_specs,
        out_shape=out_shape,
        compiler_params=_cparams(1),
        name="gate_out",
    )(*args)


def _bucket_thresholds():
    nb = REL_BUCKETS // 2
    max_exact = nb // 2
    n = np.arange(0, 4 * REL_MAX_DIST)
    nf = np.maximum(n, 1).astype(np.float32)
    large = max_exact + (np.log(nf / np.float32(max_exact))
                         / np.float32(math.log(REL_MAX_DIST / max_exact))
                         * np.float32(nb - max_exact)).astype(np.int32)
    bucket = np.where(n < max_exact, n, np.minimum(large, nb - 1))
    assert np.all(np.diff(bucket) >= 0) and bucket[0] == 0 and bucket[-1] == nb - 1
    thr = [int(np.argmax(bucket >= k)) for k in range(1, nb)]
    assert thr[-1] < REL_MAX_DIST
    return thr


def _bias_tiles_kernel(tab_ref, near_ref, win_ref):
    h = pl.program_id(0)
    thr = _bucket_thresholds()
    nb = REL_BUCKETS // 2
    row = lax.broadcasted_iota(jnp.int32, (BLOCK, BLOCK), 0)
    col = lax.broadcasted_iota(jnp.int32, (BLOCK, BLOCK), 1)
    def tile_of(rel):
        n = jnp.abs(rel)
        neg = jnp.full((BLOCK, BLOCK), tab_ref[0, h], F32)
        pos = jnp.full((BLOCK, BLOCK), tab_ref[nb, h], F32)
        for b in range(1, nb):
            ge = n >= thr[b - 1]
            neg = jnp.where(ge, tab_ref[b, h], neg)
            pos = jnp.where(ge, tab_ref[nb + b, h], pos)
        return jnp.where(rel > 0, pos, neg) * LOG2E

    for k in range(2 * BIAS_FAR + 1):
        rel = (k - BIAS_FAR) * BLOCK + row - col
        tile = tile_of(rel)
        near_ref[0, k] = tile
        if abs(k - BIAS_FAR) <= 1:
            j = k - BIAS_FAR + 1
            win_ref[j * BLOCK:(j + 1) * BLOCK, :] = jnp.where(jnp.abs(rel) <= WINDOW, tile, NEG_INF)


def _bias_tiles(rel_bias):
    assert _bucket_thresholds()[-1] <= (BIAS_FAR - 1) * BLOCK + 1
    n_tiles = 2 * BIAS_FAR + 1
    n_heads = rel_bias.shape[1]
    return pl.pallas_call(
        _bias_tiles_kernel,
        grid=(n_heads,),
        in_specs=[pl.BlockSpec(memory_space=pltpu.SMEM)],
        out_specs=[pl.BlockSpec((1, n_tiles, BLOCK, BLOCK), lambda h: (h, 0, 0, 0)),
                   pl.BlockSpec((3 * BLOCK, BLOCK), lambda h: (0, h))],
        out_shape=[jax.ShapeDtypeStruct((n_heads, n_tiles, BLOCK, BLOCK), F32),
                   jax.ShapeDtypeStruct((3 * BLOCK, n_heads * BLOCK), F32)],
        compiler_params=_cparams(1),
        name="bias_tiles",
    )(rel_bias)


def _stack_gqa_heads(q_tiles):
    lo = _lane_lo(q_tiles[0].shape)
    swap = lambda q: pltpu.roll(q, HEAD_DIM, 1)
    heads_lo = [q_tiles[0], swap(q_tiles[0]), q_tiles[1]]
    heads_hi = [q_tiles[1], swap(q_tiles[2]), q_tiles[2]]
    return jnp.concatenate([jnp.where(lo, q, 0.0).astype(BF16) for q in heads_lo]
                           + [jnp.where(lo, 0.0, q).astype(BF16) for q in heads_hi], axis=0)


def _unstack_gqa_heads(out_lo, out_hi, width):
    blk = lambda o, j: o[:, j * width:(j + 1) * width]
    return [jnp.concatenate([blk(out_lo, 0), blk(out_lo, 1)], axis=0),
            jnp.concatenate([blk(out_lo, 2), blk(out_hi, 0)], axis=0),
            jnp.concatenate([blk(out_hi, 1), blk(out_hi, 2)], axis=0)]


def _stack_pair_heads(q):
    lo = _lane_lo(q.shape)
    return jnp.concatenate([jnp.where(lo, q, 0.0).astype(BF16),
                            jnp.where(lo, 0.0, q).astype(BF16)], axis=0)


def _with_shift_lane(lhs, kmax_lo, kmax_hi, bias_max_lo, bias_max_hi):
    half = lhs.shape[0] // 2
    lf = lhs.astype(F32)
    q_sq = jnp.sum(lf * lf, axis=-1, keepdims=True)
    q_norm = q_sq * lax.rsqrt(jnp.maximum(q_sq, 1e-30))
    row = lax.broadcasted_iota(jnp.int32, lhs.shape, 0)
    lane = lax.broadcasted_iota(jnp.int32, lhs.shape, 1)
    lo_row = row < half
    shift = (q_norm * jnp.where(lo_row, kmax_lo, kmax_hi)
             + jnp.where(lo_row, bias_max_lo, bias_max_hi))
    shift = shift + jnp.abs(shift) * SHIFT_MARGIN
    slot_lane = jnp.where(lo_row, HEAD_DIM, 0)
    return jnp.where(lane == slot_lane, -shift, lf).astype(BF16)


def _keys_with_ones(k_pair):
    lane = lax.broadcasted_iota(jnp.int32, k_pair.shape, 1)
    kf = k_pair.astype(F32)
    k_lo = jnp.where(lane < HEAD_DIM, kf, jnp.where(lane == HEAD_DIM, 1.0, 0.0))
    k_hi = jnp.where(lane >= HEAD_DIM, kf, jnp.where(lane == 0, 1.0, 0.0))
    return k_lo.astype(BF16), k_hi.astype(BF16)


def _shifted_scores(ka_lo, ka_hi, lhs_shifted):
    half = lhs_shifted.shape[0] // 2
    return jnp.concatenate([_nt_dot(ka_lo, lhs_shifted[:half]), _nt_dot(ka_hi, lhs_shifted[half:])],
                           axis=1)


def _pair_key_norm_max(k_pair):
    kf = k_pair.astype(F32)
    row = lax.broadcasted_iota(jnp.int32, (LANES, 2 * LANES), 0)
    col = lax.broadcasted_iota(jnp.int32, (LANES, 2 * LANES), 1)
    half_sel = jnp.where((row < HEAD_DIM) == (col < LANES), 1.0, 0.0).astype(BF16)
    sq = jnp.dot((kf * kf).astype(BF16), half_sel, preferred_element_type=F32)
    norm_max = jnp.sqrt(jnp.max(sq, axis=0, keepdims=True))
    return norm_max[:, :LANES], norm_max[:, LANES:]


def _window_attn_kernel(q_ref, k_ref, v_ref, bias_ref, sink_row_ref,
                        o_ref, kpad_scr, vtpad_scr, *, n_blocks, group):
    i = pl.program_id(1)
    half = 3 * BLOCK

    @pl.when(i == 0)
    def _():
        zero_blk = jnp.zeros((1, BLOCK, LANES), BF16)
        kpad_scr[0:1] = zero_blk
        kpad_scr[n_blocks + 1:n_blocks + 2] = zero_blk
        kpad_scr[1:n_blocks + 1] = k_ref[...].reshape(n_blocks, BLOCK, LANES)
        v_t = v_ref[...].astype(F32).T.astype(BF16)
        ones = jnp.ones((ONES_ROWS, BLOCK), BF16)
        zero_v = jnp.zeros((HEAD_DIM + ONES_ROWS, BLOCK), BF16)
        for g in range(2):
            vtpad_scr[g, 0] = zero_v
            vtpad_scr[g, n_blocks + 1] = zero_v
            for blk in range(n_blocks):
                vtpad_scr[g, blk + 1, :HEAD_DIM, :] = v_t[g * HEAD_DIM:(g + 1) * HEAD_DIM,
                                                           blk * BLOCK:(blk + 1) * BLOCK]
                vtpad_scr[g, blk + 1, HEAD_DIM:, :] = ones

    sink_row = sink_row_ref[...]

    for u in range(group):
        n = i * group + u
        qf = q_ref[u * BLOCK:(u + 1) * BLOCK, :].astype(F32) * (HEAD_DIM ** -0.5 * LOG2E)
        lhs = _stack_gqa_heads([qf[:, j * LANES:(j + 1) * LANES] for j in range(3)])
        k_win = kpad_scr[pl.ds(n, 3)].reshape(3 * BLOCK, LANES)
        s = _nt_dot(k_win, lhs) + bias_ref[...]
        s = jnp.concatenate([jnp.where(n > 0, s[:BLOCK], NEG_INF), s[BLOCK:2 * BLOCK],
                             jnp.where(n < n_blocks - 1, s[2 * BLOCK:], NEG_INF)], axis=0)
        m = jnp.maximum(jnp.max(s, axis=0, keepdims=True), sink_row)
        p = jnp.exp2(s - m).astype(BF16)
        p_sink = jnp.exp2(sink_row - m)
        outs = []
        for g in range(2):
            acc = None
            for j in range(3):
                pv = jnp.dot(vtpad_scr[g, n + j], p[j * BLOCK:(j + 1) * BLOCK, g * half:(g + 1) * half],
                             preferred_element_type=F32)
                acc = pv if acc is None else acc + pv
            total = acc[HEAD_DIM:HEAD_DIM + 1] + p_sink[:, g * half:(g + 1) * half]
            outs.append(acc[:HEAD_DIM] * (1.0 / total))
        for j, pair_t in enumerate(_unstack_gqa_heads(outs[0], outs[1], BLOCK)):
            o_ref[u * BLOCK:(u + 1) * BLOCK, j * LANES:(j + 1) * LANES] = pair_t.T.astype(o_ref.dtype)


def _window_attn(q, k, v, bias_win_t, sink, batch, seq, group):
    nb = seq // BLOCK
    steps = nb // group
    rows = A_HEADS * BLOCK
    sink_row = jnp.repeat(sink, BLOCK).reshape(1, rows)
    const = lambda shape: pl.BlockSpec(shape, lambda b, i: (0,) * len(shape))
    return pl.pallas_call(
        functools.partial(_window_attn_kernel, n_blocks=nb, group=group),
        grid=(batch, steps),
        in_specs=[pl.BlockSpec((group * BLOCK, 3 * LANES), lambda b, i: (b * steps + i, 0)),
                  pl.BlockSpec((seq, LANES), lambda b, i: (b, 0)),
                  pl.BlockSpec((seq, LANES), lambda b, i: (b, 0)),
                  const(bias_win_t.shape), const((1, rows))],
        out_specs=pl.BlockSpec((group * BLOCK, 3 * LANES), lambda b, i: (b * steps + i, 0)),
        out_shape=jax.ShapeDtypeStruct(q.shape, BF16),
        scratch_shapes=[pltpu.VMEM((nb + 2, BLOCK, LANES), BF16),
                        pltpu.VMEM((2, nb + 2, HEAD_DIM + ONES_ROWS, BLOCK), BF16)],
        compiler_params=_cparams(2),
        name="window_attn",
    )(q, k, v, bias_win_t, sink_row)


def _norm_rope_pair(x, g, cos, sin_signed):
    lo = _lane_lo(x.shape)
    x2 = x * x
    ss_lo = jnp.sum(jnp.where(lo, x2, 0.0), axis=-1, keepdims=True)
    ss_hi = jnp.sum(jnp.where(lo, 0.0, x2), axis=-1, keepdims=True)
    ms = jnp.where(lo, ss_lo, ss_hi) * (1.0 / HEAD_DIM)
    y = x * lax.rsqrt(ms + EPS) * g
    lane = lax.broadcasted_iota(jnp.int32, x.shape, 1)
    quarter = HEAD_DIM // 4
    first = (lane & quarter) == 0
    partner = jnp.where(first, pltpu.roll(y, LANES - quarter, 1), pltpu.roll(y, quarter, 1))
    return y * cos + partner * sin_signed


def _dense_attn_kernel(q_ref, k_ref, v_ref, o_ref, ka_scr, vt_scr, kmax_scr, *, tq):
    seq = k_ref.shape[0]
    kc = 2 * BLOCK
    n_chunks = seq // kc
    grp = 3 * tq

    ka_scr[0], ka_scr[1] = _keys_with_ones(k_ref[...])
    kmax_lo, kmax_hi = _pair_key_norm_max(k_ref[...])
    kmax_scr[0:1, :] = kmax_lo
    kmax_scr[1:2, :] = kmax_hi
    v_t = v_ref[...].astype(F32).T.astype(BF16)
    for g in range(2):
        vt_scr[g, :HEAD_DIM, :] = v_t[g * HEAD_DIM:(g + 1) * HEAD_DIM]
        vt_scr[g, HEAD_DIM:, :] = jnp.ones((ONES_ROWS, seq), BF16)

    def query_tile(i, carry):
        rows = pl.ds(pl.multiple_of(i * tq, tq), tq)
        lhs = _stack_gqa_heads([q_ref[rows, j * LANES:(j + 1) * LANES].astype(F32)
                                for j in range(3)])

        def write(acc):
            outs = [a[:HEAD_DIM] * (1.0 / a[HEAD_DIM:HEAD_DIM + 1]) for a in acc]
            for j, pair_t in enumerate(_unstack_gqa_heads(outs[0], outs[1], tq)):
                o_ref[rows, j * LANES:(j + 1) * LANES] = pair_t.T.astype(o_ref.dtype)

        s_all = _shifted_scores(ka_scr[0], ka_scr[1], _with_shift_lane(
            lhs, kmax_scr[0:1, :], kmax_scr[1:2, :], 0.0, 0.0))
        p = jnp.exp2(s_all).astype(BF16)
        acc = [jnp.dot(vt_scr[g], p[:, g * grp:(g + 1) * grp], preferred_element_type=F32)
               for g in range(2)]
        write(acc)
        sums = jnp.concatenate([a[HEAD_DIM:HEAD_DIM + 1] for a in acc], axis=0)

        @pl.when(jnp.logical_not(jnp.min(sums) >= SUM_FLOOR))
        def _():
            s_exact = _nt_dot(k_ref[...], lhs)
            m = None
            acc2 = [None, None]
            for c in range(n_chunks):
                s = s_exact[c * kc:(c + 1) * kc, :]
                m_c = jnp.max(s, axis=0, keepdims=True)
                m_new = m_c if m is None else jnp.maximum(m, m_c)
                pc = jnp.exp2(s - m_new).astype(BF16)
                if m is not None:
                    alpha = jnp.exp2(m - m_new)
                for g in range(2):
                    cols = slice(g * grp, (g + 1) * grp)
                    pv = jnp.dot(vt_scr[g, :, c * kc:(c + 1) * kc], pc[:, cols],
                                 preferred_element_type=F32)
                    acc2[g] = pv if m is None else acc2[g] * alpha[:, cols] + pv
                m = m_new
            write(acc2)

        return carry

    lax.fori_loop(0, seq // tq, query_tile, 0)


def _dense_attn(q, k, v, batch, seq, tq):
    return pl.pallas_call(
        functools.partial(_dense_attn_kernel, tq=tq),
        grid=(batch,),
        in_specs=[pl.BlockSpec((seq, 3 * LANES), lambda b: (b, 0)),
                  pl.BlockSpec((seq, LANES), lambda b: (b, 0)),
                  pl.BlockSpec((seq, LANES), lambda b: (b, 0))],
        out_specs=pl.BlockSpec((seq, 3 * LANES), lambda b: (b, 0)),
        out_shape=jax.ShapeDtypeStruct(q.shape, BF16),
        scratch_shapes=[pltpu.VMEM((2, seq, LANES), BF16),
                        pltpu.VMEM((2, HEAD_DIM + ONES_ROWS, seq), BF16),
                        pltpu.VMEM((2, LANES), F32)],
        compiler_params=_cparams(1),
        name="dense_attn",
    )(q, k, v)


def _cross_attn_kernel(q_ref, mk_ref, mv_ref, o_ref, vt_scr, *, tq):
    i = pl.program_id(1)
    n_pairs = q_ref.shape[1] // LANES
    mem_len = mk_ref.shape[0]

    @pl.when(i == 0)
    def _():
        v_t = mv_ref[...].astype(F32).T.astype(BF16)
        for h in range(2 * n_pairs):
            vt_scr[h, :HEAD_DIM, :] = v_t[h * HEAD_DIM:(h + 1) * HEAD_DIM]
            vt_scr[h, HEAD_DIM:, :] = jnp.ones((ONES_ROWS, mem_len), BF16)

    for j in range(n_pairs):
        cols = slice(j * LANES, (j + 1) * LANES)
        lhs = _stack_pair_heads(q_ref[:, cols].astype(F32) * (HEAD_DIM ** -0.5 * LOG2E))
        s = _nt_dot(mk_ref[:, cols], lhs)
        p = jnp.exp2(s - jnp.max(s, axis=0, keepdims=True)).astype(BF16)
        acc = [jnp.dot(vt_scr[2 * j + h], p[:, h * tq:(h + 1) * tq], preferred_element_type=F32)
               for h in range(2)]
        pair_t = jnp.concatenate([a[:HEAD_DIM] * (1.0 / a[HEAD_DIM:HEAD_DIM + 1]) for a in acc],
                                 axis=0)
        o_ref[:, cols] = pair_t.T.astype(o_ref.dtype)


def _cross_attn(q, mk, mv, batch, seq, tq):
    nq = seq // tq
    mem_len = mk.shape[0] // batch
    width = q.shape[1]
    n_pairs = width // LANES
    return pl.pallas_call(
        functools.partial(_cross_attn_kernel, tq=tq),
        grid=(batch, nq),
        in_specs=[pl.BlockSpec((tq, width), lambda b, i: (b * nq + i, 0)),
                  pl.BlockSpec((mem_len, width), lambda b, i: (b, 0)),
                  pl.BlockSpec((mem_len, width), lambda b, i: (b, 0))],
        out_specs=pl.BlockSpec((tq, width), lambda b, i: (b * nq + i, 0)),
        out_shape=jax.ShapeDtypeStruct(q.shape, BF16),
        scratch_shapes=[pltpu.VMEM((2 * n_pairs, HEAD_DIM + ONES_ROWS, mem_len), BF16)],
        compiler_params=_cparams(2),
        name="cross_attn",
    )(q, mk, mv)


def _diff_attn_kernel(q1_ref, q2_ref, k1_ref, k2_ref, v_ref, bias_ref, bmax_ref, lam_ref, g_ref,
                      o_ref, vt_scr, ka_scr, kmax_scr, *, tq, lam_init):
    sub = tq // BLOCK
    seq = k1_ref.shape[0]
    kc = 2 * BLOCK
    n_chunks = seq // kc
    k_refs = (k1_ref, k2_ref)

    for head in range(2):
        v_t = v_ref[:, head * C_V_DIM:(head + 1) * C_V_DIM].astype(F32).T
        vt_scr[head, :C_V_DIM, :] = v_t.astype(BF16)
        vt_scr[head, C_V_DIM:, :] = jnp.ones((ONES_ROWS, seq), BF16)
    for t in range(2):
        ka_scr[t, 0], ka_scr[t, 1] = _keys_with_ones(k_refs[t][...])
        kmax_lo, kmax_hi = _pair_key_norm_max(k_refs[t][...])
        kmax_scr[t, 0:1, :] = kmax_lo
        kmax_scr[t, 1:2, :] = kmax_hi

    lam_vec = lam_ref[...]
    lam = (jnp.exp(jnp.sum(lam_vec[0:1] * lam_vec[1:2], axis=-1, keepdims=True))
           - jnp.exp(jnp.sum(lam_vec[2:3] * lam_vec[3:4], axis=-1, keepdims=True)) + lam_init)

    def query_tile(i, carry):
        rows = pl.ds(pl.multiple_of(i * tq, tq), tq)
        lhs = [_stack_pair_heads(q_ref[rows, :].astype(F32) * (HEAD_DIM ** -0.5 * LOG2E))
               for q_ref in (q1_ref, q2_ref)]
        _diff_attn_tile(i, rows, lhs, lam, bias_ref, bmax_ref, g_ref, o_ref, vt_scr, ka_scr,
                        kmax_scr, k_refs, tq=tq, sub=sub, kc=kc, n_chunks=n_chunks,
                        lam_init=lam_init)
        return carry

    lax.fori_loop(0, seq // tq, query_tile, 0)


def _diff_attn_tile(i, rows, lhs, lam, bias_ref, bmax_ref, g_ref, o_ref, vt_scr, ka_scr, kmax_scr,
                    k_refs, *, tq, sub, kc, n_chunks, lam_init):
    def bias_chunk(c):
        def tile(head, u, e):
            offset = (2 * c + e) - (i * sub + u)
            return bias_ref[head, jnp.clip(offset, -BIAS_FAR, BIAS_FAR) + BIAS_FAR]
        return jnp.concatenate(
            [jnp.concatenate([tile(head, u, e) for head in range(2) for u in range(sub)], axis=1)
             for e in range(2)], axis=0)

    def shifted_scores(t):
        return _shifted_scores(ka_scr[t, 0], ka_scr[t, 1], _with_shift_lane(
            lhs[t], kmax_scr[t, 0:1, :], kmax_scr[t, 1:2, :], bmax_ref[0], bmax_ref[1]))

    def pv_of_shifted(s_all):
        p = jnp.concatenate([jnp.exp2(s_all[c * kc:(c + 1) * kc, :] + bias_chunk(c)).astype(BF16)
                             for c in range(n_chunks)], axis=0)
        return [jnp.dot(vt_scr[head], p[:, head * tq:(head + 1) * tq],
                        preferred_element_type=F32) for head in range(2)]

    def pv_running_max(s_all):
        m = None
        acc = [None, None]
        for c in range(n_chunks):
            s = s_all[c * kc:(c + 1) * kc, :] + bias_chunk(c)
            m_c = jnp.max(s, axis=0, keepdims=True)
            m_new = m_c if m is None else jnp.maximum(m, m_c)
            p = jnp.exp2(s - m_new).astype(BF16)
            if m is not None:
                alpha = jnp.exp2(m - m_new)
            for head in range(2):
                cols = slice(head * tq, (head + 1) * tq)
                pv = jnp.dot(vt_scr[head, :, c * kc:(c + 1) * kc], p[:, cols],
                             preferred_element_type=F32)
                acc[head] = pv if m is None else acc[head] * alpha[:, cols] + pv
            m = m_new
        return acc

    def write(acc1, acc2):
        g = g_ref[...] * (1.0 - lam_init)
        for head in range(2):
            o1 = acc1[head][:C_V_DIM] * (1.0 / acc1[head][C_V_DIM:C_V_DIM + 1])
            o2 = acc2[head][:C_V_DIM] * (1.0 / acc2[head][C_V_DIM:C_V_DIM + 1])
            out = o1 - lam * o2
            ms = jnp.mean(out * out, axis=0, keepdims=True)
            y = out * lax.rsqrt(ms + EPS) * g
            o_ref[rows, head * C_V_DIM:(head + 1) * C_V_DIM] = y.T.astype(o_ref.dtype)

    s1 = shifted_scores(0)
    s2 = shifted_scores(1)
    acc1 = pv_of_shifted(s1)
    acc2 = pv_of_shifted(s2)
    write(acc1, acc2)
    sums = jnp.concatenate([a[C_V_DIM:C_V_DIM + 1] for a in acc1 + acc2], axis=0)

    @pl.when(jnp.logical_not(jnp.min(sums) >= SUM_FLOOR))
    def _():
        write(pv_running_max(_nt_dot(k_refs[0][...], lhs[0])),
              pv_running_max(_nt_dot(k_refs[1][...], lhs[1])))


def _diff_attn(q1, q2, k1, k2, v, bias_t, bias_max, lam_vecs, subln_g, lam_init, batch, seq, tq):
    n_pairs = C_HEADS // 2
    n_tiles = bias_t.shape[1]
    seq_spec = pl.BlockSpec((seq, LANES), lambda b, p: (b, p))
    g_cols = jnp.broadcast_to(subln_g.reshape(C_V_DIM, 1), (C_V_DIM, tq))
    return pl.pallas_call(
        functools.partial(_diff_attn_kernel, tq=tq, lam_init=lam_init),
        grid=(batch, n_pairs),
        in_specs=[seq_spec, seq_spec, seq_spec, seq_spec,
                  pl.BlockSpec((seq, 2 * C_V_DIM), lambda b, p: (b, p)),
                  pl.BlockSpec((2, n_tiles, BLOCK, BLOCK), lambda b, p: (p, 0, 0, 0)),
                  pl.BlockSpec((2, 1, LANES), lambda b, p: (p, 0, 0)),
                  pl.BlockSpec(lam_vecs.shape, lambda b, p: (0, 0)),
                  pl.BlockSpec((C_V_DIM, tq), lambda b, p: (0, 0))],
        out_specs=pl.BlockSpec((seq, 2 * C_V_DIM), lambda b, p: (b, p)),
        out_shape=jax.ShapeDtypeStruct(v.shape, BF16),
        scratch_shapes=[pltpu.VMEM((2, C_V_DIM + ONES_ROWS, seq), BF16),
                        pltpu.VMEM((2, 2, seq, LANES), BF16),
                        pltpu.VMEM((2, 2, LANES), F32)],
        compiler_params=_cparams(2),
        name="diff_attn",
    )(q1, q2, k1, k2, v, bias_t, bias_max, lam_vecs, g_cols)


def _rope_tables(seq):
    rows = seq // GRID_W
    row = jnp.broadcast_to(jnp.arange(rows)[:, None], (rows, GRID_W)).reshape(-1)
    col = jnp.broadcast_to(jnp.arange(GRID_W)[None, :], (rows, GRID_W)).reshape(-1)
    half = HEAD_DIM // 2
    inv = 1.0 / (ROPE_THETA ** (jnp.arange(0, half, 2, dtype=F32) / half))
    ang_row = row.astype(F32)[:, None] * inv
    ang_col = col.astype(F32)[:, None] * inv
    cos = jnp.concatenate([jnp.cos(ang_row)] * 2 + [jnp.cos(ang_col)] * 2, axis=-1)
    sin = jnp.concatenate([-jnp.sin(ang_row), jnp.sin(ang_row),
                           -jnp.sin(ang_col), jnp.sin(ang_col)], axis=-1)
    return jnp.tile(cos, (1, 2)), jnp.tile(sin, (1, 2))


def kernel(x, mem, rel_bias, mem_norm, final_norm, even_norm, even_w_in, even_sink, even_q_norm, even_k_norm, even_w_mem_kv, even_w_out, odd_norm, odd_w_in, odd_lambda_q1, odd_lambda_k1, odd_lambda_q2, odd_lambda_k2, odd_subln, odd_w_mem_kv, odd_w_out):
    batch, seq, d = x.shape
    mem_len = mem.shape[1]
    tokens = batch * seq
    xw = X_HEADS * HEAD_DIM
    gw = A_HEADS * HEAD_DIM

    w_in0 = even_w_in[0].astype(BF16)
    w_out0 = even_w_out[0].astype(BF16)
    w_in1 = odd_w_in[0].astype(BF16)
    w_out1 = odd_w_out[0].astype(BF16)

    x2 = x.reshape(tokens, d)
    mem2 = mem.reshape(batch * mem_len, d)

    bias_near, bias_win = _bias_tiles(rel_bias)
    cos, sin_signed = _rope_tables(seq)
    head_bias_max = jnp.max(rel_bias, axis=0) * LOG2E

    w_mem = jnp.concatenate([even_w_mem_kv[0], odd_w_mem_kv[0]], axis=1).astype(BF16)
    mk0, mv0, mk1, mv1 = _norm_proj(mem2, mem_norm, w_mem, (xw,) * 4, TM_MEM, gate_last=False)

    rope_args = (cos, sin_signed, jnp.tile(even_q_norm[0], 2).reshape(1, LANES),
                 jnp.tile(even_k_norm[0], 2).reshape(1, LANES))
    aq, ak, av, bq, bk, bv, xq, gate = _norm_proj(
        x2, even_norm[0], w_in0, (gw, 128, 128, gw, 128, 128, xw, D_MODEL), TM_PROJ,
        gate_last=True, qk_rope=(3, 4), rope_args=rope_args, seq=seq)
    y_a = _window_attn(aq, ak, av, bias_win, even_sink[0] * LOG2E, batch, seq, WINDOW_GROUP)
    y_b = _dense_attn(bq, bk, bv, batch, seq, TQ_DENSE)
    y_x = _cross_attn(xq, mk0, mv0, batch, seq, TQ_CROSS)
    lam_init = 0.8 - 0.6 * math.exp(-0.3 * 1)
    cw = C_HEADS * HEAD_DIM
    h1, q1, q2, k1, k2, v, xq1, gate1 = _gate_out(
        [y_a, y_b, y_x], gate, w_out0, x2, odd_norm[0], TM_PROJ, w_next=w_in1,
        splits=(cw, cw, cw, cw, C_HEADS * C_V_DIM, xw, D_MODEL))
    lam_vecs = jnp.stack([odd_lambda_q1[0], odd_lambda_k1[0], odd_lambda_q2[0], odd_lambda_k2[0]])
    bias_max = jnp.broadcast_to(head_bias_max.reshape(C_HEADS, 1, 1), (C_HEADS, 1, LANES))
    y_c = _diff_attn(q1, q2, k1, k2, v, bias_near, bias_max, lam_vecs, odd_subln[0], lam_init,
                     batch, seq, TQ_DIFF)
    y_x1 = _cross_attn(xq1, mk1, mv1, batch, seq, TQ_CROSS)
    (out,) = _gate_out([y_c, y_x1], gate1, w_out1, h1, final_norm, TM_PROJ)
    return out.reshape(batch, seq, d)
```

```python
import functools
import math

import numpy as np
import jax
import jax.numpy as jnp
from jax import lax
from jax.experimental import pallas as pl
from jax.experimental.pallas import tpu as pltpu

D_MODEL = 1024
HEAD_DIM = 64
BLOCK = 128
WINDOW = 128
GRID_W = 64
A_HEADS = 6
B_HEADS = 6
C_HEADS = 6
C_V_DIM = 128
X_HEADS = 4
REL_BUCKETS = 32
REL_MAX_DIST = 128
ROPE_THETA = 10000.0
EPS = 1e-6
NEG_INF = -1e30
LOG2E = math.log2(math.e)
LANES = 128
ONES_ROWS = 16
PROJ_SUB_ROWS = 256
GATE_SUB_ROWS = 512
TM_PROJ = 1024
TM_MEM = 256
TQ_DENSE = 256
TQ_DIFF = 512
TQ_CROSS = 2048
WINDOW_GROUP = 16
SUM_FLOOR = 2.0 ** -64
SHIFT_MARGIN = 2.0 ** -7
BIAS_FAR = 2
VMEM_LIMIT = 56 * 1024 * 1024

F32 = jnp.float32
BF16 = jnp.bfloat16


def _cparams(n_axes, flags=None):
    return pltpu.CompilerParams(dimension_semantics=("arbitrary",) * n_axes,
                                vmem_limit_bytes=VMEM_LIMIT, flags=flags)


def _lane_lo(shape):
    return lax.broadcasted_iota(jnp.int32, shape, len(shape) - 1) < HEAD_DIM


def _rmsnorm_rows(x, g):
    ms = jnp.mean(x * x, axis=-1, keepdims=True)
    return x * lax.rsqrt(ms + EPS) * g


def _nt_dot(a, b):
    return lax.dot_general(a, b, (((1,), (1,)), ((), ())), preferred_element_type=F32)


def _norm_proj_kernel(*refs, qk_rope, gate_last):
    x_ref, g_ref, w_ref = refs[:3]
    if qk_rope is None:
        out_refs = refs[3:]
    else:
        cos_ref, sin_ref, gq_ref, gk_ref = refs[3:7]
        out_refs = refs[7:]
    tm = x_ref.shape[0]
    for r0 in range(0, tm, PROJ_SUB_ROWS):
        rows = slice(r0, min(r0 + PROJ_SUB_ROWS, tm))
        xn = _rmsnorm_rows(x_ref[rows, :], g_ref[...]).astype(BF16)
        y_all = jnp.dot(xn, w_ref[...], preferred_element_type=F32)
        c0 = 0
        for j, o_ref in enumerate(out_refs):
            width = o_ref.shape[1]
            y = y_all[:, c0:c0 + width]
            if qk_rope is not None and j in qk_rope:
                gain, scale = ((gq_ref, HEAD_DIM ** -0.5 * LOG2E) if j == qk_rope[0]
                               else (gk_ref, 1.0))
                y = jnp.concatenate(
                    [_norm_rope_pair(y[:, t * LANES:(t + 1) * LANES], gain[...],
                                     cos_ref[rows, :], sin_ref[rows, :]) * scale
                     for t in range(width // LANES)], axis=1)
            if gate_last and j == len(out_refs) - 1:
                y = _silu(y)
            o_ref[rows, :] = y.astype(o_ref.dtype)
            c0 += width


def _silu(y):
    return y * (1.0 / (1.0 + jnp.exp(-y)))


def _norm_proj(x, g, w_bf16, splits, tm, gate_last, qk_rope=None, rope_args=(), seq=None):
    rows, d = x.shape
    n = w_bf16.shape[1]
    assert sum(splits) == n and rows % tm == 0
    in_specs = [pl.BlockSpec((tm, d), lambda i: (i, 0)),
                pl.BlockSpec((1, d), lambda i: (0, 0)),
                pl.BlockSpec((d, n), lambda i: (0, 0))]
    if qk_rope is not None:
        pos_blocks = seq // tm
        in_specs += [pl.BlockSpec((tm, LANES), lambda i: (i % pos_blocks, 0)),
                     pl.BlockSpec((tm, LANES), lambda i: (i % pos_blocks, 0)),
                     pl.BlockSpec((1, LANES), lambda i: (0, 0)),
                     pl.BlockSpec((1, LANES), lambda i: (0, 0))]
    return pl.pallas_call(
        functools.partial(_norm_proj_kernel, qk_rope=qk_rope, gate_last=gate_last),
        grid=(rows // tm,),
        in_specs=in_specs,
        out_specs=[pl.BlockSpec((tm, s), lambda i: (i, 0)) for s in splits],
        out_shape=[jax.ShapeDtypeStruct((rows, s), BF16) for s in splits],
        compiler_params=_cparams(1),
        name="norm_proj",
    )(x, g.reshape(1, d), w_bf16, *rope_args)


def _gate_out_kernel(*refs, n_parts, after):
    y_refs = refs[:n_parts]
    gate_ref, w_ref, res_ref, g_ref = refs[n_parts:n_parts + 4]
    if after == "proj":
        w_next_ref, h_ref = refs[n_parts + 4:n_parts + 6]
        out_refs = refs[n_parts + 6:]
    else:
        h_ref = refs[n_parts + 4]
    tm = res_ref.shape[0]
    for r0 in range(0, tm, GATE_SUB_ROWS):
        rows = slice(r0, min(r0 + GATE_SUB_ROWS, tm))
        parts = []
        c0 = 0
        for y_ref in y_refs:
            width = y_ref.shape[1]
            parts.append(y_ref[rows, :] * gate_ref[rows, c0:c0 + width])
            c0 += width
        h = res_ref[rows, :] + jnp.dot(jnp.concatenate(parts, axis=1), w_ref[...],
                                       preferred_element_type=F32)
        if after == "norm":
            h_ref[rows, :] = _rmsnorm_rows(h, g_ref[...])
            continue
        h_ref[rows, :] = h
        y_all = jnp.dot(_rmsnorm_rows(h, g_ref[...]).astype(BF16), w_next_ref[...],
                        preferred_element_type=F32)
        c0 = 0
        for j, o_ref in enumerate(out_refs):
            width = o_ref.shape[1]
            y = y_all[:, c0:c0 + width]
            if j == len(out_refs) - 1:
                y = _silu(y)
            o_ref[rows, :] = y.astype(o_ref.dtype)
            c0 += width


def _gate_out(y_parts, gate, w_bf16, resid, norm_g, tm, w_next=None, splits=()):
    rows, d = resid.shape
    mix = w_bf16.shape[0]
    after = "norm" if w_next is None else "proj"
    row_block = lambda width: pl.BlockSpec((tm, width), lambda i: (i, 0))
    whole = lambda a: pl.BlockSpec(a.shape, lambda i: (0, 0))
    in_specs = [row_block(y.shape[1]) for y in y_parts]
    in_specs += [row_block(mix), whole(w_bf16), row_block(d), pl.BlockSpec((1, d), lambda i: (0, 0))]
    args = list(y_parts) + [gate, w_bf16, resid, norm_g.reshape(1, d)]
    out_specs = [row_block(d)]
    out_shape = [jax.ShapeDtypeStruct((rows, d), F32)]
    if after == "proj":
        assert sum(splits) == w_next.shape[1]
        in_specs.append(whole(w_next))
        args.append(w_next)
        out_specs += [row_block(s) for s in splits]
        out_shape += [jax.ShapeDtypeStruct((rows, s), BF16) for s in splits]
    return pl.pallas_call(
        functools.partial(_gate_out_kernel, n_parts=len(y_parts), after=after),
        grid=(rows // tm,),
        in_specs=in_specs,
        out_specs=out_specs,
        out_shape=out_shape,
        compiler_params=_cparams(1),
        name="gate_out",
    )(*args)


def _bucket_thresholds():
    nb = REL_BUCKETS // 2
    max_exact = nb // 2
    n = np.arange(0, 4 * REL_MAX_DIST)
    nf = np.maximum(n, 1).astype(np.float32)
    large = max_exact + (np.log(nf / np.float32(max_exact))
                         / np.float32(math.log(REL_MAX_DIST / max_exact))
                         * np.float32(nb - max_exact)).astype(np.int32)
    bucket = np.where(n < max_exact, n, np.minimum(large, nb - 1))
    assert np.all(np.diff(bucket) >= 0) and bucket[0] == 0 and bucket[-1] == nb - 1
    thr = [int(np.argmax(bucket >= k)) for k in range(1, nb)]
    assert thr[-1] < REL_MAX_DIST
    return thr


def _bias_tiles_kernel(tab_ref, near_ref, win_ref):
    h = pl.program_id(0)
    thr = _bucket_thresholds()
    nb = REL_BUCKETS // 2
    row = lax.broadcasted_iota(jnp.int32, (BLOCK, BLOCK), 0)
    col = lax.broadcasted_iota(jnp.int32, (BLOCK, BLOCK), 1)
    def tile_of(rel):
        n = jnp.abs(rel)
        neg = jnp.full((BLOCK, BLOCK), tab_ref[0, h], F32)
        pos = jnp.full((BLOCK, BLOCK), tab_ref[nb, h], F32)
        for b in range(1, nb):
            ge = n >= thr[b - 1]
            neg = jnp.where(ge, tab_ref[b, h], neg)
            pos = jnp.where(ge, tab_ref[nb + b, h], pos)
        return jnp.where(rel > 0, pos, neg) * LOG2E

    for k in range(2 * BIAS_FAR + 1):
        rel = (k - BIAS_FAR) * BLOCK + row - col
        tile = tile_of(rel)
        near_ref[0, k] = tile
        if abs(k - BIAS_FAR) <= 1:
            j = k - BIAS_FAR + 1
            win_ref[j * BLOCK:(j + 1) * BLOCK, :] = jnp.where(jnp.abs(rel) <= WINDOW, tile, NEG_INF)


def _bias_tiles(rel_bias):
    assert _bucket_thresholds()[-1] <= (BIAS_FAR - 1) * BLOCK + 1
    n_tiles = 2 * BIAS_FAR + 1
    n_heads = rel_bias.shape[1]
    return pl.pallas_call(
        _bias_tiles_kernel,
        grid=(n_heads,),
        in_specs=[pl.BlockSpec(memory_space=pltpu.SMEM)],
        out_specs=[pl.BlockSpec((1, n_tiles, BLOCK, BLOCK), lambda h: (h, 0, 0, 0)),
                   pl.BlockSpec((3 * BLOCK, BLOCK), lambda h: (0, h))],
        out_shape=[jax.ShapeDtypeStruct((n_heads, n_tiles, BLOCK, BLOCK), F32),
                   jax.ShapeDtypeStruct((3 * BLOCK, n_heads * BLOCK), F32)],
        compiler_params=_cparams(1),
        name="bias_tiles",
    )(rel_bias)


def _stack_gqa_heads(q_tiles):
    lo = _lane_lo(q_tiles[0].shape)
    swap = lambda q: pltpu.roll(q, HEAD_DIM, 1)
    heads_lo = [q_tiles[0], swap(q_tiles[0]), q_tiles[1]]
    heads_hi = [q_tiles[1], swap(q_tiles[2]), q_tiles[2]]
    return jnp.concatenate([jnp.where(lo, q, 0.0).astype(BF16) for q in heads_lo]
                           + [jnp.where(lo, 0.0, q).astype(BF16) for q in heads_hi], axis=0)


def _unstack_gqa_heads(out_lo, out_hi, width):
    blk = lambda o, j: o[:, j * width:(j + 1) * width]
    return [jnp.concatenate([blk(out_lo, 0), blk(out_lo, 1)], axis=0),
            jnp.concatenate([blk(out_lo, 2), blk(out_hi, 0)], axis=0),
            jnp.concatenate([blk(out_hi, 1), blk(out_hi, 2)], axis=0)]


def _stack_pair_heads(q):
    lo = _lane_lo(q.shape)
    return jnp.concatenate([jnp.where(lo, q, 0.0).astype(BF16),
                            jnp.where(lo, 0.0, q).astype(BF16)], axis=0)


def _with_shift_lane(lhs, kmax_lo, kmax_hi, bias_max_lo, bias_max_hi):
    half = lhs.shape[0] // 2
    lf = lhs.astype(F32)
    q_sq = jnp.sum(lf * lf, axis=-1, keepdims=True)
    q_norm = q_sq * lax.rsqrt(jnp.maximum(q_sq, 1e-30))
    row = lax.broadcasted_iota(jnp.int32, lhs.shape, 0)
    lane = lax.broadcasted_iota(jnp.int32, lhs.shape, 1)
    lo_row = row < half
    shift = (q_norm * jnp.where(lo_row, kmax_lo, kmax_hi)
             + jnp.where(lo_row, bias_max_lo, bias_max_hi))
    shift = shift + jnp.abs(shift) * SHIFT_MARGIN
    slot_lane = jnp.where(lo_row, HEAD_DIM, 0)
    return jnp.where(lane == slot_lane, -shift, lf).astype(BF16)


def _keys_with_ones(k_pair):
    lane = lax.broadcasted_iota(jnp.int32, k_pair.shape, 1)
    kf = k_pair.astype(F32)
    k_lo = jnp.where(lane < HEAD_DIM, kf, jnp.where(lane == HEAD_DIM, 1.0, 0.0))
    k_hi = jnp.where(lane >= HEAD_DIM, kf, jnp.where(lane == 0, 1.0, 0.0))
    return k_lo.astype(BF16), k_hi.astype(BF16)


def _shifted_scores(ka_lo, ka_hi, lhs_shifted):
    half = lhs_shifted.shape[0] // 2
    return jnp.concatenate([_nt_dot(ka_lo, lhs_shifted[:half]), _nt_dot(ka_hi, lhs_shifted[half:])],
                           axis=1)


def _pair_key_norm_max(k_pair):
    kf = k_pair.astype(F32)
    row = lax.broadcasted_iota(jnp.int32, (LANES, 2 * LANES), 0)
    col = lax.broadcasted_iota(jnp.int32, (LANES, 2 * LANES), 1)
    half_sel = jnp.where((row < HEAD_DIM) == (col < LANES), 1.0, 0.0).astype(BF16)
    sq = jnp.dot((kf * kf).astype(BF16), half_sel, preferred_element_type=F32)
    norm_max = jnp.sqrt(jnp.max(sq, axis=0, keepdims=True))
    return norm_max[:, :LANES], norm_max[:, LANES:]


def _window_attn_kernel(q_ref, k_ref, v_ref, bias_ref, sink_row_ref,
                        o_ref, kpad_scr, vtpad_scr, *, n_blocks, group):
    i = pl.program_id(1)
    half = 3 * BLOCK

    @pl.when(i == 0)
    def _():
        zero_blk = jnp.zeros((1, BLOCK, LANES), BF16)
        kpad_scr[0:1] = zero_blk
        kpad_scr[n_blocks + 1:n_blocks + 2] = zero_blk
        kpad_scr[1:n_blocks + 1] = k_ref[...].reshape(n_blocks, BLOCK, LANES)
        v_t = v_ref[...].astype(F32).T.astype(BF16)
        ones = jnp.ones((ONES_ROWS, BLOCK), BF16)
        zero_v = jnp.zeros((HEAD_DIM + ONES_ROWS, BLOCK), BF16)
        for g in range(2):
            vtpad_scr[g, 0] = zero_v
            vtpad_scr[g, n_blocks + 1] = zero_v
            for blk in range(n_blocks):
                vtpad_scr[g, blk + 1, :HEAD_DIM, :] = v_t[g * HEAD_DIM:(g + 1) * HEAD_DIM,
                                                           blk * BLOCK:(blk + 1) * BLOCK]
                vtpad_scr[g, blk + 1, HEAD_DIM:, :] = ones

    sink_row = sink_row_ref[...]

    for u in range(group):
        n = i * group + u
        qf = q_ref[u * BLOCK:(u + 1) * BLOCK, :].astype(F32) * (HEAD_DIM ** -0.5 * LOG2E)
        lhs = _stack_gqa_heads([qf[:, j * LANES:(j + 1) * LANES] for j in range(3)])
        k_win = kpad_scr[pl.ds(n, 3)].reshape(3 * BLOCK, LANES)
        s = _nt_dot(k_win, lhs) + bias_ref[...]
        s = jnp.concatenate([jnp.where(n > 0, s[:BLOCK], NEG_INF), s[BLOCK:2 * BLOCK],
                             jnp.where(n < n_blocks - 1, s[2 * BLOCK:], NEG_INF)], axis=0)
        m = jnp.maximum(jnp.max(s, axis=0, keepdims=True), sink_row)
        p = jnp.exp2(s - m).astype(BF16)
        p_sink = jnp.exp2(sink_row - m)
        outs = []
        for g in range(2):
            acc = None
            for j in range(3):
                pv = jnp.dot(vtpad_scr[g, n + j], p[j * BLOCK:(j + 1) * BLOCK, g * half:(g + 1) * half],
                             preferred_element_type=F32)
                acc = pv if acc is None else acc + pv
            total = acc[HEAD_DIM:HEAD_DIM + 1] + p_sink[:, g * half:(g + 1) * half]
            outs.append(acc[:HEAD_DIM] * (1.0 / total))
        for j, pair_t in enumerate(_unstack_gqa_heads(outs[0], outs[1], BLOCK)):
            o_ref[u * BLOCK:(u + 1) * BLOCK, j * LANES:(j + 1) * LANES] = pair_t.T.astype(o_ref.dtype)


def _window_attn(q, k, v, bias_win_t, sink, batch, seq, group):
    nb = seq // BLOCK
    steps = nb // group
    rows = A_HEADS * BLOCK
    sink_row = jnp.repeat(sink, BLOCK).reshape(1, rows)
    const = lambda shape: pl.BlockSpec(shape, lambda b, i: (0,) * len(shape))
    return pl.pallas_call(
        functools.partial(_window_attn_kernel, n_blocks=nb, group=group),
        grid=(batch, steps),
        in_specs=[pl.BlockSpec((group * BLOCK, 3 * LANES), lambda b, i: (b * steps + i, 0)),
                  pl.BlockSpec((seq, LANES), lambda b, i: (b, 0)),
                  pl.BlockSpec((seq, LANES), lambda b, i: (b, 0)),
                  const(bias_win_t.shape), const((1, rows))],
        out_specs=pl.BlockSpec((group * BLOCK, 3 * LANES), lambda b, i: (b * steps + i, 0)),
        out_shape=jax.ShapeDtypeStruct(q.shape, BF16),
        scratch_shapes=[pltpu.VMEM((nb + 2, BLOCK, LANES), BF16),
                        pltpu.VMEM((2, nb + 2, HEAD_DIM + ONES_ROWS, BLOCK), BF16)],
        compiler_params=_cparams(2),
        name="window_attn",
    )(q, k, v, bias_win_t, sink_row)


def _norm_rope_pair(x, g, cos, sin_signed):
    lo = _lane_lo(x.shape)
    x2 = x * x
    ss_lo = jnp.sum(jnp.where(lo, x2, 0.0), axis=-1, keepdims=True)
    ss_hi = jnp.sum(jnp.where(lo, 0.0, x2), axis=-1, keepdims=True)
    ms = jnp.where(lo, ss_lo, ss_hi) * (1.0 / HEAD_DIM)
    y = x * lax.rsqrt(ms + EPS) * g
    lane = lax.broadcasted_iota(jnp.int32, x.shape, 1)
    quarter = HEAD_DIM // 4
    first = (lane & quarter) == 0
    partner = jnp.where(first, pltpu.roll(y, LANES - quarter, 1), pltpu.roll(y, quarter, 1))
    return y * cos + partner * sin_signed


def _dense_attn_kernel(q_ref, k_ref, v_ref, o_ref, ka_scr, vt_scr, kmax_scr, *, tq):
    seq = k_ref.shape[0]
    kc = 2 * BLOCK
    n_chunks = seq // kc
    grp = 3 * tq

    ka_scr[0], ka_scr[1] = _keys_with_ones(k_ref[...])
    kmax_lo, kmax_hi = _pair_key_norm_max(k_ref[...])
    kmax_scr[0:1, :] = kmax_lo
    kmax_scr[1:2, :] = kmax_hi
    v_t = v_ref[...].astype(F32).T.astype(BF16)
    for g in range(2):
        vt_scr[g, :HEAD_DIM, :] = v_t[g * HEAD_DIM:(g + 1) * HEAD_DIM]
        vt_scr[g, HEAD_DIM:, :] = jnp.ones((ONES_ROWS, seq), BF16)

    def query_tile(i, carry):
        rows = pl.ds(pl.multiple_of(i * tq, tq), tq)
        lhs = _stack_gqa_heads([q_ref[rows, j * LANES:(j + 1) * LANES].astype(F32)
                                for j in range(3)])

        def write(acc):
            outs = [a[:HEAD_DIM] * (1.0 / a[HEAD_DIM:HEAD_DIM + 1]) for a in acc]
            for j, pair_t in enumerate(_unstack_gqa_heads(outs[0], outs[1], tq)):
                o_ref[rows, j * LANES:(j + 1) * LANES] = pair_t.T.astype(o_ref.dtype)

        s_all = _shifted_scores(ka_scr[0], ka_scr[1], _with_shift_lane(
            lhs, kmax_scr[0:1, :], kmax_scr[1:2, :], 0.0, 0.0))
        p = jnp.exp2(s_all).astype(BF16)
        acc = [jnp.dot(vt_scr[g], p[:, g * grp:(g + 1) * grp], preferred_element_type=F32)
               for g in range(2)]
        write(acc)
        sums = jnp.concatenate([a[HEAD_DIM:HEAD_DIM + 1] for a in acc], axis=0)

        @pl.when(jnp.logical_not(jnp.min(sums) >= SUM_FLOOR))
        def _():
            s_exact = _nt_dot(k_ref[...], lhs)
            m = None
            acc2 = [None, None]
            for c in range(n_chunks):
                s = s_exact[c * kc:(c + 1) * kc, :]
                m_c = jnp.max(s, axis=0, keepdims=True)
                m_new = m_c if m is None else jnp.maximum(m, m_c)
                pc = jnp.exp2(s - m_new).astype(BF16)
                if m is not None:
                    alpha = jnp.exp2(m - m_new)
                for g in range(2):
                    cols = slice(g * grp, (g + 1) * grp)
                    pv = jnp.dot(vt_scr[g, :, c * kc:(c + 1) * kc], pc[:, cols],
                                 preferred_element_type=F32)
                    acc2[g] = pv if m is None else acc2[g] * alpha[:, cols] + pv
                m = m_new
            write(acc2)

        return carry

    lax.fori_loop(0, seq // tq, query_tile, 0)


def _dense_attn(q, k, v, batch, seq, tq):
    return pl.pallas_call(
        functools.partial(_dense_attn_kernel, tq=tq),
        grid=(batch,),
        in_specs=[pl.BlockSpec((seq, 3 * LANES), lambda b: (b, 0)),
                  pl.BlockSpec((seq, LANES), lambda b: (b, 0)),
                  pl.BlockSpec((seq, LANES), lambda b: (b, 0))],
        out_specs=pl.BlockSpec((seq, 3 * LANES), lambda b: (b, 0)),
        out_shape=jax.ShapeDtypeStruct(q.shape, BF16),
        scratch_shapes=[pltpu.VMEM((2, seq, LANES), BF16),
                        pltpu.VMEM((2, HEAD_DIM + ONES_ROWS, seq), BF16),
                        pltpu.VMEM((2, LANES), F32)],
        compiler_params=_cparams(1),
        name="dense_attn",
    )(q, k, v)


def _cross_attn_kernel(q_ref, mk_ref, mv_ref, o_ref, vt_scr, *, tq):
    i = pl.program_id(1)
    n_pairs = q_ref.shape[1] // LANES
    mem_len = mk_ref.shape[0]

    @pl.when(i == 0)
    def _():
        v_t = mv_ref[...].astype(F32).T.astype(BF16)
        for h in range(2 * n_pairs):
            vt_scr[h, :HEAD_DIM, :] = v_t[h * HEAD_DIM:(h + 1) * HEAD_DIM]
            vt_scr[h, HEAD_DIM:, :] = jnp.ones((ONES_ROWS, mem_len), BF16)

    for j in range(n_pairs):
        cols = slice(j * LANES, (j + 1) * LANES)
        lhs = _stack_pair_heads(q_ref[:, cols].astype(F32) * (HEAD_DIM ** -0.5 * LOG2E))
        s = _nt_dot(mk_ref[:, cols], lhs)
        p = jnp.exp2(s - jnp.max(s, axis=0, keepdims=True)).astype(BF16)
        acc = [jnp.dot(vt_scr[2 * j + h], p[:, h * tq:(h + 1) * tq], preferred_element_type=F32)
               for h in range(2)]
        pair_t = jnp.concatenate([a[:HEAD_DIM] * (1.0 / a[HEAD_DIM:HEAD_DIM + 1]) for a in acc],
                                 axis=0)
        o_ref[:, cols] = pair_t.T.astype(o_ref.dtype)


def _cross_attn(q, mk, mv, batch, seq, tq):
    nq = seq // tq
    mem_len = mk.shape[0] // batch
    width = q.shape[1]
    n_pairs = width // LANES
    return pl.pallas_call(
        functools.partial(_cross_attn_kernel, tq=tq),
        grid=(batch, nq),
        in_specs=[pl.BlockSpec((tq, width), lambda b, i: (b * nq + i, 0)),
                  pl.BlockSpec((mem_len, width), lambda b, i: (b, 0)),
                  pl.BlockSpec((mem_len, width), lambda b, i: (b, 0))],
        out_specs=pl.BlockSpec((tq, width), lambda b, i: (b * nq + i, 0)),
        out_shape=jax.ShapeDtypeStruct(q.shape, BF16),
        scratch_shapes=[pltpu.VMEM((2 * n_pairs, HEAD_DIM + ONES_ROWS, mem_len), BF16)],
        compiler_params=_cparams(2),
        name="cross_attn",
    )(q, mk, mv)


def _diff_attn_kernel(q1_ref, q2_ref, k1_ref, k2_ref, v_ref, bias_ref, bmax_ref, lam_ref, g_ref,
                      o_ref, vt_scr, ka_scr, kmax_scr, *, tq, lam_init):
    sub = tq // BLOCK
    seq = k1_ref.shape[0]
    kc = 2 * BLOCK
    n_chunks = seq // kc
    k_refs = (k1_ref, k2_ref)

    for head in range(2):
        v_t = v_ref[:, head * C_V_DIM:(head + 1) * C_V_DIM].astype(F32).T
        vt_scr[head, :C_V_DIM, :] = v_t.astype(BF16)
        vt_scr[head, C_V_DIM:, :] = jnp.ones((ONES_ROWS, seq), BF16)
    for t in range(2):
        ka_scr[t, 0], ka_scr[t, 1] = _keys_with_ones(k_refs[t][...])
        kmax_lo, kmax_hi = _pair_key_norm_max(k_refs[t][...])
        kmax_scr[t, 0:1, :] = kmax_lo
        kmax_scr[t, 1:2, :] = kmax_hi

    lam_vec = lam_ref[...]
    lam = (jnp.exp(jnp.sum(lam_vec[0:1] * lam_vec[1:2], axis=-1, keepdims=True))
           - jnp.exp(jnp.sum(lam_vec[2:3] * lam_vec[3:4], axis=-1, keepdims=True)) + lam_init)

    def query_tile(i, carry):
        rows = pl.ds(pl.multiple_of(i * tq, tq), tq)
        lhs = [_stack_pair_heads(q_ref[rows, :].astype(F32) * (HEAD_DIM ** -0.5 * LOG2E))
               for q_ref in (q1_ref, q2_ref)]
        _diff_attn_tile(i, rows, lhs, lam, bias_ref, bmax_ref, g_ref, o_ref, vt_scr, ka_scr,
                        kmax_scr, k_refs, tq=tq, sub=sub, kc=kc, n_chunks=n_chunks,
                        lam_init=lam_init)
        return carry

    lax.fori_loop(0, seq // tq, query_tile, 0)


def _diff_attn_tile(i, rows, lhs, lam, bias_ref, bmax_ref, g_ref, o_ref, vt_scr, ka_scr, kmax_scr,
                    k_refs, *, tq, sub, kc, n_chunks, lam_init):
    def bias_chunk(c):
        def tile(head, u, e):
            offset = (2 * c + e) - (i * sub + u)
            return bias_ref[head, jnp.clip(offset, -BIAS_FAR, BIAS_FAR) + BIAS_FAR]
        return jnp.concatenate(
            [jnp.concatenate([tile(head, u, e) for head in range(2) for u in range(sub)], axis=1)
             for e in range(2)], axis=0)

    def shifted_scores(t):
        return _shifted_scores(ka_scr[t, 0], ka_scr[t, 1], _with_shift_lane(
            lhs[t], kmax_scr[t, 0:1, :], kmax_scr[t, 1:2, :], bmax_ref[0], bmax_ref[1]))

    def pv_of_shifted(s_all):
        p = jnp.concatenate([jnp.exp2(s_all[c * kc:(c + 1) * kc, :] + bias_chunk(c)).astype(BF16)
                             for c in range(n_chunks)], axis=0)
        return [jnp.dot(vt_scr[head], p[:, head * tq:(head + 1) * tq],
                        preferred_element_type=F32) for head in range(2)]

    def pv_running_max(s_all):
        m = None
        acc = [None, None]
        for c in range(n_chunks):
            s = s_all[c * kc:(c + 1) * kc, :] + bias_chunk(c)
            m_c = jnp.max(s, axis=0, keepdims=True)
            m_new = m_c if m is None else jnp.maximum(m, m_c)
            p = jnp.exp2(s - m_new).astype(BF16)
            if m is not None:
                alpha = jnp.exp2(m - m_new)
            for head in range(2):
                cols = slice(head * tq, (head + 1) * tq)
                pv = jnp.dot(vt_scr[head, :, c * kc:(c + 1) * kc], p[:, cols],
                             preferred_element_type=F32)
                acc[head] = pv if m is None else acc[head] * alpha[:, cols] + pv
            m = m_new
        return acc

    def write(acc1, acc2):
        g = g_ref[...] * (1.0 - lam_init)
        for head in range(2):
            o1 = acc1[head][:C_V_DIM] * (1.0 / acc1[head][C_V_DIM:C_V_DIM + 1])
            o2 = acc2[head][:C_V_DIM] * (1.0 / acc2[head][C_V_DIM:C_V_DIM + 1])
            out = o1 - lam * o2
            ms = jnp.mean(out * out, axis=0, keepdims=True)
            y = out * lax.rsqrt(ms + EPS) * g
            o_ref[rows, head * C_V_DIM:(head + 1) * C_V_DIM] = y.T.astype(o_ref.dtype)

    s1 = shifted_scores(0)
    s2 = shifted_scores(1)
    acc1 = pv_of_shifted(s1)
    acc2 = pv_of_shifted(s2)
    write(acc1, acc2)
    sums = jnp.concatenate([a[C_V_DIM:C_V_DIM + 1] for a in acc1 + acc2], axis=0)

    @pl.when(jnp.logical_not(jnp.min(sums) >= SUM_FLOOR))
    def _():
        write(pv_running_max(_nt_dot(k_refs[0][...], lhs[0])),
              pv_running_max(_nt_dot(k_refs[1][...], lhs[1])))


def _diff_attn(q1, q2, k1, k2, v, bias_t, bias_max, lam_vecs, subln_g, lam_init, batch, seq, tq):
    n_pairs = C_HEADS // 2
    n_tiles = bias_t.shape[1]
    seq_spec = pl.BlockSpec((seq, LANES), lambda b, p: (b, p))
    g_cols = jnp.broadcast_to(subln_g.reshape(C_V_DIM, 1), (C_V_DIM, tq))
    return pl.pallas_call(
        functools.partial(_diff_attn_kernel, tq=tq, lam_init=lam_init),
        grid=(batch, n_pairs),
        in_specs=[seq_spec, seq_spec, seq_spec, seq_spec,
                  pl.BlockSpec((seq, 2 * C_V_DIM), lambda b, p: (b, p)),
                  pl.BlockSpec((2, n_tiles, BLOCK, BLOCK), lambda b, p: (p, 0, 0, 0)),
                  pl.BlockSpec((2, 1, LANES), lambda b, p: (p, 0, 0)),
                  pl.BlockSpec(lam_vecs.shape, lambda b, p: (0, 0)),
                  pl.BlockSpec((C_V_DIM, tq), lambda b, p: (0, 0))],
        out_specs=pl.BlockSpec((seq, 2 * C_V_DIM), lambda b, p: (b, p)),
        out_shape=jax.ShapeDtypeStruct(v.shape, BF16),
        scratch_shapes=[pltpu.VMEM((2, C_V_DIM + ONES_ROWS, seq), BF16),
                        pltpu.VMEM((2, 2, seq, LANES), BF16),
                        pltpu.VMEM((2, 2, LANES), F32)],
        compiler_params=_cparams(2),
        name="diff_attn",
    )(q1, q2, k1, k2, v, bias_t, bias_max, lam_vecs, g_cols)


def _rope_tables(seq):
    rows = seq // GRID_W
    row = jnp.broadcast_to(jnp.arange(rows)[:, None], (rows, GRID_W)).reshape(-1)
    col = jnp.broadcast_to(jnp.arange(GRID_W)[None, :], (rows, GRID_W)).reshape(-1)
    half = HEAD_DIM // 2
    inv = 1.0 / (ROPE_THETA ** (jnp.arange(0, half, 2, dtype=F32) / half))
    ang_row = row.astype(F32)[:, None] * inv
    ang_col = col.astype(F32)[:, None] * inv
    cos = jnp.concatenate([jnp.cos(ang_row)] * 2 + [jnp.cos(ang_col)] * 2, axis=-1)
    sin = jnp.concatenate([-jnp.sin(ang_row), jnp.sin(ang_row),
                           -jnp.sin(ang_col), jnp.sin(ang_col)], axis=-1)
    return jnp.tile(cos, (1, 2)), jnp.tile(sin, (1, 2))


def kernel(x, mem, rel_bias, mem_norm, final_norm, even_norm, even_w_in, even_sink, even_q_norm, even_k_norm, even_w_mem_kv, even_w_out, odd_norm, odd_w_in, odd_lambda_q1, odd_lambda_k1, odd_lambda_q2, odd_lambda_k2, odd_subln, odd_w_mem_kv, odd_w_out):
    batch, seq, d = x.shape
    mem_len = mem.shape[1]
    tokens = batch * seq
    xw = X_HEADS * HEAD_DIM
    gw = A_HEADS * HEAD_DIM

    w_in0 = even_w_in[0].astype(BF16)
    w_out0 = even_w_out[0].astype(BF16)
    w_in1 = odd_w_in[0].astype(BF16)
    w_out1 = odd_w_out[0].astype(BF16)

    x2 = x.reshape(tokens, d)
    mem2 = mem.reshape(batch * mem_len, d)

    bias_near, bias_win = _bias_tiles(rel_bias)
    cos, sin_signed = _rope_tables(seq)
    head_bias_max = jnp.max(rel_bias, axis=0) * LOG2E

    w_mem = jnp.concatenate([even_w_mem_kv[0], odd_w_mem_kv[0]], axis=1).astype(BF16)
    mk0, mv0, mk1, mv1 = _norm_proj(mem2, mem_norm, w_mem, (xw,) * 4, TM_MEM, gate_last=False)

    rope_args = (cos, sin_signed, jnp.tile(even_q_norm[0], 2).reshape(1, LANES),
                 jnp.tile(even_k_norm[0], 2).reshape(1, LANES))
    aq, ak, av, bq, bk, bv, xq, gate = _norm_proj(
        x2, even_norm[0], w_in0, (gw, 128, 128, gw, 128, 128, xw, D_MODEL), TM_PROJ,
        gate_last=True, qk_rope=(3, 4), rope_args=rope_args, seq=seq)
    y_a = _window_attn(aq, ak, av, bias_win, even_sink[0] * LOG2E, batch, seq, WINDOW_GROUP)
    y_b = _dense_attn(bq, bk, bv, batch, seq, TQ_DENSE)
    y_x = _cross_attn(xq, mk0, mv0, batch, seq, TQ_CROSS)
    lam_init = 0.8 - 0.6 * math.exp(-0.3 * 1)
    cw = C_HEADS * HEAD_DIM
    h1, q1, q2, k1, k2, v, xq1, gate1 = _gate_out(
        [y_a, y_b, y_x], gate, w_out0, x2, odd_norm[0], TM_PROJ, w_next=w_in1,
        splits=(cw, cw, cw, cw, C_HEADS * C_V_DIM, xw, D_MODEL))
    lam_vecs = jnp.stack([odd_lambda_q1[0], odd_lambda_k1[0], odd_lambda_q2[0], odd_lambda_k2[0]])
    bias_max = jnp.broadcast_to(head_bias_max.reshape(C_HEADS, 1, 1), (C_HEADS, 1, LANES))
    y_c = _diff_attn(q1, q2, k1, k2, v, bias_near, bias_max, lam_vecs, odd_subln[0], lam_init,
                     batch, seq, TQ_DIFF)
    y_x1 = _cross_attn(xq1, mk1, mv1, batch, seq, TQ_CROSS)
    (out,) = _gate_out([y_c, y_x1], gate1, w_out1, h1, final_norm, TM_PROJ)
    return out.reshape(batch, seq, d)
```

```python
import functools
import math

import numpy as np
import jax
import jax.numpy as jnp
from jax import lax
from jax.experimental import pallas as pl
from jax.experimental.pallas import tpu as pltpu

D_MODEL = 1024
HEAD_DIM = 64
BLOCK = 128
WINDOW = 128
GRID_W = 64
A_HEADS = 6
B_HEADS = 6
C_HEADS = 6
C_V_DIM = 128
X_HEADS = 4
REL_BUCKETS = 32
REL_MAX_DIST = 128
ROPE_THETA = 10000.0
EPS = 1e-6
NEG_INF = -1e30
LOG2E = math.log2(math.e)
LANES = 128
ONES_ROWS = 16
PROJ_SUB_ROWS = 512
GATE_SUB_ROWS = 1024
PROJ_COL_GROUP = 512
TM_PROJ = 1024
TM_MEM = 256
TQ_DENSE = 256
TQ_DIFF = 512
TQ_CROSS = 2048
WINDOW_GROUP = 16
SUM_FLOOR = 2.0 ** -64
SHIFT_MARGIN = 2.0 ** -7
BIAS_FAR = 2
VMEM_LIMIT = 56 * 1024 * 1024

F32 = jnp.float32
BF16 = jnp.bfloat16


def _cparams(n_axes, flags=None):
    return pltpu.CompilerParams(dimension_semantics=("arbitrary",) * n_axes,
                                vmem_limit_bytes=VMEM_LIMIT, flags=flags)


def _lane_lo(shape):
    return lax.broadcasted_iota(jnp.int32, shape, len(shape) - 1) < HEAD_DIM


def _rmsnorm_rows(x, g):
    ms = jnp.mean(x * x, axis=-1, keepdims=True)
    return x * lax.rsqrt(ms + EPS) * g


def _nt_dot(a, b):
    return lax.dot_general(a, b, (((1,), (1,)), ((), ())), preferred_element_type=F32)


def _norm_proj_kernel(*refs, qk_rope, gate_last):
    x_ref, g_ref, w_ref = refs[:3]
    if qk_rope is None:
        out_refs = refs[3:]
    else:
        cos_ref, sin_ref, gq_ref, gk_ref = refs[3:7]
        out_refs = refs[7:]
    tm = x_ref.shape[0]
    for r0 in range(0, tm, PROJ_SUB_ROWS):
        rows = slice(r0, min(r0 + PROJ_SUB_ROWS, tm))

        def post(j, y):
            if qk_rope is not None and j in qk_rope:
                gain, scale = ((gq_ref, HEAD_DIM ** -0.5 * LOG2E) if j == qk_rope[0]
                               else (gk_ref, 1.0))
                y = jnp.concatenate(
                    [_norm_rope_pair(y[:, t * LANES:(t + 1) * LANES], gain[...],
                                     cos_ref[rows, :], sin_ref[rows, :]) * scale
                     for t in range(y.shape[1] // LANES)], axis=1)
            if gate_last and j == len(out_refs) - 1:
                y = _silu(y)
            return y

        xn = _rmsnorm_rows(x_ref[rows, :], g_ref[...]).astype(BF16)
        _project_and_store(xn, w_ref, out_refs, rows, post)


def _project_and_store(xn, w_ref, out_refs, rows, post):
    n = w_ref.shape[1]
    starts = [0]
    for o_ref in out_refs:
        starts.append(starts[-1] + o_ref.shape[1])
    assert starts[-1] == n
    for c0 in range(0, n, PROJ_COL_GROUP):
        c1 = min(c0 + PROJ_COL_GROUP, n)
        y = jnp.dot(xn, w_ref[:, c0:c1], preferred_element_type=F32)
        for j, o_ref in enumerate(out_refs):
            lo, hi = max(c0, starts[j]), min(c1, starts[j + 1])
            if lo < hi:
                o_ref[rows, lo - starts[j]:hi - starts[j]] = post(
                    j, y[:, lo - c0:hi - c0]).astype(o_ref.dtype)


def _silu(y):
    return y * (1.0 / (1.0 + jnp.exp(-y)))


def _norm_proj(x, g, w_bf16, splits, tm, gate_last, qk_rope=None, rope_args=(), seq=None):
    rows, d = x.shape
    n = w_bf16.shape[1]
    assert sum(splits) == n and rows % tm == 0
    in_specs = [pl.BlockSpec((tm, d), lambda i: (i, 0)),
                pl.BlockSpec((1, d), lambda i: (0, 0)),
                pl.BlockSpec((d, n), lambda i: (0, 0))]
    if qk_rope is not None:
        pos_blocks = seq // tm
        in_specs += [pl.BlockSpec((tm, LANES), lambda i: (i % pos_blocks, 0)),
                     pl.BlockSpec((tm, LANES), lambda i: (i % pos_blocks, 0)),
                     pl.BlockSpec((1, LANES), lambda i: (0, 0)),
                     pl.BlockSpec((1, LANES), lambda i: (0, 0))]
    return pl.pallas_call(
        functools.partial(_norm_proj_kernel, qk_rope=qk_rope, gate_last=gate_last),
        grid=(rows // tm,),
        in_specs=in_specs,
        out_specs=[pl.BlockSpec((tm, s), lambda i: (i, 0)) for s in splits],
        out_shape=[jax.ShapeDtypeStruct((rows, s), BF16) for s in splits],
        compiler_params=_cparams(1),
        name="norm_proj",
    )(x, g.reshape(1, d), w_bf16, *rope_args)


def _gate_out_kernel(*refs, n_parts, after):
    y_refs = refs[:n_parts]
    gate_ref, w_ref, res_ref, g_ref = refs[n_parts:n_parts + 4]
    if after == "proj":
        w_next_ref, h_ref = refs[n_parts + 4:n_parts + 6]
        out_refs = refs[n_parts + 6:]
    else:
        h_ref = refs[n_parts + 4]
    tm = res_ref.shape[0]
    for r0 in range(0, tm, GATE_SUB_ROWS):
        rows = slice(r0, min(r0 + GATE_SUB_ROWS, tm))
        parts = []
        c0 = 0
        for y_ref in y_refs:
            width = y_ref.shape[1]
            parts.append(y_ref[rows, :] * gate_ref[rows, c0:c0 + width])
            c0 += width
        h = res_ref[rows, :] + jnp.dot(jnp.concatenate(parts, axis=1), w_ref[...],
                                       preferred_element_type=F32)
        if after == "norm":
            h_ref[rows, :] = _rmsnorm_rows(h, g_ref[...])
            continue
        h_ref[rows, :] = h
        gate_path = len(out_refs) - 1
        _project_and_store(_rmsnorm_rows(h, g_ref[...]).astype(BF16), w_next_ref, out_refs, rows,
                           lambda j, y: _silu(y) if j == gate_path else y)


def _gate_out(y_parts, gate, w_bf16, resid, norm_g, tm, w_next=None, splits=()):
    rows, d = resid.shape
    mix = w_bf16.shape[0]
    after = "norm" if w_next is None else "proj"
    row_block = lambda width: pl.BlockSpec((tm, width), lambda i: (i, 0))
    whole = lambda a: pl.BlockSpec(a.shape, lambda i: (0, 0))
    in_specs = [row_block(y.shape[1]) for y in y_parts]
    in_specs += [row_block(mix), whole(w_bf16), row_block(d), pl.BlockSpec((1, d), lambda i: (0, 0))]
    args = list(y_parts) + [gate, w_bf16, resid, norm_g.reshape(1, d)]
    out_specs = [row_block(d)]
    out_shape = [jax.ShapeDtypeStruct((rows, d), F32)]
    if after == "proj":
        assert sum(splits) == w_next.shape[1]
        in_specs.append(whole(w_next))
        args.append(w_next)
        out_specs += [row_block(s) for s in splits]
        out_shape += [jax.ShapeDtypeStruct((rows, s), BF16) for s in splits]
    return pl.pallas_call(
        functools.partial(_gate_out_kernel, n_parts=len(y_parts), after=after),
        grid=(rows // tm,),
        in_specs=in_specs,
        out_specs=out_specs,
        out_shape=out_shape,
        compiler_params=_cparams(1),
        name="gate_out",
    )(*args)


def _bucket_thresholds():
    nb = REL_BUCKETS // 2
    max_exact = nb // 2
    n = np.arange(0, 4 * REL_MAX_DIST)
    nf = np.maximum(n, 1).astype(np.float32)
    large = max_exact + (np.log(nf / np.float32(max_exact))
                         / np.float32(math.log(REL_MAX_DIST / max_exact))
                         * np.float32(nb - max_exact)).astype(np.int32)
    bucket = np.where(n < max_exact, n, np.minimum(large, nb - 1))
    assert np.all(np.diff(bucket) >= 0) and bucket[0] == 0 and bucket[-1] == nb - 1
    thr = [int(np.argmax(bucket >= k)) for k in range(1, nb)]
    assert thr[-1] < REL_MAX_DIST
    return thr


def _bias_tiles_kernel(tab_ref, near_ref, win_ref):
    h = pl.program_id(0)
    thr = _bucket_thresholds()
    nb = REL_BUCKETS // 2
    row = lax.broadcasted_iota(jnp.int32, (BLOCK, BLOCK), 0)
    col = lax.broadcasted_iota(jnp.int32, (BLOCK, BLOCK), 1)
    def tile_of(rel):
        n = jnp.abs(rel)
        neg = jnp.full((BLOCK, BLOCK), tab_ref[0, h], F32)
        pos = jnp.full((BLOCK, BLOCK), tab_ref[nb, h], F32)
        for b in range(1, nb):
            ge = n >= thr[b - 1]
            neg = jnp.where(ge, tab_ref[b, h], neg)
            pos = jnp.where(ge, tab_ref[nb + b, h], pos)
        return jnp.where(rel > 0, pos, neg) * LOG2E

    for k in range(2 * BIAS_FAR + 1):
        rel = (k - BIAS_FAR) * BLOCK + row - col
        tile = tile_of(rel)
        near_ref[0, k] = tile
        if abs(k - BIAS_FAR) <= 1:
            j = k - BIAS_FAR + 1
            win_ref[j * BLOCK:(j + 1) * BLOCK, :] = jnp.where(jnp.abs(rel) <= WINDOW, tile, NEG_INF)


def _bias_tiles(rel_bias):
    assert _bucket_thresholds()[-1] <= (BIAS_FAR - 1) * BLOCK + 1
    n_tiles = 2 * BIAS_FAR + 1
    n_heads = rel_bias.shape[1]
    return pl.pallas_call(
        _bias_tiles_kernel,
        grid=(n_heads,),
        in_specs=[pl.BlockSpec(memory_space=pltpu.SMEM)],
        out_specs=[pl.BlockSpec((1, n_tiles, BLOCK, BLOCK), lambda h: (h, 0, 0, 0)),
                   pl.BlockSpec((3 * BLOCK, BLOCK), lambda h: (0, h))],
        out_shape=[jax.ShapeDtypeStruct((n_heads, n_tiles, BLOCK, BLOCK), F32),
                   jax.ShapeDtypeStruct((3 * BLOCK, n_heads * BLOCK), F32)],
        compiler_params=_cparams(1),
        name="bias_tiles",
    )(rel_bias)


def _stack_gqa_heads(q_tiles):
    lo = _lane_lo(q_tiles[0].shape)
    swap = lambda q: pltpu.roll(q, HEAD_DIM, 1)
    heads_lo = [q_tiles[0], swap(q_tiles[0]), q_tiles[1]]
    heads_hi = [q_tiles[1], swap(q_tiles[2]), q_tiles[2]]
    return jnp.concatenate([jnp.where(lo, q, 0.0).astype(BF16) for q in heads_lo]
                           + [jnp.where(lo, 0.0, q).astype(BF16) for q in heads_hi], axis=0)


def _unstack_gqa_heads(out_lo, out_hi, width):
    blk = lambda o, j: o[:, j * width:(j + 1) * width]
    return [jnp.concatenate([blk(out_lo, 0), blk(out_lo, 1)], axis=0),
            jnp.concatenate([blk(out_lo, 2), blk(out_hi, 0)], axis=0),
            jnp.concatenate([blk(out_hi, 1), blk(out_hi, 2)], axis=0)]


def _stack_pair_heads(q):
    lo = _lane_lo(q.shape)
    return jnp.concatenate([jnp.where(lo, q, 0.0).astype(BF16),
                            jnp.where(lo, 0.0, q).astype(BF16)], axis=0)


def _with_shift_lane(lhs, kmax_lo, kmax_hi, bias_max_lo, bias_max_hi):
    half = lhs.shape[0] // 2
    lf = lhs.astype(F32)
    q_sq = jnp.sum(lf * lf, axis=-1, keepdims=True)
    q_norm = q_sq * lax.rsqrt(jnp.maximum(q_sq, 1e-30))
    row = lax.broadcasted_iota(jnp.int32, lhs.shape, 0)
    lane = lax.broadcasted_iota(jnp.int32, lhs.shape, 1)
    lo_row = row < half
    shift = (q_norm * jnp.where(lo_row, kmax_lo, kmax_hi)
             + jnp.where(lo_row, bias_max_lo, bias_max_hi))
    shift = shift + jnp.abs(shift) * SHIFT_MARGIN
    slot_lane = jnp.where(lo_row, HEAD_DIM, 0)
    return jnp.where(lane == slot_lane, -shift, lf).astype(BF16)


def _keys_with_ones(k_pair):
    lane = lax.broadcasted_iota(jnp.int32, k_pair.shape, 1)
    kf = k_pair.astype(F32)
    k_lo = jnp.where(lane < HEAD_DIM, kf, jnp.where(lane == HEAD_DIM, 1.0, 0.0))
    k_hi = jnp.where(lane >= HEAD_DIM, kf, jnp.where(lane == 0, 1.0, 0.0))
    return k_lo.astype(BF16), k_hi.astype(BF16)


def _shifted_scores(ka_lo, ka_hi, lhs_shifted):
    half = lhs_shifted.shape[0] // 2
    return jnp.concatenate([_nt_dot(ka_lo, lhs_shifted[:half]), _nt_dot(ka_hi, lhs_shifted[half:])],
                           axis=1)


def _pair_key_norm_max(k_pair):
    kf = k_pair.astype(F32)
    row = lax.broadcasted_iota(jnp.int32, (LANES, 2 * LANES), 0)
    col = lax.broadcasted_iota(jnp.int32, (LANES, 2 * LANES), 1)
    half_sel = jnp.where((row < HEAD_DIM) == (col < LANES), 1.0, 0.0).astype(BF16)
    sq = jnp.dot((kf * kf).astype(BF16), half_sel, preferred_element_type=F32)
    norm_max = jnp.sqrt(jnp.max(sq, axis=0, keepdims=True))
    return norm_max[:, :LANES], norm_max[:, LANES:]


def _window_attn_kernel(q_ref, k_ref, v_ref, bias_ref, sink_row_ref,
                        o_ref, kpad_scr, vtpad_scr, *, n_blocks, group):
    i = pl.program_id(1)
    half = 3 * BLOCK

    @pl.when(i == 0)
    def _():
        zero_blk = jnp.zeros((1, BLOCK, LANES), BF16)
        kpad_scr[0:1] = zero_blk
        kpad_scr[n_blocks + 1:n_blocks + 2] = zero_blk
        kpad_scr[1:n_blocks + 1] = k_ref[...].reshape(n_blocks, BLOCK, LANES)
        v_t = v_ref[...].astype(F32).T.astype(BF16)
        ones = jnp.ones((ONES_ROWS, BLOCK), BF16)
        zero_v = jnp.zeros((HEAD_DIM + ONES_ROWS, BLOCK), BF16)
        for g in range(2):
            vtpad_scr[g, 0] = zero_v
            vtpad_scr[g, n_blocks + 1] = zero_v
            for blk in range(n_blocks):
                vtpad_scr[g, blk + 1, :HEAD_DIM, :] = v_t[g * HEAD_DIM:(g + 1) * HEAD_DIM,
                                                           blk * BLOCK:(blk + 1) * BLOCK]
                vtpad_scr[g, blk + 1, HEAD_DIM:, :] = ones

    sink_row = sink_row_ref[...]

    for u in range(group):
        n = i * group + u
        qf = q_ref[u * BLOCK:(u + 1) * BLOCK, :].astype(F32) * (HEAD_DIM ** -0.5 * LOG2E)
        lhs = _stack_gqa_heads([qf[:, j * LANES:(j + 1) * LANES] for j in range(3)])
        k_win = kpad_scr[pl.ds(n, 3)].reshape(3 * BLOCK, LANES)
        s = _nt_dot(k_win, lhs) + bias_ref[...]
        s = jnp.concatenate([jnp.where(n > 0, s[:BLOCK], NEG_INF), s[BLOCK:2 * BLOCK],
                             jnp.where(n < n_blocks - 1, s[2 * BLOCK:], NEG_INF)], axis=0)
        m = jnp.maximum(jnp.max(s, axis=0, keepdims=True), sink_row)
        p = jnp.exp2(s - m).astype(BF16)
        p_sink = jnp.exp2(sink_row - m)
        outs = []
        for g in range(2):
            acc = None
            for j in range(3):
                pv = jnp.dot(vtpad_scr[g, n + j], p[j * BLOCK:(j + 1) * BLOCK, g * half:(g + 1) * half],
                             preferred_element_type=F32)
                acc = pv if acc is None else acc + pv
            total = acc[HEAD_DIM:HEAD_DIM + 1] + p_sink[:, g * half:(g + 1) * half]
            outs.append(acc[:HEAD_DIM] * (1.0 / total))
        for j, pair_t in enumerate(_unstack_gqa_heads(outs[0], outs[1], BLOCK)):
            o_ref[u * BLOCK:(u + 1) * BLOCK, j * LANES:(j + 1) * LANES] = pair_t.T.astype(o_ref.dtype)


def _window_attn(q, k, v, bias_win_t, sink, batch, seq, group):
    nb = seq // BLOCK
    steps = nb // group
    rows = A_HEADS * BLOCK
    sink_row = jnp.repeat(sink, BLOCK).reshape(1, rows)
    const = lambda shape: pl.BlockSpec(shape, lambda b, i: (0,) * len(shape))
    return pl.pallas_call(
        functools.partial(_window_attn_kernel, n_blocks=nb, group=group),
        grid=(batch, steps),
        in_specs=[pl.BlockSpec((group * BLOCK, 3 * LANES), lambda b, i: (b * steps + i, 0)),
                  pl.BlockSpec((seq, LANES), lambda b, i: (b, 0)),
                  pl.BlockSpec((seq, LANES), lambda b, i: (b, 0)),
                  const(bias_win_t.shape), const((1, rows))],
        out_specs=pl.BlockSpec((group * BLOCK, 3 * LANES), lambda b, i: (b * steps + i, 0)),
        out_shape=jax.ShapeDtypeStruct(q.shape, BF16),
        scratch_shapes=[pltpu.VMEM((nb + 2, BLOCK, LANES), BF16),
                        pltpu.VMEM((2, nb + 2, HEAD_DIM + ONES_ROWS, BLOCK), BF16)],
        compiler_params=_cparams(2),
        name="window_attn",
    )(q, k, v, bias_win_t, sink_row)


def _norm_rope_pair(x, g, cos, sin_signed):
    lo = _lane_lo(x.shape)
    x2 = x * x
    ss_lo = jnp.sum(jnp.where(lo, x2, 0.0), axis=-1, keepdims=True)
    ss_hi = jnp.sum(jnp.where(lo, 0.0, x2), axis=-1, keepdims=True)
    ms = jnp.where(lo, ss_lo, ss_hi) * (1.0 / HEAD_DIM)
    y = x * lax.rsqrt(ms + EPS) * g
    lane = lax.broadcasted_iota(jnp.int32, x.shape, 1)
    quarter = HEAD_DIM // 4
    first = (lane & quarter) == 0
    partner = jnp.where(first, pltpu.roll(y, LANES - quarter, 1), pltpu.roll(y, quarter, 1))
    return y * cos + partner * sin_signed


def _dense_attn_kernel(q_ref, k_ref, v_ref, o_ref, ka_scr, vt_scr, kmax_scr, *, tq):
    seq = k_ref.shape[0]
    kc = 2 * BLOCK
    n_chunks = seq // kc
    grp = 3 * tq

    ka_scr[0], ka_scr[1] = _keys_with_ones(k_ref[...])
    kmax_lo, kmax_hi = _pair_key_norm_max(k_ref[...])
    kmax_scr[0:1, :] = kmax_lo
    kmax_scr[1:2, :] = kmax_hi
    v_t = v_ref[...].astype(F32).T.astype(BF16)
    for g in range(2):
        vt_scr[g, :HEAD_DIM, :] = v_t[g * HEAD_DIM:(g + 1) * HEAD_DIM]
        vt_scr[g, HEAD_DIM:, :] = jnp.ones((ONES_ROWS, seq), BF16)

    def query_tile(i, carry):
        rows = pl.ds(pl.multiple_of(i * tq, tq), tq)
        lhs = _stack_gqa_heads([q_ref[rows, j * LANES:(j + 1) * LANES].astype(F32)
                                for j in range(3)])

        def write(acc):
            outs = [a[:HEAD_DIM] * (1.0 / a[HEAD_DIM:HEAD_DIM + 1]) for a in acc]
            for j, pair_t in enumerate(_unstack_gqa_heads(outs[0], outs[1], tq)):
                o_ref[rows, j * LANES:(j + 1) * LANES] = pair_t.T.astype(o_ref.dtype)

        s_all = _shifted_scores(ka_scr[0], ka_scr[1], _with_shift_lane(
            lhs, kmax_scr[0:1, :], kmax_scr[1:2, :], 0.0, 0.0))
        p = jnp.exp2(s_all).astype(BF16)
        acc = [jnp.dot(vt_scr[g], p[:, g * grp:(g + 1) * grp], preferred_element_type=F32)
               for g in range(2)]
        write(acc)
        sums = jnp.concatenate([a[HEAD_DIM:HEAD_DIM + 1] for a in acc], axis=0)

        @pl.when(jnp.logical_not(jnp.min(sums) >= SUM_FLOOR))
        def _():
            s_exact = _nt_dot(k_ref[...], lhs)
            m = None
            acc2 = [None, None]
            for c in range(n_chunks):
                s = s_exact[c * kc:(c + 1) * kc, :]
                m_c = jnp.max(s, axis=0, keepdims=True)
                m_new = m_c if m is None else jnp.maximum(m, m_c)
                pc = jnp.exp2(s - m_new).astype(BF16)
                if m is not None:
                    alpha = jnp.exp2(m - m_new)
                for g in range(2):
                    cols = slice(g * grp, (g + 1) * grp)
                    pv = jnp.dot(vt_scr[g, :, c * kc:(c + 1) * kc], pc[:, cols],
                                 preferred_element_type=F32)
                    acc2[g] = pv if m is None else acc2[g] * alpha[:, cols] + pv
                m = m_new
            write(acc2)

        return carry

    lax.fori_loop(0, seq // tq, query_tile, 0)


def _dense_attn(q, k, v, batch, seq, tq):
    return pl.pallas_call(
        functools.partial(_dense_attn_kernel, tq=tq),
        grid=(batch,),
        in_specs=[pl.BlockSpec((seq, 3 * LANES), lambda b: (b, 0)),
                  pl.BlockSpec((seq, LANES), lambda b: (b, 0)),
                  pl.BlockSpec((seq, LANES), lambda b: (b, 0))],
        out_specs=pl.BlockSpec((seq, 3 * LANES), lambda b: (b, 0)),
        out_shape=jax.ShapeDtypeStruct(q.shape, BF16),
        scratch_shapes=[pltpu.VMEM((2, seq, LANES), BF16),
                        pltpu.VMEM((2, HEAD_DIM + ONES_ROWS, seq), BF16),
                        pltpu.VMEM((2, LANES), F32)],
        compiler_params=_cparams(1),
        name="dense_attn",
    )(q, k, v)


def _cross_attn_kernel(q_ref, mk_ref, mv_ref, o_ref, vt_scr, *, tq):
    i = pl.program_id(1)
    n_pairs = q_ref.shape[1] // LANES
    mem_len = mk_ref.shape[0]

    @pl.when(i == 0)
    def _():
        v_t = mv_ref[...].astype(F32).T.astype(BF16)
        for h in range(2 * n_pairs):
            vt_scr[h, :HEAD_DIM, :] = v_t[h * HEAD_DIM:(h + 1) * HEAD_DIM]
            vt_scr[h, HEAD_DIM:, :] = jnp.ones((ONES_ROWS, mem_len), BF16)

    for j in range(n_pairs):
        cols = slice(j * LANES, (j + 1) * LANES)
        lhs = _stack_pair_heads(q_ref[:, cols].astype(F32) * (HEAD_DIM ** -0.5 * LOG2E))
        s = _nt_dot(mk_ref[:, cols], lhs)
        p = jnp.exp2(s - jnp.max(s, axis=0, keepdims=True)).astype(BF16)
        acc = [jnp.dot(vt_scr[2 * j + h], p[:, h * tq:(h + 1) * tq], preferred_element_type=F32)
               for h in range(2)]
        pair_t = jnp.concatenate([a[:HEAD_DIM] * (1.0 / a[HEAD_DIM:HEAD_DIM + 1]) for a in acc],
                                 axis=0)
        o_ref[:, cols] = pair_t.T.astype(o_ref.dtype)


def _cross_attn(q, mk, mv, batch, seq, tq):
    nq = seq // tq
    mem_len = mk.shape[0] // batch
    width = q.shape[1]
    n_pairs = width // LANES
    return pl.pallas_call(
        functools.partial(_cross_attn_kernel, tq=tq),
        grid=(batch, nq),
        in_specs=[pl.BlockSpec((tq, width), lambda b, i: (b * nq + i, 0)),
                  pl.BlockSpec((mem_len, width), lambda b, i: (b, 0)),
                  pl.BlockSpec((mem_len, width), lambda b, i: (b, 0))],
        out_specs=pl.BlockSpec((tq, width), lambda b, i: (b * nq + i, 0)),
        out_shape=jax.ShapeDtypeStruct(q.shape, BF16),
        scratch_shapes=[pltpu.VMEM((2 * n_pairs, HEAD_DIM + ONES_ROWS, mem_len), BF16)],
        compiler_params=_cparams(2),
        name="cross_attn",
    )(q, mk, mv)


def _diff_attn_kernel(q1_ref, q2_ref, k1_ref, k2_ref, v_ref, bias_ref, bmax_ref, lam_ref, g_ref,
                      o_ref, vt_scr, ka_scr, kmax_scr, *, tq, lam_init):
    sub = tq // BLOCK
    seq = k1_ref.shape[0]
    kc = 2 * BLOCK
    n_chunks = seq // kc
    k_refs = (k1_ref, k2_ref)

    for head in range(2):
        v_t = v_ref[:, head * C_V_DIM:(head + 1) * C_V_DIM].astype(F32).T
        vt_scr[head, :C_V_DIM, :] = v_t.astype(BF16)
        vt_scr[head, C_V_DIM:, :] = jnp.ones((ONES_ROWS, seq), BF16)
    for t in range(2):
        ka_scr[t, 0], ka_scr[t, 1] = _keys_with_ones(k_refs[t][...])
        kmax_lo, kmax_hi = _pair_key_norm_max(k_refs[t][...])
        kmax_scr[t, 0:1, :] = kmax_lo
        kmax_scr[t, 1:2, :] = kmax_hi

    lam_vec = lam_ref[...]
    lam = (jnp.exp(jnp.sum(lam_vec[0:1] * lam_vec[1:2], axis=-1, keepdims=True))
           - jnp.exp(jnp.sum(lam_vec[2:3] * lam_vec[3:4], axis=-1, keepdims=True)) + lam_init)

    def query_tile(i, carry):
        rows = pl.ds(pl.multiple_of(i * tq, tq), tq)
        lhs = [_stack_pair_heads(q_ref[rows, :].astype(F32) * (HEAD_DIM ** -0.5 * LOG2E))
               for q_ref in (q1_ref, q2_ref)]
        _diff_attn_tile(i, rows, lhs, lam, bias_ref, bmax_ref, g_ref, o_ref, vt_scr, ka_scr,
                        kmax_scr, k_refs, tq=tq, sub=sub, kc=kc, n_chunks=n_chunks,
                        lam_init=lam_init)
        return carry

    lax.fori_loop(0, seq // tq, query_tile, 0)


def _diff_attn_tile(i, rows, lhs, lam, bias_ref, bmax_ref, g_ref, o_ref, vt_scr, ka_scr, kmax_scr,
                    k_refs, *, tq, sub, kc, n_chunks, lam_init):
    def bias_chunk(c):
        def tile(head, u, e):
            offset = (2 * c + e) - (i * sub + u)
            return bias_ref[head, jnp.clip(offset, -BIAS_FAR, BIAS_FAR) + BIAS_FAR]
        return jnp.concatenate(
            [jnp.concatenate([tile(head, u, e) for head in range(2) for u in range(sub)], axis=1)
             for e in range(2)], axis=0)

    def shifted_scores(t):
        return _shifted_scores(ka_scr[t, 0], ka_scr[t, 1], _with_shift_lane(
            lhs[t], kmax_scr[t, 0:1, :], kmax_scr[t, 1:2, :], bmax_ref[0], bmax_ref[1]))

    def pv_of_shifted(s_all):
        p = jnp.concatenate([jnp.exp2(s_all[c * kc:(c + 1) * kc, :] + bias_chunk(c)).astype(BF16)
                             for c in range(n_chunks)], axis=0)
        return [jnp.dot(vt_scr[head], p[:, head * tq:(head + 1) * tq],
                        preferred_element_type=F32) for head in range(2)]

    def pv_running_max(s_all):
        m = None
        acc = [None, None]
        for c in range(n_chunks):
            s = s_all[c * kc:(c + 1) * kc, :] + bias_chunk(c)
            m_c = jnp.max(s, axis=0, keepdims=True)
            m_new = m_c if m is None else jnp.maximum(m, m_c)
            p = jnp.exp2(s - m_new).astype(BF16)
            if m is not None:
                alpha = jnp.exp2(m - m_new)
            for head in range(2):
                cols = slice(head * tq, (head + 1) * tq)
                pv = jnp.dot(vt_scr[head, :, c * kc:(c + 1) * kc], p[:, cols],
                             preferred_element_type=F32)
                acc[head] = pv if m is None else acc[head] * alpha[:, cols] + pv
            m = m_new
        return acc

    def write(acc1, acc2):
        g = g_ref[...] * (1.0 - lam_init)
        for head in range(2):
            o1 = acc1[head][:C_V_DIM] * (1.0 / acc1[head][C_V_DIM:C_V_DIM + 1])
            o2 = acc2[head][:C_V_DIM] * (1.0 / acc2[head][C_V_DIM:C_V_DIM + 1])
            out = o1 - lam * o2
            ms = jnp.mean(out * out, axis=0, keepdims=True)
            y = out * lax.rsqrt(ms + EPS) * g
            o_ref[rows, head * C_V_DIM:(head + 1) * C_V_DIM] = y.T.astype(o_ref.dtype)

    s1 = shifted_scores(0)
    s2 = shifted_scores(1)
    acc1 = pv_of_shifted(s1)
    acc2 = pv_of_shifted(s2)
    write(acc1, acc2)
    sums = jnp.concatenate([a[C_V_DIM:C_V_DIM + 1] for a in acc1 + acc2], axis=0)

    @pl.when(jnp.logical_not(jnp.min(sums) >= SUM_FLOOR))
    def _():
        write(pv_running_max(_nt_dot(k_refs[0][...], lhs[0])),
              pv_running_max(_nt_dot(k_refs[1][...], lhs[1])))


def _diff_attn(q1, q2, k1, k2, v, bias_t, bias_max, lam_vecs, subln_g, lam_init, batch, seq, tq):
    n_pairs = C_HEADS // 2
    n_tiles = bias_t.shape[1]
    seq_spec = pl.BlockSpec((seq, LANES), lambda b, p: (b, p))
    g_cols = jnp.broadcast_to(subln_g.reshape(C_V_DIM, 1), (C_V_DIM, tq))
    return pl.pallas_call(
        functools.partial(_diff_attn_kernel, tq=tq, lam_init=lam_init),
        grid=(batch, n_pairs),
        in_specs=[seq_spec, seq_spec, seq_spec, seq_spec,
                  pl.BlockSpec((seq, 2 * C_V_DIM), lambda b, p: (b, p)),
                  pl.BlockSpec((2, n_tiles, BLOCK, BLOCK), lambda b, p: (p, 0, 0, 0)),
                  pl.BlockSpec((2, 1, LANES), lambda b, p: (p, 0, 0)),
                  pl.BlockSpec(lam_vecs.shape, lambda b, p: (0, 0)),
                  pl.BlockSpec((C_V_DIM, tq), lambda b, p: (0, 0))],
        out_specs=pl.BlockSpec((seq, 2 * C_V_DIM), lambda b, p: (b, p)),
        out_shape=jax.ShapeDtypeStruct(v.shape, BF16),
        scratch_shapes=[pltpu.VMEM((2, C_V_DIM + ONES_ROWS, seq), BF16),
                        pltpu.VMEM((2, 2, seq, LANES), BF16),
                        pltpu.VMEM((2, 2, LANES), F32)],
        compiler_params=_cparams(2),
        name="diff_attn",
    )(q1, q2, k1, k2, v, bias_t, bias_max, lam_vecs, g_cols)


def _rope_tables(seq):
    rows = seq // GRID_W
    row = jnp.broadcast_to(jnp.arange(rows)[:, None], (rows, GRID_W)).reshape(-1)
    col = jnp.broadcast_to(jnp.arange(GRID_W)[None, :], (rows, GRID_W)).reshape(-1)
    half = HEAD_DIM // 2
    inv = 1.0 / (ROPE_THETA ** (jnp.arange(0, half, 2, dtype=F32) / half))
    ang_row = row.astype(F32)[:, None] * inv
    ang_col = col.astype(F32)[:, None] * inv
    cos = jnp.concatenate([jnp.cos(ang_row)] * 2 + [jnp.cos(ang_col)] * 2, axis=-1)
    sin = jnp.concatenate([-jnp.sin(ang_row), jnp.sin(ang_row),
                           -jnp.sin(ang_col), jnp.sin(ang_col)], axis=-1)
    return jnp.tile(cos, (1, 2)), jnp.tile(sin, (1, 2))


def kernel(x, mem, rel_bias, mem_norm, final_norm, even_norm, even_w_in, even_sink, even_q_norm, even_k_norm, even_w_mem_kv, even_w_out, odd_norm, odd_w_in, odd_lambda_q1, odd_lambda_k1, odd_lambda_q2, odd_lambda_k2, odd_subln, odd_w_mem_kv, odd_w_out):
    batch, seq, d = x.shape
    mem_len = mem.shape[1]
    tokens = batch * seq
    xw = X_HEADS * HEAD_DIM
    gw = A_HEADS * HEAD_DIM

    w_in0 = even_w_in[0].astype(BF16)
    w_out0 = even_w_out[0].astype(BF16)
    w_in1 = odd_w_in[0].astype(BF16)
    w_out1 = odd_w_out[0].astype(BF16)

    x2 = x.reshape(tokens, d)
    mem2 = mem.reshape(batch * mem_len, d)

    bias_near, bias_win = _bias_tiles(rel_bias)
    cos, sin_signed = _rope_tables(seq)
    head_bias_max = jnp.max(rel_bias, axis=0) * LOG2E

    w_mem = jnp.concatenate([even_w_mem_kv[0], odd_w_mem_kv[0]], axis=1).astype(BF16)
    mk0, mv0, mk1, mv1 = _norm_proj(mem2, mem_norm, w_mem, (xw,) * 4, TM_MEM, gate_last=False)

    rope_args = (cos, sin_signed, jnp.tile(even_q_norm[0], 2).reshape(1, LANES),
                 jnp.tile(even_k_norm[0], 2).reshape(1, LANES))
    aq, ak, av, bq, bk, bv, xq, gate = _norm_proj(
        x2, even_norm[0], w_in0, (gw, 128, 128, gw, 128, 128, xw, D_MODEL), TM_PROJ,
        gate_last=True, qk_rope=(3, 4), rope_args=rope_args, seq=seq)
    y_a = _window_attn(aq, ak, av, bias_win, even_sink[0] * LOG2E, batch, seq, WINDOW_GROUP)
    y_b = _dense_attn(bq, bk, bv, batch, seq, TQ_DENSE)
    y_x = _cross_attn(xq, mk0, mv0, batch, seq, TQ_CROSS)
    lam_init = 0.8 - 0.6 * math.exp(-0.3 * 1)
    cw = C_HEADS * HEAD_DIM
    h1, q1, q2, k1, k2, v, xq1, gate1 = _gate_out(
        [y_a, y_b, y_x], gate, w_out0, x2, odd_norm[0], TM_PROJ, w_next=w_in1,
        splits=(cw, cw, cw, cw, C_HEADS * C_V_DIM, xw, D_MODEL))
    lam_vecs = jnp.stack([odd_lambda_q1[0], odd_lambda_k1[0], odd_lambda_q2[0], odd_lambda_k2[0]])
    bias_max = jnp.broadcast_to(head_bias_max.reshape(C_HEADS, 1, 1), (C_HEADS, 1, LANES))
    y_c = _diff_attn(q1, q2, k1, k2, v, bias_near, bias_max, lam_vecs, odd_subln[0], lam_init,
                     batch, seq, TQ_DIFF)
    y_x1 = _cross_attn(xq1, mk1, mv1, batch, seq, TQ_CROSS)
    (out,) = _gate_out([y_c, y_x1], gate1, w_out1, h1, final_norm, TM_PROJ)
    return out.reshape(batch, seq, d)
```

```python
import functools
import math

import numpy as np
import jax
import jax.numpy as jnp
from jax import lax
from jax.experimental import pallas as pl
from jax.experimental.pallas import tpu as pltpu

D_MODEL = 1024
HEAD_DIM = 64
BLOCK = 128
WINDOW = 128
GRID_W = 64
A_HEADS = 6
C_HEADS = 6
C_V_DIM = 128
X_HEADS = 4
REL_BUCKETS = 32
REL_MAX_DIST = 128
ROPE_THETA = 10000.0
EPS = 1e-6
NEG_INF = -1e30
LOG2E = math.log2(math.e)
LANES = 128
ONES_ROWS = 16
PROJ_SUB_ROWS = 256
GATE_SUB_ROWS = 1024
PROJ_COL_GROUP = 512
TM_PROJ = 1024
TM_MEM = 256
TQ_DENSE = 256
TQ_DIFF = 512
TQ_CROSS = 2048
WINDOW_GROUP = 16
SUM_FLOOR = 2.0 ** -64
SHIFT_MARGIN = 2.0 ** -7
BIAS_FAR = 2
VMEM_LIMIT = 56 * 1024 * 1024

F32 = jnp.float32
BF16 = jnp.bfloat16


def _cparams(n_axes):
    return pltpu.CompilerParams(dimension_semantics=("arbitrary",) * n_axes,
                                vmem_limit_bytes=VMEM_LIMIT)


def _lane_lo(shape):
    return lax.broadcasted_iota(jnp.int32, shape, len(shape) - 1) < HEAD_DIM


def _rmsnorm_rows(x, g):
    ms = jnp.mean(x * x, axis=-1, keepdims=True)
    return x * lax.rsqrt(ms + EPS) * g


def _nt_dot(a, b):
    return lax.dot_general(a, b, (((1,), (1,)), ((), ())), preferred_element_type=F32)


def _norm_proj_kernel(*refs, qk_rope, gate_last):
    x_ref, g_ref, w_ref = refs[:3]
    if qk_rope is None:
        out_refs = refs[3:]
    else:
        cos_ref, sin_ref, gq_ref, gk_ref = refs[3:7]
        out_refs = refs[7:]
    tm = x_ref.shape[0]
    for r0 in range(0, tm, PROJ_SUB_ROWS):
        rows = slice(r0, min(r0 + PROJ_SUB_ROWS, tm))

        def post(j, y):
            if qk_rope is not None and j in qk_rope:
                gain, scale = ((gq_ref, HEAD_DIM ** -0.5 * LOG2E) if j == qk_rope[0]
                               else (gk_ref, 1.0))
                y = jnp.concatenate(
                    [_norm_rope_pair(y[:, t * LANES:(t + 1) * LANES], gain[...],
                                     cos_ref[rows, :], sin_ref[rows, :]) * scale
                     for t in range(y.shape[1] // LANES)], axis=1)
            if gate_last and j == len(out_refs) - 1:
                y = _silu(y)
            return y

        xn = _rmsnorm_rows(x_ref[rows, :], g_ref[...]).astype(BF16)
        _project_and_store(xn, w_ref, out_refs, rows, post)


def _project_and_store(xn, w_ref, out_refs, rows, post):
    n = w_ref.shape[1]
    starts = [0]
    for o_ref in out_refs:
        starts.append(starts[-1] + o_ref.shape[1])
    assert starts[-1] == n
    for c0 in range(0, n, PROJ_COL_GROUP):
        c1 = min(c0 + PROJ_COL_GROUP, n)
        y = jnp.dot(xn, w_ref[:, c0:c1], preferred_element_type=F32)
        for j, o_ref in enumerate(out_refs):
            lo, hi = max(c0, starts[j]), min(c1, starts[j + 1])
            if lo < hi:
                o_ref[rows, lo - starts[j]:hi - starts[j]] = post(
                    j, y[:, lo - c0:hi - c0]).astype(o_ref.dtype)


def _silu(y):
    return y * (1.0 / (1.0 + jnp.exp(-y)))


def _norm_proj(x, g, w_bf16, splits, tm, gate_last, qk_rope=None, rope_args=(), seq=None):
    rows, d = x.shape
    n = w_bf16.shape[1]
    assert sum(splits) == n and rows % tm == 0
    in_specs = [pl.BlockSpec((tm, d), lambda i: (i, 0)),
                pl.BlockSpec((1, d), lambda i: (0, 0)),
                pl.BlockSpec((d, n), lambda i: (0, 0))]
    if qk_rope is not None:
        pos_blocks = seq // tm
        in_specs += [pl.BlockSpec((tm, LANES), lambda i: (i % pos_blocks, 0)),
                     pl.BlockSpec((tm, LANES), lambda i: (i % pos_blocks, 0)),
                     pl.BlockSpec((1, LANES), lambda i: (0, 0)),
                     pl.BlockSpec((1, LANES), lambda i: (0, 0))]
    return pl.pallas_call(
        functools.partial(_norm_proj_kernel, qk_rope=qk_rope, gate_last=gate_last),
        grid=(rows // tm,),
        in_specs=in_specs,
        out_specs=[pl.BlockSpec((tm, s), lambda i: (i, 0)) for s in splits],
        out_shape=[jax.ShapeDtypeStruct((rows, s), BF16) for s in splits],
        compiler_params=_cparams(1),
        name="norm_proj",
    )(x, g.reshape(1, d), w_bf16, *rope_args)


def _gate_out_kernel(*refs, n_parts, after):
    y_refs = refs[:n_parts]
    gate_ref, w_ref, res_ref, g_ref = refs[n_parts:n_parts + 4]
    if after == "proj":
        w_next_ref, h_ref = refs[n_parts + 4:n_parts + 6]
        out_refs = refs[n_parts + 6:]
    else:
        h_ref = refs[n_parts + 4]
    tm = res_ref.shape[0]
    for r0 in range(0, tm, GATE_SUB_ROWS):
        rows = slice(r0, min(r0 + GATE_SUB_ROWS, tm))
        parts = []
        c0 = 0
        for y_ref in y_refs:
            width = y_ref.shape[1]
            parts.append(y_ref[rows, :] * gate_ref[rows, c0:c0 + width])
            c0 += width
        h = res_ref[rows, :] + jnp.dot(jnp.concatenate(parts, axis=1), w_ref[...],
                                       preferred_element_type=F32)
        if after == "norm":
            h_ref[rows, :] = _rmsnorm_rows(h, g_ref[...])
            continue
        h_ref[rows, :] = h
        gate_path = len(out_refs) - 1
        _project_and_store(_rmsnorm_rows(h, g_ref[...]).astype(BF16), w_next_ref, out_refs, rows,
                           lambda j, y: _silu(y) if j == gate_path else y)


def _gate_out(y_parts, gate, w_bf16, resid, norm_g, tm, w_next=None, splits=()):
    rows, d = resid.shape
    mix = w_bf16.shape[0]
    after = "norm" if w_next is None else "proj"
    row_block = lambda width: pl.BlockSpec((tm, width), lambda i: (i, 0))
    whole = lambda a: pl.BlockSpec(a.shape, lambda i: (0, 0))
    in_specs = [row_block(y.shape[1]) for y in y_parts]
    in_specs += [row_block(mix), whole(w_bf16), row_block(d), pl.BlockSpec((1, d), lambda i: (0, 0))]
    args = list(y_parts) + [gate, w_bf16, resid, norm_g.reshape(1, d)]
    out_specs = [row_block(d)]
    out_shape = [jax.ShapeDtypeStruct((rows, d), F32)]
    if after == "proj":
        assert sum(splits) == w_next.shape[1]
        in_specs.append(whole(w_next))
        args.append(w_next)
        out_specs += [row_block(s) for s in splits]
        out_shape += [jax.ShapeDtypeStruct((rows, s), BF16) for s in splits]
    return pl.pallas_call(
        functools.partial(_gate_out_kernel, n_parts=len(y_parts), after=after),
        grid=(rows // tm,),
        in_specs=in_specs,
        out_specs=out_specs,
        out_shape=out_shape,
        compiler_params=_cparams(1),
        name="gate_out",
    )(*args)


def _bucket_thresholds():
    nb = REL_BUCKETS // 2
    max_exact = nb // 2
    n = np.arange(0, 4 * REL_MAX_DIST)
    nf = np.maximum(n, 1).astype(np.float32)
    large = max_exact + (np.log(nf / np.float32(max_exact))
                         / np.float32(math.log(REL_MAX_DIST / max_exact))
                         * np.float32(nb - max_exact)).astype(np.int32)
    bucket = np.where(n < max_exact, n, np.minimum(large, nb - 1))
    assert np.all(np.diff(bucket) >= 0) and bucket[0] == 0 and bucket[-1] == nb - 1
    thr = [int(np.argmax(bucket >= k)) for k in range(1, nb)]
    assert thr[-1] < REL_MAX_DIST
    return thr


def _bias_tiles_kernel(tab_ref, near_ref, win_ref):
    h = pl.program_id(0)
    thr = _bucket_thresholds()
    nb = REL_BUCKETS // 2
    row = lax.broadcasted_iota(jnp.int32, (BLOCK, BLOCK), 0)
    col = lax.broadcasted_iota(jnp.int32, (BLOCK, BLOCK), 1)
    def tile_of(rel):
        n = jnp.abs(rel)
        neg = jnp.full((BLOCK, BLOCK), tab_ref[0, h], F32)
        pos = jnp.full((BLOCK, BLOCK), tab_ref[nb, h], F32)
        for b in range(1, nb):
            ge = n >= thr[b - 1]
            neg = jnp.where(ge, tab_ref[b, h], neg)
            pos = jnp.where(ge, tab_ref[nb + b, h], pos)
        return jnp.where(rel > 0, pos, neg) * LOG2E

    for k in range(2 * BIAS_FAR + 1):
        rel = (k - BIAS_FAR) * BLOCK + row - col
        tile = tile_of(rel)
        near_ref[0, k] = tile
        if abs(k - BIAS_FAR) <= 1:
            j = k - BIAS_FAR + 1
            win_ref[j * BLOCK:(j + 1) * BLOCK, :] = jnp.where(jnp.abs(rel) <= WINDOW, tile, NEG_INF)


def _bias_tiles(rel_bias):
    assert _bucket_thresholds()[-1] <= (BIAS_FAR - 1) * BLOCK + 1
    n_tiles = 2 * BIAS_FAR + 1
    n_heads = rel_bias.shape[1]
    return pl.pallas_call(
        _bias_tiles_kernel,
        grid=(n_heads,),
        in_specs=[pl.BlockSpec(memory_space=pltpu.SMEM)],
        out_specs=[pl.BlockSpec((1, n_tiles, BLOCK, BLOCK), lambda h: (h, 0, 0, 0)),
                   pl.BlockSpec((3 * BLOCK, BLOCK), lambda h: (0, h))],
        out_shape=[jax.ShapeDtypeStruct((n_heads, n_tiles, BLOCK, BLOCK), F32),
                   jax.ShapeDtypeStruct((3 * BLOCK, n_heads * BLOCK), F32)],
        compiler_params=_cparams(1),
        name="bias_tiles",
    )(rel_bias)


def _stack_gqa_heads(q_tiles):
    lo = _lane_lo(q_tiles[0].shape)
    swap = lambda q: pltpu.roll(q, HEAD_DIM, 1)
    heads_lo = [q_tiles[0], swap(q_tiles[0]), q_tiles[1]]
    heads_hi = [q_tiles[1], swap(q_tiles[2]), q_tiles[2]]
    return jnp.concatenate([jnp.where(lo, q, 0.0).astype(BF16) for q in heads_lo]
                           + [jnp.where(lo, 0.0, q).astype(BF16) for q in heads_hi], axis=0)


def _unstack_gqa_heads(out_lo, out_hi, width):
    blk = lambda o, j: o[:, j * width:(j + 1) * width]
    return [jnp.concatenate([blk(out_lo, 0), blk(out_lo, 1)], axis=0),
            jnp.concatenate([blk(out_lo, 2), blk(out_hi, 0)], axis=0),
            jnp.concatenate([blk(out_hi, 1), blk(out_hi, 2)], axis=0)]


def _stack_pair_heads(q):
    lo = _lane_lo(q.shape)
    return jnp.concatenate([jnp.where(lo, q, 0.0).astype(BF16),
                            jnp.where(lo, 0.0, q).astype(BF16)], axis=0)


def _with_shift_lane(lhs, kmax_lo, kmax_hi, bias_max_lo, bias_max_hi):
    half = lhs.shape[0] // 2
    lf = lhs.astype(F32)
    q_sq = jnp.sum(lf * lf, axis=-1, keepdims=True)
    q_norm = q_sq * lax.rsqrt(jnp.maximum(q_sq, 1e-30))
    row = lax.broadcasted_iota(jnp.int32, lhs.shape, 0)
    lane = lax.broadcasted_iota(jnp.int32, lhs.shape, 1)
    lo_row = row < half
    shift = (q_norm * jnp.where(lo_row, kmax_lo, kmax_hi)
             + jnp.where(lo_row, bias_max_lo, bias_max_hi))
    shift = shift + jnp.abs(shift) * SHIFT_MARGIN
    slot_lane = jnp.where(lo_row, HEAD_DIM, 0)
    return jnp.where(lane == slot_lane, -shift, lf).astype(BF16)


def _keys_with_ones(k_pair):
    lane = lax.broadcasted_iota(jnp.int32, k_pair.shape, 1)
    kf = k_pair.astype(F32)
    k_lo = jnp.where(lane < HEAD_DIM, kf, jnp.where(lane == HEAD_DIM, 1.0, 0.0))
    k_hi = jnp.where(lane >= HEAD_DIM, kf, jnp.where(lane == 0, 1.0, 0.0))
    return k_lo.astype(BF16), k_hi.astype(BF16)


def _shifted_scores(ka_lo, ka_hi, lhs_shifted):
    half = lhs_shifted.shape[0] // 2
    return jnp.concatenate([_nt_dot(ka_lo, lhs_shifted[:half]), _nt_dot(ka_hi, lhs_shifted[half:])],
                           axis=1)


def _pair_key_norm_max(k_pair):
    kf = k_pair.astype(F32)
    row = lax.broadcasted_iota(jnp.int32, (LANES, 2 * LANES), 0)
    col = lax.broadcasted_iota(jnp.int32, (LANES, 2 * LANES), 1)
    half_sel = jnp.where((row < HEAD_DIM) == (col < LANES), 1.0, 0.0).astype(BF16)
    sq = jnp.dot((kf * kf).astype(BF16), half_sel, preferred_element_type=F32)
    norm_max = jnp.sqrt(jnp.max(sq, axis=0, keepdims=True))
    return norm_max[:, :LANES], norm_max[:, LANES:]


def _window_attn_kernel(q_ref, k_ref, v_ref, bias_ref, sink_row_ref,
                        o_ref, kpad_scr, vtpad_scr, *, n_blocks, group):
    i = pl.program_id(1)
    half = 3 * BLOCK

    @pl.when(i == 0)
    def _():
        zero_blk = jnp.zeros((1, BLOCK, LANES), BF16)
        kpad_scr[0:1] = zero_blk
        kpad_scr[n_blocks + 1:n_blocks + 2] = zero_blk
        kpad_scr[1:n_blocks + 1] = k_ref[...].reshape(n_blocks, BLOCK, LANES)
        v_t = v_ref[...].astype(F32).T.astype(BF16)
        ones = jnp.ones((ONES_ROWS, BLOCK), BF16)
        zero_v = jnp.zeros((HEAD_DIM + ONES_ROWS, BLOCK), BF16)
        for g in range(2):
            vtpad_scr[g, 0] = zero_v
            vtpad_scr[g, n_blocks + 1] = zero_v
            for blk in range(n_blocks):
                vtpad_scr[g, blk + 1, :HEAD_DIM, :] = v_t[g * HEAD_DIM:(g + 1) * HEAD_DIM,
                                                           blk * BLOCK:(blk + 1) * BLOCK]
                vtpad_scr[g, blk + 1, HEAD_DIM:, :] = ones

    sink_row = sink_row_ref[...]

    for u in range(group):
        n = i * group + u
        qf = q_ref[u * BLOCK:(u + 1) * BLOCK, :].astype(F32) * (HEAD_DIM ** -0.5 * LOG2E)
        lhs = _stack_gqa_heads([qf[:, j * LANES:(j + 1) * LANES] for j in range(3)])
        k_win = kpad_scr[pl.ds(n, 3)].reshape(3 * BLOCK, LANES)
        s = _nt_dot(k_win, lhs) + bias_ref[...]
        s = jnp.concatenate([jnp.where(n > 0, s[:BLOCK], NEG_INF), s[BLOCK:2 * BLOCK],
                             jnp.where(n < n_blocks - 1, s[2 * BLOCK:], NEG_INF)], axis=0)
        m = jnp.maximum(jnp.max(s, axis=0, keepdims=True), sink_row)
        p = jnp.exp2(s - m).astype(BF16)
        p_sink = jnp.exp2(sink_row - m)
        outs = []
        for g in range(2):
            acc = None
            for j in range(3):
                pv = jnp.dot(vtpad_scr[g, n + j], p[j * BLOCK:(j + 1) * BLOCK, g * half:(g + 1) * half],
                             preferred_element_type=F32)
                acc = pv if acc is None else acc + pv
            total = acc[HEAD_DIM:HEAD_DIM + 1] + p_sink[:, g * half:(g + 1) * half]
            outs.append(acc[:HEAD_DIM] * (1.0 / total))
        for j, pair_t in enumerate(_unstack_gqa_heads(outs[0], outs[1], BLOCK)):
            o_ref[u * BLOCK:(u + 1) * BLOCK, j * LANES:(j + 1) * LANES] = pair_t.T.astype(o_ref.dtype)


def _window_attn(q, k, v, bias_win_t, sink, batch, seq, group):
    nb = seq // BLOCK
    steps = nb // group
    rows = A_HEADS * BLOCK
    sink_row = jnp.repeat(sink, BLOCK).reshape(1, rows)
    const = lambda shape: pl.BlockSpec(shape, lambda b, i: (0,) * len(shape))
    return pl.pallas_call(
        functools.partial(_window_attn_kernel, n_blocks=nb, group=group),
        grid=(batch, steps),
        in_specs=[pl.BlockSpec((group * BLOCK, 3 * LANES), lambda b, i: (b * steps + i, 0)),
                  pl.BlockSpec((seq, LANES), lambda b, i: (b, 0)),
                  pl.BlockSpec((seq, LANES), lambda b, i: (b, 0)),
                  const(bias_win_t.shape), const((1, rows))],
        out_specs=pl.BlockSpec((group * BLOCK, 3 * LANES), lambda b, i: (b * steps + i, 0)),
        out_shape=jax.ShapeDtypeStruct(q.shape, BF16),
        scratch_shapes=[pltpu.VMEM((nb + 2, BLOCK, LANES), BF16),
                        pltpu.VMEM((2, nb + 2, HEAD_DIM + ONES_ROWS, BLOCK), BF16)],
        compiler_params=_cparams(2),
        name="window_attn",
    )(q, k, v, bias_win_t, sink_row)


def _norm_rope_pair(x, g, cos, sin_signed):
    lo = _lane_lo(x.shape)
    x2 = x * x
    ss_lo = jnp.sum(jnp.where(lo, x2, 0.0), axis=-1, keepdims=True)
    ss_hi = jnp.sum(jnp.where(lo, 0.0, x2), axis=-1, keepdims=True)
    ms = jnp.where(lo, ss_lo, ss_hi) * (1.0 / HEAD_DIM)
    y = x * lax.rsqrt(ms + EPS) * g
    lane = lax.broadcasted_iota(jnp.int32, x.shape, 1)
    quarter = HEAD_DIM // 4
    first = (lane & quarter) == 0
    partner = jnp.where(first, pltpu.roll(y, LANES - quarter, 1), pltpu.roll(y, quarter, 1))
    return y * cos + partner * sin_signed


def _dense_attn_kernel(q_ref, k_ref, v_ref, o_ref, ka_scr, vt_scr, kmax_scr, *, tq):
    seq = k_ref.shape[0]
    kc = 2 * BLOCK
    n_chunks = seq // kc
    grp = 3 * tq

    ka_scr[0], ka_scr[1] = _keys_with_ones(k_ref[...])
    kmax_lo, kmax_hi = _pair_key_norm_max(k_ref[...])
    kmax_scr[0:1, :] = kmax_lo
    kmax_scr[1:2, :] = kmax_hi
    v_t = v_ref[...].astype(F32).T.astype(BF16)
    for g in range(2):
        vt_scr[g, :HEAD_DIM, :] = v_t[g * HEAD_DIM:(g + 1) * HEAD_DIM]
        vt_scr[g, HEAD_DIM:, :] = jnp.ones((ONES_ROWS, seq), BF16)

    def query_tile(i, carry):
        rows = pl.ds(pl.multiple_of(i * tq, tq), tq)
        lhs = _stack_gqa_heads([q_ref[rows, j * LANES:(j + 1) * LANES].astype(F32)
                                for j in range(3)])

        def write(acc):
            outs = [a[:HEAD_DIM] * (1.0 / a[HEAD_DIM:HEAD_DIM + 1]) for a in acc]
            for j, pair_t in enumerate(_unstack_gqa_heads(outs[0], outs[1], tq)):
                o_ref[rows, j * LANES:(j + 1) * LANES] = pair_t.T.astype(o_ref.dtype)

        s_all = _shifted_scores(ka_scr[0], ka_scr[1], _with_shift_lane(
            lhs, kmax_scr[0:1, :], kmax_scr[1:2, :], 0.0, 0.0))
        p = jnp.exp2(s_all).astype(BF16)
        acc = [jnp.dot(vt_scr[g], p[:, g * grp:(g + 1) * grp], preferred_element_type=F32)
               for g in range(2)]
        write(acc)
        sums = jnp.concatenate([a[HEAD_DIM:HEAD_DIM + 1] for a in acc], axis=0)

        @pl.when(jnp.logical_not(jnp.min(sums) >= SUM_FLOOR))
        def _():
            s_exact = _nt_dot(k_ref[...], lhs)
            m = None
            acc2 = [None, None]
            for c in range(n_chunks):
                s = s_exact[c * kc:(c + 1) * kc, :]
                m_c = jnp.max(s, axis=0, keepdims=True)
                m_new = m_c if m is None else jnp.maximum(m, m_c)
                pc = jnp.exp2(s - m_new).astype(BF16)
                if m is not None:
                    alpha = jnp.exp2(m - m_new)
                for g in range(2):
                    cols = slice(g * grp, (g + 1) * grp)
                    pv = jnp.dot(vt_scr[g, :, c * kc:(c + 1) * kc], pc[:, cols],
                                 preferred_element_type=F32)
                    acc2[g] = pv if m is None else acc2[g] * alpha[:, cols] + pv
                m = m_new
            write(acc2)

        return carry

    lax.fori_loop(0, seq // tq, query_tile, 0)


def _dense_attn(q, k, v, batch, seq, tq):
    return pl.pallas_call(
        functools.partial(_dense_attn_kernel, tq=tq),
        grid=(batch,),
        in_specs=[pl.BlockSpec((seq, 3 * LANES), lambda b: (b, 0)),
                  pl.BlockSpec((seq, LANES), lambda b: (b, 0)),
                  pl.BlockSpec((seq, LANES), lambda b: (b, 0))],
        out_specs=pl.BlockSpec((seq, 3 * LANES), lambda b: (b, 0)),
        out_shape=jax.ShapeDtypeStruct(q.shape, BF16),
        scratch_shapes=[pltpu.VMEM((2, seq, LANES), BF16),
                        pltpu.VMEM((2, HEAD_DIM + ONES_ROWS, seq), BF16),
                        pltpu.VMEM((2, LANES), F32)],
        compiler_params=_cparams(1),
        name="dense_attn",
    )(q, k, v)


def _cross_attn_kernel(q_ref, mk_ref, mv_ref, o_ref, vt_scr, *, tq):
    i = pl.program_id(1)
    n_pairs = q_ref.shape[1] // LANES
    mem_len = mk_ref.shape[0]

    @pl.when(i == 0)
    def _():
        v_t = mv_ref[...].astype(F32).T.astype(BF16)
        for h in range(2 * n_pairs):
            vt_scr[h, :HEAD_DIM, :] = v_t[h * HEAD_DIM:(h + 1) * HEAD_DIM]
            vt_scr[h, HEAD_DIM:, :] = jnp.ones((ONES_ROWS, mem_len), BF16)

    for j in range(n_pairs):
        cols = slice(j * LANES, (j + 1) * LANES)
        lhs = _stack_pair_heads(q_ref[:, cols].astype(F32) * (HEAD_DIM ** -0.5 * LOG2E))
        s = _nt_dot(mk_ref[:, cols], lhs)
        p = jnp.exp2(s - jnp.max(s, axis=0, keepdims=True)).astype(BF16)
        acc = [jnp.dot(vt_scr[2 * j + h], p[:, h * tq:(h + 1) * tq], preferred_element_type=F32)
               for h in range(2)]
        pair_t = jnp.concatenate([a[:HEAD_DIM] * (1.0 / a[HEAD_DIM:HEAD_DIM + 1]) for a in acc],
                                 axis=0)
        o_ref[:, cols] = pair_t.T.astype(o_ref.dtype)


def _cross_attn(q, mk, mv, batch, seq, tq):
    nq = seq // tq
    mem_len = mk.shape[0] // batch
    width = q.shape[1]
    n_pairs = width // LANES
    return pl.pallas_call(
        functools.partial(_cross_attn_kernel, tq=tq),
        grid=(batch, nq),
        in_specs=[pl.BlockSpec((tq, width), lambda b, i: (b * nq + i, 0)),
                  pl.BlockSpec((mem_len, width), lambda b, i: (b, 0)),
                  pl.BlockSpec((mem_len, width), lambda b, i: (b, 0))],
        out_specs=pl.BlockSpec((tq, width), lambda b, i: (b * nq + i, 0)),
        out_shape=jax.ShapeDtypeStruct(q.shape, BF16),
        scratch_shapes=[pltpu.VMEM((2 * n_pairs, HEAD_DIM + ONES_ROWS, mem_len), BF16)],
        compiler_params=_cparams(2),
        name="cross_attn",
    )(q, mk, mv)


def _diff_attn_kernel(q1_ref, q2_ref, k1_ref, k2_ref, v_ref, bias_ref, bmax_ref, lam_ref, g_ref,
                      o_ref, vt_scr, ka_scr, kmax_scr, *, tq, lam_init):
    sub = tq // BLOCK
    seq = k1_ref.shape[0]
    kc = 2 * BLOCK
    n_chunks = seq // kc
    k_refs = (k1_ref, k2_ref)

    for head in range(2):
        v_t = v_ref[:, head * C_V_DIM:(head + 1) * C_V_DIM].astype(F32).T
        vt_scr[head, :C_V_DIM, :] = v_t.astype(BF16)
        vt_scr[head, C_V_DIM:, :] = jnp.ones((ONES_ROWS, seq), BF16)
    for t in range(2):
        ka_scr[t, 0], ka_scr[t, 1] = _keys_with_ones(k_refs[t][...])
        kmax_lo, kmax_hi = _pair_key_norm_max(k_refs[t][...])
        kmax_scr[t, 0:1, :] = kmax_lo
        kmax_scr[t, 1:2, :] = kmax_hi

    lam_vec = lam_ref[...]
    lam = (jnp.exp(jnp.sum(lam_vec[0:1] * lam_vec[1:2], axis=-1, keepdims=True))
           - jnp.exp(jnp.sum(lam_vec[2:3] * lam_vec[3:4], axis=-1, keepdims=True)) + lam_init)

    def query_tile(i, carry):
        rows = pl.ds(pl.multiple_of(i * tq, tq), tq)
        lhs = [_stack_pair_heads(q_ref[rows, :].astype(F32) * (HEAD_DIM ** -0.5 * LOG2E))
               for q_ref in (q1_ref, q2_ref)]
        _diff_attn_tile(i, rows, lhs, lam, bias_ref, bmax_ref, g_ref, o_ref, vt_scr, ka_scr,
                        kmax_scr, k_refs, tq=tq, sub=sub, kc=kc, n_chunks=n_chunks,
                        lam_init=lam_init)
        return carry

    lax.fori_loop(0, seq // tq, query_tile, 0)


def _diff_attn_tile(i, rows, lhs, lam, bias_ref, bmax_ref, g_ref, o_ref, vt_scr, ka_scr, kmax_scr,
                    k_refs, *, tq, sub, kc, n_chunks, lam_init):
    def bias_chunk(c):
        def tile(head, u, e):
            offset = (2 * c + e) - (i * sub + u)
            return bias_ref[head, jnp.clip(offset, -BIAS_FAR, BIAS_FAR) + BIAS_FAR]
        return jnp.concatenate(
            [jnp.concatenate([tile(head, u, e) for head in range(2) for u in range(sub)], axis=1)
             for e in range(2)], axis=0)

    def shifted_scores(t):
        return _shifted_scores(ka_scr[t, 0], ka_scr[t, 1], _with_shift_lane(
            lhs[t], kmax_scr[t, 0:1, :], kmax_scr[t, 1:2, :], bmax_ref[0], bmax_ref[1]))

    def pv_of_shifted(s_all):
        p = jnp.concatenate([jnp.exp2(s_all[c * kc:(c + 1) * kc, :] + bias_chunk(c)).astype(BF16)
                             for c in range(n_chunks)], axis=0)
        return [jnp.dot(vt_scr[head], p[:, head * tq:(head + 1) * tq],
                        preferred_element_type=F32) for head in range(2)]

    def pv_running_max(s_all):
        m = None
        acc = [None, None]
        for c in range(n_chunks):
            s = s_all[c * kc:(c + 1) * kc, :] + bias_chunk(c)
            m_c = jnp.max(s, axis=0, keepdims=True)
            m_new = m_c if m is None else jnp.maximum(m, m_c)
            p = jnp.exp2(s - m_new).astype(BF16)
            if m is not None:
                alpha = jnp.exp2(m - m_new)
            for head in range(2):
                cols = slice(head * tq, (head + 1) * tq)
                pv = jnp.dot(vt_scr[head, :, c * kc:(c + 1) * kc], p[:, cols],
                             preferred_element_type=F32)
                acc[head] = pv if m is None else acc[head] * alpha[:, cols] + pv
            m = m_new
        return acc

    def write(acc1, acc2):
        g = g_ref[...] * (1.0 - lam_init)
        for head in range(2):
            o1 = acc1[head][:C_V_DIM] * (1.0 / acc1[head][C_V_DIM:C_V_DIM + 1])
            o2 = acc2[head][:C_V_DIM] * (1.0 / acc2[head][C_V_DIM:C_V_DIM + 1])
            out = o1 - lam * o2
            ms = jnp.mean(out * out, axis=0, keepdims=True)
            y = out * lax.rsqrt(ms + EPS) * g
            o_ref[rows, head * C_V_DIM:(head + 1) * C_V_DIM] = y.T.astype(o_ref.dtype)

    s1 = shifted_scores(0)
    s2 = shifted_scores(1)
    acc1 = pv_of_shifted(s1)
    acc2 = pv_of_shifted(s2)
    write(acc1, acc2)
    sums = jnp.concatenate([a[C_V_DIM:C_V_DIM + 1] for a in acc1 + acc2], axis=0)

    @pl.when(jnp.logical_not(jnp.min(sums) >= SUM_FLOOR))
    def _():
        write(pv_running_max(_nt_dot(k_refs[0][...], lhs[0])),
              pv_running_max(_nt_dot(k_refs[1][...], lhs[1])))


def _diff_attn(q1, q2, k1, k2, v, bias_t, bias_max, lam_vecs, subln_g, lam_init, batch, seq, tq):
    n_pairs = C_HEADS // 2
    n_tiles = bias_t.shape[1]
    seq_spec = pl.BlockSpec((seq, LANES), lambda b, p: (b, p))
    g_cols = jnp.broadcast_to(subln_g.reshape(C_V_DIM, 1), (C_V_DIM, tq))
    return pl.pallas_call(
        functools.partial(_diff_attn_kernel, tq=tq, lam_init=lam_init),
        grid=(batch, n_pairs),
        in_specs=[seq_spec, seq_spec, seq_spec, seq_spec,
                  pl.BlockSpec((seq, 2 * C_V_DIM), lambda b, p: (b, p)),
                  pl.BlockSpec((2, n_tiles, BLOCK, BLOCK), lambda b, p: (p, 0, 0, 0)),
                  pl.BlockSpec((2, 1, LANES), lambda b, p: (p, 0, 0)),
                  pl.BlockSpec(lam_vecs.shape, lambda b, p: (0, 0)),
                  pl.BlockSpec((C_V_DIM, tq), lambda b, p: (0, 0))],
        out_specs=pl.BlockSpec((seq, 2 * C_V_DIM), lambda b, p: (b, p)),
        out_shape=jax.ShapeDtypeStruct(v.shape, BF16),
        scratch_shapes=[pltpu.VMEM((2, C_V_DIM + ONES_ROWS, seq), BF16),
                        pltpu.VMEM((2, 2, seq, LANES), BF16),
                        pltpu.VMEM((2, 2, LANES), F32)],
        compiler_params=_cparams(2),
        name="diff_attn",
    )(q1, q2, k1, k2, v, bias_t, bias_max, lam_vecs, g_cols)


def _rope_tables(seq):
    rows = seq // GRID_W
    row = jnp.broadcast_to(jnp.arange(rows)[:, None], (rows, GRID_W)).reshape(-1)
    col = jnp.broadcast_to(jnp.arange(GRID_W)[None, :], (rows, GRID_W)).reshape(-1)
    half = HEAD_DIM // 2
    inv = 1.0 / (ROPE_THETA ** (jnp.arange(0, half, 2, dtype=F32) / half))
    ang_row = row.astype(F32)[:, None] * inv
    ang_col = col.astype(F32)[:, None] * inv
    cos = jnp.concatenate([jnp.cos(ang_row)] * 2 + [jnp.cos(ang_col)] * 2, axis=-1)
    sin = jnp.concatenate([-jnp.sin(ang_row), jnp.sin(ang_row),
                           -jnp.sin(ang_col), jnp.sin(ang_col)], axis=-1)
    return jnp.tile(cos, (1, 2)), jnp.tile(sin, (1, 2))


def kernel(x, mem, rel_bias, mem_norm, final_norm, even_norm, even_w_in, even_sink, even_q_norm, even_k_norm, even_w_mem_kv, even_w_out, odd_norm, odd_w_in, odd_lambda_q1, odd_lambda_k1, odd_lambda_q2, odd_lambda_k2, odd_subln, odd_w_mem_kv, odd_w_out):
    batch, seq, d = x.shape
    mem_len = mem.shape[1]
    tokens = batch * seq
    xw = X_HEADS * HEAD_DIM
    gw = A_HEADS * HEAD_DIM

    w_in0 = even_w_in[0].astype(BF16)
    w_out0 = even_w_out[0].astype(BF16)
    w_in1 = odd_w_in[0].astype(BF16)
    w_out1 = odd_w_out[0].astype(BF16)

    x2 = x.reshape(tokens, d)
    mem2 = mem.reshape(batch * mem_len, d)

    bias_near, bias_win = _bias_tiles(rel_bias)
    cos, sin_signed = _rope_tables(seq)
    head_bias_max = jnp.max(rel_bias, axis=0) * LOG2E

    w_mem = jnp.concatenate([even_w_mem_kv[0], odd_w_mem_kv[0]], axis=1).astype(BF16)
    mk0, mv0, mk1, mv1 = _norm_proj(mem2, mem_norm, w_mem, (xw,) * 4, TM_MEM, gate_last=False)

    rope_args = (cos, sin_signed, jnp.tile(even_q_norm[0], 2).reshape(1, LANES),
                 jnp.tile(even_k_norm[0], 2).reshape(1, LANES))
    aq, ak, av, bq, bk, bv, xq, gate = _norm_proj(
        x2, even_norm[0], w_in0, (gw, 128, 128, gw, 128, 128, xw, D_MODEL), TM_PROJ,
        gate_last=True, qk_rope=(3, 4), rope_args=rope_args, seq=seq)
    y_a = _window_attn(aq, ak, av, bias_win, even_sink[0] * LOG2E, batch, seq, WINDOW_GROUP)
    y_b = _dense_attn(bq, bk, bv, batch, seq, TQ_DENSE)
    y_x = _cross_attn(xq, mk0, mv0, batch, seq, TQ_CROSS)
    lam_init = 0.8 - 0.6 * math.exp(-0.3 * 1)
    cw = C_HEADS * HEAD_DIM
    h1, q1, q2, k1, k2, v, xq1, gate1 = _gate_out(
        [y_a, y_b, y_x], gate, w_out0, x2, odd_norm[0], TM_PROJ, w_next=w_in1,
        splits=(cw, cw, cw, cw, C_HEADS * C_V_DIM, xw, D_MODEL))
    lam_vecs = jnp.stack([odd_lambda_q1[0], odd_lambda_k1[0], odd_lambda_q2[0], odd_lambda_k2[0]])
    bias_max = jnp.broadcast_to(head_bias_max.reshape(C_HEADS, 1, 1), (C_HEADS, 1, LANES))
    y_c = _diff_attn(q1, q2, k1, k2, v, bias_near, bias_max, lam_vecs, odd_subln[0], lam_init,
                     batch, seq, TQ_DIFF)
    y_x1 = _cross_attn(xq1, mk1, mv1, batch, seq, TQ_CROSS)
    (out,) = _gate_out([y_c, y_x1], gate1, w_out1, h1, final_norm, TM_PROJ)
    return out.reshape(batch, seq, d)
```

```python
import functools
import math

import numpy as np
import jax
import jax.numpy as jnp
from jax import lax
from jax.experimental import pallas as pl
from jax.experimental.pallas import tpu as pltpu

D_MODEL = 1024
HEAD_DIM = 64
BLOCK = 128
WINDOW = 128
GRID_W = 64
A_HEADS = 6
C_HEADS = 6
C_V_DIM = 128
X_HEADS = 4
REL_BUCKETS = 32
REL_MAX_DIST = 128
ROPE_THETA = 10000.0
EPS = 1e-6
NEG_INF = -1e30
LOG2E = math.log2(math.e)
LANES = 128
ONES_ROWS = 16
PROJ_SUB_ROWS = 256
GATE_SUB_ROWS = 1024
PROJ_COL_GROUP = 512
TM_PROJ = 1024
TM_MEM = 256
TQ_DENSE = 256
TQ_DIFF = 512
TQ_CROSS = 2048
WINDOW_GROUP = 16
SUM_FLOOR = 2.0 ** -64
SHIFT_MARGIN = 2.0 ** -7
BIAS_FAR = 2
VMEM_LIMIT = 56 * 1024 * 1024

F32 = jnp.float32
BF16 = jnp.bfloat16


def _cparams(n_axes):
    return pltpu.CompilerParams(dimension_semantics=("arbitrary",) * n_axes,
                                vmem_limit_bytes=VMEM_LIMIT)


def _lane_lo(shape):
    return lax.broadcasted_iota(jnp.int32, shape, len(shape) - 1) < HEAD_DIM


def _rmsnorm_rows(x, g):
    ms = jnp.mean(x * x, axis=-1, keepdims=True)
    return x * lax.rsqrt(ms + EPS) * g


def _nt_dot(a, b):
    return lax.dot_general(a, b, (((1,), (1,)), ((), ())), preferred_element_type=F32)


def _norm_proj_kernel(*refs, qk_rope, gate_last):
    x_ref, g_ref, w_ref = refs[:3]
    if qk_rope is None:
        out_refs = refs[3:]
    else:
        cos_ref, sin_ref, gq_ref, gk_ref = refs[3:7]
        out_refs = refs[7:]
    tm = x_ref.shape[0]
    for r0 in range(0, tm, PROJ_SUB_ROWS):
        rows = slice(r0, min(r0 + PROJ_SUB_ROWS, tm))

        def post(j, y):
            if qk_rope is not None and j in qk_rope:
                gain, scale = ((gq_ref, HEAD_DIM ** -0.5 * LOG2E) if j == qk_rope[0]
                               else (gk_ref, 1.0))
                y = jnp.concatenate(
                    [_norm_rope_pair(y[:, t * LANES:(t + 1) * LANES], gain[...],
                                     cos_ref[rows, :], sin_ref[rows, :]) * scale
                     for t in range(y.shape[1] // LANES)], axis=1)
            if gate_last and j == len(out_refs) - 1:
                y = _silu(y)
            return y

        xn = _rmsnorm_rows(x_ref[rows, :], g_ref[...]).astype(BF16)
        _project_and_store(xn, w_ref, out_refs, rows, post)


def _project_and_store(xn, w_ref, out_refs, rows, post):
    n = w_ref.shape[1]
    starts = [0]
    for o_ref in out_refs:
        starts.append(starts[-1] + o_ref.shape[1])
    assert starts[-1] == n
    for c0 in range(0, n, PROJ_COL_GROUP):
        c1 = min(c0 + PROJ_COL_GROUP, n)
        y = jnp.dot(xn, w_ref[:, c0:c1], preferred_element_type=F32)
        for j, o_ref in enumerate(out_refs):
            lo, hi = max(c0, starts[j]), min(c1, starts[j + 1])
            if lo < hi:
                o_ref[rows, lo - starts[j]:hi - starts[j]] = post(
                    j, y[:, lo - c0:hi - c0]).astype(o_ref.dtype)


def _silu(y):
    return y * (1.0 / (1.0 + jnp.exp(-y)))


def _norm_proj(x, g, w_bf16, splits, tm, gate_last, qk_rope=None, rope_args=(), seq=None):
    rows, d = x.shape
    n = w_bf16.shape[1]
    assert sum(splits) == n and rows % tm == 0
    in_specs = [pl.BlockSpec((tm, d), lambda i: (i, 0)),
                pl.BlockSpec((1, d), lambda i: (0, 0)),
                pl.BlockSpec((d, n), lambda i: (0, 0))]
    if qk_rope is not None:
        pos_blocks = seq // tm
        in_specs += [pl.BlockSpec((tm, LANES), lambda i: (i % pos_blocks, 0)),
                     pl.BlockSpec((tm, LANES), lambda i: (i % pos_blocks, 0)),
                     pl.BlockSpec((1, LANES), lambda i: (0, 0)),
                     pl.BlockSpec((1, LANES), lambda i: (0, 0))]
    return pl.pallas_call(
        functools.partial(_norm_proj_kernel, qk_rope=qk_rope, gate_last=gate_last),
        grid=(rows // tm,),
        in_specs=in_specs,
        out_specs=[pl.BlockSpec((tm, s), lambda i: (i, 0)) for s in splits],
        out_shape=[jax.ShapeDtypeStruct((rows, s), BF16) for s in splits],
        compiler_params=_cparams(1),
        name="norm_proj",
    )(x, g.reshape(1, d), w_bf16, *rope_args)


def _gate_out_kernel(*refs, n_parts, after):
    y_refs = refs[:n_parts]
    gate_ref, w_ref, res_ref, g_ref = refs[n_parts:n_parts + 4]
    if after == "proj":
        w_next_ref, h_ref = refs[n_parts + 4:n_parts + 6]
        out_refs = refs[n_parts + 6:]
    else:
        h_ref = refs[n_parts + 4]
    tm = res_ref.shape[0]
    for r0 in range(0, tm, GATE_SUB_ROWS):
        rows = slice(r0, min(r0 + GATE_SUB_ROWS, tm))
        parts = []
        c0 = 0
        for y_ref in y_refs:
            width = y_ref.shape[1]
            parts.append(y_ref[rows, :] * gate_ref[rows, c0:c0 + width])
            c0 += width
        h = res_ref[rows, :] + jnp.dot(jnp.concatenate(parts, axis=1), w_ref[...],
                                       preferred_element_type=F32)
        if after == "norm":
            h_ref[rows, :] = _rmsnorm_rows(h, g_ref[...])
            continue
        h_ref[rows, :] = h
        gate_path = len(out_refs) - 1
        _project_and_store(_rmsnorm_rows(h, g_ref[...]).astype(BF16), w_next_ref, out_refs, rows,
                           lambda j, y: _silu(y) if j == gate_path else y)


def _gate_out(y_parts, gate, w_bf16, resid, norm_g, tm, w_next=None, splits=()):
    rows, d = resid.shape
    mix = w_bf16.shape[0]
    after = "norm" if w_next is None else "proj"
    row_block = lambda width: pl.BlockSpec((tm, width), lambda i: (i, 0))
    whole = lambda a: pl.BlockSpec(a.shape, lambda i: (0, 0))
    in_specs = [row_block(y.shape[1]) for y in y_parts]
    in_specs += [row_block(mix), whole(w_bf16), row_block(d), pl.BlockSpec((1, d), lambda i: (0, 0))]
    args = list(y_parts) + [gate, w_bf16, resid, norm_g.reshape(1, d)]
    out_specs = [row_block(d)]
    out_shape = [jax.ShapeDtypeStruct((rows, d), F32)]
    if after == "proj":
        assert sum(splits) == w_next.shape[1]
        in_specs.append(whole(w_next))
        args.append(w_next)
        out_specs += [row_block(s) for s in splits]
        out_shape += [jax.ShapeDtypeStruct((rows, s), BF16) for s in splits]
    return pl.pallas_call(
        functools.partial(_gate_out_kernel, n_parts=len(y_parts), after=after),
        grid=(rows // tm,),
        in_specs=in_specs,
        out_specs=out_specs,
        out_shape=out_shape,
        compiler_params=_cparams(1),
        name="gate_out",
    )(*args)


def _bucket_thresholds():
    nb = REL_BUCKETS // 2
    max_exact = nb // 2
    n = np.arange(0, 4 * REL_MAX_DIST)
    nf = np.maximum(n, 1).astype(np.float32)
    large = max_exact + (np.log(nf / np.float32(max_exact))
                         / np.float32(math.log(REL_MAX_DIST / max_exact))
                         * np.float32(nb - max_exact)).astype(np.int32)
    bucket = np.where(n < max_exact, n, np.minimum(large, nb - 1))
    assert np.all(np.diff(bucket) >= 0) and bucket[0] == 0 and bucket[-1] == nb - 1
    thr = [int(np.argmax(bucket >= k)) for k in range(1, nb)]
    assert thr[-1] < REL_MAX_DIST
    return thr


def _bias_tiles_kernel(tab_ref, near_ref, win_ref):
    h = pl.program_id(0)
    thr = _bucket_thresholds()
    nb = REL_BUCKETS // 2
    row = lax.broadcasted_iota(jnp.int32, (BLOCK, BLOCK), 0)
    col = lax.broadcasted_iota(jnp.int32, (BLOCK, BLOCK), 1)
    def tile_of(rel):
        n = jnp.abs(rel)
        neg = jnp.full((BLOCK, BLOCK), tab_ref[0, h], F32)
        pos = jnp.full((BLOCK, BLOCK), tab_ref[nb, h], F32)
        for b in range(1, nb):
            ge = n >= thr[b - 1]
            neg = jnp.where(ge, tab_ref[b, h], neg)
            pos = jnp.where(ge, tab_ref[nb + b, h], pos)
        return jnp.where(rel > 0, pos, neg) * LOG2E

    for k in range(2 * BIAS_FAR + 1):
        rel = (k - BIAS_FAR) * BLOCK + row - col
        tile = tile_of(rel)
        near_ref[0, k] = tile
        if abs(k - BIAS_FAR) <= 1:
            j = k - BIAS_FAR + 1
            win_ref[j * BLOCK:(j + 1) * BLOCK, :] = jnp.where(jnp.abs(rel) <= WINDOW, tile, NEG_INF)


def _bias_tiles(rel_bias):
    assert _bucket_thresholds()[-1] <= (BIAS_FAR - 1) * BLOCK + 1
    n_tiles = 2 * BIAS_FAR + 1
    n_heads = rel_bias.shape[1]
    return pl.pallas_call(
        _bias_tiles_kernel,
        grid=(n_heads,),
        in_specs=[pl.BlockSpec(memory_space=pltpu.SMEM)],
        out_specs=[pl.BlockSpec((1, n_tiles, BLOCK, BLOCK), lambda h: (h, 0, 0, 0)),
                   pl.BlockSpec((3 * BLOCK, BLOCK), lambda h: (0, h))],
        out_shape=[jax.ShapeDtypeStruct((n_heads, n_tiles, BLOCK, BLOCK), F32),
                   jax.ShapeDtypeStruct((3 * BLOCK, n_heads * BLOCK), F32)],
        compiler_params=_cparams(1),
        name="bias_tiles",
    )(rel_bias)


def _stack_gqa_heads(q_tiles):
    lo = _lane_lo(q_tiles[0].shape)
    swap = lambda q: pltpu.roll(q, HEAD_DIM, 1)
    heads_lo = [q_tiles[0], swap(q_tiles[0]), q_tiles[1]]
    heads_hi = [q_tiles[1], swap(q_tiles[2]), q_tiles[2]]
    return jnp.concatenate([jnp.where(lo, q, 0.0).astype(BF16) for q in heads_lo]
                           + [jnp.where(lo, 0.0, q).astype(BF16) for q in heads_hi], axis=0)


def _unstack_gqa_heads(out_lo, out_hi, width):
    blk = lambda o, j: o[:, j * width:(j + 1) * width]
    return [jnp.concatenate([blk(out_lo, 0), blk(out_lo, 1)], axis=0),
            jnp.concatenate([blk(out_lo, 2), blk(out_hi, 0)], axis=0),
            jnp.concatenate([blk(out_hi, 1), blk(out_hi, 2)], axis=0)]


def _stack_pair_heads(q):
    lo = _lane_lo(q.shape)
    return jnp.concatenate([jnp.where(lo, q, 0.0).astype(BF16),
                            jnp.where(lo, 0.0, q).astype(BF16)], axis=0)


def _with_shift_lane(lhs, kmax_lo, kmax_hi, bias_max_lo, bias_max_hi):
    half = lhs.shape[0] // 2
    lf = lhs.astype(F32)
    q_sq = jnp.sum(lf * lf, axis=-1, keepdims=True)
    q_norm = q_sq * lax.rsqrt(jnp.maximum(q_sq, 1e-30))
    row = lax.broadcasted_iota(jnp.int32, lhs.shape, 0)
    lane = lax.broadcasted_iota(jnp.int32, lhs.shape, 1)
    lo_row = row < half
    shift = (q_norm * jnp.where(lo_row, kmax_lo, kmax_hi)
             + jnp.where(lo_row, bias_max_lo, bias_max_hi))
    shift = shift + jnp.abs(shift) * SHIFT_MARGIN
    slot_lane = jnp.where(lo_row, HEAD_DIM, 0)
    return jnp.where(lane == slot_lane, -shift, lf).astype(BF16)


def _keys_with_ones(k_pair):
    lane = lax.broadcasted_iota(jnp.int32, k_pair.shape, 1)
    kf = k_pair.astype(F32)
    k_lo = jnp.where(lane < HEAD_DIM, kf, jnp.where(lane == HEAD_DIM, 1.0, 0.0))
    k_hi = jnp.where(lane >= HEAD_DIM, kf, jnp.where(lane == 0, 1.0, 0.0))
    return k_lo.astype(BF16), k_hi.astype(BF16)


def _shifted_scores(ka_lo, ka_hi, lhs_shifted):
    half = lhs_shifted.shape[0] // 2
    return jnp.concatenate([_nt_dot(ka_lo, lhs_shifted[:half]), _nt_dot(ka_hi, lhs_shifted[half:])],
                           axis=1)


def _pair_key_norm_max(k_pair):
    kf = k_pair.astype(F32)
    row = lax.broadcasted_iota(jnp.int32, (LANES, 2 * LANES), 0)
    col = lax.broadcasted_iota(jnp.int32, (LANES, 2 * LANES), 1)
    half_sel = jnp.where((row < HEAD_DIM) == (col < LANES), 1.0, 0.0).astype(BF16)
    sq = jnp.dot((kf * kf).astype(BF16), half_sel, preferred_element_type=F32)
    norm_max = jnp.sqrt(jnp.max(sq, axis=0, keepdims=True))
    return norm_max[:, :LANES], norm_max[:, LANES:]


def _window_attn_kernel(q_ref, k_ref, v_ref, bias_ref, sink_row_ref,
                        o_ref, kpad_scr, vtpad_scr, *, n_blocks, group):
    i = pl.program_id(1)
    half = 3 * BLOCK

    @pl.when(i == 0)
    def _():
        zero_blk = jnp.zeros((1, BLOCK, LANES), BF16)
        kpad_scr[0:1] = zero_blk
        kpad_scr[n_blocks + 1:n_blocks + 2] = zero_blk
        kpad_scr[1:n_blocks + 1] = k_ref[...].reshape(n_blocks, BLOCK, LANES)
        v_t = v_ref[...].astype(F32).T.astype(BF16)
        ones = jnp.ones((ONES_ROWS, BLOCK), BF16)
        zero_v = jnp.zeros((HEAD_DIM + ONES_ROWS, BLOCK), BF16)
        for g in range(2):
            vtpad_scr[g, 0] = zero_v
            vtpad_scr[g, n_blocks + 1] = zero_v
            for blk in range(n_blocks):
                vtpad_scr[g, blk + 1, :HEAD_DIM, :] = v_t[g * HEAD_DIM:(g + 1) * HEAD_DIM,
                                                           blk * BLOCK:(blk + 1) * BLOCK]
                vtpad_scr[g, blk + 1, HEAD_DIM:, :] = ones

    sink_row = sink_row_ref[...]

    for u in range(group):
        n = i * group + u
        qf = q_ref[u * BLOCK:(u + 1) * BLOCK, :].astype(F32) * (HEAD_DIM ** -0.5 * LOG2E)
        lhs = _stack_gqa_heads([qf[:, j * LANES:(j + 1) * LANES] for j in range(3)])
        k_win = kpad_scr[pl.ds(n, 3)].reshape(3 * BLOCK, LANES)
        s = _nt_dot(k_win, lhs) + bias_ref[...]
        s = jnp.concatenate([jnp.where(n > 0, s[:BLOCK], NEG_INF), s[BLOCK:2 * BLOCK],
                             jnp.where(n < n_blocks - 1, s[2 * BLOCK:], NEG_INF)], axis=0)
        m = jnp.maximum(jnp.max(s, axis=0, keepdims=True), sink_row)
        p = jnp.exp2(s - m).astype(BF16)
        p_sink = jnp.exp2(sink_row - m)
        outs = []
        for g in range(2):
            acc = None
            for j in range(3):
                pv = jnp.dot(vtpad_scr[g, n + j], p[j * BLOCK:(j + 1) * BLOCK, g * half:(g + 1) * half],
                             preferred_element_type=F32)
                acc = pv if acc is None else acc + pv
            total = acc[HEAD_DIM:HEAD_DIM + 1] + p_sink[:, g * half:(g + 1) * half]
            outs.append(acc[:HEAD_DIM] * (1.0 / total))
        for j, pair_t in enumerate(_unstack_gqa_heads(outs[0], outs[1], BLOCK)):
            o_ref[u * BLOCK:(u + 1) * BLOCK, j * LANES:(j + 1) * LANES] = pair_t.T.astype(o_ref.dtype)


def _window_attn(q, k, v, bias_win_t, sink, batch, seq, group):
    nb = seq // BLOCK
    steps = nb // group
    rows = A_HEADS * BLOCK
    sink_row = jnp.repeat(sink, BLOCK).reshape(1, rows)
    const = lambda shape: pl.BlockSpec(shape, lambda b, i: (0,) * len(shape))
    return pl.pallas_call(
        functools.partial(_window_attn_kernel, n_blocks=nb, group=group),
        grid=(batch, steps),
        in_specs=[pl.BlockSpec((group * BLOCK, 3 * LANES), lambda b, i: (b * steps + i, 0)),
                  pl.BlockSpec((seq, LANES), lambda b, i: (b, 0)),
                  pl.BlockSpec((seq, LANES), lambda b, i: (b, 0)),
                  const(bias_win_t.shape), const((1, rows))],
        out_specs=pl.BlockSpec((group * BLOCK, 3 * LANES), lambda b, i: (b * steps + i, 0)),
        out_shape=jax.ShapeDtypeStruct(q.shape, BF16),
        scratch_shapes=[pltpu.VMEM((nb + 2, BLOCK, LANES), BF16),
                        pltpu.VMEM((2, nb + 2, HEAD_DIM + ONES_ROWS, BLOCK), BF16)],
        compiler_params=_cparams(2),
        name="window_attn",
    )(q, k, v, bias_win_t, sink_row)


def _norm_rope_pair(x, g, cos, sin_signed):
    lo = _lane_lo(x.shape)
    x2 = x * x
    ss_lo = jnp.sum(jnp.where(lo, x2, 0.0), axis=-1, keepdims=True)
    ss_hi = jnp.sum(jnp.where(lo, 0.0, x2), axis=-1, keepdims=True)
    ms = jnp.where(lo, ss_lo, ss_hi) * (1.0 / HEAD_DIM)
    y = x * lax.rsqrt(ms + EPS) * g
    lane = lax.broadcasted_iota(jnp.int32, x.shape, 1)
    quarter = HEAD_DIM // 4
    first = (lane & quarter) == 0
    partner = jnp.where(first, pltpu.roll(y, LANES - quarter, 1), pltpu.roll(y, quarter, 1))
    return y * cos + partner * sin_signed


def _dense_attn_kernel(q_ref, k_ref, v_ref, o_ref, ka_scr, vt_scr, kmax_scr, *, tq):
    seq = k_ref.shape[0]
    kc = 2 * BLOCK
    n_chunks = seq // kc
    grp = 3 * tq

    ka_scr[0], ka_scr[1] = _keys_with_ones(k_ref[...])
    kmax_lo, kmax_hi = _pair_key_norm_max(k_ref[...])
    kmax_scr[0:1, :] = kmax_lo
    kmax_scr[1:2, :] = kmax_hi
    v_t = v_ref[...].astype(F32).T.astype(BF16)
    for g in range(2):
        vt_scr[g, :HEAD_DIM, :] = v_t[g * HEAD_DIM:(g + 1) * HEAD_DIM]
        vt_scr[g, HEAD_DIM:, :] = jnp.ones((ONES_ROWS, seq), BF16)

    def query_tile(i, carry):
        rows = pl.ds(pl.multiple_of(i * tq, tq), tq)
        lhs = _stack_gqa_heads([q_ref[rows, j * LANES:(j + 1) * LANES].astype(F32)
                                for j in range(3)])

        def write(acc):
            outs = [a[:HEAD_DIM] * (1.0 / a[HEAD_DIM:HEAD_DIM + 1]) for a in acc]
            for j, pair_t in enumerate(_unstack_gqa_heads(outs[0], outs[1], tq)):
                o_ref[rows, j * LANES:(j + 1) * LANES] = pair_t.T.astype(o_ref.dtype)

        s_all = _shifted_scores(ka_scr[0], ka_scr[1], _with_shift_lane(
            lhs, kmax_scr[0:1, :], kmax_scr[1:2, :], 0.0, 0.0))
        p = jnp.exp2(s_all).astype(BF16)
        acc = [jnp.dot(vt_scr[g], p[:, g * grp:(g + 1) * grp], preferred_element_type=F32)
               for g in range(2)]
        write(acc)
        sums = jnp.concatenate([a[HEAD_DIM:HEAD_DIM + 1] for a in acc], axis=0)

        @pl.when(jnp.logical_not(jnp.min(sums) >= SUM_FLOOR))
        def _():
            s_exact = _nt_dot(k_ref[...], lhs)
            m = None
            acc2 = [None, None]
            for c in range(n_chunks):
                s = s_exact[c * kc:(c + 1) * kc, :]
                m_c = jnp.max(s, axis=0, keepdims=True)
                m_new = m_c if m is None else jnp.maximum(m, m_c)
                pc = jnp.exp2(s - m_new).astype(BF16)
                if m is not None:
                    alpha = jnp.exp2(m - m_new)
                for g in range(2):
                    cols = slice(g * grp, (g + 1) * grp)
                    pv = jnp.dot(vt_scr[g, :, c * kc:(c + 1) * kc], pc[:, cols],
                                 preferred_element_type=F32)
                    acc2[g] = pv if m is None else acc2[g] * alpha[:, cols] + pv
                m = m_new
            write(acc2)

        return carry

    lax.fori_loop(0, seq // tq, query_tile, 0)


def _dense_attn(q, k, v, batch, seq, tq):
    return pl.pallas_call(
        functools.partial(_dense_attn_kernel, tq=tq),
        grid=(batch,),
        in_specs=[pl.BlockSpec((seq, 3 * LANES), lambda b: (b, 0)),
                  pl.BlockSpec((seq, LANES), lambda b: (b, 0)),
                  pl.BlockSpec((seq, LANES), lambda b: (b, 0))],
        out_specs=pl.BlockSpec((seq, 3 * LANES), lambda b: (b, 0)),
        out_shape=jax.ShapeDtypeStruct(q.shape, BF16),
        scratch_shapes=[pltpu.VMEM((2, seq, LANES), BF16),
                        pltpu.VMEM((2, HEAD_DIM + ONES_ROWS, seq), BF16),
                        pltpu.VMEM((2, LANES), F32)],
        compiler_params=_cparams(1),
        name="dense_attn",
    )(q, k, v)


def _cross_attn_kernel(q_ref, mk_ref, mv_ref, o_ref, vt_scr, *, tq):
    i = pl.program_id(1)
    n_pairs = q_ref.shape[1] // LANES
    mem_len = mk_ref.shape[0]

    @pl.when(i == 0)
    def _():
        v_t = mv_ref[...].astype(F32).T.astype(BF16)
        for h in range(2 * n_pairs):
            vt_scr[h, :HEAD_DIM, :] = v_t[h * HEAD_DIM:(h + 1) * HEAD_DIM]
            vt_scr[h, HEAD_DIM:, :] = jnp.ones((ONES_ROWS, mem_len), BF16)

    for j in range(n_pairs):
        cols = slice(j * LANES, (j + 1) * LANES)
        lhs = _stack_pair_heads(q_ref[:, cols].astype(F32) * (HEAD_DIM ** -0.5 * LOG2E))
        s = _nt_dot(mk_ref[:, cols], lhs)
        p = jnp.exp2(s - jnp.max(s, axis=0, keepdims=True)).astype(BF16)
        acc = [jnp.dot(vt_scr[2 * j + h], p[:, h * tq:(h + 1) * tq], preferred_element_type=F32)
               for h in range(2)]
        pair_t = jnp.concatenate([a[:HEAD_DIM] * (1.0 / a[HEAD_DIM:HEAD_DIM + 1]) for a in acc],
                                 axis=0)
        o_ref[:, cols] = pair_t.T.astype(o_ref.dtype)


def _cross_attn(q, mk, mv, batch, seq, tq):
    nq = seq // tq
    mem_len = mk.shape[0] // batch
    width = q.shape[1]
    n_pairs = width // LANES
    return pl.pallas_call(
        functools.partial(_cross_attn_kernel, tq=tq),
        grid=(batch, nq),
        in_specs=[pl.BlockSpec((tq, width), lambda b, i: (b * nq + i, 0)),
                  pl.BlockSpec((mem_len, width), lambda b, i: (b, 0)),
                  pl.BlockSpec((mem_len, width), lambda b, i: (b, 0))],
        out_specs=pl.BlockSpec((tq, width), lambda b, i: (b * nq + i, 0)),
        out_shape=jax.ShapeDtypeStruct(q.shape, BF16),
        scratch_shapes=[pltpu.VMEM((2 * n_pairs, HEAD_DIM + ONES_ROWS, mem_len), BF16)],
        compiler_params=_cparams(2),
        name="cross_attn",
    )(q, mk, mv)


def _diff_attn_kernel(q1_ref, q2_ref, k1_ref, k2_ref, v_ref, bias_ref, bmax_ref, lam_ref, g_ref,
                      o_ref, vt_scr, ka_scr, kmax_scr, *, tq, lam_init):
    sub = tq // BLOCK
    seq = k1_ref.shape[0]
    kc = 2 * BLOCK
    n_chunks = seq // kc
    k_refs = (k1_ref, k2_ref)

    for head in range(2):
        v_t = v_ref[:, head * C_V_DIM:(head + 1) * C_V_DIM].astype(F32).T
        vt_scr[head, :C_V_DIM, :] = v_t.astype(BF16)
        vt_scr[head, C_V_DIM:, :] = jnp.ones((ONES_ROWS, seq), BF16)
    for t in range(2):
        ka_scr[t, 0], ka_scr[t, 1] = _keys_with_ones(k_refs[t][...])
        kmax_lo, kmax_hi = _pair_key_norm_max(k_refs[t][...])
        kmax_scr[t, 0:1, :] = kmax_lo
        kmax_scr[t, 1:2, :] = kmax_hi

    lam_vec = lam_ref[...]
    lam = (jnp.exp(jnp.sum(lam_vec[0:1] * lam_vec[1:2], axis=-1, keepdims=True))
           - jnp.exp(jnp.sum(lam_vec[2:3] * lam_vec[3:4], axis=-1, keepdims=True)) + lam_init)

    def query_tile(i, carry):
        rows = pl.ds(pl.multiple_of(i * tq, tq), tq)
        lhs = [_stack_pair_heads(q_ref[rows, :].astype(F32) * (HEAD_DIM ** -0.5 * LOG2E))
               for q_ref in (q1_ref, q2_ref)]
        _diff_attn_tile(i, rows, lhs, lam, bias_ref, bmax_ref, g_ref, o_ref, vt_scr, ka_scr,
                        kmax_scr, k_refs, tq=tq, sub=sub, kc=kc, n_chunks=n_chunks,
                        lam_init=lam_init)
        return carry

    lax.fori_loop(0, seq // tq, query_tile, 0)


def _diff_attn_tile(i, rows, lhs, lam, bias_ref, bmax_ref, g_ref, o_ref, vt_scr, ka_scr, kmax_scr,
                    k_refs, *, tq, sub, kc, n_chunks, lam_init):
    def bias_chunk(c):
        def tile(head, u, e):
            offset = (2 * c + e) - (i * sub + u)
            return bias_ref[head, jnp.clip(offset, -BIAS_FAR, BIAS_FAR) + BIAS_FAR]
        return jnp.concatenate(
            [jnp.concatenate([tile(head, u, e) for head in range(2) for u in range(sub)], axis=1)
             for e in range(2)], axis=0)

    def shifted_scores(t):
        return _shifted_scores(ka_scr[t, 0], ka_scr[t, 1], _with_shift_lane(
            lhs[t], kmax_scr[t, 0:1, :], kmax_scr[t, 1:2, :], bmax_ref[0], bmax_ref[1]))

    def combined_pv(s1_all, s2_all):
        p1, p2 = [], []
        for c in range(n_chunks):
            bias = bias_chunk(c)
            p1.append(jnp.exp2(s1_all[c * kc:(c + 1) * kc, :] + bias))
            p2.append(jnp.exp2(s2_all[c * kc:(c + 1) * kc, :] + bias))
        l1 = sum(jnp.sum(p, axis=0, keepdims=True) for p in p1)
        l2 = sum(jnp.sum(p, axis=0, keepdims=True) for p in p2)
        r1, r2 = 1.0 / l1, lam / l2
        w = jnp.concatenate([(a * r1 - b * r2).astype(BF16) for a, b in zip(p1, p2)], axis=0)
        outs = [jnp.dot(vt_scr[head, :C_V_DIM, :], w[:, head * tq:(head + 1) * tq],
                        preferred_element_type=F32) for head in range(2)]
        return outs, jnp.concatenate([l1, l2], axis=0)

    def pv_running_max(s_all):
        m = None
        acc = [None, None]
        for c in range(n_chunks):
            s = s_all[c * kc:(c + 1) * kc, :] + bias_chunk(c)
            m_c = jnp.max(s, axis=0, keepdims=True)
            m_new = m_c if m is None else jnp.maximum(m, m_c)
            p = jnp.exp2(s - m_new).astype(BF16)
            if m is not None:
                alpha = jnp.exp2(m - m_new)
            for head in range(2):
                cols = slice(head * tq, (head + 1) * tq)
                pv = jnp.dot(vt_scr[head, :, c * kc:(c + 1) * kc], p[:, cols],
                             preferred_element_type=F32)
                acc[head] = pv if m is None else acc[head] * alpha[:, cols] + pv
            m = m_new
        return acc

    def write(outs):
        g = g_ref[...] * (1.0 - lam_init)
        for head, out in enumerate(outs):
            ms = jnp.mean(out * out, axis=0, keepdims=True)
            y = out * lax.rsqrt(ms + EPS) * g
            o_ref[rows, head * C_V_DIM:(head + 1) * C_V_DIM] = y.T.astype(o_ref.dtype)

    outs, sums = combined_pv(shifted_scores(0), shifted_scores(1))
    write(outs)

    @pl.when(jnp.logical_not(jnp.min(sums) >= SUM_FLOOR))
    def _():
        acc1 = pv_running_max(_nt_dot(k_refs[0][...], lhs[0]))
        acc2 = pv_running_max(_nt_dot(k_refs[1][...], lhs[1]))
        normalised = lambda a: a[:C_V_DIM] * (1.0 / a[C_V_DIM:C_V_DIM + 1])
        write([normalised(a1) - lam * normalised(a2) for a1, a2 in zip(acc1, acc2)])


def _diff_attn(q1, q2, k1, k2, v, bias_t, bias_max, lam_vecs, subln_g, lam_init, batch, seq, tq):
    n_pairs = C_HEADS // 2
    n_tiles = bias_t.shape[1]
    seq_spec = pl.BlockSpec((seq, LANES), lambda b, p: (b, p))
    g_cols = jnp.broadcast_to(subln_g.reshape(C_V_DIM, 1), (C_V_DIM, tq))
    return pl.pallas_call(
        functools.partial(_diff_attn_kernel, tq=tq, lam_init=lam_init),
        grid=(batch, n_pairs),
        in_specs=[seq_spec, seq_spec, seq_spec, seq_spec,
                  pl.BlockSpec((seq, 2 * C_V_DIM), lambda b, p: (b, p)),
                  pl.BlockSpec((2, n_tiles, BLOCK, BLOCK), lambda b, p: (p, 0, 0, 0)),
                  pl.BlockSpec((2, 1, LANES), lambda b, p: (p, 0, 0)),
                  pl.BlockSpec(lam_vecs.shape, lambda b, p: (0, 0)),
                  pl.BlockSpec((C_V_DIM, tq), lambda b, p: (0, 0))],
        out_specs=pl.BlockSpec((seq, 2 * C_V_DIM), lambda b, p: (b, p)),
        out_shape=jax.ShapeDtypeStruct(v.shape, BF16),
        scratch_shapes=[pltpu.VMEM((2, C_V_DIM + ONES_ROWS, seq), BF16),
                        pltpu.VMEM((2, 2, seq, LANES), BF16),
                        pltpu.VMEM((2, 2, LANES), F32)],
        compiler_params=_cparams(2),
        name="diff_attn",
    )(q1, q2, k1, k2, v, bias_t, bias_max, lam_vecs, g_cols)


def _rope_tables(seq):
    rows = seq // GRID_W
    row = jnp.broadcast_to(jnp.arange(rows)[:, None], (rows, GRID_W)).reshape(-1)
    col = jnp.broadcast_to(jnp.arange(GRID_W)[None, :], (rows, GRID_W)).reshape(-1)
    half = HEAD_DIM // 2
    inv = 1.0 / (ROPE_THETA ** (jnp.arange(0, half, 2, dtype=F32) / half))
    ang_row = row.astype(F32)[:, None] * inv
    ang_col = col.astype(F32)[:, None] * inv
    cos = jnp.concatenate([jnp.cos(ang_row)] * 2 + [jnp.cos(ang_col)] * 2, axis=-1)
    sin = jnp.concatenate([-jnp.sin(ang_row), jnp.sin(ang_row),
                           -jnp.sin(ang_col), jnp.sin(ang_col)], axis=-1)
    return jnp.tile(cos, (1, 2)), jnp.tile(sin, (1, 2))


def kernel(x, mem, rel_bias, mem_norm, final_norm, even_norm, even_w_in, even_sink, even_q_norm, even_k_norm, even_w_mem_kv, even_w_out, odd_norm, odd_w_in, odd_lambda_q1, odd_lambda_k1, odd_lambda_q2, odd_lambda_k2, odd_subln, odd_w_mem_kv, odd_w_out):
    batch, seq, d = x.shape
    mem_len = mem.shape[1]
    tokens = batch * seq
    xw = X_HEADS * HEAD_DIM
    gw = A_HEADS * HEAD_DIM

    w_in0 = even_w_in[0].astype(BF16)
    w_out0 = even_w_out[0].astype(BF16)
    w_in1 = odd_w_in[0].astype(BF16)
    w_out1 = odd_w_out[0].astype(BF16)

    x2 = x.reshape(tokens, d)
    mem2 = mem.reshape(batch * mem_len, d)

    bias_near, bias_win = _bias_tiles(rel_bias)
    cos, sin_signed = _rope_tables(seq)
    head_bias_max = jnp.max(rel_bias, axis=0) * LOG2E

    w_mem = jnp.concatenate([even_w_mem_kv[0], odd_w_mem_kv[0]], axis=1).astype(BF16)
    mk0, mv0, mk1, mv1 = _norm_proj(mem2, mem_norm, w_mem, (xw,) * 4, TM_MEM, gate_last=False)

    rope_args = (cos, sin_signed, jnp.tile(even_q_norm[0], 2).reshape(1, LANES),
                 jnp.tile(even_k_norm[0], 2).reshape(1, LANES))
    aq, ak, av, bq, bk, bv, xq, gate = _norm_proj(
        x2, even_norm[0], w_in0, (gw, 128, 128, gw, 128, 128, xw, D_MODEL), TM_PROJ,
        gate_last=True, qk_rope=(3, 4), rope_args=rope_args, seq=seq)
    y_a = _window_attn(aq, ak, av, bias_win, even_sink[0] * LOG2E, batch, seq, WINDOW_GROUP)
    y_b = _dense_attn(bq, bk, bv, batch, seq, TQ_DENSE)
    y_x = _cross_attn(xq, mk0, mv0, batch, seq, TQ_CROSS)
    lam_init = 0.8 - 0.6 * math.exp(-0.3 * 1)
    cw = C_HEADS * HEAD_DIM
    h1, q1, q2, k1, k2, v, xq1, gate1 = _gate_out(
        [y_a, y_b, y_x], gate, w_out0, x2, odd_norm[0], TM_PROJ, w_next=w_in1,
        splits=(cw, cw, cw, cw, C_HEADS * C_V_DIM, xw, D_MODEL))
    lam_vecs = jnp.stack([odd_lambda_q1[0], odd_lambda_k1[0], odd_lambda_q2[0], odd_lambda_k2[0]])
    bias_max = jnp.broadcast_to(head_bias_max.reshape(C_HEADS, 1, 1), (C_HEADS, 1, LANES))
    y_c = _diff_attn(q1, q2, k1, k2, v, bias_near, bias_max, lam_vecs, odd_subln[0], lam_init,
                     batch, seq, TQ_DIFF)
    y_x1 = _cross_attn(xq1, mk1, mv1, batch, seq, TQ_CROSS)
    (out,) = _gate_out([y_c, y_x1], gate1, w_out1, h1, final_norm, TM_PROJ)
    return out.reshape(batch, seq, d)
```

```python
import functools
import math

import numpy as np
import jax
import jax.numpy as jnp
from jax import lax
from jax.experimental import pallas as pl
from jax.experimental.pallas import tpu as pltpu

D_MODEL = 1024
HEAD_DIM = 64
BLOCK = 128
WINDOW = 128
GRID_W = 64
A_HEADS = 6
C_HEADS = 6
C_V_DIM = 128
X_HEADS = 4
REL_BUCKETS = 32
REL_MAX_DIST = 128
ROPE_THETA = 10000.0
EPS = 1e-6
NEG_INF = -1e30
LOG2E = math.log2(math.e)
LANES = 128
ONES_ROWS = 16
PROJ_SUB_ROWS = 256
GATE_SUB_ROWS = 1024
PROJ_COL_GROUP = 512
TM_PROJ = 1024
TM_MEM = 256
TQ_DENSE = 128
TQ_DIFF = 256
TQ_CROSS = 2048
WINDOW_GROUP = 16
SUM_FLOOR = 2.0 ** -64
SHIFT_MARGIN = 2.0 ** -7
BIAS_FAR = 2
VMEM_LIMIT = 56 * 1024 * 1024

F32 = jnp.float32
BF16 = jnp.bfloat16


def _cparams(n_axes):
    return pltpu.CompilerParams(dimension_semantics=("arbitrary",) * n_axes,
                                vmem_limit_bytes=VMEM_LIMIT)


def _lane_lo(shape):
    return lax.broadcasted_iota(jnp.int32, shape, len(shape) - 1) < HEAD_DIM


def _rmsnorm_rows(x, g):
    ms = jnp.mean(x * x, axis=-1, keepdims=True)
    return x * lax.rsqrt(ms + EPS) * g


def _nt_dot(a, b):
    return lax.dot_general(a, b, (((1,), (1,)), ((), ())), preferred_element_type=F32)


def _norm_proj_kernel(*refs, qk_rope, gate_last):
    x_ref, g_ref, w_ref = refs[:3]
    if qk_rope is None:
        out_refs = refs[3:]
    else:
        cos_ref, sin_ref, gq_ref, gk_ref = refs[3:7]
        out_refs = refs[7:]
    tm = x_ref.shape[0]
    for r0 in range(0, tm, PROJ_SUB_ROWS):
        rows = slice(r0, min(r0 + PROJ_SUB_ROWS, tm))

        def post(j, y):
            if qk_rope is not None and j in qk_rope:
                gain, scale = ((gq_ref, HEAD_DIM ** -0.5 * LOG2E) if j == qk_rope[0]
                               else (gk_ref, 1.0))
                y = jnp.concatenate(
                    [_norm_rope_pair(y[:, t * LANES:(t + 1) * LANES], gain[...],
                                     cos_ref[rows, :], sin_ref[rows, :]) * scale
                     for t in range(y.shape[1] // LANES)], axis=1)
            if gate_last and j == len(out_refs) - 1:
                y = _silu(y)
            return y

        xn = _rmsnorm_rows(x_ref[rows, :], g_ref[...]).astype(BF16)
        _project_and_store(xn, w_ref, out_refs, rows, post)


def _project_and_store(xn, w_ref, out_refs, rows, post):
    n = w_ref.shape[1]
    starts = [0]
    for o_ref in out_refs:
        starts.append(starts[-1] + o_ref.shape[1])
    assert starts[-1] == n
    for c0 in range(0, n, PROJ_COL_GROUP):
        c1 = min(c0 + PROJ_COL_GROUP, n)
        y = jnp.dot(xn, w_ref[:, c0:c1], preferred_element_type=F32)
        for j, o_ref in enumerate(out_refs):
            lo, hi = max(c0, starts[j]), min(c1, starts[j + 1])
            if lo < hi:
                o_ref[rows, lo - starts[j]:hi - starts[j]] = post(
                    j, y[:, lo - c0:hi - c0]).astype(o_ref.dtype)


def _silu(y):
    return y * (1.0 / (1.0 + jnp.exp(-y)))


def _norm_proj(x, g, w_bf16, splits, tm, gate_last, qk_rope=None, rope_args=(), seq=None):
    rows, d = x.shape
    n = w_bf16.shape[1]
    assert sum(splits) == n and rows % tm == 0
    in_specs = [pl.BlockSpec((tm, d), lambda i: (i, 0)),
                pl.BlockSpec((1, d), lambda i: (0, 0)),
                pl.BlockSpec((d, n), lambda i: (0, 0))]
    if qk_rope is not None:
        pos_blocks = seq // tm
        in_specs += [pl.BlockSpec((tm, LANES), lambda i: (i % pos_blocks, 0)),
                     pl.BlockSpec((tm, LANES), lambda i: (i % pos_blocks, 0)),
                     pl.BlockSpec((1, LANES), lambda i: (0, 0)),
                     pl.BlockSpec((1, LANES), lambda i: (0, 0))]
    return pl.pallas_call(
        functools.partial(_norm_proj_kernel, qk_rope=qk_rope, gate_last=gate_last),
        grid=(rows // tm,),
        in_specs=in_specs,
        out_specs=[pl.BlockSpec((tm, s), lambda i: (i, 0)) for s in splits],
        out_shape=[jax.ShapeDtypeStruct((rows, s), BF16) for s in splits],
        compiler_params=_cparams(1),
        name="norm_proj",
    )(x, g.reshape(1, d), w_bf16, *rope_args)


def _gate_out_kernel(*refs, n_parts, after):
    y_refs = refs[:n_parts]
    gate_ref, w_ref, res_ref, g_ref = refs[n_parts:n_parts + 4]
    if after == "proj":
        w_next_ref, h_ref = refs[n_parts + 4:n_parts + 6]
        out_refs = refs[n_parts + 6:]
    else:
        h_ref = refs[n_parts + 4]
    tm = res_ref.shape[0]
    for r0 in range(0, tm, GATE_SUB_ROWS):
        rows = slice(r0, min(r0 + GATE_SUB_ROWS, tm))
        parts = []
        c0 = 0
        for y_ref in y_refs:
            width = y_ref.shape[1]
            parts.append(y_ref[rows, :] * gate_ref[rows, c0:c0 + width])
            c0 += width
        h = res_ref[rows, :] + jnp.dot(jnp.concatenate(parts, axis=1), w_ref[...],
                                       preferred_element_type=F32)
        if after == "norm":
            h_ref[rows, :] = _rmsnorm_rows(h, g_ref[...])
            continue
        h_ref[rows, :] = h
        gate_path = len(out_refs) - 1
        _project_and_store(_rmsnorm_rows(h, g_ref[...]).astype(BF16), w_next_ref, out_refs, rows,
                           lambda j, y: _silu(y) if j == gate_path else y)


def _gate_out(y_parts, gate, w_bf16, resid, norm_g, tm, w_next=None, splits=()):
    rows, d = resid.shape
    mix = w_bf16.shape[0]
    after = "norm" if w_next is None else "proj"
    row_block = lambda width: pl.BlockSpec((tm, width), lambda i: (i, 0))
    whole = lambda a: pl.BlockSpec(a.shape, lambda i: (0, 0))
    in_specs = [row_block(y.shape[1]) for y in y_parts]
    in_specs += [row_block(mix), whole(w_bf16), row_block(d), pl.BlockSpec((1, d), lambda i: (0, 0))]
    args = list(y_parts) + [gate, w_bf16, resid, norm_g.reshape(1, d)]
    out_specs = [row_block(d)]
    out_shape = [jax.ShapeDtypeStruct((rows, d), F32)]
    if after == "proj":
        assert sum(splits) == w_next.shape[1]
        in_specs.append(whole(w_next))
        args.append(w_next)
        out_specs += [row_block(s) for s in splits]
        out_shape += [jax.ShapeDtypeStruct((rows, s), BF16) for s in splits]
    return pl.pallas_call(
        functools.partial(_gate_out_kernel, n_parts=len(y_parts), after=after),
        grid=(rows // tm,),
        in_specs=in_specs,
        out_specs=out_specs,
        out_shape=out_shape,
        compiler_params=_cparams(1),
        name="gate_out",
    )(*args)


def _bucket_thresholds():
    nb = REL_BUCKETS // 2
    max_exact = nb // 2
    n = np.arange(0, 4 * REL_MAX_DIST)
    nf = np.maximum(n, 1).astype(np.float32)
    large = max_exact + (np.log(nf / np.float32(max_exact))
                         / np.float32(math.log(REL_MAX_DIST / max_exact))
                         * np.float32(nb - max_exact)).astype(np.int32)
    bucket = np.where(n < max_exact, n, np.minimum(large, nb - 1))
    assert np.all(np.diff(bucket) >= 0) and bucket[0] == 0 and bucket[-1] == nb - 1
    thr = [int(np.argmax(bucket >= k)) for k in range(1, nb)]
    assert thr[-1] < REL_MAX_DIST
    return thr


def _bias_tiles_kernel(tab_ref, near_ref, win_ref):
    h = pl.program_id(0)
    thr = _bucket_thresholds()
    nb = REL_BUCKETS // 2
    row = lax.broadcasted_iota(jnp.int32, (BLOCK, BLOCK), 0)
    col = lax.broadcasted_iota(jnp.int32, (BLOCK, BLOCK), 1)
    def tile_of(rel):
        n = jnp.abs(rel)
        neg = jnp.full((BLOCK, BLOCK), tab_ref[0, h], F32)
        pos = jnp.full((BLOCK, BLOCK), tab_ref[nb, h], F32)
        for b in range(1, nb):
            ge = n >= thr[b - 1]
            neg = jnp.where(ge, tab_ref[b, h], neg)
            pos = jnp.where(ge, tab_ref[nb + b, h], pos)
        return jnp.where(rel > 0, pos, neg) * LOG2E

    for k in range(2 * BIAS_FAR + 1):
        rel = (k - BIAS_FAR) * BLOCK + row - col
        tile = tile_of(rel)
        near_ref[0, k] = tile
        if abs(k - BIAS_FAR) <= 1:
            j = k - BIAS_FAR + 1
            win_ref[j * BLOCK:(j + 1) * BLOCK, :] = jnp.where(jnp.abs(rel) <= WINDOW, tile, NEG_INF)


def _bias_tiles(rel_bias):
    assert _bucket_thresholds()[-1] <= (BIAS_FAR - 1) * BLOCK + 1
    n_tiles = 2 * BIAS_FAR + 1
    n_heads = rel_bias.shape[1]
    return pl.pallas_call(
        _bias_tiles_kernel,
        grid=(n_heads,),
        in_specs=[pl.BlockSpec(memory_space=pltpu.SMEM)],
        out_specs=[pl.BlockSpec((1, n_tiles, BLOCK, BLOCK), lambda h: (h, 0, 0, 0)),
                   pl.BlockSpec((3 * BLOCK, BLOCK), lambda h: (0, h))],
        out_shape=[jax.ShapeDtypeStruct((n_heads, n_tiles, BLOCK, BLOCK), F32),
                   jax.ShapeDtypeStruct((3 * BLOCK, n_heads * BLOCK), F32)],
        compiler_params=_cparams(1),
        name="bias_tiles",
    )(rel_bias)


def _stack_gqa_heads(q_tiles):
    lo = _lane_lo(q_tiles[0].shape)
    swap = lambda q: pltpu.roll(q, HEAD_DIM, 1)
    heads_lo = [q_tiles[0], swap(q_tiles[0]), q_tiles[1]]
    heads_hi = [q_tiles[1], swap(q_tiles[2]), q_tiles[2]]
    return jnp.concatenate([jnp.where(lo, q, 0.0).astype(BF16) for q in heads_lo]
                           + [jnp.where(lo, 0.0, q).astype(BF16) for q in heads_hi], axis=0)


def _unstack_gqa_heads(out_lo, out_hi, width):
    blk = lambda o, j: o[:, j * width:(j + 1) * width]
    return [jnp.concatenate([blk(out_lo, 0), blk(out_lo, 1)], axis=0),
            jnp.concatenate([blk(out_lo, 2), blk(out_hi, 0)], axis=0),
            jnp.concatenate([blk(out_hi, 1), blk(out_hi, 2)], axis=0)]


def _stack_pair_heads(q):
    lo = _lane_lo(q.shape)
    return jnp.concatenate([jnp.where(lo, q, 0.0).astype(BF16),
                            jnp.where(lo, 0.0, q).astype(BF16)], axis=0)


def _with_shift_lane(lhs, kmax_lo, kmax_hi, bias_max_lo, bias_max_hi):
    half = lhs.shape[0] // 2
    lf = lhs.astype(F32)
    q_sq = jnp.sum(lf * lf, axis=-1, keepdims=True)
    q_norm = q_sq * lax.rsqrt(jnp.maximum(q_sq, 1e-30))
    row = lax.broadcasted_iota(jnp.int32, lhs.shape, 0)
    lane = lax.broadcasted_iota(jnp.int32, lhs.shape, 1)
    lo_row = row < half
    shift = (q_norm * jnp.where(lo_row, kmax_lo, kmax_hi)
             + jnp.where(lo_row, bias_max_lo, bias_max_hi))
    shift = shift + jnp.abs(shift) * SHIFT_MARGIN
    slot_lane = jnp.where(lo_row, HEAD_DIM, 0)
    return jnp.where(lane == slot_lane, -shift, lf).astype(BF16)


def _keys_with_ones(k_pair):
    lane = lax.broadcasted_iota(jnp.int32, k_pair.shape, 1)
    kf = k_pair.astype(F32)
    k_lo = jnp.where(lane < HEAD_DIM, kf, jnp.where(lane == HEAD_DIM, 1.0, 0.0))
    k_hi = jnp.where(lane >= HEAD_DIM, kf, jnp.where(lane == 0, 1.0, 0.0))
    return k_lo.astype(BF16), k_hi.astype(BF16)


def _shifted_scores(ka_lo, ka_hi, lhs_shifted):
    half = lhs_shifted.shape[0] // 2
    return jnp.concatenate([_nt_dot(ka_lo, lhs_shifted[:half]), _nt_dot(ka_hi, lhs_shifted[half:])],
                           axis=1)


def _pair_key_norm_max(k_pair):
    kf = k_pair.astype(F32)
    row = lax.broadcasted_iota(jnp.int32, (LANES, 2 * LANES), 0)
    col = lax.broadcasted_iota(jnp.int32, (LANES, 2 * LANES), 1)
    half_sel = jnp.where((row < HEAD_DIM) == (col < LANES), 1.0, 0.0).astype(BF16)
    sq = jnp.dot((kf * kf).astype(BF16), half_sel, preferred_element_type=F32)
    norm_max = jnp.sqrt(jnp.max(sq, axis=0, keepdims=True))
    return norm_max[:, :LANES], norm_max[:, LANES:]


def _window_attn_kernel(q_ref, k_ref, v_ref, bias_ref, sink_row_ref,
                        o_ref, kpad_scr, vtpad_scr, *, n_blocks, group):
    i = pl.program_id(1)
    half = 3 * BLOCK

    @pl.when(i == 0)
    def _():
        zero_blk = jnp.zeros((1, BLOCK, LANES), BF16)
        kpad_scr[0:1] = zero_blk
        kpad_scr[n_blocks + 1:n_blocks + 2] = zero_blk
        kpad_scr[1:n_blocks + 1] = k_ref[...].reshape(n_blocks, BLOCK, LANES)
        v_t = v_ref[...].astype(F32).T.astype(BF16)
        ones = jnp.ones((ONES_ROWS, BLOCK), BF16)
        zero_v = jnp.zeros((HEAD_DIM + ONES_ROWS, BLOCK), BF16)
        for g in range(2):
            vtpad_scr[g, 0] = zero_v
            vtpad_scr[g, n_blocks + 1] = zero_v
            for blk in range(n_blocks):
                vtpad_scr[g, blk + 1, :HEAD_DIM, :] = v_t[g * HEAD_DIM:(g + 1) * HEAD_DIM,
                                                           blk * BLOCK:(blk + 1) * BLOCK]
                vtpad_scr[g, blk + 1, HEAD_DIM:, :] = ones

    sink_row = sink_row_ref[...]

    for u in range(group):
        n = i * group + u
        qf = q_ref[u * BLOCK:(u + 1) * BLOCK, :].astype(F32) * (HEAD_DIM ** -0.5 * LOG2E)
        lhs = _stack_gqa_heads([qf[:, j * LANES:(j + 1) * LANES] for j in range(3)])
        k_win = kpad_scr[pl.ds(n, 3)].reshape(3 * BLOCK, LANES)
        s = _nt_dot(k_win, lhs) + bias_ref[...]
        s = jnp.concatenate([jnp.where(n > 0, s[:BLOCK], NEG_INF), s[BLOCK:2 * BLOCK],
                             jnp.where(n < n_blocks - 1, s[2 * BLOCK:], NEG_INF)], axis=0)
        m = jnp.maximum(jnp.max(s, axis=0, keepdims=True), sink_row)
        p = jnp.exp2(s - m).astype(BF16)
        p_sink = jnp.exp2(sink_row - m)
        outs = []
        for g in range(2):
            acc = None
            for j in range(3):
                pv = jnp.dot(vtpad_scr[g, n + j], p[j * BLOCK:(j + 1) * BLOCK, g * half:(g + 1) * half],
                             preferred_element_type=F32)
                acc = pv if acc is None else acc + pv
            total = acc[HEAD_DIM:HEAD_DIM + 1] + p_sink[:, g * half:(g + 1) * half]
            outs.append(acc[:HEAD_DIM] * (1.0 / total))
        for j, pair_t in enumerate(_unstack_gqa_heads(outs[0], outs[1], BLOCK)):
            o_ref[u * BLOCK:(u + 1) * BLOCK, j * LANES:(j + 1) * LANES] = pair_t.T.astype(o_ref.dtype)


def _window_attn(q, k, v, bias_win_t, sink, batch, seq, group):
    nb = seq // BLOCK
    steps = nb // group
    rows = A_HEADS * BLOCK
    sink_row = jnp.repeat(sink, BLOCK).reshape(1, rows)
    const = lambda shape: pl.BlockSpec(shape, lambda b, i: (0,) * len(shape))
    return pl.pallas_call(
        functools.partial(_window_attn_kernel, n_blocks=nb, group=group),
        grid=(batch, steps),
        in_specs=[pl.BlockSpec((group * BLOCK, 3 * LANES), lambda b, i: (b * steps + i, 0)),
                  pl.BlockSpec((seq, LANES), lambda b, i: (b, 0)),
                  pl.BlockSpec((seq, LANES), lambda b, i: (b, 0)),
                  const(bias_win_t.shape), const((1, rows))],
        out_specs=pl.BlockSpec((group * BLOCK, 3 * LANES), lambda b, i: (b * steps + i, 0)),
        out_shape=jax.ShapeDtypeStruct(q.shape, BF16),
        scratch_shapes=[pltpu.VMEM((nb + 2, BLOCK, LANES), BF16),
                        pltpu.VMEM((2, nb + 2, HEAD_DIM + ONES_ROWS, BLOCK), BF16)],
        compiler_params=_cparams(2),
        name="window_attn",
    )(q, k, v, bias_win_t, sink_row)


def _norm_rope_pair(x, g, cos, sin_signed):
    lo = _lane_lo(x.shape)
    x2 = x * x
    ss_lo = jnp.sum(jnp.where(lo, x2, 0.0), axis=-1, keepdims=True)
    ss_hi = jnp.sum(jnp.where(lo, 0.0, x2), axis=-1, keepdims=True)
    ms = jnp.where(lo, ss_lo, ss_hi) * (1.0 / HEAD_DIM)
    y = x * lax.rsqrt(ms + EPS) * g
    lane = lax.broadcasted_iota(jnp.int32, x.shape, 1)
    quarter = HEAD_DIM // 4
    first = (lane & quarter) == 0
    partner = jnp.where(first, pltpu.roll(y, LANES - quarter, 1), pltpu.roll(y, quarter, 1))
    return y * cos + partner * sin_signed


def _dense_attn_kernel(q_ref, k_ref, v_ref, o_ref, ka_scr, vt_scr, kmax_scr, *, tq):
    seq = k_ref.shape[0]
    kc = 2 * BLOCK
    n_chunks = seq // kc
    grp = 3 * tq

    ka_scr[0], ka_scr[1] = _keys_with_ones(k_ref[...])
    kmax_lo, kmax_hi = _pair_key_norm_max(k_ref[...])
    kmax_scr[0:1, :] = kmax_lo
    kmax_scr[1:2, :] = kmax_hi
    v_t = v_ref[...].astype(F32).T.astype(BF16)
    for g in range(2):
        vt_scr[g, :HEAD_DIM, :] = v_t[g * HEAD_DIM:(g + 1) * HEAD_DIM]
        vt_scr[g, HEAD_DIM:, :] = jnp.ones((ONES_ROWS, seq), BF16)

    def query_tile(i, carry):
        rows = pl.ds(pl.multiple_of(i * tq, tq), tq)
        lhs = _stack_gqa_heads([q_ref[rows, j * LANES:(j + 1) * LANES].astype(F32)
                                for j in range(3)])

        def write(acc):
            outs = [a[:HEAD_DIM] * (1.0 / a[HEAD_DIM:HEAD_DIM + 1]) for a in acc]
            for j, pair_t in enumerate(_unstack_gqa_heads(outs[0], outs[1], tq)):
                o_ref[rows, j * LANES:(j + 1) * LANES] = pair_t.T.astype(o_ref.dtype)

        s_all = _shifted_scores(ka_scr[0], ka_scr[1], _with_shift_lane(
            lhs, kmax_scr[0:1, :], kmax_scr[1:2, :], 0.0, 0.0))
        p = jnp.exp2(s_all).astype(BF16)
        acc = [jnp.dot(vt_scr[g], p[:, g * grp:(g + 1) * grp], preferred_element_type=F32)
               for g in range(2)]
        write(acc)
        sums = jnp.concatenate([a[HEAD_DIM:HEAD_DIM + 1] for a in acc], axis=0)

        @pl.when(jnp.logical_not(jnp.min(sums) >= SUM_FLOOR))
        def _():
            s_exact = _nt_dot(k_ref[...], lhs)
            m = None
            acc2 = [None, None]
            for c in range(n_chunks):
                s = s_exact[c * kc:(c + 1) * kc, :]
                m_c = jnp.max(s, axis=0, keepdims=True)
                m_new = m_c if m is None else jnp.maximum(m, m_c)
                pc = jnp.exp2(s - m_new).astype(BF16)
                if m is not None:
                    alpha = jnp.exp2(m - m_new)
                for g in range(2):
                    cols = slice(g * grp, (g + 1) * grp)
                    pv = jnp.dot(vt_scr[g, :, c * kc:(c + 1) * kc], pc[:, cols],
                                 preferred_element_type=F32)
                    acc2[g] = pv if m is None else acc2[g] * alpha[:, cols] + pv
                m = m_new
            write(acc2)

        return carry

    lax.fori_loop(0, seq // tq, query_tile, 0)


def _dense_attn(q, k, v, batch, seq, tq):
    return pl.pallas_call(
        functools.partial(_dense_attn_kernel, tq=tq),
        grid=(batch,),
        in_specs=[pl.BlockSpec((seq, 3 * LANES), lambda b: (b, 0)),
                  pl.BlockSpec((seq, LANES), lambda b: (b, 0)),
                  pl.BlockSpec((seq, LANES), lambda b: (b, 0))],
        out_specs=pl.BlockSpec((seq, 3 * LANES), lambda b: (b, 0)),
        out_shape=jax.ShapeDtypeStruct(q.shape, BF16),
        scratch_shapes=[pltpu.VMEM((2, seq, LANES), BF16),
                        pltpu.VMEM((2, HEAD_DIM + ONES_ROWS, seq), BF16),
                        pltpu.VMEM((2, LANES), F32)],
        compiler_params=_cparams(1),
        name="dense_attn",
    )(q, k, v)


def _cross_attn_kernel(q_ref, mk_ref, mv_ref, o_ref, vt_scr, *, tq):
    i = pl.program_id(1)
    n_pairs = q_ref.shape[1] // LANES
    mem_len = mk_ref.shape[0]

    @pl.when(i == 0)
    def _():
        v_t = mv_ref[...].astype(F32).T.astype(BF16)
        for h in range(2 * n_pairs):
            vt_scr[h, :HEAD_DIM, :] = v_t[h * HEAD_DIM:(h + 1) * HEAD_DIM]
            vt_scr[h, HEAD_DIM:, :] = jnp.ones((ONES_ROWS, mem_len), BF16)

    for j in range(n_pairs):
        cols = slice(j * LANES, (j + 1) * LANES)
        lhs = _stack_pair_heads(q_ref[:, cols].astype(F32) * (HEAD_DIM ** -0.5 * LOG2E))
        s = _nt_dot(mk_ref[:, cols], lhs)
        p = jnp.exp2(s - jnp.max(s, axis=0, keepdims=True)).astype(BF16)
        acc = [jnp.dot(vt_scr[2 * j + h], p[:, h * tq:(h + 1) * tq], preferred_element_type=F32)
               for h in range(2)]
        pair_t = jnp.concatenate([a[:HEAD_DIM] * (1.0 / a[HEAD_DIM:HEAD_DIM + 1]) for a in acc],
                                 axis=0)
        o_ref[:, cols] = pair_t.T.astype(o_ref.dtype)


def _cross_attn(q, mk, mv, batch, seq, tq):
    nq = seq // tq
    mem_len = mk.shape[0] // batch
    width = q.shape[1]
    n_pairs = width // LANES
    return pl.pallas_call(
        functools.partial(_cross_attn_kernel, tq=tq),
        grid=(batch, nq),
        in_specs=[pl.BlockSpec((tq, width), lambda b, i: (b * nq + i, 0)),
                  pl.BlockSpec((mem_len, width), lambda b, i: (b, 0)),
                  pl.BlockSpec((mem_len, width), lambda b, i: (b, 0))],
        out_specs=pl.BlockSpec((tq, width), lambda b, i: (b * nq + i, 0)),
        out_shape=jax.ShapeDtypeStruct(q.shape, BF16),
        scratch_shapes=[pltpu.VMEM((2 * n_pairs, HEAD_DIM + ONES_ROWS, mem_len), BF16)],
        compiler_params=_cparams(2),
        name="cross_attn",
    )(q, mk, mv)


def _diff_attn_kernel(q1_ref, q2_ref, k1_ref, k2_ref, v_ref, bias_ref, bmax_ref, lam_ref, g_ref,
                      o_ref, vt_scr, ka_scr, kmax_scr, *, tq, lam_init):
    sub = tq // BLOCK
    seq = k1_ref.shape[0]
    kc = 2 * BLOCK
    n_chunks = seq // kc
    k_refs = (k1_ref, k2_ref)

    for head in range(2):
        v_t = v_ref[:, head * C_V_DIM:(head + 1) * C_V_DIM].astype(F32).T
        vt_scr[head, :C_V_DIM, :] = v_t.astype(BF16)
        vt_scr[head, C_V_DIM:, :] = jnp.ones((ONES_ROWS, seq), BF16)
    for t in range(2):
        ka_scr[t, 0], ka_scr[t, 1] = _keys_with_ones(k_refs[t][...])
        kmax_lo, kmax_hi = _pair_key_norm_max(k_refs[t][...])
        kmax_scr[t, 0:1, :] = kmax_lo
        kmax_scr[t, 1:2, :] = kmax_hi

    lam_vec = lam_ref[...]
    lam = (jnp.exp(jnp.sum(lam_vec[0:1] * lam_vec[1:2], axis=-1, keepdims=True))
           - jnp.exp(jnp.sum(lam_vec[2:3] * lam_vec[3:4], axis=-1, keepdims=True)) + lam_init)

    def query_tile(i, carry):
        rows = pl.ds(pl.multiple_of(i * tq, tq), tq)
        lhs = [_stack_pair_heads(q_ref[rows, :].astype(F32) * (HEAD_DIM ** -0.5 * LOG2E))
               for q_ref in (q1_ref, q2_ref)]
        _diff_attn_tile(i, rows, lhs, lam, bias_ref, bmax_ref, g_ref, o_ref, vt_scr, ka_scr,
                        kmax_scr, k_refs, tq=tq, sub=sub, kc=kc, n_chunks=n_chunks,
                        lam_init=lam_init)
        return carry

    lax.fori_loop(0, seq // tq, query_tile, 0)


def _diff_attn_tile(i, rows, lhs, lam, bias_ref, bmax_ref, g_ref, o_ref, vt_scr, ka_scr, kmax_scr,
                    k_refs, *, tq, sub, kc, n_chunks, lam_init):
    def bias_chunk(c):
        def tile(head, u, e):
            offset = (2 * c + e) - (i * sub + u)
            return bias_ref[head, jnp.clip(offset, -BIAS_FAR, BIAS_FAR) + BIAS_FAR]
        return jnp.concatenate(
            [jnp.concatenate([tile(head, u, e) for head in range(2) for u in range(sub)], axis=1)
             for e in range(2)], axis=0)

    def shifted_scores(t):
        return _shifted_scores(ka_scr[t, 0], ka_scr[t, 1], _with_shift_lane(
            lhs[t], kmax_scr[t, 0:1, :], kmax_scr[t, 1:2, :], bmax_ref[0], bmax_ref[1]))

    def pv_of_shifted(s_all):
        p = jnp.concatenate([jnp.exp2(s_all[c * kc:(c + 1) * kc, :] + bias_chunk(c)).astype(BF16)
                             for c in range(n_chunks)], axis=0)
        return [jnp.dot(vt_scr[head], p[:, head * tq:(head + 1) * tq],
                        preferred_element_type=F32) for head in range(2)]

    def pv_running_max(s_all):
        m = None
        acc = [None, None]
        for c in range(n_chunks):
            s = s_all[c * kc:(c + 1) * kc, :] + bias_chunk(c)
            m_c = jnp.max(s, axis=0, keepdims=True)
            m_new = m_c if m is None else jnp.maximum(m, m_c)
            p = jnp.exp2(s - m_new).astype(BF16)
            if m is not None:
                alpha = jnp.exp2(m - m_new)
            for head in range(2):
                cols = slice(head * tq, (head + 1) * tq)
                pv = jnp.dot(vt_scr[head, :, c * kc:(c + 1) * kc], p[:, cols],
                             preferred_element_type=F32)
                acc[head] = pv if m is None else acc[head] * alpha[:, cols] + pv
            m = m_new
        return acc

    def write(acc1, acc2):
        g = g_ref[...] * (1.0 - lam_init)
        for head in range(2):
            o1 = acc1[head][:C_V_DIM] * (1.0 / acc1[head][C_V_DIM:C_V_DIM + 1])
            o2 = acc2[head][:C_V_DIM] * (1.0 / acc2[head][C_V_DIM:C_V_DIM + 1])
            out = o1 - lam * o2
            ms = jnp.mean(out * out, axis=0, keepdims=True)
            y = out * lax.rsqrt(ms + EPS) * g
            o_ref[rows, head * C_V_DIM:(head + 1) * C_V_DIM] = y.T.astype(o_ref.dtype)

    s1 = shifted_scores(0)
    s2 = shifted_scores(1)
    acc1 = pv_of_shifted(s1)
    acc2 = pv_of_shifted(s2)
    write(acc1, acc2)
    sums = jnp.concatenate([a[C_V_DIM:C_V_DIM + 1] for a in acc1 + acc2], axis=0)

    @pl.when(jnp.logical_not(jnp.min(sums) >= SUM_FLOOR))
    def _():
        write(pv_running_max(_nt_dot(k_refs[0][...], lhs[0])),
              pv_running_max(_nt_dot(k_refs[1][...], lhs[1])))


def _diff_attn(q1, q2, k1, k2, v, bias_t, bias_max, lam_vecs, subln_g, lam_init, batch, seq, tq):
    n_pairs = C_HEADS // 2
    n_tiles = bias_t.shape[1]
    seq_spec = pl.BlockSpec((seq, LANES), lambda b, p: (b, p))
    g_cols = jnp.broadcast_to(subln_g.reshape(C_V_DIM, 1), (C_V_DIM, tq))
    return pl.pallas_call(
        functools.partial(_diff_attn_kernel, tq=tq, lam_init=lam_init),
        grid=(batch, n_pairs),
        in_specs=[seq_spec, seq_spec, seq_spec, seq_spec,
                  pl.BlockSpec((seq, 2 * C_V_DIM), lambda b, p: (b, p)),
                  pl.BlockSpec((2, n_tiles, BLOCK, BLOCK), lambda b, p: (p, 0, 0, 0)),
                  pl.BlockSpec((2, 1, LANES), lambda b, p: (p, 0, 0)),
                  pl.BlockSpec(lam_vecs.shape, lambda b, p: (0, 0)),
                  pl.BlockSpec((C_V_DIM, tq), lambda b, p: (0, 0))],
        out_specs=pl.BlockSpec((seq, 2 * C_V_DIM), lambda b, p: (b, p)),
        out_shape=jax.ShapeDtypeStruct(v.shape, BF16),
        scratch_shapes=[pltpu.VMEM((2, C_V_DIM + ONES_ROWS, seq), BF16),
                        pltpu.VMEM((2, 2, seq, LANES), BF16),
                        pltpu.VMEM((2, 2, LANES), F32)],
        compiler_params=_cparams(2),
        name="diff_attn",
    )(q1, q2, k1, k2, v, bias_t, bias_max, lam_vecs, g_cols)


def _rope_tables(seq):
    rows = seq // GRID_W
    row = jnp.broadcast_to(jnp.arange(rows)[:, None], (rows, GRID_W)).reshape(-1)
    col = jnp.broadcast_to(jnp.arange(GRID_W)[None, :], (rows, GRID_W)).reshape(-1)
    half = HEAD_DIM // 2
    inv = 1.0 / (ROPE_THETA ** (jnp.arange(0, half, 2, dtype=F32) / half))
    ang_row = row.astype(F32)[:, None] * inv
    ang_col = col.astype(F32)[:, None] * inv
    cos = jnp.concatenate([jnp.cos(ang_row)] * 2 + [jnp.cos(ang_col)] * 2, axis=-1)
    sin = jnp.concatenate([-jnp.sin(ang_row), jnp.sin(ang_row),
                           -jnp.sin(ang_col), jnp.sin(ang_col)], axis=-1)
    return jnp.tile(cos, (1, 2)), jnp.tile(sin, (1, 2))


def kernel(x, mem, rel_bias, mem_norm, final_norm, even_norm, even_w_in, even_sink, even_q_norm, even_k_norm, even_w_mem_kv, even_w_out, odd_norm, odd_w_in, odd_lambda_q1, odd_lambda_k1, odd_lambda_q2, odd_lambda_k2, odd_subln, odd_w_mem_kv, odd_w_out):
    batch, seq, d = x.shape
    mem_len = mem.shape[1]
    tokens = batch * seq
    xw = X_HEADS * HEAD_DIM
    gw = A_HEADS * HEAD_DIM

    w_in0 = even_w_in[0].astype(BF16)
    w_out0 = even_w_out[0].astype(BF16)
    w_in1 = odd_w_in[0].astype(BF16)
    w_out1 = odd_w_out[0].astype(BF16)

    x2 = x.reshape(tokens, d)
    mem2 = mem.reshape(batch * mem_len, d)

    bias_near, bias_win = _bias_tiles(rel_bias)
    cos, sin_signed = _rope_tables(seq)
    head_bias_max = jnp.max(rel_bias, axis=0) * LOG2E

    w_mem = jnp.concatenate([even_w_mem_kv[0], odd_w_mem_kv[0]], axis=1).astype(BF16)
    mk0, mv0, mk1, mv1 = _norm_proj(mem2, mem_norm, w_mem, (xw,) * 4, TM_MEM, gate_last=False)

    rope_args = (cos, sin_signed, jnp.tile(even_q_norm[0], 2).reshape(1, LANES),
                 jnp.tile(even_k_norm[0], 2).reshape(1, LANES))
    aq, ak, av, bq, bk, bv, xq, gate = _norm_proj(
        x2, even_norm[0], w_in0, (gw, 128, 128, gw, 128, 128, xw, D_MODEL), TM_PROJ,
        gate_last=True, qk_rope=(3, 4), rope_args=rope_args, seq=seq)
    y_a = _window_attn(aq, ak, av, bias_win, even_sink[0] * LOG2E, batch, seq, WINDOW_GROUP)
    y_b = _dense_attn(bq, bk, bv, batch, seq, TQ_DENSE)
    y_x = _cross_attn(xq, mk0, mv0, batch, seq, TQ_CROSS)
    lam_init = 0.8 - 0.6 * math.exp(-0.3 * 1)
    cw = C_HEADS * HEAD_DIM
    h1, q1, q2, k1, k2, v, xq1, gate1 = _gate_out(
        [y_a, y_b, y_x], gate, w_out0, x2, odd_norm[0], TM_PROJ, w_next=w_in1,
        splits=(cw, cw, cw, cw, C_HEADS * C_V_DIM, xw, D_MODEL))
    lam_vecs = jnp.stack([odd_lambda_q1[0], odd_lambda_k1[0], odd_lambda_q2[0], odd_lambda_k2[0]])
    bias_max = jnp.broadcast_to(head_bias_max.reshape(C_HEADS, 1, 1), (C_HEADS, 1, LANES))
    y_c = _diff_attn(q1, q2, k1, k2, v, bias_near, bias_max, lam_vecs, odd_subln[0], lam_init,
                     batch, seq, TQ_DIFF)
    y_x1 = _cross_attn(xq1, mk1, mv1, batch, seq, TQ_CROSS)
    (out,) = _gate_out([y_c, y_x1], gate1, w_out1, h1, final_norm, TM_PROJ)
    return out.reshape(batch, seq, d)
```

```python
import functools
import math

import numpy as np
import jax
import jax.numpy as jnp
from jax import lax
from jax.experimental import pallas as pl
from jax.experimental.pallas import tpu as pltpu

D_MODEL = 1024
HEAD_DIM = 64
BLOCK = 128
WINDOW = 128
GRID_W = 64
A_HEADS = 6
C_HEADS = 6
C_V_DIM = 128
X_HEADS = 4
REL_BUCKETS = 32
REL_MAX_DIST = 128
ROPE_THETA = 10000.0
EPS = 1e-6
NEG_INF = -1e30
LOG2E = math.log2(math.e)
LANES = 128
ONES_ROWS = 16
PROJ_SUB_ROWS = 256
GATE_SUB_ROWS = 1024
PROJ_COL_GROUP = 512
TM_PROJ = 1024
TM_MEM = 256
TQ_DENSE = 512
TQ_DIFF = 1024
TQ_CROSS = 2048
WINDOW_GROUP = 16
SUM_FLOOR = 2.0 ** -64
SHIFT_MARGIN = 2.0 ** -7
BIAS_FAR = 2
VMEM_LIMIT = 56 * 1024 * 1024

F32 = jnp.float32
BF16 = jnp.bfloat16


def _cparams(n_axes):
    return pltpu.CompilerParams(dimension_semantics=("arbitrary",) * n_axes,
                                vmem_limit_bytes=VMEM_LIMIT)


def _lane_lo(shape):
    return lax.broadcasted_iota(jnp.int32, shape, len(shape) - 1) < HEAD_DIM


def _rmsnorm_rows(x, g):
    ms = jnp.mean(x * x, axis=-1, keepdims=True)
    return x * lax.rsqrt(ms + EPS) * g


def _nt_dot(a, b):
    return lax.dot_general(a, b, (((1,), (1,)), ((), ())), preferred_element_type=F32)


def _norm_proj_kernel(*refs, qk_rope, gate_last):
    x_ref, g_ref, w_ref = refs[:3]
    if qk_rope is None:
        out_refs = refs[3:]
    else:
        cos_ref, sin_ref, gq_ref, gk_ref = refs[3:7]
        out_refs = refs[7:]
    tm = x_ref.shape[0]
    for r0 in range(0, tm, PROJ_SUB_ROWS):
        rows = slice(r0, min(r0 + PROJ_SUB_ROWS, tm))

        def post(j, y):
            if qk_rope is not None and j in qk_rope:
                gain, scale = ((gq_ref, HEAD_DIM ** -0.5 * LOG2E) if j == qk_rope[0]
                               else (gk_ref, 1.0))
                y = jnp.concatenate(
                    [_norm_rope_pair(y[:, t * LANES:(t + 1) * LANES], gain[...],
                                     cos_ref[rows, :], sin_ref[rows, :]) * scale
                     for t in range(y.shape[1] // LANES)], axis=1)
            if gate_last and j == len(out_refs) - 1:
                y = _silu(y)
            return y

        xn = _rmsnorm_rows(x_ref[rows, :], g_ref[...]).astype(BF16)
        _project_and_store(xn, w_ref, out_refs, rows, post)


def _project_and_store(xn, w_ref, out_refs, rows, post):
    n = w_ref.shape[1]
    starts = [0]
    for o_ref in out_refs:
        starts.append(starts[-1] + o_ref.shape[1])
    assert starts[-1] == n
    for c0 in range(0, n, PROJ_COL_GROUP):
        c1 = min(c0 + PROJ_COL_GROUP, n)
        y = jnp.dot(xn, w_ref[:, c0:c1], preferred_element_type=F32)
        for j, o_ref in enumerate(out_refs):
            lo, hi = max(c0, starts[j]), min(c1, starts[j + 1])
            if lo < hi:
                o_ref[rows, lo - starts[j]:hi - starts[j]] = post(
                    j, y[:, lo - c0:hi - c0]).astype(o_ref.dtype)


def _silu(y):
    return y * (1.0 / (1.0 + jnp.exp(-y)))


def _norm_proj(x, g, w_bf16, splits, tm, gate_last, qk_rope=None, rope_args=(), seq=None):
    rows, d = x.shape
    n = w_bf16.shape[1]
    assert sum(splits) == n and rows % tm == 0
    in_specs = [pl.BlockSpec((tm, d), lambda i: (i, 0)),
                pl.BlockSpec((1, d), lambda i: (0, 0)),
                pl.BlockSpec((d, n), lambda i: (0, 0))]
    if qk_rope is not None:
        pos_blocks = seq // tm
        in_specs += [pl.BlockSpec((tm, LANES), lambda i: (i % pos_blocks, 0)),
                     pl.BlockSpec((tm, LANES), lambda i: (i % pos_blocks, 0)),
                     pl.BlockSpec((1, LANES), lambda i: (0, 0)),
                     pl.BlockSpec((1, LANES), lambda i: (0, 0))]
    return pl.pallas_call(
        functools.partial(_norm_proj_kernel, qk_rope=qk_rope, gate_last=gate_last),
        grid=(rows // tm,),
        in_specs=in_specs,
        out_specs=[pl.BlockSpec((tm, s), lambda i: (i, 0)) for s in splits],
        out_shape=[jax.ShapeDtypeStruct((rows, s), BF16) for s in splits],
        compiler_params=_cparams(1),
        name="norm_proj",
    )(x, g.reshape(1, d), w_bf16, *rope_args)


def _gate_out_kernel(*refs, n_parts, after):
    y_refs = refs[:n_parts]
    gate_ref, w_ref, res_ref, g_ref = refs[n_parts:n_parts + 4]
    if after == "proj":
        w_next_ref, h_ref = refs[n_parts + 4:n_parts + 6]
        out_refs = refs[n_parts + 6:]
    else:
        h_ref = refs[n_parts + 4]
    tm = res_ref.shape[0]
    for r0 in range(0, tm, GATE_SUB_ROWS):
        rows = slice(r0, min(r0 + GATE_SUB_ROWS, tm))
        parts = []
        c0 = 0
        for y_ref in y_refs:
            width = y_ref.shape[1]
            parts.append(y_ref[rows, :] * gate_ref[rows, c0:c0 + width])
            c0 += width
        h = res_ref[rows, :] + jnp.dot(jnp.concatenate(parts, axis=1), w_ref[...],
                                       preferred_element_type=F32)
        if after == "norm":
            h_ref[rows, :] = _rmsnorm_rows(h, g_ref[...])
            continue
        h_ref[rows, :] = h
        gate_path = len(out_refs) - 1
        _project_and_store(_rmsnorm_rows(h, g_ref[...]).astype(BF16), w_next_ref, out_refs, rows,
                           lambda j, y: _silu(y) if j == gate_path else y)


def _gate_out(y_parts, gate, w_bf16, resid, norm_g, tm, w_next=None, splits=()):
    rows, d = resid.shape
    mix = w_bf16.shape[0]
    after = "norm" if w_next is None else "proj"
    row_block = lambda width: pl.BlockSpec((tm, width), lambda i: (i, 0))
    whole = lambda a: pl.BlockSpec(a.shape, lambda i: (0, 0))
    in_specs = [row_block(y.shape[1]) for y in y_parts]
    in_specs += [row_block(mix), whole(w_bf16), row_block(d), pl.BlockSpec((1, d), lambda i: (0, 0))]
    args = list(y_parts) + [gate, w_bf16, resid, norm_g.reshape(1, d)]
    out_specs = [row_block(d)]
    out_shape = [jax.ShapeDtypeStruct((rows, d), F32)]
    if after == "proj":
        assert sum(splits) == w_next.shape[1]
        in_specs.append(whole(w_next))
        args.append(w_next)
        out_specs += [row_block(s) for s in splits]
        out_shape += [jax.ShapeDtypeStruct((rows, s), BF16) for s in splits]
    return pl.pallas_call(
        functools.partial(_gate_out_kernel, n_parts=len(y_parts), after=after),
        grid=(rows // tm,),
        in_specs=in_specs,
        out_specs=out_specs,
        out_shape=out_shape,
        compiler_params=_cparams(1),
        name="gate_out",
    )(*args)


def _bucket_thresholds():
    nb = REL_BUCKETS // 2
    max_exact = nb // 2
    n = np.arange(0, 4 * REL_MAX_DIST)
    nf = np.maximum(n, 1).astype(np.float32)
    large = max_exact + (np.log(nf / np.float32(max_exact))
                         / np.float32(math.log(REL_MAX_DIST / max_exact))
                         * np.float32(nb - max_exact)).astype(np.int32)
    bucket = np.where(n < max_exact, n, np.minimum(large, nb - 1))
    assert np.all(np.diff(bucket) >= 0) and bucket[0] == 0 and bucket[-1] == nb - 1
    thr = [int(np.argmax(bucket >= k)) for k in range(1, nb)]
    assert thr[-1] < REL_MAX_DIST
    return thr


def _bias_tiles_kernel(tab_ref, near_ref, win_ref):
    h = pl.program_id(0)
    thr = _bucket_thresholds()
    nb = REL_BUCKETS // 2
    row = lax.broadcasted_iota(jnp.int32, (BLOCK, BLOCK), 0)
    col = lax.broadcasted_iota(jnp.int32, (BLOCK, BLOCK), 1)
    def tile_of(rel):
        n = jnp.abs(rel)
        neg = jnp.full((BLOCK, BLOCK), tab_ref[0, h], F32)
        pos = jnp.full((BLOCK, BLOCK), tab_ref[nb, h], F32)
        for b in range(1, nb):
            ge = n >= thr[b - 1]
            neg = jnp.where(ge, tab_ref[b, h], neg)
            pos = jnp.where(ge, tab_ref[nb + b, h], pos)
        return jnp.where(rel > 0, pos, neg) * LOG2E

    for k in range(2 * BIAS_FAR + 1):
        rel = (k - BIAS_FAR) * BLOCK + row - col
        tile = tile_of(rel)
        near_ref[0, k] = tile
        if abs(k - BIAS_FAR) <= 1:
            j = k - BIAS_FAR + 1
            win_ref[j * BLOCK:(j + 1) * BLOCK, :] = jnp.where(jnp.abs(rel) <= WINDOW, tile, NEG_INF)


def _bias_tiles(rel_bias):
    assert _bucket_thresholds()[-1] <= (BIAS_FAR - 1) * BLOCK + 1
    n_tiles = 2 * BIAS_FAR + 1
    n_heads = rel_bias.shape[1]
    return pl.pallas_call(
        _bias_tiles_kernel,
        grid=(n_heads,),
        in_specs=[pl.BlockSpec(memory_space=pltpu.SMEM)],
        out_specs=[pl.BlockSpec((1, n_tiles, BLOCK, BLOCK), lambda h: (h, 0, 0, 0)),
                   pl.BlockSpec((3 * BLOCK, BLOCK), lambda h: (0, h))],
        out_shape=[jax.ShapeDtypeStruct((n_heads, n_tiles, BLOCK, BLOCK), F32),
                   jax.ShapeDtypeStruct((3 * BLOCK, n_heads * BLOCK), F32)],
        compiler_params=_cparams(1),
        name="bias_tiles",
    )(rel_bias)


def _stack_gqa_heads(q_tiles):
    lo = _lane_lo(q_tiles[0].shape)
    swap = lambda q: pltpu.roll(q, HEAD_DIM, 1)
    heads_lo = [q_tiles[0], swap(q_tiles[0]), q_tiles[1]]
    heads_hi = [q_tiles[1], swap(q_tiles[2]), q_tiles[2]]
    return jnp.concatenate([jnp.where(lo, q, 0.0).astype(BF16) for q in heads_lo]
                           + [jnp.where(lo, 0.0, q).astype(BF16) for q in heads_hi], axis=0)


def _unstack_gqa_heads(out_lo, out_hi, width):
    blk = lambda o, j: o[:, j * width:(j + 1) * width]
    return [jnp.concatenate([blk(out_lo, 0), blk(out_lo, 1)], axis=0),
            jnp.concatenate([blk(out_lo, 2), blk(out_hi, 0)], axis=0),
            jnp.concatenate([blk(out_hi, 1), blk(out_hi, 2)], axis=0)]


def _stack_pair_heads(q):
    lo = _lane_lo(q.shape)
    return jnp.concatenate([jnp.where(lo, q, 0.0).astype(BF16),
                            jnp.where(lo, 0.0, q).astype(BF16)], axis=0)


def _with_shift_lane(lhs, kmax_lo, kmax_hi, bias_max_lo, bias_max_hi):
    half = lhs.shape[0] // 2
    lf = lhs.astype(F32)
    q_sq = jnp.sum(lf * lf, axis=-1, keepdims=True)
    q_norm = q_sq * lax.rsqrt(jnp.maximum(q_sq, 1e-30))
    row = lax.broadcasted_iota(jnp.int32, lhs.shape, 0)
    lane = lax.broadcasted_iota(jnp.int32, lhs.shape, 1)
    lo_row = row < half
    shift = (q_norm * jnp.where(lo_row, kmax_lo, kmax_hi)
             + jnp.where(lo_row, bias_max_lo, bias_max_hi))
    shift = shift + jnp.abs(shift) * SHIFT_MARGIN
    slot_lane = jnp.where(lo_row, HEAD_DIM, 0)
    return jnp.where(lane == slot_lane, -shift, lf).astype(BF16)


def _keys_with_ones(k_pair):
    lane = lax.broadcasted_iota(jnp.int32, k_pair.shape, 1)
    kf = k_pair.astype(F32)
    k_lo = jnp.where(lane < HEAD_DIM, kf, jnp.where(lane == HEAD_DIM, 1.0, 0.0))
    k_hi = jnp.where(lane >= HEAD_DIM, kf, jnp.where(lane == 0, 1.0, 0.0))
    return k_lo.astype(BF16), k_hi.astype(BF16)


def _shifted_scores(ka_lo, ka_hi, lhs_shifted):
    half = lhs_shifted.shape[0] // 2
    return jnp.concatenate([_nt_dot(ka_lo, lhs_shifted[:half]), _nt_dot(ka_hi, lhs_shifted[half:])],
                           axis=1)


def _pair_key_norm_max(k_pair):
    kf = k_pair.astype(F32)
    row = lax.broadcasted_iota(jnp.int32, (LANES, 2 * LANES), 0)
    col = lax.broadcasted_iota(jnp.int32, (LANES, 2 * LANES), 1)
    half_sel = jnp.where((row < HEAD_DIM) == (col < LANES), 1.0, 0.0).astype(BF16)
    sq = jnp.dot((kf * kf).astype(BF16), half_sel, preferred_element_type=F32)
    norm_max = jnp.sqrt(jnp.max(sq, axis=0, keepdims=True))
    return norm_max[:, :LANES], norm_max[:, LANES:]


def _window_attn_kernel(q_ref, k_ref, v_ref, bias_ref, sink_row_ref,
                        o_ref, kpad_scr, vtpad_scr, *, n_blocks, group):
    i = pl.program_id(1)
    half = 3 * BLOCK

    @pl.when(i == 0)
    def _():
        zero_blk = jnp.zeros((1, BLOCK, LANES), BF16)
        kpad_scr[0:1] = zero_blk
        kpad_scr[n_blocks + 1:n_blocks + 2] = zero_blk
        kpad_scr[1:n_blocks + 1] = k_ref[...].reshape(n_blocks, BLOCK, LANES)
        v_t = v_ref[...].astype(F32).T.astype(BF16)
        ones = jnp.ones((ONES_ROWS, BLOCK), BF16)
        zero_v = jnp.zeros((HEAD_DIM + ONES_ROWS, BLOCK), BF16)
        for g in range(2):
            vtpad_scr[g, 0] = zero_v
            vtpad_scr[g, n_blocks + 1] = zero_v
            for blk in range(n_blocks):
                vtpad_scr[g, blk + 1, :HEAD_DIM, :] = v_t[g * HEAD_DIM:(g + 1) * HEAD_DIM,
                                                           blk * BLOCK:(blk + 1) * BLOCK]
                vtpad_scr[g, blk + 1, HEAD_DIM:, :] = ones

    sink_row = sink_row_ref[...]

    for u in range(group):
        n = i * group + u
        qf = q_ref[u * BLOCK:(u + 1) * BLOCK, :].astype(F32) * (HEAD_DIM ** -0.5 * LOG2E)
        lhs = _stack_gqa_heads([qf[:, j * LANES:(j + 1) * LANES] for j in range(3)])
        k_win = kpad_scr[pl.ds(n, 3)].reshape(3 * BLOCK, LANES)
        s = _nt_dot(k_win, lhs) + bias_ref[...]
        s = jnp.concatenate([jnp.where(n > 0, s[:BLOCK], NEG_INF), s[BLOCK:2 * BLOCK],
                             jnp.where(n < n_blocks - 1, s[2 * BLOCK:], NEG_INF)], axis=0)
        m = jnp.maximum(jnp.max(s, axis=0, keepdims=True), sink_row)
        p = jnp.exp2(s - m).astype(BF16)
        p_sink = jnp.exp2(sink_row - m)
        outs = []
        for g in range(2):
            acc = None
            for j in range(3):
                pv = jnp.dot(vtpad_scr[g, n + j], p[j * BLOCK:(j + 1) * BLOCK, g * half:(g + 1) * half],
                             preferred_element_type=F32)
                acc = pv if acc is None else acc + pv
            total = acc[HEAD_DIM:HEAD_DIM + 1] + p_sink[:, g * half:(g + 1) * half]
            outs.append(acc[:HEAD_DIM] * (1.0 / total))
        for j, pair_t in enumerate(_unstack_gqa_heads(outs[0], outs[1], BLOCK)):
            o_ref[u * BLOCK:(u + 1) * BLOCK, j * LANES:(j + 1) * LANES] = pair_t.T.astype(o_ref.dtype)


def _window_attn(q, k, v, bias_win_t, sink, batch, seq, group):
    nb = seq // BLOCK
    steps = nb // group
    rows = A_HEADS * BLOCK
    sink_row = jnp.repeat(sink, BLOCK).reshape(1, rows)
    const = lambda shape: pl.BlockSpec(shape, lambda b, i: (0,) * len(shape))
    return pl.pallas_call(
        functools.partial(_window_attn_kernel, n_blocks=nb, group=group),
        grid=(batch, steps),
        in_specs=[pl.BlockSpec((group * BLOCK, 3 * LANES), lambda b, i: (b * steps + i, 0)),
                  pl.BlockSpec((seq, LANES), lambda b, i: (b, 0)),
                  pl.BlockSpec((seq, LANES), lambda b, i: (b, 0)),
                  const(bias_win_t.shape), const((1, rows))],
        out_specs=pl.BlockSpec((group * BLOCK, 3 * LANES), lambda b, i: (b * steps + i, 0)),
        out_shape=jax.ShapeDtypeStruct(q.shape, BF16),
        scratch_shapes=[pltpu.VMEM((nb + 2, BLOCK, LANES), BF16),
                        pltpu.VMEM((2, nb + 2, HEAD_DIM + ONES_ROWS, BLOCK), BF16)],
        compiler_params=_cparams(2),
        name="window_attn",
    )(q, k, v, bias_win_t, sink_row)


def _norm_rope_pair(x, g, cos, sin_signed):
    lo = _lane_lo(x.shape)
    x2 = x * x
    ss_lo = jnp.sum(jnp.where(lo, x2, 0.0), axis=-1, keepdims=True)
    ss_hi = jnp.sum(jnp.where(lo, 0.0, x2), axis=-1, keepdims=True)
    ms = jnp.where(lo, ss_lo, ss_hi) * (1.0 / HEAD_DIM)
    y = x * lax.rsqrt(ms + EPS) * g
    lane = lax.broadcasted_iota(jnp.int32, x.shape, 1)
    quarter = HEAD_DIM // 4
    first = (lane & quarter) == 0
    partner = jnp.where(first, pltpu.roll(y, LANES - quarter, 1), pltpu.roll(y, quarter, 1))
    return y * cos + partner * sin_signed


def _dense_attn_kernel(q_ref, k_ref, v_ref, o_ref, ka_scr, vt_scr, kmax_scr, *, tq):
    seq = k_ref.shape[0]
    kc = 2 * BLOCK
    n_chunks = seq // kc
    grp = 3 * tq

    ka_scr[0], ka_scr[1] = _keys_with_ones(k_ref[...])
    kmax_lo, kmax_hi = _pair_key_norm_max(k_ref[...])
    kmax_scr[0:1, :] = kmax_lo
    kmax_scr[1:2, :] = kmax_hi
    v_t = v_ref[...].astype(F32).T.astype(BF16)
    for g in range(2):
        vt_scr[g, :HEAD_DIM, :] = v_t[g * HEAD_DIM:(g + 1) * HEAD_DIM]
        vt_scr[g, HEAD_DIM:, :] = jnp.ones((ONES_ROWS, seq), BF16)

    def query_tile(i, carry):
        rows = pl.ds(pl.multiple_of(i * tq, tq), tq)
        lhs = _stack_gqa_heads([q_ref[rows, j * LANES:(j + 1) * LANES].astype(F32)
                                for j in range(3)])

        def write(acc):
            outs = [a[:HEAD_DIM] * (1.0 / a[HEAD_DIM:HEAD_DIM + 1]) for a in acc]
            for j, pair_t in enumerate(_unstack_gqa_heads(outs[0], outs[1], tq)):
                o_ref[rows, j * LANES:(j + 1) * LANES] = pair_t.T.astype(o_ref.dtype)

        lhs_shifted = _with_shift_lane(lhs, kmax_scr[0:1, :], kmax_scr[1:2, :], 0.0, 0.0)
        acc = []
        for g in range(2):
            p = jnp.exp2(_nt_dot(ka_scr[g], lhs_shifted[g * grp:(g + 1) * grp])).astype(BF16)
            acc.append(jnp.dot(vt_scr[g], p, preferred_element_type=F32))
        write(acc)
        sums = jnp.concatenate([a[HEAD_DIM:HEAD_DIM + 1] for a in acc], axis=0)

        @pl.when(jnp.logical_not(jnp.min(sums) >= SUM_FLOOR))
        def _():
            acc2 = []
            for g in range(2):
                s_exact = _nt_dot(k_ref[...], lhs[g * grp:(g + 1) * grp])
                m = None
                for c in range(n_chunks):
                    s = s_exact[c * kc:(c + 1) * kc, :]
                    m_c = jnp.max(s, axis=0, keepdims=True)
                    m_new = m_c if m is None else jnp.maximum(m, m_c)
                    pc = jnp.exp2(s - m_new).astype(BF16)
                    pv = jnp.dot(vt_scr[g, :, c * kc:(c + 1) * kc], pc, preferred_element_type=F32)
                    a = pv if m is None else a * jnp.exp2(m - m_new) + pv
                    m = m_new
                acc2.append(a)
            write(acc2)

        return carry

    lax.fori_loop(0, seq // tq, query_tile, 0)


def _dense_attn(q, k, v, batch, seq, tq):
    return pl.pallas_call(
        functools.partial(_dense_attn_kernel, tq=tq),
        grid=(batch,),
        in_specs=[pl.BlockSpec((seq, 3 * LANES), lambda b: (b, 0)),
                  pl.BlockSpec((seq, LANES), lambda b: (b, 0)),
                  pl.BlockSpec((seq, LANES), lambda b: (b, 0))],
        out_specs=pl.BlockSpec((seq, 3 * LANES), lambda b: (b, 0)),
        out_shape=jax.ShapeDtypeStruct(q.shape, BF16),
        scratch_shapes=[pltpu.VMEM((2, seq, LANES), BF16),
                        pltpu.VMEM((2, HEAD_DIM + ONES_ROWS, seq), BF16),
                        pltpu.VMEM((2, LANES), F32)],
        compiler_params=_cparams(1),
        name="dense_attn",
    )(q, k, v)


def _cross_attn_kernel(q_ref, mk_ref, mv_ref, o_ref, vt_scr, *, tq):
    i = pl.program_id(1)
    n_pairs = q_ref.shape[1] // LANES
    mem_len = mk_ref.shape[0]

    @pl.when(i == 0)
    def _():
        v_t = mv_ref[...].astype(F32).T.astype(BF16)
        for h in range(2 * n_pairs):
            vt_scr[h, :HEAD_DIM, :] = v_t[h * HEAD_DIM:(h + 1) * HEAD_DIM]
            vt_scr[h, HEAD_DIM:, :] = jnp.ones((ONES_ROWS, mem_len), BF16)

    for j in range(n_pairs):
        cols = slice(j * LANES, (j + 1) * LANES)
        lhs = _stack_pair_heads(q_ref[:, cols].astype(F32) * (HEAD_DIM ** -0.5 * LOG2E))
        s = _nt_dot(mk_ref[:, cols], lhs)
        p = jnp.exp2(s - jnp.max(s, axis=0, keepdims=True)).astype(BF16)
        acc = [jnp.dot(vt_scr[2 * j + h], p[:, h * tq:(h + 1) * tq], preferred_element_type=F32)
               for h in range(2)]
        pair_t = jnp.concatenate([a[:HEAD_DIM] * (1.0 / a[HEAD_DIM:HEAD_DIM + 1]) for a in acc],
                                 axis=0)
        o_ref[:, cols] = pair_t.T.astype(o_ref.dtype)


def _cross_attn(q, mk, mv, batch, seq, tq):
    nq = seq // tq
    mem_len = mk.shape[0] // batch
    width = q.shape[1]
    n_pairs = width // LANES
    return pl.pallas_call(
        functools.partial(_cross_attn_kernel, tq=tq),
        grid=(batch, nq),
        in_specs=[pl.BlockSpec((tq, width), lambda b, i: (b * nq + i, 0)),
                  pl.BlockSpec((mem_len, width), lambda b, i: (b, 0)),
                  pl.BlockSpec((mem_len, width), lambda b, i: (b, 0))],
        out_specs=pl.BlockSpec((tq, width), lambda b, i: (b * nq + i, 0)),
        out_shape=jax.ShapeDtypeStruct(q.shape, BF16),
        scratch_shapes=[pltpu.VMEM((2 * n_pairs, HEAD_DIM + ONES_ROWS, mem_len), BF16)],
        compiler_params=_cparams(2),
        name="cross_attn",
    )(q, mk, mv)


def _diff_attn_kernel(q1_ref, q2_ref, k1_ref, k2_ref, v_ref, bias_ref, bmax_ref, lam_ref, g_ref,
                      o_ref, vt_scr, ka_scr, kmax_scr, *, tq, lam_init):
    sub = tq // BLOCK
    seq = k1_ref.shape[0]
    kc = 2 * BLOCK
    n_chunks = seq // kc
    k_refs = (k1_ref, k2_ref)

    for head in range(2):
        v_t = v_ref[:, head * C_V_DIM:(head + 1) * C_V_DIM].astype(F32).T
        vt_scr[head, :C_V_DIM, :] = v_t.astype(BF16)
        vt_scr[head, C_V_DIM:, :] = jnp.ones((ONES_ROWS, seq), BF16)
    for t in range(2):
        ka_scr[t, 0], ka_scr[t, 1] = _keys_with_ones(k_refs[t][...])
        kmax_lo, kmax_hi = _pair_key_norm_max(k_refs[t][...])
        kmax_scr[t, 0:1, :] = kmax_lo
        kmax_scr[t, 1:2, :] = kmax_hi

    lam_vec = lam_ref[...]
    lam = (jnp.exp(jnp.sum(lam_vec[0:1] * lam_vec[1:2], axis=-1, keepdims=True))
           - jnp.exp(jnp.sum(lam_vec[2:3] * lam_vec[3:4], axis=-1, keepdims=True)) + lam_init)

    def query_tile(i, carry):
        rows = pl.ds(pl.multiple_of(i * tq, tq), tq)
        lhs = [_stack_pair_heads(q_ref[rows, :].astype(F32) * (HEAD_DIM ** -0.5 * LOG2E))
               for q_ref in (q1_ref, q2_ref)]
        _diff_attn_tile(i, rows, lhs, lam, bias_ref, bmax_ref, g_ref, o_ref, vt_scr, ka_scr,
                        kmax_scr, k_refs, tq=tq, sub=sub, kc=kc, n_chunks=n_chunks,
                        lam_init=lam_init)
        return carry

    lax.fori_loop(0, seq // tq, query_tile, 0)


def _diff_attn_tile(i, rows, lhs, lam, bias_ref, bmax_ref, g_ref, o_ref, vt_scr, ka_scr, kmax_scr,
                    k_refs, *, tq, sub, kc, n_chunks, lam_init):
    def bias_chunk(c):
        def tile(head, u, e):
            offset = (2 * c + e) - (i * sub + u)
            return bias_ref[head, jnp.clip(offset, -BIAS_FAR, BIAS_FAR) + BIAS_FAR]
        return jnp.concatenate(
            [jnp.concatenate([tile(head, u, e) for head in range(2) for u in range(sub)], axis=1)
             for e in range(2)], axis=0)

    def shifted_scores(t):
        return _shifted_scores(ka_scr[t, 0], ka_scr[t, 1], _with_shift_lane(
            lhs[t], kmax_scr[t, 0:1, :], kmax_scr[t, 1:2, :], bmax_ref[0], bmax_ref[1]))

    def pv_of_shifted(s_all):
        p = jnp.concatenate([jnp.exp2(s_all[c * kc:(c + 1) * kc, :] + bias_chunk(c)).astype(BF16)
                             for c in range(n_chunks)], axis=0)
        return [jnp.dot(vt_scr[head], p[:, head * tq:(head + 1) * tq],
                        preferred_element_type=F32) for head in range(2)]

    def pv_running_max(s_all):
        m = None
        acc = [None, None]
        for c in range(n_chunks):
            s = s_all[c * kc:(c + 1) * kc, :] + bias_chunk(c)
            m_c = jnp.max(s, axis=0, keepdims=True)
            m_new = m_c if m is None else jnp.maximum(m, m_c)
            p = jnp.exp2(s - m_new).astype(BF16)
            if m is not None:
                alpha = jnp.exp2(m - m_new)
            for head in range(2):
                cols = slice(head * tq, (head + 1) * tq)
                pv = jnp.dot(vt_scr[head, :, c * kc:(c + 1) * kc], p[:, cols],
                             preferred_element_type=F32)
                acc[head] = pv if m is None else acc[head] * alpha[:, cols] + pv
            m = m_new
        return acc

    def write(acc1, acc2):
        g = g_ref[...] * (1.0 - lam_init)
        for head in range(2):
            o1 = acc1[head][:C_V_DIM] * (1.0 / acc1[head][C_V_DIM:C_V_DIM + 1])
            o2 = acc2[head][:C_V_DIM] * (1.0 / acc2[head][C_V_DIM:C_V_DIM + 1])
            out = o1 - lam * o2
            ms = jnp.mean(out * out, axis=0, keepdims=True)
            y = out * lax.rsqrt(ms + EPS) * g
            o_ref[rows, head * C_V_DIM:(head + 1) * C_V_DIM] = y.T.astype(o_ref.dtype)

    acc1 = pv_of_shifted(shifted_scores(0))
    acc2 = pv_of_shifted(shifted_scores(1))
    write(acc1, acc2)
    sums = jnp.concatenate([a[C_V_DIM:C_V_DIM + 1] for a in acc1 + acc2], axis=0)

    @pl.when(jnp.logical_not(jnp.min(sums) >= SUM_FLOOR))
    def _():
        write(pv_running_max(_nt_dot(k_refs[0][...], lhs[0])),
              pv_running_max(_nt_dot(k_refs[1][...], lhs[1])))


def _diff_attn(q1, q2, k1, k2, v, bias_t, bias_max, lam_vecs, subln_g, lam_init, batch, seq, tq):
    n_pairs = C_HEADS // 2
    n_tiles = bias_t.shape[1]
    seq_spec = pl.BlockSpec((seq, LANES), lambda b, p: (b, p))
    g_cols = jnp.broadcast_to(subln_g.reshape(C_V_DIM, 1), (C_V_DIM, tq))
    return pl.pallas_call(
        functools.partial(_diff_attn_kernel, tq=tq, lam_init=lam_init),
        grid=(batch, n_pairs),
        in_specs=[seq_spec, seq_spec, seq_spec, seq_spec,
                  pl.BlockSpec((seq, 2 * C_V_DIM), lambda b, p: (b, p)),
                  pl.BlockSpec((2, n_tiles, BLOCK, BLOCK), lambda b, p: (p, 0, 0, 0)),
                  pl.BlockSpec((2, 1, LANES), lambda b, p: (p, 0, 0)),
                  pl.BlockSpec(lam_vecs.shape, lambda b, p: (0, 0)),
                  pl.BlockSpec((C_V_DIM, tq), lambda b, p: (0, 0))],
        out_specs=pl.BlockSpec((seq, 2 * C_V_DIM), lambda b, p: (b, p)),
        out_shape=jax.ShapeDtypeStruct(v.shape, BF16),
        scratch_shapes=[pltpu.VMEM((2, C_V_DIM + ONES_ROWS, seq), BF16),
                        pltpu.VMEM((2, 2, seq, LANES), BF16),
                        pltpu.VMEM((2, 2, LANES), F32)],
        compiler_params=_cparams(2),
        name="diff_attn",
    )(q1, q2, k1, k2, v, bias_t, bias_max, lam_vecs, g_cols)


def _rope_tables(seq):
    rows = seq // GRID_W
    row = jnp.broadcast_to(jnp.arange(rows)[:, None], (rows, GRID_W)).reshape(-1)
    col = jnp.broadcast_to(jnp.arange(GRID_W)[None, :], (rows, GRID_W)).reshape(-1)
    half = HEAD_DIM // 2
    inv = 1.0 / (ROPE_THETA ** (jnp.arange(0, half, 2, dtype=F32) / half))
    ang_row = row.astype(F32)[:, None] * inv
    ang_col = col.astype(F32)[:, None] * inv
    cos = jnp.concatenate([jnp.cos(ang_row)] * 2 + [jnp.cos(ang_col)] * 2, axis=-1)
    sin = jnp.concatenate([-jnp.sin(ang_row), jnp.sin(ang_row),
                           -jnp.sin(ang_col), jnp.sin(ang_col)], axis=-1)
    return jnp.tile(cos, (1, 2)), jnp.tile(sin, (1, 2))


def kernel(x, mem, rel_bias, mem_norm, final_norm, even_norm, even_w_in, even_sink, even_q_norm, even_k_norm, even_w_mem_kv, even_w_out, odd_norm, odd_w_in, odd_lambda_q1, odd_lambda_k1, odd_lambda_q2, odd_lambda_k2, odd_subln, odd_w_mem_kv, odd_w_out):
    batch, seq, d = x.shape
    mem_len = mem.shape[1]
    tokens = batch * seq
    xw = X_HEADS * HEAD_DIM
    gw = A_HEADS * HEAD_DIM

    w_in0 = even_w_in[0].astype(BF16)
    w_out0 = even_w_out[0].astype(BF16)
    w_in1 = odd_w_in[0].astype(BF16)
    w_out1 = odd_w_out[0].astype(BF16)

    x2 = x.reshape(tokens, d)
    mem2 = mem.reshape(batch * mem_len, d)

    bias_near, bias_win = _bias_tiles(rel_bias)
    cos, sin_signed = _rope_tables(seq)
    head_bias_max = jnp.max(rel_bias, axis=0) * LOG2E

    w_mem = jnp.concatenate([even_w_mem_kv[0], odd_w_mem_kv[0]], axis=1).astype(BF16)
    mk0, mv0, mk1, mv1 = _norm_proj(mem2, mem_norm, w_mem, (xw,) * 4, TM_MEM, gate_last=False)

    rope_args = (cos, sin_signed, jnp.tile(even_q_norm[0], 2).reshape(1, LANES),
                 jnp.tile(even_k_norm[0], 2).reshape(1, LANES))
    aq, ak, av, bq, bk, bv, xq, gate = _norm_proj(
        x2, even_norm[0], w_in0, (gw, 128, 128, gw, 128, 128, xw, D_MODEL), TM_PROJ,
        gate_last=True, qk_rope=(3, 4), rope_args=rope_args, seq=seq)
    y_a = _window_attn(aq, ak, av, bias_win, even_sink[0] * LOG2E, batch, seq, WINDOW_GROUP)
    y_b = _dense_attn(bq, bk, bv, batch, seq, TQ_DENSE)
    y_x = _cross_attn(xq, mk0, mv0, batch, seq, TQ_CROSS)
    lam_init = 0.8 - 0.6 * math.exp(-0.3 * 1)
    cw = C_HEADS * HEAD_DIM
    h1, q1, q2, k1, k2, v, xq1, gate1 = _gate_out(
        [y_a, y_b, y_x], gate, w_out0, x2, odd_norm[0], TM_PROJ, w_next=w_in1,
        splits=(cw, cw, cw, cw, C_HEADS * C_V_DIM, xw, D_MODEL))
    lam_vecs = jnp.stack([odd_lambda_q1[0], odd_lambda_k1[0], odd_lambda_q2[0], odd_lambda_k2[0]])
    bias_max = jnp.broadcast_to(head_bias_max.reshape(C_HEADS, 1, 1), (C_HEADS, 1, LANES))
    y_c = _diff_attn(q1, q2, k1, k2, v, bias_near, bias_max, lam_vecs, odd_subln[0], lam_init,
                     batch, seq, TQ_DIFF)
    y_x1 = _cross_attn(xq1, mk1, mv1, batch, seq, TQ_CROSS)
    (out,) = _gate_out([y_c, y_x1], gate1, w_out1, h1, final_norm, TM_PROJ)
    return out.reshape(batch, seq, d)
```

```python
import functools
import math

import numpy as np
import jax
import jax.numpy as jnp
from jax import lax
from jax.experimental import pallas as pl
from jax.experimental.pallas import tpu as pltpu

D_MODEL = 1024
HEAD_DIM = 64
BLOCK = 128
WINDOW = 128
GRID_W = 64
A_HEADS = 6
C_HEADS = 6
C_V_DIM = 128
X_HEADS = 4
REL_BUCKETS = 32
REL_MAX_DIST = 128
ROPE_THETA = 10000.0
EPS = 1e-6
NEG_INF = -1e30
LOG2E = math.log2(math.e)
LANES = 128
ONES_ROWS = 16
PROJ_SUB_ROWS = 256
GATE_SUB_ROWS = 1024
PROJ_COL_GROUP = 512
TM_PROJ = 1024
TM_MEM = 256
TQ_DENSE = 1024
TQ_DIFF = 2048
TQ_CROSS = 2048
WINDOW_GROUP = 16
SUM_FLOOR = 2.0 ** -64
SHIFT_MARGIN = 2.0 ** -7
BIAS_FAR = 2
VMEM_LIMIT = 56 * 1024 * 1024

F32 = jnp.float32
BF16 = jnp.bfloat16


def _cparams(n_axes):
    return pltpu.CompilerParams(dimension_semantics=("arbitrary",) * n_axes,
                                vmem_limit_bytes=VMEM_LIMIT)


def _lane_lo(shape):
    return lax.broadcasted_iota(jnp.int32, shape, len(shape) - 1) < HEAD_DIM


def _rmsnorm_rows(x, g):
    ms = jnp.mean(x * x, axis=-1, keepdims=True)
    return x * lax.rsqrt(ms + EPS) * g


def _nt_dot(a, b):
    return lax.dot_general(a, b, (((1,), (1,)), ((), ())), preferred_element_type=F32)


def _norm_proj_kernel(*refs, qk_rope, gate_last):
    x_ref, g_ref, w_ref = refs[:3]
    if qk_rope is None:
        out_refs = refs[3:]
    else:
        cos_ref, sin_ref, gq_ref, gk_ref = refs[3:7]
        out_refs = refs[7:]
    tm = x_ref.shape[0]
    for r0 in range(0, tm, PROJ_SUB_ROWS):
        rows = slice(r0, min(r0 + PROJ_SUB_ROWS, tm))

        def post(j, y):
            if qk_rope is not None and j in qk_rope:
                gain, scale = ((gq_ref, HEAD_DIM ** -0.5 * LOG2E) if j == qk_rope[0]
                               else (gk_ref, 1.0))
                y = jnp.concatenate(
                    [_norm_rope_pair(y[:, t * LANES:(t + 1) * LANES], gain[...],
                                     cos_ref[rows, :], sin_ref[rows, :]) * scale
                     for t in range(y.shape[1] // LANES)], axis=1)
            if gate_last and j == len(out_refs) - 1:
                y = _silu(y)
            return y

        xn = _rmsnorm_rows(x_ref[rows, :], g_ref[...]).astype(BF16)
        _project_and_store(xn, w_ref, out_refs, rows, post)


def _project_and_store(xn, w_ref, out_refs, rows, post):
    n = w_ref.shape[1]
    starts = [0]
    for o_ref in out_refs:
        starts.append(starts[-1] + o_ref.shape[1])
    assert starts[-1] == n
    for c0 in range(0, n, PROJ_COL_GROUP):
        c1 = min(c0 + PROJ_COL_GROUP, n)
        y = jnp.dot(xn, w_ref[:, c0:c1], preferred_element_type=F32)
        for j, o_ref in enumerate(out_refs):
            lo, hi = max(c0, starts[j]), min(c1, starts[j + 1])
            if lo < hi:
                o_ref[rows, lo - starts[j]:hi - starts[j]] = post(
                    j, y[:, lo - c0:hi - c0]).astype(o_ref.dtype)


def _silu(y):
    return y * (1.0 / (1.0 + jnp.exp(-y)))


def _norm_proj(x, g, w_bf16, splits, tm, gate_last, qk_rope=None, rope_args=(), seq=None):
    rows, d = x.shape
    n = w_bf16.shape[1]
    assert sum(splits) == n and rows % tm == 0
    in_specs = [pl.BlockSpec((tm, d), lambda i: (i, 0)),
                pl.BlockSpec((1, d), lambda i: (0, 0)),
                pl.BlockSpec((d, n), lambda i: (0, 0))]
    if qk_rope is not None:
        pos_blocks = seq // tm
        in_specs += [pl.BlockSpec((tm, LANES), lambda i: (i % pos_blocks, 0)),
                     pl.BlockSpec((tm, LANES), lambda i: (i % pos_blocks, 0)),
                     pl.BlockSpec((1, LANES), lambda i: (0, 0)),
                     pl.BlockSpec((1, LANES), lambda i: (0, 0))]
    return pl.pallas_call(
        functools.partial(_norm_proj_kernel, qk_rope=qk_rope, gate_last=gate_last),
        grid=(rows // tm,),
        in_specs=in_specs,
        out_specs=[pl.BlockSpec((tm, s), lambda i: (i, 0)) for s in splits],
        out_shape=[jax.ShapeDtypeStruct((rows, s), BF16) for s in splits],
        compiler_params=_cparams(1),
        name="norm_proj",
    )(x, g.reshape(1, d), w_bf16, *rope_args)


def _gate_out_kernel(*refs, n_parts, after):
    y_refs = refs[:n_parts]
    gate_ref, w_ref, res_ref, g_ref = refs[n_parts:n_parts + 4]
    if after == "proj":
        w_next_ref, h_ref = refs[n_parts + 4:n_parts + 6]
        out_refs = refs[n_parts + 6:]
    else:
        h_ref = refs[n_parts + 4]
    tm = res_ref.shape[0]
    for r0 in range(0, tm, GATE_SUB_ROWS):
        rows = slice(r0, min(r0 + GATE_SUB_ROWS, tm))
        parts = []
        c0 = 0
        for y_ref in y_refs:
            width = y_ref.shape[1]
            parts.append(y_ref[rows, :] * gate_ref[rows, c0:c0 + width])
            c0 += width
        h = res_ref[rows, :] + jnp.dot(jnp.concatenate(parts, axis=1), w_ref[...],
                                       preferred_element_type=F32)
        if after == "norm":
            h_ref[rows, :] = _rmsnorm_rows(h, g_ref[...])
            continue
        h_ref[rows, :] = h
        gate_path = len(out_refs) - 1
        _project_and_store(_rmsnorm_rows(h, g_ref[...]).astype(BF16), w_next_ref, out_refs, rows,
                           lambda j, y: _silu(y) if j == gate_path else y)


def _gate_out(y_parts, gate, w_bf16, resid, norm_g, tm, w_next=None, splits=()):
    rows, d = resid.shape
    mix = w_bf16.shape[0]
    after = "norm" if w_next is None else "proj"
    row_block = lambda width: pl.BlockSpec((tm, width), lambda i: (i, 0))
    whole = lambda a: pl.BlockSpec(a.shape, lambda i: (0, 0))
    in_specs = [row_block(y.shape[1]) for y in y_parts]
    in_specs += [row_block(mix), whole(w_bf16), row_block(d), pl.BlockSpec((1, d), lambda i: (0, 0))]
    args = list(y_parts) + [gate, w_bf16, resid, norm_g.reshape(1, d)]
    out_specs = [row_block(d)]
    out_shape = [jax.ShapeDtypeStruct((rows, d), F32)]
    if after == "proj":
        assert sum(splits) == w_next.shape[1]
        in_specs.append(whole(w_next))
        args.append(w_next)
        out_specs += [row_block(s) for s in splits]
        out_shape += [jax.ShapeDtypeStruct((rows, s), BF16) for s in splits]
    return pl.pallas_call(
        functools.partial(_gate_out_kernel, n_parts=len(y_parts), after=after),
        grid=(rows // tm,),
        in_specs=in_specs,
        out_specs=out_specs,
        out_shape=out_shape,
        compiler_params=_cparams(1),
        name="gate_out",
    )(*args)


def _bucket_thresholds():
    nb = REL_BUCKETS // 2
    max_exact = nb // 2
    n = np.arange(0, 4 * REL_MAX_DIST)
    nf = np.maximum(n, 1).astype(np.float32)
    large = max_exact + (np.log(nf / np.float32(max_exact))
                         / np.float32(math.log(REL_MAX_DIST / max_exact))
                         * np.float32(nb - max_exact)).astype(np.int32)
    bucket = np.where(n < max_exact, n, np.minimum(large, nb - 1))
    assert np.all(np.diff(bucket) >= 0) and bucket[0] == 0 and bucket[-1] == nb - 1
    thr = [int(np.argmax(bucket >= k)) for k in range(1, nb)]
    assert thr[-1] < REL_MAX_DIST
    return thr


def _bias_tiles_kernel(tab_ref, near_ref, win_ref):
    h = pl.program_id(0)
    thr = _bucket_thresholds()
    nb = REL_BUCKETS // 2
    row = lax.broadcasted_iota(jnp.int32, (BLOCK, BLOCK), 0)
    col = lax.broadcasted_iota(jnp.int32, (BLOCK, BLOCK), 1)
    def tile_of(rel):
        n = jnp.abs(rel)
        neg = jnp.full((BLOCK, BLOCK), tab_ref[0, h], F32)
        pos = jnp.full((BLOCK, BLOCK), tab_ref[nb, h], F32)
        for b in range(1, nb):
            ge = n >= thr[b - 1]
            neg = jnp.where(ge, tab_ref[b, h], neg)
            pos = jnp.where(ge, tab_ref[nb + b, h], pos)
        return jnp.where(rel > 0, pos, neg) * LOG2E

    for k in range(2 * BIAS_FAR + 1):
        rel = (k - BIAS_FAR) * BLOCK + row - col
        tile = tile_of(rel)
        near_ref[0, k] = tile
        if abs(k - BIAS_FAR) <= 1:
            j = k - BIAS_FAR + 1
            win_ref[j * BLOCK:(j + 1) * BLOCK, :] = jnp.where(jnp.abs(rel) <= WINDOW, tile, NEG_INF)


def _bias_tiles(rel_bias):
    assert _bucket_thresholds()[-1] <= (BIAS_FAR - 1) * BLOCK + 1
    n_tiles = 2 * BIAS_FAR + 1
    n_heads = rel_bias.shape[1]
    return pl.pallas_call(
        _bias_tiles_kernel,
        grid=(n_heads,),
        in_specs=[pl.BlockSpec(memory_space=pltpu.SMEM)],
        out_specs=[pl.BlockSpec((1, n_tiles, BLOCK, BLOCK), lambda h: (h, 0, 0, 0)),
                   pl.BlockSpec((3 * BLOCK, BLOCK), lambda h: (0, h))],
        out_shape=[jax.ShapeDtypeStruct((n_heads, n_tiles, BLOCK, BLOCK), F32),
                   jax.ShapeDtypeStruct((3 * BLOCK, n_heads * BLOCK), F32)],
        compiler_params=_cparams(1),
        name="bias_tiles",
    )(rel_bias)


def _stack_gqa_heads(q_tiles):
    lo = _lane_lo(q_tiles[0].shape)
    swap = lambda q: pltpu.roll(q, HEAD_DIM, 1)
    heads_lo = [q_tiles[0], swap(q_tiles[0]), q_tiles[1]]
    heads_hi = [q_tiles[1], swap(q_tiles[2]), q_tiles[2]]
    return jnp.concatenate([jnp.where(lo, q, 0.0).astype(BF16) for q in heads_lo]
                           + [jnp.where(lo, 0.0, q).astype(BF16) for q in heads_hi], axis=0)


def _unstack_gqa_heads(out_lo, out_hi, width):
    blk = lambda o, j: o[:, j * width:(j + 1) * width]
    return [jnp.concatenate([blk(out_lo, 0), blk(out_lo, 1)], axis=0),
            jnp.concatenate([blk(out_lo, 2), blk(out_hi, 0)], axis=0),
            jnp.concatenate([blk(out_hi, 1), blk(out_hi, 2)], axis=0)]


def _stack_pair_heads(q):
    lo = _lane_lo(q.shape)
    return jnp.concatenate([jnp.where(lo, q, 0.0).astype(BF16),
                            jnp.where(lo, 0.0, q).astype(BF16)], axis=0)


def _with_shift_lane(lhs, kmax_lo, kmax_hi, bias_max_lo, bias_max_hi):
    half = lhs.shape[0] // 2
    lf = lhs.astype(F32)
    q_sq = jnp.sum(lf * lf, axis=-1, keepdims=True)
    q_norm = q_sq * lax.rsqrt(jnp.maximum(q_sq, 1e-30))
    row = lax.broadcasted_iota(jnp.int32, lhs.shape, 0)
    lane = lax.broadcasted_iota(jnp.int32, lhs.shape, 1)
    lo_row = row < half
    shift = (q_norm * jnp.where(lo_row, kmax_lo, kmax_hi)
             + jnp.where(lo_row, bias_max_lo, bias_max_hi))
    shift = shift + jnp.abs(shift) * SHIFT_MARGIN
    slot_lane = jnp.where(lo_row, HEAD_DIM, 0)
    return jnp.where(lane == slot_lane, -shift, lf).astype(BF16)


def _keys_with_ones(k_pair):
    lane = lax.broadcasted_iota(jnp.int32, k_pair.shape, 1)
    kf = k_pair.astype(F32)
    k_lo = jnp.where(lane < HEAD_DIM, kf, jnp.where(lane == HEAD_DIM, 1.0, 0.0))
    k_hi = jnp.where(lane >= HEAD_DIM, kf, jnp.where(lane == 0, 1.0, 0.0))
    return k_lo.astype(BF16), k_hi.astype(BF16)


def _shifted_scores(ka_lo, ka_hi, lhs_shifted):
    half = lhs_shifted.shape[0] // 2
    return jnp.concatenate([_nt_dot(ka_lo, lhs_shifted[:half]), _nt_dot(ka_hi, lhs_shifted[half:])],
                           axis=1)


def _pair_key_norm_max(k_pair):
    kf = k_pair.astype(F32)
    row = lax.broadcasted_iota(jnp.int32, (LANES, 2 * LANES), 0)
    col = lax.broadcasted_iota(jnp.int32, (LANES, 2 * LANES), 1)
    half_sel = jnp.where((row < HEAD_DIM) == (col < LANES), 1.0, 0.0).astype(BF16)
    sq = jnp.dot((kf * kf).astype(BF16), half_sel, preferred_element_type=F32)
    norm_max = jnp.sqrt(jnp.max(sq, axis=0, keepdims=True))
    return norm_max[:, :LANES], norm_max[:, LANES:]


def _window_attn_kernel(q_ref, k_ref, v_ref, bias_ref, sink_row_ref,
                        o_ref, kpad_scr, vtpad_scr, *, n_blocks, group):
    i = pl.program_id(1)
    half = 3 * BLOCK

    @pl.when(i == 0)
    def _():
        zero_blk = jnp.zeros((1, BLOCK, LANES), BF16)
        kpad_scr[0:1] = zero_blk
        kpad_scr[n_blocks + 1:n_blocks + 2] = zero_blk
        kpad_scr[1:n_blocks + 1] = k_ref[...].reshape(n_blocks, BLOCK, LANES)
        v_t = v_ref[...].astype(F32).T.astype(BF16)
        ones = jnp.ones((ONES_ROWS, BLOCK), BF16)
        zero_v = jnp.zeros((HEAD_DIM + ONES_ROWS, BLOCK), BF16)
        for g in range(2):
            vtpad_scr[g, 0] = zero_v
            vtpad_scr[g, n_blocks + 1] = zero_v
            for blk in range(n_blocks):
                vtpad_scr[g, blk + 1, :HEAD_DIM, :] = v_t[g * HEAD_DIM:(g + 1) * HEAD_DIM,
                                                           blk * BLOCK:(blk + 1) * BLOCK]
                vtpad_scr[g, blk + 1, HEAD_DIM:, :] = ones

    sink_row = sink_row_ref[...]

    for u in range(group):
        n = i * group + u
        qf = q_ref[u * BLOCK:(u + 1) * BLOCK, :].astype(F32) * (HEAD_DIM ** -0.5 * LOG2E)
        lhs = _stack_gqa_heads([qf[:, j * LANES:(j + 1) * LANES] for j in range(3)])
        k_win = kpad_scr[pl.ds(n, 3)].reshape(3 * BLOCK, LANES)
        s = _nt_dot(k_win, lhs) + bias_ref[...]
        s = jnp.concatenate([jnp.where(n > 0, s[:BLOCK], NEG_INF), s[BLOCK:2 * BLOCK],
                             jnp.where(n < n_blocks - 1, s[2 * BLOCK:], NEG_INF)], axis=0)
        m = jnp.maximum(jnp.max(s, axis=0, keepdims=True), sink_row)
        p = jnp.exp2(s - m).astype(BF16)
        p_sink = jnp.exp2(sink_row - m)
        outs = []
        for g in range(2):
            acc = None
            for j in range(3):
                pv = jnp.dot(vtpad_scr[g, n + j], p[j * BLOCK:(j + 1) * BLOCK, g * half:(g + 1) * half],
                             preferred_element_type=F32)
                acc = pv if acc is None else acc + pv
            total = acc[HEAD_DIM:HEAD_DIM + 1] + p_sink[:, g * half:(g + 1) * half]
            outs.append(acc[:HEAD_DIM] * (1.0 / total))
        for j, pair_t in enumerate(_unstack_gqa_heads(outs[0], outs[1], BLOCK)):
            o_ref[u * BLOCK:(u + 1) * BLOCK, j * LANES:(j + 1) * LANES] = pair_t.T.astype(o_ref.dtype)


def _window_attn(q, k, v, bias_win_t, sink, batch, seq, group):
    nb = seq // BLOCK
    steps = nb // group
    rows = A_HEADS * BLOCK
    sink_row = jnp.repeat(sink, BLOCK).reshape(1, rows)
    const = lambda shape: pl.BlockSpec(shape, lambda b, i: (0,) * len(shape))
    return pl.pallas_call(
        functools.partial(_window_attn_kernel, n_blocks=nb, group=group),
        grid=(batch, steps),
        in_specs=[pl.BlockSpec((group * BLOCK, 3 * LANES), lambda b, i: (b * steps + i, 0)),
                  pl.BlockSpec((seq, LANES), lambda b, i: (b, 0)),
                  pl.BlockSpec((seq, LANES), lambda b, i: (b, 0)),
                  const(bias_win_t.shape), const((1, rows))],
        out_specs=pl.BlockSpec((group * BLOCK, 3 * LANES), lambda b, i: (b * steps + i, 0)),
        out_shape=jax.ShapeDtypeStruct(q.shape, BF16),
        scratch_shapes=[pltpu.VMEM((nb + 2, BLOCK, LANES), BF16),
                        pltpu.VMEM((2, nb + 2, HEAD_DIM + ONES_ROWS, BLOCK), BF16)],
        compiler_params=_cparams(2),
        name="window_attn",
    )(q, k, v, bias_win_t, sink_row)


def _norm_rope_pair(x, g, cos, sin_signed):
    lo = _lane_lo(x.shape)
    x2 = x * x
    ss_lo = jnp.sum(jnp.where(lo, x2, 0.0), axis=-1, keepdims=True)
    ss_hi = jnp.sum(jnp.where(lo, 0.0, x2), axis=-1, keepdims=True)
    ms = jnp.where(lo, ss_lo, ss_hi) * (1.0 / HEAD_DIM)
    y = x * lax.rsqrt(ms + EPS) * g
    lane = lax.broadcasted_iota(jnp.int32, x.shape, 1)
    quarter = HEAD_DIM // 4
    first = (lane & quarter) == 0
    partner = jnp.where(first, pltpu.roll(y, LANES - quarter, 1), pltpu.roll(y, quarter, 1))
    return y * cos + partner * sin_signed


def _dense_attn_kernel(q_ref, k_ref, v_ref, o_ref, ka_scr, vt_scr, kmax_scr, *, tq):
    seq = k_ref.shape[0]
    kc = 2 * BLOCK
    n_chunks = seq // kc

    ka_scr[0], ka_scr[1] = _keys_with_ones(k_ref[...])
    kmax_lo, kmax_hi = _pair_key_norm_max(k_ref[...])
    kmax_scr[0:1, :] = kmax_lo
    kmax_scr[1:2, :] = kmax_hi
    v_t = v_ref[...].astype(F32).T.astype(BF16)
    for g in range(2):
        vt_scr[g, :HEAD_DIM, :] = v_t[g * HEAD_DIM:(g + 1) * HEAD_DIM]
        vt_scr[g, HEAD_DIM:, :] = jnp.ones((ONES_ROWS, seq), BF16)

    def query_tile(i, carry):
        rows = pl.ds(pl.multiple_of(i * tq, tq), tq)
        lhs = _stack_gqa_heads([q_ref[rows, j * LANES:(j + 1) * LANES].astype(F32)
                                for j in range(3)])

        def write(acc):
            outs = [a[:HEAD_DIM] * (1.0 / a[HEAD_DIM:HEAD_DIM + 1]) for a in acc]
            for j, pair_t in enumerate(_unstack_gqa_heads(outs[0], outs[1], tq)):
                o_ref[rows, j * LANES:(j + 1) * LANES] = pair_t.T.astype(o_ref.dtype)

        lhs_shifted = _with_shift_lane(lhs, kmax_scr[0:1, :], kmax_scr[1:2, :], 0.0, 0.0)
        acc = []
        for g in range(2):
            per_head = []
            for h in range(3 * g, 3 * g + 3):
                p = jnp.exp2(_nt_dot(ka_scr[g], lhs_shifted[h * tq:(h + 1) * tq])).astype(BF16)
                per_head.append(jnp.dot(vt_scr[g], p, preferred_element_type=F32))
            acc.append(jnp.concatenate(per_head, axis=1))
        write(acc)
        sums = jnp.concatenate([a[HEAD_DIM:HEAD_DIM + 1] for a in acc], axis=0)

        @pl.when(jnp.logical_not(jnp.min(sums) >= SUM_FLOOR))
        def _():
            acc2 = []
            for g in range(2):
                per_head = []
                for h in range(3 * g, 3 * g + 3):
                    s_exact = _nt_dot(k_ref[...], lhs[h * tq:(h + 1) * tq])
                    m = None
                    for c in range(n_chunks):
                        s = s_exact[c * kc:(c + 1) * kc, :]
                        m_c = jnp.max(s, axis=0, keepdims=True)
                        m_new = m_c if m is None else jnp.maximum(m, m_c)
                        pc = jnp.exp2(s - m_new).astype(BF16)
                        pv = jnp.dot(vt_scr[g, :, c * kc:(c + 1) * kc], pc,
                                     preferred_element_type=F32)
                        a = pv if m is None else a * jnp.exp2(m - m_new) + pv
                        m = m_new
                    per_head.append(a)
                acc2.append(jnp.concatenate(per_head, axis=1))
            write(acc2)

        return carry

    lax.fori_loop(0, seq // tq, query_tile, 0)


def _dense_attn(q, k, v, batch, seq, tq):
    return pl.pallas_call(
        functools.partial(_dense_attn_kernel, tq=tq),
        grid=(batch,),
        in_specs=[pl.BlockSpec((seq, 3 * LANES), lambda b: (b, 0)),
                  pl.BlockSpec((seq, LANES), lambda b: (b, 0)),
                  pl.BlockSpec((seq, LANES), lambda b: (b, 0))],
        out_specs=pl.BlockSpec((seq, 3 * LANES), lambda b: (b, 0)),
        out_shape=jax.ShapeDtypeStruct(q.shape, BF16),
        scratch_shapes=[pltpu.VMEM((2, seq, LANES), BF16),
                        pltpu.VMEM((2, HEAD_DIM + ONES_ROWS, seq), BF16),
                        pltpu.VMEM((2, LANES), F32)],
        compiler_params=_cparams(1),
        name="dense_attn",
    )(q, k, v)


def _cross_attn_kernel(q_ref, mk_ref, mv_ref, o_ref, vt_scr, *, tq):
    i = pl.program_id(1)
    n_pairs = q_ref.shape[1] // LANES
    mem_len = mk_ref.shape[0]

    @pl.when(i == 0)
    def _():
        v_t = mv_ref[...].astype(F32).T.astype(BF16)
        for h in range(2 * n_pairs):
            vt_scr[h, :HEAD_DIM, :] = v_t[h * HEAD_DIM:(h + 1) * HEAD_DIM]
            vt_scr[h, HEAD_DIM:, :] = jnp.ones((ONES_ROWS, mem_len), BF16)

    for j in range(n_pairs):
        cols = slice(j * LANES, (j + 1) * LANES)
        lhs = _stack_pair_heads(q_ref[:, cols].astype(F32) * (HEAD_DIM ** -0.5 * LOG2E))
        s = _nt_dot(mk_ref[:, cols], lhs)
        p = jnp.exp2(s - jnp.max(s, axis=0, keepdims=True)).astype(BF16)
        acc = [jnp.dot(vt_scr[2 * j + h], p[:, h * tq:(h + 1) * tq], preferred_element_type=F32)
               for h in range(2)]
        pair_t = jnp.concatenate([a[:HEAD_DIM] * (1.0 / a[HEAD_DIM:HEAD_DIM + 1]) for a in acc],
                                 axis=0)
        o_ref[:, cols] = pair_t.T.astype(o_ref.dtype)


def _cross_attn(q, mk, mv, batch, seq, tq):
    nq = seq // tq
    mem_len = mk.shape[0] // batch
    width = q.shape[1]
    n_pairs = width // LANES
    return pl.pallas_call(
        functools.partial(_cross_attn_kernel, tq=tq),
        grid=(batch, nq),
        in_specs=[pl.BlockSpec((tq, width), lambda b, i: (b * nq + i, 0)),
                  pl.BlockSpec((mem_len, width), lambda b, i: (b, 0)),
                  pl.BlockSpec((mem_len, width), lambda b, i: (b, 0))],
        out_specs=pl.BlockSpec((tq, width), lambda b, i: (b * nq + i, 0)),
        out_shape=jax.ShapeDtypeStruct(q.shape, BF16),
        scratch_shapes=[pltpu.VMEM((2 * n_pairs, HEAD_DIM + ONES_ROWS, mem_len), BF16)],
        compiler_params=_cparams(2),
        name="cross_attn",
    )(q, mk, mv)


def _diff_attn_kernel(q1_ref, q2_ref, k1_ref, k2_ref, v_ref, bias_ref, bmax_ref, lam_ref, g_ref,
                      o_ref, vt_scr, ka_scr, kmax_scr, *, tq, lam_init):
    sub = tq // BLOCK
    seq = k1_ref.shape[0]
    kc = 2 * BLOCK
    n_chunks = seq // kc
    k_refs = (k1_ref, k2_ref)

    for head in range(2):
        v_t = v_ref[:, head * C_V_DIM:(head + 1) * C_V_DIM].astype(F32).T
        vt_scr[head, :C_V_DIM, :] = v_t.astype(BF16)
        vt_scr[head, C_V_DIM:, :] = jnp.ones((ONES_ROWS, seq), BF16)
    for t in range(2):
        ka_scr[t, 0], ka_scr[t, 1] = _keys_with_ones(k_refs[t][...])
        kmax_lo, kmax_hi = _pair_key_norm_max(k_refs[t][...])
        kmax_scr[t, 0:1, :] = kmax_lo
        kmax_scr[t, 1:2, :] = kmax_hi

    lam_vec = lam_ref[...]
    lam = (jnp.exp(jnp.sum(lam_vec[0:1] * lam_vec[1:2], axis=-1, keepdims=True))
           - jnp.exp(jnp.sum(lam_vec[2:3] * lam_vec[3:4], axis=-1, keepdims=True)) + lam_init)

    def query_tile(i, carry):
        rows = pl.ds(pl.multiple_of(i * tq, tq), tq)
        lhs = [_stack_pair_heads(q_ref[rows, :].astype(F32) * (HEAD_DIM ** -0.5 * LOG2E))
               for q_ref in (q1_ref, q2_ref)]
        _diff_attn_tile(i, rows, lhs, lam, bias_ref, bmax_ref, g_ref, o_ref, vt_scr, ka_scr,
                        kmax_scr, k_refs, tq=tq, sub=sub, kc=kc, n_chunks=n_chunks,
                        lam_init=lam_init)
        return carry

    lax.fori_loop(0, seq // tq, query_tile, 0)


def _diff_attn_tile(i, rows, lhs, lam, bias_ref, bmax_ref, g_ref, o_ref, vt_scr, ka_scr, kmax_scr,
                    k_refs, *, tq, sub, kc, n_chunks, lam_init):
    def bias_chunk(c, head):
        def tile(u, e):
            offset = (2 * c + e) - (i * sub + u)
            return bias_ref[head, jnp.clip(offset, -BIAS_FAR, BIAS_FAR) + BIAS_FAR]
        return jnp.concatenate([jnp.concatenate([tile(u, e) for u in range(sub)], axis=1)
                                for e in range(2)], axis=0)

    def pv_of_shifted(t):
        shifted = _with_shift_lane(lhs[t], kmax_scr[t, 0:1, :], kmax_scr[t, 1:2, :],
                                   bmax_ref[0], bmax_ref[1])
        acc = []
        for head in range(2):
            s = _nt_dot(ka_scr[t, head], shifted[head * tq:(head + 1) * tq])
            p = jnp.concatenate([jnp.exp2(s[c * kc:(c + 1) * kc, :] + bias_chunk(c, head)).astype(BF16)
                                 for c in range(n_chunks)], axis=0)
            acc.append(jnp.dot(vt_scr[head], p, preferred_element_type=F32))
        return acc

    def pv_running_max(t):
        acc = []
        for head in range(2):
            s_all = _nt_dot(k_refs[t][...], lhs[t][head * tq:(head + 1) * tq])
            m = None
            for c in range(n_chunks):
                s = s_all[c * kc:(c + 1) * kc, :] + bias_chunk(c, head)
                m_c = jnp.max(s, axis=0, keepdims=True)
                m_new = m_c if m is None else jnp.maximum(m, m_c)
                p = jnp.exp2(s - m_new).astype(BF16)
                pv = jnp.dot(vt_scr[head, :, c * kc:(c + 1) * kc], p,
                             preferred_element_type=F32)
                a = pv if m is None else a * jnp.exp2(m - m_new) + pv
                m = m_new
            acc.append(a)
        return acc

    def write(acc1, acc2):
        g = g_ref[...] * (1.0 - lam_init)
        for head in range(2):
            o1 = acc1[head][:C_V_DIM] * (1.0 / acc1[head][C_V_DIM:C_V_DIM + 1])
            o2 = acc2[head][:C_V_DIM] * (1.0 / acc2[head][C_V_DIM:C_V_DIM + 1])
            out = o1 - lam * o2
            ms = jnp.mean(out * out, axis=0, keepdims=True)
            y = out * lax.rsqrt(ms + EPS) * g
            o_ref[rows, head * C_V_DIM:(head + 1) * C_V_DIM] = y.T.astype(o_ref.dtype)

    acc1 = pv_of_shifted(0)
    acc2 = pv_of_shifted(1)
    write(acc1, acc2)
    sums = jnp.concatenate([a[C_V_DIM:C_V_DIM + 1] for a in acc1 + acc2], axis=0)

    @pl.when(jnp.logical_not(jnp.min(sums) >= SUM_FLOOR))
    def _():
        write(pv_running_max(0), pv_running_max(1))


def _diff_attn(q1, q2, k1, k2, v, bias_t, bias_max, lam_vecs, subln_g, lam_init, batch, seq, tq):
    n_pairs = C_HEADS // 2
    n_tiles = bias_t.shape[1]
    seq_spec = pl.BlockSpec((seq, LANES), lambda b, p: (b, p))
    g_cols = jnp.broadcast_to(subln_g.reshape(C_V_DIM, 1), (C_V_DIM, tq))
    return pl.pallas_call(
        functools.partial(_diff_attn_kernel, tq=tq, lam_init=lam_init),
        grid=(batch, n_pairs),
        in_specs=[seq_spec, seq_spec, seq_spec, seq_spec,
                  pl.BlockSpec((seq, 2 * C_V_DIM), lambda b, p: (b, p)),
                  pl.BlockSpec((2, n_tiles, BLOCK, BLOCK), lambda b, p: (p, 0, 0, 0)),
                  pl.BlockSpec((2, 1, LANES), lambda b, p: (p, 0, 0)),
                  pl.BlockSpec(lam_vecs.shape, lambda b, p: (0, 0)),
                  pl.BlockSpec((C_V_DIM, tq), lambda b, p: (0, 0))],
        out_specs=pl.BlockSpec((seq, 2 * C_V_DIM), lambda b, p: (b, p)),
        out_shape=jax.ShapeDtypeStruct(v.shape, BF16),
        scratch_shapes=[pltpu.VMEM((2, C_V_DIM + ONES_ROWS, seq), BF16),
                        pltpu.VMEM((2, 2, seq, LANES), BF16),
                        pltpu.VMEM((2, 2, LANES), F32)],
        compiler_params=_cparams(2),
        name="diff_attn",
    )(q1, q2, k1, k2, v, bias_t, bias_max, lam_vecs, g_cols)


def _rope_tables(seq):
    rows = seq // GRID_W
    row = jnp.broadcast_to(jnp.arange(rows)[:, None], (rows, GRID_W)).reshape(-1)
    col = jnp.broadcast_to(jnp.arange(GRID_W)[None, :], (rows, GRID_W)).reshape(-1)
    half = HEAD_DIM // 2
    inv = 1.0 / (ROPE_THETA ** (jnp.arange(0, half, 2, dtype=F32) / half))
    ang_row = row.astype(F32)[:, None] * inv
    ang_col = col.astype(F32)[:, None] * inv
    cos = jnp.concatenate([jnp.cos(ang_row)] * 2 + [jnp.cos(ang_col)] * 2, axis=-1)
    sin = jnp.concatenate([-jnp.sin(ang_row), jnp.sin(ang_row),
                           -jnp.sin(ang_col), jnp.sin(ang_col)], axis=-1)
    return jnp.tile(cos, (1, 2)), jnp.tile(sin, (1, 2))


def kernel(x, mem, rel_bias, mem_norm, final_norm, even_norm, even_w_in, even_sink, even_q_norm, even_k_norm, even_w_mem_kv, even_w_out, odd_norm, odd_w_in, odd_lambda_q1, odd_lambda_k1, odd_lambda_q2, odd_lambda_k2, odd_subln, odd_w_mem_kv, odd_w_out):
    batch, seq, d = x.shape
    mem_len = mem.shape[1]
    tokens = batch * seq
    xw = X_HEADS * HEAD_DIM
    gw = A_HEADS * HEAD_DIM

    w_in0 = even_w_in[0].astype(BF16)
    w_out0 = even_w_out[0].astype(BF16)
    w_in1 = odd_w_in[0].astype(BF16)
    w_out1 = odd_w_out[0].astype(BF16)

    x2 = x.reshape(tokens, d)
    mem2 = mem.reshape(batch * mem_len, d)

    bias_near, bias_win = _bias_tiles(rel_bias)
    cos, sin_signed = _rope_tables(seq)
    head_bias_max = jnp.max(rel_bias, axis=0) * LOG2E

    w_mem = jnp.concatenate([even_w_mem_kv[0], odd_w_mem_kv[0]], axis=1).astype(BF16)
    mk0, mv0, mk1, mv1 = _norm_proj(mem2, mem_norm, w_mem, (xw,) * 4, TM_MEM, gate_last=False)

    rope_args = (cos, sin_signed, jnp.tile(even_q_norm[0], 2).reshape(1, LANES),
                 jnp.tile(even_k_norm[0], 2).reshape(1, LANES))
    aq, ak, av, bq, bk, bv, xq, gate = _norm_proj(
        x2, even_norm[0], w_in0, (gw, 128, 128, gw, 128, 128, xw, D_MODEL), TM_PROJ,
        gate_last=True, qk_rope=(3, 4), rope_args=rope_args, seq=seq)
    y_a = _window_attn(aq, ak, av, bias_win, even_sink[0] * LOG2E, batch, seq, WINDOW_GROUP)
    y_b = _dense_attn(bq, bk, bv, batch, seq, TQ_DENSE)
    y_x = _cross_attn(xq, mk0, mv0, batch, seq, TQ_CROSS)
    lam_init = 0.8 - 0.6 * math.exp(-0.3 * 1)
    cw = C_HEADS * HEAD_DIM
    h1, q1, q2, k1, k2, v, xq1, gate1 = _gate_out(
        [y_a, y_b, y_x], gate, w_out0, x2, odd_norm[0], TM_PROJ, w_next=w_in1,
        splits=(cw, cw, cw, cw, C_HEADS * C_V_DIM, xw, D_MODEL))
    lam_vecs = jnp.stack([odd_lambda_q1[0], odd_lambda_k1[0], odd_lambda_q2[0], odd_lambda_k2[0]])
    bias_max = jnp.broadcast_to(head_bias_max.reshape(C_HEADS, 1, 1), (C_HEADS, 1, LANES))
    y_c = _diff_attn(q1, q2, k1, k2, v, bias_near, bias_max, lam_vecs, odd_subln[0], lam_init,
                     batch, seq, TQ_DIFF)
    y_x1 = _cross_attn(xq1, mk1, mv1, batch, seq, TQ_CROSS)
    (out,) = _gate_out([y_c, y_x1], gate1, w_out1, h1, final_norm, TM_PROJ)
    return out.reshape(batch, seq, d)
```

```python
import functools
import math

import numpy as np
import jax
import jax.numpy as jnp
from jax import lax
from jax.experimental import pallas as pl
from jax.experimental.pallas import tpu as pltpu

D_MODEL = 1024
HEAD_DIM = 64
BLOCK = 128
WINDOW = 128
GRID_W = 64
A_HEADS = 6
C_HEADS = 6
C_V_DIM = 128
X_HEADS = 4
REL_BUCKETS = 32
REL_MAX_DIST = 128
ROPE_THETA = 10000.0
EPS = 1e-6
NEG_INF = -1e30
LOG2E = math.log2(math.e)
LANES = 128
ONES_ROWS = 16
PROJ_SUB_ROWS = 256
GATE_SUB_ROWS = 1024
PROJ_COL_GROUP = 512
TM_PROJ = 1024
TM_MEM = 256
TQ_DENSE = 512
TQ_DIFF = 1024
TQ_CROSS = 2048
WINDOW_GROUP = 16
SUM_FLOOR = 2.0 ** -64
SHIFT_MARGIN = 2.0 ** -7
BIAS_FAR = 2
VMEM_LIMIT = 56 * 1024 * 1024

F32 = jnp.float32
BF16 = jnp.bfloat16


def _cparams(n_axes, fuse_inputs=None):
    return pltpu.CompilerParams(dimension_semantics=("arbitrary",) * n_axes,
                                vmem_limit_bytes=VMEM_LIMIT, allow_input_fusion=fuse_inputs)


def _lane_lo(shape):
    return lax.broadcasted_iota(jnp.int32, shape, len(shape) - 1) < HEAD_DIM


def _rmsnorm_rows(x, g):
    ms = jnp.mean(x * x, axis=-1, keepdims=True)
    return x * lax.rsqrt(ms + EPS) * g


def _nt_dot(a, b):
    return lax.dot_general(a, b, (((1,), (1,)), ((), ())), preferred_element_type=F32)


def _norm_proj_kernel(*refs, qk_rope, gate_last):
    x_ref, g_ref, w_ref = refs[:3]
    if qk_rope is None:
        out_refs = refs[3:]
    else:
        cos_ref, sin_ref, gq_ref, gk_ref = refs[3:7]
        out_refs = refs[7:]
    tm = x_ref.shape[0]
    for r0 in range(0, tm, PROJ_SUB_ROWS):
        rows = slice(r0, min(r0 + PROJ_SUB_ROWS, tm))

        def post(j, y):
            if qk_rope is not None and j in qk_rope:
                gain, scale = ((gq_ref, HEAD_DIM ** -0.5 * LOG2E) if j == qk_rope[0]
                               else (gk_ref, 1.0))
                y = jnp.concatenate(
                    [_norm_rope_pair(y[:, t * LANES:(t + 1) * LANES], gain[...],
                                     cos_ref[rows, :], sin_ref[rows, :]) * scale
                     for t in range(y.shape[1] // LANES)], axis=1)
            if gate_last and j == len(out_refs) - 1:
                y = _silu(y)
            return y

        xn = _rmsnorm_rows(x_ref[rows, :], g_ref[...]).astype(BF16)
        _project_and_store(xn, w_ref, out_refs, rows, post)


def _project_and_store(xn, w_ref, out_refs, rows, post):
    n = w_ref.shape[1]
    starts = [0]
    for o_ref in out_refs:
        starts.append(starts[-1] + o_ref.shape[1])
    assert starts[-1] == n
    for c0 in range(0, n, PROJ_COL_GROUP):
        c1 = min(c0 + PROJ_COL_GROUP, n)
        y = jnp.dot(xn, w_ref[:, c0:c1], preferred_element_type=F32)
        for j, o_ref in enumerate(out_refs):
            lo, hi = max(c0, starts[j]), min(c1, starts[j + 1])
            if lo < hi:
                o_ref[rows, lo - starts[j]:hi - starts[j]] = post(
                    j, y[:, lo - c0:hi - c0]).astype(o_ref.dtype)


def _silu(y):
    return y * (1.0 / (1.0 + jnp.exp(-y)))


def _norm_proj(x, g, w_bf16, splits, tm, gate_last, qk_rope=None, rope_args=(), seq=None):
    rows, d = x.shape
    n = w_bf16.shape[1]
    assert sum(splits) == n and rows % tm == 0
    in_specs = [pl.BlockSpec((tm, d), lambda i: (i, 0)),
                pl.BlockSpec((1, d), lambda i: (0, 0)),
                pl.BlockSpec((d, n), lambda i: (0, 0))]
    if qk_rope is not None:
        pos_blocks = seq // tm
        in_specs += [pl.BlockSpec((tm, LANES), lambda i: (i % pos_blocks, 0)),
                     pl.BlockSpec((tm, LANES), lambda i: (i % pos_blocks, 0)),
                     pl.BlockSpec((1, LANES), lambda i: (0, 0)),
                     pl.BlockSpec((1, LANES), lambda i: (0, 0))]
    return pl.pallas_call(
        functools.partial(_norm_proj_kernel, qk_rope=qk_rope, gate_last=gate_last),
        grid=(rows // tm,),
        in_specs=in_specs,
        out_specs=[pl.BlockSpec((tm, s), lambda i: (i, 0)) for s in splits],
        out_shape=[jax.ShapeDtypeStruct((rows, s), BF16) for s in splits],
        compiler_params=_cparams(1, [j == 2 for j in range(len(in_specs))]),
        name="norm_proj",
    )(x, g.reshape(1, d), w_bf16, *rope_args)


def _gate_out_kernel(*refs, n_parts, after):
    y_refs = refs[:n_parts]
    gate_ref, w_ref, res_ref, g_ref = refs[n_parts:n_parts + 4]
    if after == "proj":
        w_next_ref, h_ref = refs[n_parts + 4:n_parts + 6]
        out_refs = refs[n_parts + 6:]
    else:
        h_ref = refs[n_parts + 4]
    tm = res_ref.shape[0]
    for r0 in range(0, tm, GATE_SUB_ROWS):
        rows = slice(r0, min(r0 + GATE_SUB_ROWS, tm))
        parts = []
        c0 = 0
        for y_ref in y_refs:
            width = y_ref.shape[1]
            parts.append(y_ref[rows, :] * gate_ref[rows, c0:c0 + width])
            c0 += width
        h = res_ref[rows, :] + jnp.dot(jnp.concatenate(parts, axis=1), w_ref[...],
                                       preferred_element_type=F32)
        if after == "norm":
            h_ref[rows, :] = _rmsnorm_rows(h, g_ref[...])
            continue
        h_ref[rows, :] = h
        gate_path = len(out_refs) - 1
        _project_and_store(_rmsnorm_rows(h, g_ref[...]).astype(BF16), w_next_ref, out_refs, rows,
                           lambda j, y: _silu(y) if j == gate_path else y)


def _gate_out(y_parts, gate, w_bf16, resid, norm_g, tm, w_next=None, splits=()):
    rows, d = resid.shape
    mix = w_bf16.shape[0]
    after = "norm" if w_next is None else "proj"
    row_block = lambda width: pl.BlockSpec((tm, width), lambda i: (i, 0))
    whole = lambda a: pl.BlockSpec(a.shape, lambda i: (0, 0))
    in_specs = [row_block(y.shape[1]) for y in y_parts]
    in_specs += [row_block(mix), whole(w_bf16), row_block(d), pl.BlockSpec((1, d), lambda i: (0, 0))]
    args = list(y_parts) + [gate, w_bf16, resid, norm_g.reshape(1, d)]
    out_specs = [row_block(d)]
    out_shape = [jax.ShapeDtypeStruct((rows, d), F32)]
    if after == "proj":
        assert sum(splits) == w_next.shape[1]
        in_specs.append(whole(w_next))
        args.append(w_next)
        out_specs += [row_block(s) for s in splits]
        out_shape += [jax.ShapeDtypeStruct((rows, s), BF16) for s in splits]
    return pl.pallas_call(
        functools.partial(_gate_out_kernel, n_parts=len(y_parts), after=after),
        grid=(rows // tm,),
        in_specs=in_specs,
        out_specs=out_specs,
        out_shape=out_shape,
        compiler_params=_cparams(1, [a is w_bf16 or a is w_next for a in args]),
        name="gate_out",
    )(*args)


def _bucket_thresholds():
    nb = REL_BUCKETS // 2
    max_exact = nb // 2
    n = np.arange(0, 4 * REL_MAX_DIST)
    nf = np.maximum(n, 1).astype(np.float32)
    large = max_exact + (np.log(nf / np.float32(max_exact))
                         / np.float32(math.log(REL_MAX_DIST / max_exact))
                         * np.float32(nb - max_exact)).astype(np.int32)
    bucket = np.where(n < max_exact, n, np.minimum(large, nb - 1))
    assert np.all(np.diff(bucket) >= 0) and bucket[0] == 0 and bucket[-1] == nb - 1
    thr = [int(np.argmax(bucket >= k)) for k in range(1, nb)]
    assert thr[-1] < REL_MAX_DIST
    return thr


def _bias_tiles_kernel(tab_ref, near_ref, win_ref):
    h = pl.program_id(0)
    thr = _bucket_thresholds()
    nb = REL_BUCKETS // 2
    row = lax.broadcasted_iota(jnp.int32, (BLOCK, BLOCK), 0)
    col = lax.broadcasted_iota(jnp.int32, (BLOCK, BLOCK), 1)
    def tile_of(rel):
        n = jnp.abs(rel)
        neg = jnp.full((BLOCK, BLOCK), tab_ref[0, h], F32)
        pos = jnp.full((BLOCK, BLOCK), tab_ref[nb, h], F32)
        for b in range(1, nb):
            ge = n >= thr[b - 1]
            neg = jnp.where(ge, tab_ref[b, h], neg)
            pos = jnp.where(ge, tab_ref[nb + b, h], pos)
        return jnp.where(rel > 0, pos, neg) * LOG2E

    for k in range(2 * BIAS_FAR + 1):
        rel = (k - BIAS_FAR) * BLOCK + row - col
        tile = tile_of(rel)
        near_ref[0, k] = tile
        if abs(k - BIAS_FAR) <= 1:
            j = k - BIAS_FAR + 1
            win_ref[j * BLOCK:(j + 1) * BLOCK, :] = jnp.where(jnp.abs(rel) <= WINDOW, tile, NEG_INF)


def _bias_tiles(rel_bias):
    assert _bucket_thresholds()[-1] <= (BIAS_FAR - 1) * BLOCK + 1
    n_tiles = 2 * BIAS_FAR + 1
    n_heads = rel_bias.shape[1]
    return pl.pallas_call(
        _bias_tiles_kernel,
        grid=(n_heads,),
        in_specs=[pl.BlockSpec(memory_space=pltpu.SMEM)],
        out_specs=[pl.BlockSpec((1, n_tiles, BLOCK, BLOCK), lambda h: (h, 0, 0, 0)),
                   pl.BlockSpec((3 * BLOCK, BLOCK), lambda h: (0, h))],
        out_shape=[jax.ShapeDtypeStruct((n_heads, n_tiles, BLOCK, BLOCK), F32),
                   jax.ShapeDtypeStruct((3 * BLOCK, n_heads * BLOCK), F32)],
        compiler_params=_cparams(1),
        name="bias_tiles",
    )(rel_bias)


def _stack_gqa_heads(q_tiles):
    lo = _lane_lo(q_tiles[0].shape)
    swap = lambda q: pltpu.roll(q, HEAD_DIM, 1)
    heads_lo = [q_tiles[0], swap(q_tiles[0]), q_tiles[1]]
    heads_hi = [q_tiles[1], swap(q_tiles[2]), q_tiles[2]]
    return jnp.concatenate([jnp.where(lo, q, 0.0).astype(BF16) for q in heads_lo]
                           + [jnp.where(lo, 0.0, q).astype(BF16) for q in heads_hi], axis=0)


def _unstack_gqa_heads(out_lo, out_hi, width):
    blk = lambda o, j: o[:, j * width:(j + 1) * width]
    return [jnp.concatenate([blk(out_lo, 0), blk(out_lo, 1)], axis=0),
            jnp.concatenate([blk(out_lo, 2), blk(out_hi, 0)], axis=0),
            jnp.concatenate([blk(out_hi, 1), blk(out_hi, 2)], axis=0)]


def _stack_pair_heads(q):
    lo = _lane_lo(q.shape)
    return jnp.concatenate([jnp.where(lo, q, 0.0).astype(BF16),
                            jnp.where(lo, 0.0, q).astype(BF16)], axis=0)


def _with_shift_lane(lhs, kmax_lo, kmax_hi, bias_max_lo, bias_max_hi):
    half = lhs.shape[0] // 2
    lf = lhs.astype(F32)
    q_sq = jnp.sum(lf * lf, axis=-1, keepdims=True)
    q_norm = q_sq * lax.rsqrt(jnp.maximum(q_sq, 1e-30))
    row = lax.broadcasted_iota(jnp.int32, lhs.shape, 0)
    lane = lax.broadcasted_iota(jnp.int32, lhs.shape, 1)
    lo_row = row < half
    shift = (q_norm * jnp.where(lo_row, kmax_lo, kmax_hi)
             + jnp.where(lo_row, bias_max_lo, bias_max_hi))
    shift = shift + jnp.abs(shift) * SHIFT_MARGIN
    slot_lane = jnp.where(lo_row, HEAD_DIM, 0)
    return jnp.where(lane == slot_lane, -shift, lf).astype(BF16)


def _keys_with_ones(k_pair):
    lane = lax.broadcasted_iota(jnp.int32, k_pair.shape, 1)
    kf = k_pair.astype(F32)
    k_lo = jnp.where(lane < HEAD_DIM, kf, jnp.where(lane == HEAD_DIM, 1.0, 0.0))
    k_hi = jnp.where(lane >= HEAD_DIM, kf, jnp.where(lane == 0, 1.0, 0.0))
    return k_lo.astype(BF16), k_hi.astype(BF16)


def _shifted_scores(ka_lo, ka_hi, lhs_shifted):
    half = lhs_shifted.shape[0] // 2
    return jnp.concatenate([_nt_dot(ka_lo, lhs_shifted[:half]), _nt_dot(ka_hi, lhs_shifted[half:])],
                           axis=1)


def _pair_key_norm_max(k_pair):
    kf = k_pair.astype(F32)
    row = lax.broadcasted_iota(jnp.int32, (LANES, 2 * LANES), 0)
    col = lax.broadcasted_iota(jnp.int32, (LANES, 2 * LANES), 1)
    half_sel = jnp.where((row < HEAD_DIM) == (col < LANES), 1.0, 0.0).astype(BF16)
    sq = jnp.dot((kf * kf).astype(BF16), half_sel, preferred_element_type=F32)
    norm_max = jnp.sqrt(jnp.max(sq, axis=0, keepdims=True))
    return norm_max[:, :LANES], norm_max[:, LANES:]


def _window_attn_kernel(q_ref, k_ref, v_ref, bias_ref, sink_row_ref,
                        o_ref, kpad_scr, vtpad_scr, *, n_blocks, group):
    i = pl.program_id(1)
    half = 3 * BLOCK

    @pl.when(i == 0)
    def _():
        zero_blk = jnp.zeros((1, BLOCK, LANES), BF16)
        kpad_scr[0:1] = zero_blk
        kpad_scr[n_blocks + 1:n_blocks + 2] = zero_blk
        kpad_scr[1:n_blocks + 1] = k_ref[...].reshape(n_blocks, BLOCK, LANES)
        v_t = v_ref[...].astype(F32).T.astype(BF16)
        ones = jnp.ones((ONES_ROWS, BLOCK), BF16)
        zero_v = jnp.zeros((HEAD_DIM + ONES_ROWS, BLOCK), BF16)
        for g in range(2):
            vtpad_scr[g, 0] = zero_v
            vtpad_scr[g, n_blocks + 1] = zero_v
            for blk in range(n_blocks):
                vtpad_scr[g, blk + 1, :HEAD_DIM, :] = v_t[g * HEAD_DIM:(g + 1) * HEAD_DIM,
                                                           blk * BLOCK:(blk + 1) * BLOCK]
                vtpad_scr[g, blk + 1, HEAD_DIM:, :] = ones

    sink_row = sink_row_ref[...]

    for u in range(group):
        n = i * group + u
        qf = q_ref[u * BLOCK:(u + 1) * BLOCK, :].astype(F32) * (HEAD_DIM ** -0.5 * LOG2E)
        lhs = _stack_gqa_heads([qf[:, j * LANES:(j + 1) * LANES] for j in range(3)])
        k_win = kpad_scr[pl.ds(n, 3)].reshape(3 * BLOCK, LANES)
        s = _nt_dot(k_win, lhs) + bias_ref[...]
        s = jnp.concatenate([jnp.where(n > 0, s[:BLOCK], NEG_INF), s[BLOCK:2 * BLOCK],
                             jnp.where(n < n_blocks - 1, s[2 * BLOCK:], NEG_INF)], axis=0)
        m = jnp.maximum(jnp.max(s, axis=0, keepdims=True), sink_row)
        p = jnp.exp2(s - m).astype(BF16)
        p_sink = jnp.exp2(sink_row - m)
        outs = []
        for g in range(2):
            acc = None
            for j in range(3):
                pv = jnp.dot(vtpad_scr[g, n + j], p[j * BLOCK:(j + 1) * BLOCK, g * half:(g + 1) * half],
                             preferred_element_type=F32)
                acc = pv if acc is None else acc + pv
            total = acc[HEAD_DIM:HEAD_DIM + 1] + p_sink[:, g * half:(g + 1) * half]
            outs.append(acc[:HEAD_DIM] * (1.0 / total))
        for j, pair_t in enumerate(_unstack_gqa_heads(outs[0], outs[1], BLOCK)):
            o_ref[u * BLOCK:(u + 1) * BLOCK, j * LANES:(j + 1) * LANES] = pair_t.T.astype(o_ref.dtype)


def _window_attn(q, k, v, bias_win_t, sink, batch, seq, group):
    nb = seq // BLOCK
    steps = nb // group
    rows = A_HEADS * BLOCK
    sink_row = jnp.repeat(sink, BLOCK).reshape(1, rows)
    const = lambda shape: pl.BlockSpec(shape, lambda b, i: (0,) * len(shape))
    return pl.pallas_call(
        functools.partial(_window_attn_kernel, n_blocks=nb, group=group),
        grid=(batch, steps),
        in_specs=[pl.BlockSpec((group * BLOCK, 3 * LANES), lambda b, i: (b * steps + i, 0)),
                  pl.BlockSpec((seq, LANES), lambda b, i: (b, 0)),
                  pl.BlockSpec((seq, LANES), lambda b, i: (b, 0)),
                  const(bias_win_t.shape), const((1, rows))],
        out_specs=pl.BlockSpec((group * BLOCK, 3 * LANES), lambda b, i: (b * steps + i, 0)),
        out_shape=jax.ShapeDtypeStruct(q.shape, BF16),
        scratch_shapes=[pltpu.VMEM((nb + 2, BLOCK, LANES), BF16),
                        pltpu.VMEM((2, nb + 2, HEAD_DIM + ONES_ROWS, BLOCK), BF16)],
        compiler_params=_cparams(2),
        name="window_attn",
    )(q, k, v, bias_win_t, sink_row)


def _norm_rope_pair(x, g, cos, sin_signed):
    lo = _lane_lo(x.shape)
    x2 = x * x
    ss_lo = jnp.sum(jnp.where(lo, x2, 0.0), axis=-1, keepdims=True)
    ss_hi = jnp.sum(jnp.where(lo, 0.0, x2), axis=-1, keepdims=True)
    ms = jnp.where(lo, ss_lo, ss_hi) * (1.0 / HEAD_DIM)
    y = x * lax.rsqrt(ms + EPS) * g
    lane = lax.broadcasted_iota(jnp.int32, x.shape, 1)
    quarter = HEAD_DIM // 4
    first = (lane & quarter) == 0
    partner = jnp.where(first, pltpu.roll(y, LANES - quarter, 1), pltpu.roll(y, quarter, 1))
    return y * cos + partner * sin_signed


def _dense_attn_kernel(q_ref, k_ref, v_ref, o_ref, ka_scr, vt_scr, kmax_scr, *, tq):
    seq = k_ref.shape[0]
    kc = 2 * BLOCK
    n_chunks = seq // kc
    grp = 3 * tq

    ka_scr[0], ka_scr[1] = _keys_with_ones(k_ref[...])
    kmax_lo, kmax_hi = _pair_key_norm_max(k_ref[...])
    kmax_scr[0:1, :] = kmax_lo
    kmax_scr[1:2, :] = kmax_hi
    v_t = v_ref[...].astype(F32).T.astype(BF16)
    for g in range(2):
        vt_scr[g, :HEAD_DIM, :] = v_t[g * HEAD_DIM:(g + 1) * HEAD_DIM]
        vt_scr[g, HEAD_DIM:, :] = jnp.ones((ONES_ROWS, seq), BF16)

    def query_tile(i, carry):
        rows = pl.ds(pl.multiple_of(i * tq, tq), tq)
        lhs = _stack_gqa_heads([q_ref[rows, j * LANES:(j + 1) * LANES].astype(F32)
                                for j in range(3)])

        def write(acc):
            outs = [a[:HEAD_DIM] * (1.0 / a[HEAD_DIM:HEAD_DIM + 1]) for a in acc]
            for j, pair_t in enumerate(_unstack_gqa_heads(outs[0], outs[1], tq)):
                o_ref[rows, j * LANES:(j + 1) * LANES] = pair_t.T.astype(o_ref.dtype)

        lhs_shifted = _with_shift_lane(lhs, kmax_scr[0:1, :], kmax_scr[1:2, :], 0.0, 0.0)
        acc = []
        for g in range(2):
            p = jnp.exp2(_nt_dot(ka_scr[g], lhs_shifted[g * grp:(g + 1) * grp])).astype(BF16)
            acc.append(jnp.dot(vt_scr[g], p, preferred_element_type=F32))
        write(acc)
        sums = jnp.concatenate([a[HEAD_DIM:HEAD_DIM + 1] for a in acc], axis=0)

        @pl.when(jnp.logical_not(jnp.min(sums) >= SUM_FLOOR))
        def _():
            acc2 = []
            for g in range(2):
                s_exact = _nt_dot(k_ref[...], lhs[g * grp:(g + 1) * grp])
                m = None
                for c in range(n_chunks):
                    s = s_exact[c * kc:(c + 1) * kc, :]
                    m_c = jnp.max(s, axis=0, keepdims=True)
                    m_new = m_c if m is None else jnp.maximum(m, m_c)
                    pc = jnp.exp2(s - m_new).astype(BF16)
                    pv = jnp.dot(vt_scr[g, :, c * kc:(c + 1) * kc], pc, preferred_element_type=F32)
                    a = pv if m is None else a * jnp.exp2(m - m_new) + pv
                    m = m_new
                acc2.append(a)
            write(acc2)

        return carry

    lax.fori_loop(0, seq // tq, query_tile, 0)


def _dense_attn(q, k, v, batch, seq, tq):
    return pl.pallas_call(
        functools.partial(_dense_attn_kernel, tq=tq),
        grid=(batch,),
        in_specs=[pl.BlockSpec((seq, 3 * LANES), lambda b: (b, 0)),
                  pl.BlockSpec((seq, LANES), lambda b: (b, 0)),
                  pl.BlockSpec((seq, LANES), lambda b: (b, 0))],
        out_specs=pl.BlockSpec((seq, 3 * LANES), lambda b: (b, 0)),
        out_shape=jax.ShapeDtypeStruct(q.shape, BF16),
        scratch_shapes=[pltpu.VMEM((2, seq, LANES), BF16),
                        pltpu.VMEM((2, HEAD_DIM + ONES_ROWS, seq), BF16),
                        pltpu.VMEM((2, LANES), F32)],
        compiler_params=_cparams(1),
        name="dense_attn",
    )(q, k, v)


def _cross_attn_kernel(q_ref, mk_ref, mv_ref, o_ref, vt_scr, *, tq):
    i = pl.program_id(1)
    n_pairs = q_ref.shape[1] // LANES
    mem_len = mk_ref.shape[0]

    @pl.when(i == 0)
    def _():
        v_t = mv_ref[...].astype(F32).T.astype(BF16)
        for h in range(2 * n_pairs):
            vt_scr[h, :HEAD_DIM, :] = v_t[h * HEAD_DIM:(h + 1) * HEAD_DIM]
            vt_scr[h, HEAD_DIM:, :] = jnp.ones((ONES_ROWS, mem_len), BF16)

    for j in range(n_pairs):
        cols = slice(j * LANES, (j + 1) * LANES)
        lhs = _stack_pair_heads(q_ref[:, cols].astype(F32) * (HEAD_DIM ** -0.5 * LOG2E))
        s = _nt_dot(mk_ref[:, cols], lhs)
        p = jnp.exp2(s - jnp.max(s, axis=0, keepdims=True)).astype(BF16)
        acc = [jnp.dot(vt_scr[2 * j + h], p[:, h * tq:(h + 1) * tq], preferred_element_type=F32)
               for h in range(2)]
        pair_t = jnp.concatenate([a[:HEAD_DIM] * (1.0 / a[HEAD_DIM:HEAD_DIM + 1]) for a in acc],
                                 axis=0)
        o_ref[:, cols] = pair_t.T.astype(o_ref.dtype)


def _cross_attn(q, mk, mv, batch, seq, tq):
    nq = seq // tq
    mem_len = mk.shape[0] // batch
    width = q.shape[1]
    n_pairs = width // LANES
    return pl.pallas_call(
        functools.partial(_cross_attn_kernel, tq=tq),
        grid=(batch, nq),
        in_specs=[pl.BlockSpec((tq, width), lambda b, i: (b * nq + i, 0)),
                  pl.BlockSpec((mem_len, width), lambda b, i: (b, 0)),
                  pl.BlockSpec((mem_len, width), lambda b, i: (b, 0))],
        out_specs=pl.BlockSpec((tq, width), lambda b, i: (b * nq + i, 0)),
        out_shape=jax.ShapeDtypeStruct(q.shape, BF16),
        scratch_shapes=[pltpu.VMEM((2 * n_pairs, HEAD_DIM + ONES_ROWS, mem_len), BF16)],
        compiler_params=_cparams(2),
        name="cross_attn",
    )(q, mk, mv)


def _diff_attn_kernel(q1_ref, q2_ref, k1_ref, k2_ref, v_ref, bias_ref, bmax_ref, lam_ref, g_ref,
                      o_ref, vt_scr, ka_scr, kmax_scr, *, tq, lam_init):
    sub = tq // BLOCK
    seq = k1_ref.shape[0]
    kc = 2 * BLOCK
    n_chunks = seq // kc
    k_refs = (k1_ref, k2_ref)

    for head in range(2):
        v_t = v_ref[:, head * C_V_DIM:(head + 1) * C_V_DIM].astype(F32).T
        vt_scr[head, :C_V_DIM, :] = v_t.astype(BF16)
        vt_scr[head, C_V_DIM:, :] = jnp.ones((ONES_ROWS, seq), BF16)
    for t in range(2):
        ka_scr[t, 0], ka_scr[t, 1] = _keys_with_ones(k_refs[t][...])
        kmax_lo, kmax_hi = _pair_key_norm_max(k_refs[t][...])
        kmax_scr[t, 0:1, :] = kmax_lo
        kmax_scr[t, 1:2, :] = kmax_hi

    lam_vec = lam_ref[...]
    lam = (jnp.exp(jnp.sum(lam_vec[0:1] * lam_vec[1:2], axis=-1, keepdims=True))
           - jnp.exp(jnp.sum(lam_vec[2:3] * lam_vec[3:4], axis=-1, keepdims=True)) + lam_init)

    def query_tile(i, carry):
        rows = pl.ds(pl.multiple_of(i * tq, tq), tq)
        lhs = [_stack_pair_heads(q_ref[rows, :].astype(F32) * (HEAD_DIM ** -0.5 * LOG2E))
               for q_ref in (q1_ref, q2_ref)]
        _diff_attn_tile(i, rows, lhs, lam, bias_ref, bmax_ref, g_ref, o_ref, vt_scr, ka_scr,
                        kmax_scr, k_refs, tq=tq, sub=sub, kc=kc, n_chunks=n_chunks,
                        lam_init=lam_init)
        return carry

    lax.fori_loop(0, seq // tq, query_tile, 0)


def _diff_attn_tile(i, rows, lhs, lam, bias_ref, bmax_ref, g_ref, o_ref, vt_scr, ka_scr, kmax_scr,
                    k_refs, *, tq, sub, kc, n_chunks, lam_init):
    def bias_chunk(c):
        def tile(head, u, e):
            offset = (2 * c + e) - (i * sub + u)
            return bias_ref[head, jnp.clip(offset, -BIAS_FAR, BIAS_FAR) + BIAS_FAR]
        return jnp.concatenate(
            [jnp.concatenate([tile(head, u, e) for head in range(2) for u in range(sub)], axis=1)
             for e in range(2)], axis=0)

    def shifted_scores(t):
        return _shifted_scores(ka_scr[t, 0], ka_scr[t, 1], _with_shift_lane(
            lhs[t], kmax_scr[t, 0:1, :], kmax_scr[t, 1:2, :], bmax_ref[0], bmax_ref[1]))

    def pv_of_shifted(s_all):
        p = jnp.concatenate([jnp.exp2(s_all[c * kc:(c + 1) * kc, :] + bias_chunk(c)).astype(BF16)
                             for c in range(n_chunks)], axis=0)
        return [jnp.dot(vt_scr[head], p[:, head * tq:(head + 1) * tq],
                        preferred_element_type=F32) for head in range(2)]

    def pv_running_max(s_all):
        m = None
        acc = [None, None]
        for c in range(n_chunks):
            s = s_all[c * kc:(c + 1) * kc, :] + bias_chunk(c)
            m_c = jnp.max(s, axis=0, keepdims=True)
            m_new = m_c if m is None else jnp.maximum(m, m_c)
            p = jnp.exp2(s - m_new).astype(BF16)
            if m is not None:
                alpha = jnp.exp2(m - m_new)
            for head in range(2):
                cols = slice(head * tq, (head + 1) * tq)
                pv = jnp.dot(vt_scr[head, :, c * kc:(c + 1) * kc], p[:, cols],
                             preferred_element_type=F32)
                acc[head] = pv if m is None else acc[head] * alpha[:, cols] + pv
            m = m_new
        return acc

    def write(acc1, acc2):
        g = g_ref[...] * (1.0 - lam_init)
        for head in range(2):
            o1 = acc1[head][:C_V_DIM] * (1.0 / acc1[head][C_V_DIM:C_V_DIM + 1])
            o2 = acc2[head][:C_V_DIM] * (1.0 / acc2[head][C_V_DIM:C_V_DIM + 1])
            out = o1 - lam * o2
            ms = jnp.mean(out * out, axis=0, keepdims=True)
            y = out * lax.rsqrt(ms + EPS) * g
            o_ref[rows, head * C_V_DIM:(head + 1) * C_V_DIM] = y.T.astype(o_ref.dtype)

    acc1 = pv_of_shifted(shifted_scores(0))
    acc2 = pv_of_shifted(shifted_scores(1))
    write(acc1, acc2)
    sums = jnp.concatenate([a[C_V_DIM:C_V_DIM + 1] for a in acc1 + acc2], axis=0)

    @pl.when(jnp.logical_not(jnp.min(sums) >= SUM_FLOOR))
    def _():
        write(pv_running_max(_nt_dot(k_refs[0][...], lhs[0])),
              pv_running_max(_nt_dot(k_refs[1][...], lhs[1])))


def _diff_attn(q1, q2, k1, k2, v, bias_t, bias_max, lam_vecs, subln_g, lam_init, batch, seq, tq):
    n_pairs = C_HEADS // 2
    n_tiles = bias_t.shape[1]
    seq_spec = pl.BlockSpec((seq, LANES), lambda b, p: (b, p))
    g_cols = jnp.broadcast_to(subln_g.reshape(C_V_DIM, 1), (C_V_DIM, tq))
    return pl.pallas_call(
        functools.partial(_diff_attn_kernel, tq=tq, lam_init=lam_init),
        grid=(batch, n_pairs),
        in_specs=[seq_spec, seq_spec, seq_spec, seq_spec,
                  pl.BlockSpec((seq, 2 * C_V_DIM), lambda b, p: (b, p)),
                  pl.BlockSpec((2, n_tiles, BLOCK, BLOCK), lambda b, p: (p, 0, 0, 0)),
                  pl.BlockSpec((2, 1, LANES), lambda b, p: (p, 0, 0)),
                  pl.BlockSpec(lam_vecs.shape, lambda b, p: (0, 0)),
                  pl.BlockSpec((C_V_DIM, tq), lambda b, p: (0, 0))],
        out_specs=pl.BlockSpec((seq, 2 * C_V_DIM), lambda b, p: (b, p)),
        out_shape=jax.ShapeDtypeStruct(v.shape, BF16),
        scratch_shapes=[pltpu.VMEM((2, C_V_DIM + ONES_ROWS, seq), BF16),
                        pltpu.VMEM((2, 2, seq, LANES), BF16),
                        pltpu.VMEM((2, 2, LANES), F32)],
        compiler_params=_cparams(2),
        name="diff_attn",
    )(q1, q2, k1, k2, v, bias_t, bias_max, lam_vecs, g_cols)


def _rope_tables(seq):
    rows = seq // GRID_W
    row = jnp.broadcast_to(jnp.arange(rows)[:, None], (rows, GRID_W)).reshape(-1)
    col = jnp.broadcast_to(jnp.arange(GRID_W)[None, :], (rows, GRID_W)).reshape(-1)
    half = HEAD_DIM // 2
    inv = 1.0 / (ROPE_THETA ** (jnp.arange(0, half, 2, dtype=F32) / half))
    ang_row = row.astype(F32)[:, None] * inv
    ang_col = col.astype(F32)[:, None] * inv
    cos = jnp.concatenate([jnp.cos(ang_row)] * 2 + [jnp.cos(ang_col)] * 2, axis=-1)
    sin = jnp.concatenate([-jnp.sin(ang_row), jnp.sin(ang_row),
                           -jnp.sin(ang_col), jnp.sin(ang_col)], axis=-1)
    return jnp.tile(cos, (1, 2)), jnp.tile(sin, (1, 2))


def kernel(x, mem, rel_bias, mem_norm, final_norm, even_norm, even_w_in, even_sink, even_q_norm, even_k_norm, even_w_mem_kv, even_w_out, odd_norm, odd_w_in, odd_lambda_q1, odd_lambda_k1, odd_lambda_q2, odd_lambda_k2, odd_subln, odd_w_mem_kv, odd_w_out):
    batch, seq, d = x.shape
    mem_len = mem.shape[1]
    tokens = batch * seq
    xw = X_HEADS * HEAD_DIM
    gw = A_HEADS * HEAD_DIM

    w_in0 = even_w_in[0].astype(BF16)
    w_out0 = even_w_out[0].astype(BF16)
    w_in1 = odd_w_in[0].astype(BF16)
    w_out1 = odd_w_out[0].astype(BF16)

    x2 = x.reshape(tokens, d)
    mem2 = mem.reshape(batch * mem_len, d)

    bias_near, bias_win = _bias_tiles(rel_bias)
    cos, sin_signed = _rope_tables(seq)
    head_bias_max = jnp.max(rel_bias, axis=0) * LOG2E

    w_mem = jnp.concatenate([even_w_mem_kv[0], odd_w_mem_kv[0]], axis=1).astype(BF16)
    mk0, mv0, mk1, mv1 = _norm_proj(mem2, mem_norm, w_mem, (xw,) * 4, TM_MEM, gate_last=False)

    rope_args = (cos, sin_signed, jnp.tile(even_q_norm[0], 2).reshape(1, LANES),
                 jnp.tile(even_k_norm[0], 2).reshape(1, LANES))
    aq, ak, av, bq, bk, bv, xq, gate = _norm_proj(
        x2, even_norm[0], w_in0, (gw, 128, 128, gw, 128, 128, xw, D_MODEL), TM_PROJ,
        gate_last=True, qk_rope=(3, 4), rope_args=rope_args, seq=seq)
    y_a = _window_attn(aq, ak, av, bias_win, even_sink[0] * LOG2E, batch, seq, WINDOW_GROUP)
    y_b = _dense_attn(bq, bk, bv, batch, seq, TQ_DENSE)
    y_x = _cross_attn(xq, mk0, mv0, batch, seq, TQ_CROSS)
    lam_init = 0.8 - 0.6 * math.exp(-0.3 * 1)
    cw = C_HEADS * HEAD_DIM
    h1, q1, q2, k1, k2, v, xq1, gate1 = _gate_out(
        [y_a, y_b, y_x], gate, w_out0, x2, odd_norm[0], TM_PROJ, w_next=w_in1,
        splits=(cw, cw, cw, cw, C_HEADS * C_V_DIM, xw, D_MODEL))
    lam_vecs = jnp.stack([odd_lambda_q1[0], odd_lambda_k1[0], odd_lambda_q2[0], odd_lambda_k2[0]])
    bias_max = jnp.broadcast_to(head_bias_max.reshape(C_HEADS, 1, 1), (C_HEADS, 1, LANES))
    y_c = _diff_attn(q1, q2, k1, k2, v, bias_near, bias_max, lam_vecs, odd_subln[0], lam_init,
                     batch, seq, TQ_DIFF)
    y_x1 = _cross_attn(xq1, mk1, mv1, batch, seq, TQ_CROSS)
    (out,) = _gate_out([y_c, y_x1], gate1, w_out1, h1, final_norm, TM_PROJ)
    return out.reshape(batch, seq, d)
```
